```python
import math
import jax
import jax.numpy as jnp
from jax import lax
import numpy as np

D_MODEL = 2048
BATCH = 4
SEQ = 2048
DEPTH = 2

N_META = 16
N_BRANCH = 4
BRANCH_WIDTH = D_MODEL // 4
LRU_WIDTH = BRANCH_WIDTH
LRU_HEADS = 4
LRU_BLOCK = LRU_WIDTH // LRU_HEADS
LRU_CONV = 4
LRU_C = 8.0
POOL_WINDOWS = (2, 4, 8, 16)
POOL_WIDTH = BRANCH_WIDTH
POOL_GROUP = POOL_WIDTH // len(POOL_WINDOWS)
HGRN_HEADS = 4
HGRN_KDIM = 128
HGRN_VDIM = BRANCH_WIDTH // HGRN_HEADS
HGRN_KWIDTH = HGRN_HEADS * HGRN_KDIM
HGRN_VWIDTH = HGRN_HEADS * HGRN_VDIM
HGRN_CHUNK = 64
DIFF_HEADS = 4
DIFF_HEAD_DIM = 64
DIFF_VDIM = 2 * DIFF_HEAD_DIM
DIFF_QK_WIDTH = DIFF_HEADS * 2 * DIFF_HEAD_DIM
DIFF_V_WIDTH = DIFF_HEADS * DIFF_VDIM
ATTN_BLOCK = 128
REL_BUCKETS = 32
REL_MAX_DIST = 128
FFN_HIDDEN = ((8 * D_MODEL // 3 + 255) // 256) * 256
SPLIT_WIDTHS = (LRU_WIDTH, LRU_WIDTH, POOL_WIDTH, HGRN_KWIDTH, HGRN_KWIDTH, HGRN_VWIDTH, HGRN_VWIDTH, DIFF_QK_WIDTH, DIFF_QK_WIDTH, DIFF_V_WIDTH, N_BRANCH * D_MODEL)
N_IN = sum(SPLIT_WIDTHS)

kernel_name = 'hybrid_gated_four_mixer_block'


def rms_norm(x, w, eps=1e-6):
    xf = x.astype(jnp.float32)
    y = xf * lax.rsqrt(jnp.mean(xf * xf, axis=-1, keepdims=True) + eps)
    return (y * w.astype(jnp.float32)).astype(x.dtype)


def rg_lru_branch(u, gate, conv_w, conv_b, w_a, b_a, w_x, b_x, lam):
    bsz, t_len, width = u.shape
    xc = lax.conv_general_dilated(u, conv_w[:, None, :].astype(u.dtype), window_strides=(1,), padding=[(LRU_CONV - 1, 0)], dimension_numbers=('NWC', 'WIO', 'NWC'), feature_group_count=width) + conv_b
    xb = xc.reshape(bsz, t_len, LRU_HEADS, LRU_BLOCK)
    r = jax.nn.sigmoid(jnp.einsum('bthi,hij->bthj', xb, w_a).reshape(bsz, t_len, width) + b_a).astype(jnp.float32)
    i = jax.nn.sigmoid(jnp.einsum('bthi,hij->bthj', xb, w_x).reshape(bsz, t_len, width) + b_x).astype(jnp.float32)
    log_a = -LRU_C * r * jax.nn.softplus(-lam.astype(jnp.float32))
    a = jnp.exp(log_a)
    b = jnp.sqrt(-jnp.expm1(2.0 * log_a)) * (i * xc.astype(jnp.float32))

    def combine(left, right):
        a_l, b_l = left
        a_r, b_r = right
        return a_l * a_r, a_r * b_l + b_r

    _, h = lax.associative_scan(combine, (a, b), axis=1)
    return h.astype(u.dtype) * jax.nn.gelu(gate)


def multiscale_pool_branch(u, pool_w, pool_scale):
    bsz, t_len, width = u.shape
    uf = u.astype(jnp.float32)
    cs = jnp.pad(jnp.cumsum(uf, axis=1), ((0, 0), (1, 0), (0, 0)))
    t = jnp.arange(t_len)
    groups = []
    for gi, win in enumerate(POOL_WINDOWS):
        sl = slice(gi * POOL_GROUP, (gi + 1) * POOL_GROUP)
        csg = cs[:, :, sl]
        start = jnp.maximum(t + 1 - win, 0)
        win_sum = csg[:, 1:] - jnp.take(csg, start, axis=1)
        count = jnp.minimum(t + 1, win).astype(jnp.float32)
        groups.append(win_sum / count[None, :, None] - uf[:, :, sl])
    pooled = jnp.stack(groups, axis=2).astype(u.dtype)
    y = jnp.einsum('btgi,gij->btgj', pooled, pool_w).reshape(bsz, t_len, width)
    return y * pool_scale


def hgrn2_chunk(state, chunk):
    q, k, v, g = chunk
    length = q.shape[2]
    b = jnp.cumsum(g, axis=2)
    causal = jnp.tril(jnp.ones((length, length), dtype=bool))
    rel = b[:, :, :, None, :] - b[:, :, None, :, :]
    decay = jnp.exp(jnp.where(causal[None, None, :, :, None], rel, -jnp.inf))
    scores = jnp.einsum('bhtc,bhsc,bhtsc->bhts', q, k, decay)
    o = jnp.einsum('bhtc,bhcv->bhtv', q * jnp.exp(b), state) + jnp.einsum('bhts,bhsv->bhtv', scores, v)
    b_last = b[:, :, -1:, :]
    new_state = jnp.exp(b_last[:, :, 0, :])[..., None] * state + jnp.einsum('bhsc,bhsv->bhcv', k * jnp.exp(b_last - b), v)
    return new_state, o


def hgrn2_branch(q, f_logit, v, out_gate, lower_bound, norm_w):
    bsz, t_len, _ = q.shape

    def heads(a, d):
        return a.astype(jnp.float32).reshape(bsz, t_len, HGRN_HEADS, d).transpose(0, 2, 1, 3)

    lb = lower_bound.astype(jnp.float32)
    z = f_logit.astype(jnp.float32)
    log_f = jnp.logaddexp(jnp.log(lb), jnp.log1p(-lb) + jax.nn.log_sigmoid(z))
    k_in = (1.0 - lb) * jax.nn.sigmoid(-z)
    qh, kh, gh, vh = heads(q, HGRN_KDIM), heads(k_in, HGRN_KDIM), heads(log_f, HGRN_KDIM), heads(v, HGRN_VDIM)
    state0 = jnp.zeros((bsz, HGRN_HEADS, HGRN_KDIM, HGRN_VDIM), jnp.float32)
    state, o_meta = hgrn2_chunk(state0, (qh[:, :, :N_META], kh[:, :, :N_META], vh[:, :, :N_META], gh[:, :, :N_META]))
    n_chunks = (t_len - N_META) // HGRN_CHUNK

    def to_chunks(a):
        a = a[:, :, N_META:]
        return a.reshape(bsz, HGRN_HEADS, n_chunks, HGRN_CHUNK, a.shape[-1]).transpose(2, 0, 1, 3, 4)

    _, o_real = lax.scan(hgrn2_chunk, state, (to_chunks(qh), to_chunks(kh), to_chunks(vh), to_chunks(gh)))
    o_real = o_real.transpose(1, 2, 0, 3, 4).reshape(bsz, HGRN_HEADS, t_len - N_META, HGRN_VDIM)
    o = jnp.concatenate([o_meta, o_real], axis=2).transpose(0, 2, 1, 3)
    o = rms_norm(o, norm_w).reshape(bsz, t_len, HGRN_VWIDTH)
    return o.astype(q.dtype) * jax.nn.silu(out_gate)


def t5_bucket(q_pos, k_pos):
    n = jnp.maximum(q_pos[:, None] - k_pos[None, :], 0)
    max_exact = REL_BUCKETS // 2
    nf = jnp.maximum(n, 1).astype(jnp.float32)
    large = max_exact + (jnp.log(nf / max_exact) / math.log(REL_MAX_DIST / max_exact) * (REL_BUCKETS - max_exact)).astype(jnp.int32)
    large = jnp.minimum(large, REL_BUCKETS - 1)
    return jnp.where(n < max_exact, n, large)


def diff_attention(q, k, v, rel_bias, lam_vec, subln_w, lambda_init):
    bsz, t_len = q.shape[0], q.shape[1]
    lf = lam_vec.astype(jnp.float32)
    lam = jnp.exp(jnp.sum(lf[0] * lf[1])) - jnp.exp(jnp.sum(lf[2] * lf[3])) + lambda_init
    scale = DIFF_HEAD_DIM ** -0.5
    bounds = [(0, N_META)] + [(N_META + j * ATTN_BLOCK, N_META + (j + 1) * ATTN_BLOCK) for j in range((t_len - N_META) // ATTN_BLOCK)]
    outs = []
    for qs, qe in bounds:
        q_pos = jnp.arange(qs, qe)
        k_pos = jnp.arange(qe)
        bias = rel_bias[t5_bucket(q_pos, k_pos)].astype(jnp.float32).transpose(2, 0, 1)
        s = jnp.einsum('bqhmd,bkhmd->bhmqk', q[:, qs:qe], k[:, :qe]).astype(jnp.float32) * scale + bias[None, :, None]
        s = jnp.where(q_pos[:, None] >= k_pos[None, :], s, -jnp.inf)
        p = jax.nn.softmax(s, axis=-1)
        w = (p[:, :, 0] - lam * p[:, :, 1]).astype(v.dtype)
        outs.append(jnp.einsum('bhqk,bkhd->bqhd', w, v[:, :qe]))
    o = jnp.concatenate(outs, axis=1)
    o = rms_norm(o, subln_w) * (1.0 - lambda_init)
    return o.reshape(bsz, t_len, DIFF_V_WIDTH)


def token_mixer(hn, w_in, conv_w, conv_b, w_a, b_a, w_x, b_x, lam, pool_w, pool_scale, lower_bound, hgrn_norm_w, rel_bias, diff_lam, diff_subln_w, lambda_init, w_branch, w_out):
    bsz, t_len, _ = hn.shape
    z = hn @ w_in
    parts = jnp.split(z, np.cumsum(SPLIT_WIDTHS)[:-1].tolist(), axis=-1)
    lru_x, lru_gate, pool_u, hq, hf, hi, hog, dq, dk, dv, gates = parts
    y_a = rg_lru_branch(lru_x, lru_gate, conv_w, conv_b, w_a, b_a, w_x, b_x, lam)
    y_b = multiscale_pool_branch(pool_u, pool_w, pool_scale)
    y_c = hgrn2_branch(hq, hf, hi, hog, lower_bound, hgrn_norm_w)
    y_d = diff_attention(dq.reshape(bsz, t_len, DIFF_HEADS, 2, DIFF_HEAD_DIM), dk.reshape(bsz, t_len, DIFF_HEADS, 2, DIFF_HEAD_DIM), dv.reshape(bsz, t_len, DIFF_HEADS, DIFF_VDIM), rel_bias, diff_lam, diff_subln_w, lambda_init)
    ys = jnp.stack([y_a, y_b, y_c, y_d], axis=2)
    proj = jnp.einsum('btkc,kcd->btkd', ys, w_branch)
    g = jax.nn.sigmoid(gates.reshape(bsz, t_len, N_BRANCH, D_MODEL))
    merged = jnp.sum(g * proj, axis=2)
    return merged @ w_out


def swiglu(h, w_gu, w_down):
    g, u = jnp.split(h @ w_gu, 2, axis=-1)
    return (jax.nn.silu(g) * u) @ w_down


def setup_inputs(seed: int = 0) -> dict:
    key = jax.random.key(seed)
    ks = jax.random.split(key, 32)
    f32 = jnp.float32

    def nrm(k, shape, scale):
        return jax.random.normal(k, shape, f32) * scale

    u = jax.random.uniform(ks[9], (DEPTH, LRU_WIDTH), f32, minval=0.9, maxval=0.999)
    a_base = u ** (1.0 / LRU_C)
    return {
        'x': nrm(ks[0], (BATCH, SEQ, D_MODEL), 1.0),
        'meta_tokens': nrm(ks[1], (N_META, D_MODEL), 1.0),
        'rel_bias': nrm(ks[2], (REL_BUCKETS, DIFF_HEADS), 0.5),
        'hgrn_lower_bounds': nrm(ks[3], (DEPTH, HGRN_KWIDTH), 0.1),
        'norm_mix_pre': 1.0 + nrm(ks[4], (DEPTH, D_MODEL), 0.02),
        'norm_mix_post': 1.0 + nrm(ks[5], (DEPTH, D_MODEL), 0.02),
        'norm_ffn_pre': 1.0 + nrm(ks[6], (DEPTH, D_MODEL), 0.02),
        'norm_ffn_post': 1.0 + nrm(ks[7], (DEPTH, D_MODEL), 0.02),
        'w_in': nrm(ks[8], (DEPTH, D_MODEL, N_IN), D_MODEL ** -0.5),
        'lru_conv_w': nrm(ks[10], (DEPTH, LRU_CONV, LRU_WIDTH), LRU_CONV ** -0.5),
        'lru_conv_b': nrm(ks[11], (DEPTH, LRU_WIDTH), 0.01),
        'lru_w_a': nrm(ks[12], (DEPTH, LRU_HEADS, LRU_BLOCK, LRU_BLOCK), LRU_BLOCK ** -0.5),
        'lru_b_a': nrm(ks[13], (DEPTH, LRU_WIDTH), 0.01),
        'lru_w_x': nrm(ks[14], (DEPTH, LRU_HEADS, LRU_BLOCK, LRU_BLOCK), LRU_BLOCK ** -0.5),
        'lru_b_x': nrm(ks[15], (DEPTH, LRU_WIDTH), 0.01),
        'lru_lambda': jnp.log(a_base) - jnp.log1p(-a_base),
        'pool_w': nrm(ks[16], (DEPTH, len(POOL_WINDOWS), POOL_GROUP, POOL_GROUP), POOL_GROUP ** -0.5),
        'pool_scale': 1.0 + nrm(ks[17], (DEPTH, POOL_WIDTH), 0.02),
        'hgrn_norm': 1.0 + nrm(ks[18], (DEPTH, HGRN_VDIM), 0.02),
        'diff_lambda': nrm(ks[19], (DEPTH, 4, DIFF_HEAD_DIM), 0.1),
        'diff_subln': 1.0 + nrm(ks[20], (DEPTH, DIFF_VDIM), 0.02),
        'w_branch': nrm(ks[21], (DEPTH, N_BRANCH, BRANCH_WIDTH, D_MODEL), BRANCH_WIDTH ** -0.5),
        'w_out': nrm(ks[22], (DEPTH, D_MODEL, D_MODEL), D_MODEL ** -0.5),
        'ffn_w_gu': nrm(ks[23], (DEPTH, D_MODEL, 2 * FFN_HIDDEN), D_MODEL ** -0.5),
        'ffn_w_down': nrm(ks[24], (DEPTH, FFN_HIDDEN, D_MODEL), FFN_HIDDEN ** -0.5),
    }


def reference(x, meta_tokens, rel_bias, hgrn_lower_bounds, norm_mix_pre, norm_mix_post, norm_ffn_pre, norm_ffn_post, w_in, lru_conv_w, lru_conv_b, lru_w_a, lru_b_a, lru_w_x, lru_b_x, lru_lambda, pool_w, pool_scale, hgrn_norm, diff_lambda, diff_subln, w_branch, w_out, ffn_w_gu, ffn_w_down):
    bsz = x.shape[0]
    meta = jnp.broadcast_to(meta_tokens[None].astype(x.dtype), (bsz, N_META, D_MODEL))
    h = jnp.concatenate([meta, x], axis=1)
    lb_cum = jnp.cumsum(jax.nn.softmax(hgrn_lower_bounds.astype(jnp.float32), axis=0), axis=0)
    lower_bounds = lb_cum - lb_cum[0:1]
    for layer in range(DEPTH):
        lambda_init = 0.8 - 0.6 * math.exp(-0.3 * layer)
        mix = token_mixer(rms_norm(h, norm_mix_pre[layer]), w_in[layer], lru_conv_w[layer], lru_conv_b[layer], lru_w_a[layer], lru_b_a[layer], lru_w_x[layer], lru_b_x[layer], lru_lambda[layer], pool_w[layer], pool_scale[layer], lower_bounds[layer], hgrn_norm[layer], rel_bias, diff_lambda[layer], diff_subln[layer], lambda_init, w_branch[layer], w_out[layer])
        h = h + rms_norm(mix, norm_mix_post[layer])
        ffn = swiglu(rms_norm(h, norm_ffn_pre[layer]), ffn_w_gu[layer], ffn_w_down[layer])
        h = h + rms_norm(ffn, norm_ffn_post[layer])
    return h[:, N_META:]
```

```python
import functools
import math

import numpy as np
import jax
import jax.numpy as jnp
from jax import lax
from jax.experimental import pallas as pl
from jax.experimental.pallas import tpu as pltpu

F32 = jnp.float32
BF16 = jnp.bfloat16

D_MODEL = 2048
SEQ = 2048
DEPTH = 2
N_META = 16
BLK = 128
PAD_ROWS = BLK - N_META
WIDTH = 512
HEADS = 4
HEAD_W = 128
CHUNK = 64
SUB = 16
LRU_C = 8.0
POOL_WINDOWS = (2, 4, 8, 16)
DIFF_HEAD_DIM = 64
REL_BUCKETS = 32
REL_MAX_DIST = 128
FFN_HIDDEN = 5632
MIX_COLS = 10 * WIDTH
NEG = -1e30
EPS = 1e-6
VMEM_LIMIT = 56 * 1024 * 1024


def _cparams(sem):
    return pltpu.CompilerParams(dimension_semantics=sem, vmem_limit_bytes=VMEM_LIMIT)


def _rms(x, w):
    return x * lax.rsqrt(jnp.mean(x * x, axis=-1, keepdims=True) + EPS) * w


def _log_sigmoid(z):
    return -(jnp.maximum(-z, 0.0) + jnp.log1p(jnp.exp(-jnp.abs(z))))


def _sigmoid(z):
    return 1.0 / (1.0 + jnp.exp(-z))


def _gelu_tanh(x):
    c = math.sqrt(2.0 / math.pi)
    return 0.5 * x * (1.0 + jnp.tanh(c * (x + 0.044715 * (x * x * x))))


def _shift_rows(x, s, fill, row):
    return jnp.where(row >= s, pltpu.roll(x, s, axis=0), fill)


def _bucket_tiles():
    r = np.arange(BLK)[:, None]
    c = np.arange(BLK)[None, :]
    max_exact = REL_BUCKETS // 2

    def bucket(n):
        nf = np.maximum(n, 1).astype(np.float32)
        large = max_exact + (np.log(nf / np.float32(max_exact)) / np.float32(math.log(REL_MAX_DIST / max_exact))
                             * np.float32(REL_BUCKETS - max_exact)).astype(np.int32)
        large = np.minimum(large, REL_BUCKETS - 1)
        return np.where(n < max_exact, n, large).astype(np.int32)

    d0 = r - c
    t0 = np.where(d0 >= 0, bucket(np.maximum(d0, 0)), -1)
    t1 = bucket(BLK + r - c)
    return np.stack([t0, t1]).astype(np.int32)


def _prologue_kernel(lbraw_ref, relb_ref, idx_ref, lb_ref, bias_ref):
    raw = lbraw_ref[...]
    mx = jnp.max(raw, axis=0, keepdims=True)
    e = jnp.exp(raw - mx)
    sm = e / jnp.sum(e, axis=0, keepdims=True)
    cum = sm[0:1]
    lb_ref[0:1, :] = cum - sm[0:1]
    for l in range(1, DEPTH):
        cum = cum + sm[l:l + 1]
        lb_ref[l:l + 1, :] = cum - sm[0:1]
    for t in range(2):
        idx = idx_ref[t]
        for hd in range(HEADS):
            acc = jnp.zeros((BLK, BLK), F32)
            for bk in range(REL_BUCKETS):
                acc = jnp.where(idx == bk, relb_ref[bk, hd], acc)
            bias_ref[hd, t] = jnp.where(idx < 0, NEG, acc)


def _prologue(hgrn_lower_bounds, rel_bias):
    idx = jnp.asarray(_bucket_tiles())
    return pl.pallas_call(
        _prologue_kernel,
        out_shape=(jax.ShapeDtypeStruct((DEPTH, WIDTH), F32),
                   jax.ShapeDtypeStruct((HEADS, 2, BLK, BLK), F32)),
        in_specs=[pl.BlockSpec(memory_space=pltpu.VMEM),
                  pl.BlockSpec(memory_space=pltpu.SMEM),
                  pl.BlockSpec(memory_space=pltpu.VMEM)],
        out_specs=(pl.BlockSpec(memory_space=pltpu.VMEM), pl.BlockSpec(memory_space=pltpu.VMEM)),
        name="prologue",
    )(hgrn_lower_bounds, rel_bias, idx)


def _embed_kernel(x_ref, meta_ref, w_ref, h_ref, hn_ref, *, nreal):
    i = pl.program_id(0)

    @pl.when(i < nreal)
    def _():
        h_ref[...] = x_ref[...]

    @pl.when(i == nreal)
    def _():
        h_ref[0:PAD_ROWS, :] = jnp.zeros((PAD_ROWS, D_MODEL), F32)
        h_ref[PAD_ROWS:BLK, :] = meta_ref[...]

    hn_ref[...] = _rms(h_ref[...], w_ref[...]).astype(BF16)


def _embed(x2d, meta, w_pre, nreal):
    rows = (nreal + 1) * BLK
    return pl.pallas_call(
        functools.partial(_embed_kernel, nreal=nreal),
        out_shape=(jax.ShapeDtypeStruct((rows, D_MODEL), F32),
                   jax.ShapeDtypeStruct((rows, D_MODEL), BF16)),
        grid=(nreal + 1,),
        in_specs=[pl.BlockSpec((BLK, D_MODEL), lambda i: (jnp.minimum(i, nreal - 1), 0)),
                  pl.BlockSpec((N_META, D_MODEL), lambda i: (0, 0)),
                  pl.BlockSpec((1, D_MODEL), lambda i: (0, 0))],
        out_specs=(pl.BlockSpec((BLK, D_MODEL), lambda i: (i, 0)),
                   pl.BlockSpec((BLK, D_MODEL), lambda i: (i, 0))),
        compiler_params=_cparams(("arbitrary",)),
        name="embed",
    )(x2d, meta, w_pre)


def _matmul_kernel(x_ref, w_ref, o_ref):
    o_ref[...] = jnp.dot(x_ref[...], w_ref[...], preferred_element_type=F32).astype(o_ref.dtype)


def _mixer_in_proj(hn, w_in, tm, tn):
    rows, k = hn.shape
    return pl.pallas_call(
        _matmul_kernel,
        out_shape=jax.ShapeDtypeStruct((rows, MIX_COLS), F32),
        grid=(MIX_COLS // tn, rows // tm),
        in_specs=[pl.BlockSpec((tm, k), lambda n, m: (m, 0)),
                  pl.BlockSpec((k, tn), lambda n, m: (0, n))],
        out_specs=pl.BlockSpec((tm, tn), lambda n, m: (m, n)),
        compiler_params=_cparams(("arbitrary", "arbitrary")),
        name="mixer_in_proj",
    )(hn, w_in)


def _gate_merge_kernel(hn_ref, g0, g1, g2, g3, y0, y1, y2, y3, wb_ref, o_ref):
    hn = hn_ref[...]
    acc = None
    for k, (g_ref, y_ref) in enumerate(((g0, y0), (g1, y1), (g2, y2), (g3, y3))):
        gate = _sigmoid(jnp.dot(hn, g_ref[...], preferred_element_type=F32))
        proj = jnp.dot(y_ref[...], wb_ref[k], preferred_element_type=F32)
        acc = gate * proj if acc is None else acc + gate * proj
    o_ref[...] = acc.astype(o_ref.dtype)


def _gate_merge(hn, w_in, ys, w_branch, tm, tn):
    rows, k = hn.shape
    gate_specs = [
        pl.BlockSpec((k, tn), functools.partial(
            lambda n, m, base: (0, base + n), base=(MIX_COLS + br * D_MODEL) // tn))
        for br in range(4)]
    y_specs = [pl.BlockSpec((tm, WIDTH), lambda n, m: (m, 0)) for _ in range(4)]
    return pl.pallas_call(
        _gate_merge_kernel,
        out_shape=jax.ShapeDtypeStruct((rows, D_MODEL), BF16),
        grid=(D_MODEL // tn, rows // tm),
        in_specs=[pl.BlockSpec((tm, k), lambda n, m: (m, 0))] + gate_specs + y_specs
                 + [pl.BlockSpec((4, WIDTH, tn), lambda n, m: (0, 0, n))],
        out_specs=pl.BlockSpec((tm, tn), lambda n, m: (m, n)),
        compiler_params=_cparams(("arbitrary", "arbitrary")),
        name="gate_merge",
    )(hn, w_in, w_in, w_in, w_in, *ys, w_branch)


def _residual_epilogue(acc, h_ref, wpost_ref, wnext_ref, hnew_ref, hn_ref):
    h_new = h_ref[...] + _rms(acc, wpost_ref[...])
    hnew_ref[...] = h_new
    if hn_ref is not None:
        hn_ref[...] = _rms(h_new, wnext_ref[...]).astype(BF16)


def _out_proj_kernel(x_ref, w_ref, h_ref, wpost_ref, wnext_ref, hnew_ref, hn_ref):
    acc = jnp.dot(x_ref[...], w_ref[...], preferred_element_type=F32)
    _residual_epilogue(acc, h_ref, wpost_ref, wnext_ref, hnew_ref, hn_ref)


def _out_proj(merged, w_out, h, w_post, w_next, tm):
    rows = h.shape[0]
    return pl.pallas_call(
        _out_proj_kernel,
        out_shape=(jax.ShapeDtypeStruct((rows, D_MODEL), F32),
                   jax.ShapeDtypeStruct((rows, D_MODEL), BF16)),
        grid=(rows // tm,),
        in_specs=[pl.BlockSpec((tm, D_MODEL), lambda m: (m, 0)),
                  pl.BlockSpec((D_MODEL, D_MODEL), lambda m: (0, 0)),
                  pl.BlockSpec((tm, D_MODEL), lambda m: (m, 0)),
                  pl.BlockSpec((1, D_MODEL), lambda m: (0, 0)),
                  pl.BlockSpec((1, D_MODEL), lambda m: (0, 0))],
        out_specs=(pl.BlockSpec((tm, D_MODEL), lambda m: (m, 0)),
                   pl.BlockSpec((tm, D_MODEL), lambda m: (m, 0))),
        compiler_params=_cparams(("arbitrary",)),
        name="out_proj",
    )(merged, w_out, h, w_post, w_next)


def _swiglu_up_kernel(x_ref, wg_ref, wu_ref, o_ref):
    x = x_ref[...]
    g = jnp.dot(x, wg_ref[...], preferred_element_type=F32)
    u = jnp.dot(x, wu_ref[...], preferred_element_type=F32)
    o_ref[...] = (g * _sigmoid(g) * u).astype(o_ref.dtype)


def _swiglu_up(hn, w_gu, tm, tn):
    rows, k = hn.shape
    nt = FFN_HIDDEN // tn
    return pl.pallas_call(
        _swiglu_up_kernel,
        out_shape=jax.ShapeDtypeStruct((rows, FFN_HIDDEN), BF16),
        grid=(nt, rows // tm),
        in_specs=[pl.BlockSpec((tm, k), lambda n, m: (m, 0)),
                  pl.BlockSpec((k, tn), lambda n, m: (0, n)),
                  pl.BlockSpec((k, tn), lambda n, m: (0, nt + n))],
        out_specs=pl.BlockSpec((tm, tn), lambda n, m: (m, n)),
        compiler_params=_cparams(("arbitrary", "arbitrary")),
        name="swiglu_up",
    )(hn, w_gu, w_gu)


def _down_proj_kernel(a_ref, w_ref, h_ref, wpost_ref, wnext_ref, *rest, emit_hn):
    if emit_hn:
        hnew_ref, hn_ref, acc_ref = rest
    else:
        hnew_ref, acc_ref = rest
        hn_ref = None
    kk = pl.program_id(1)

    @pl.when(kk == 0)
    def _():
        acc_ref[...] = jnp.zeros_like(acc_ref)

    acc_ref[...] += jnp.dot(a_ref[...], w_ref[...], preferred_element_type=F32)

    @pl.when(kk == pl.num_programs(1) - 1)
    def _():
        _residual_epilogue(acc_ref[...], h_ref, wpost_ref, wnext_ref, hnew_ref, hn_ref)


def _down_proj(a, w_down, h, w_post, w_next, rows_out, tm, tk, emit_hn):
    out_shape = [jax.ShapeDtypeStruct((rows_out, D_MODEL), F32)]
    out_specs = [pl.BlockSpec((tm, D_MODEL), lambda m, k: (m, 0))]
    if emit_hn:
        out_shape.append(jax.ShapeDtypeStruct((rows_out, D_MODEL), BF16))
        out_specs.append(pl.BlockSpec((tm, D_MODEL), lambda m, k: (m, 0)))
    res = pl.pallas_call(
        functools.partial(_down_proj_kernel, emit_hn=emit_hn),
        out_shape=tuple(out_shape),
        grid=(rows_out // tm, FFN_HIDDEN // tk),
        in_specs=[pl.BlockSpec((tm, tk), lambda m, k: (m, k)),
                  pl.BlockSpec((tk, D_MODEL), lambda m, k: (k, 0)),
                  pl.BlockSpec((tm, D_MODEL), lambda m, k: (m, 0)),
                  pl.BlockSpec((1, D_MODEL), lambda m, k: (0, 0)),
                  pl.BlockSpec((1, D_MODEL), lambda m, k: (0, 0))],
        out_specs=tuple(out_specs),
        scratch_shapes=[pltpu.VMEM((tm, D_MODEL), F32)],
        compiler_params=_cparams(("arbitrary", "arbitrary")),
        name="down_proj",
    )(a, w_down, h, w_post, w_next)
    return res if emit_hn else (res[0], None)


def _row_block(i, nreal):
    return (i + nreal) % (nreal + 1)


def _zspec(col_block, nreal):
    return pl.BlockSpec((BLK, WIDTH), lambda i: (_row_block(i, nreal), col_block))


def _pspec(shape):
    nd = len(shape)
    return pl.BlockSpec(shape, lambda *_: (0,) * nd)


def _lru_kernel(u_ref, gate_ref, cw_ref, cb_ref, wa_ref, ba_ref, wx_ref, bx_ref, lam_ref, y_ref,
                ubuf, hst, hist_meta, h_meta, *, nblk):
    i = pl.program_id(0)
    is_meta = i == 0

    @pl.when(is_meta)
    def _():
        ubuf[0:8, :] = jnp.zeros((8, WIDTH), F32)
        hst[...] = jnp.zeros_like(hst)

    @pl.when(jnp.logical_and(i >= 1, (i - 1) % nblk == 0))
    def _():
        ubuf[0:8, :] = hist_meta[...]
        hst[...] = h_meta[...]

    u = u_ref[...]
    ubuf[8:8 + BLK, :] = u
    cw = cw_ref[...]
    xc = (cb_ref[...] + cw[3:4] * u + cw[2:3] * ubuf[7:7 + BLK, :]
          + cw[1:2] * ubuf[6:6 + BLK, :] + cw[0:1] * ubuf[5:5 + BLK, :])
    xb = xc.astype(BF16)
    ra, ia = [], []
    for hd in range(HEADS):
        sl = slice(hd * HEAD_W, (hd + 1) * HEAD_W)
        ra.append(jnp.dot(xb[:, sl], wa_ref[hd], preferred_element_type=F32))
        ia.append(jnp.dot(xb[:, sl], wx_ref[hd], preferred_element_type=F32))
    r = _sigmoid(jnp.concatenate(ra, axis=1) + ba_ref[...])
    ig = _sigmoid(jnp.concatenate(ia, axis=1) + bx_ref[...])
    lam = lam_ref[...]
    softplus_neg_lam = jnp.maximum(-lam, 0.0) + jnp.log1p(jnp.exp(-jnp.abs(lam)))
    log_a = -LRU_C * r * softplus_neg_lam
    a = jnp.exp(log_a)
    bb = jnp.sqrt(-jnp.tanh(log_a) * (a * a + 1.0)) * (ig * xc)
    row = lax.broadcasted_iota(jnp.int32, (BLK, 1), 0)
    bb = jnp.where(row >= PAD_ROWS * is_meta.astype(jnp.int32), bb, 0.0)

    acum, bcum = a, bb
    s = 1
    while s < BLK:
        a_sh = _shift_rows(acum, s, 1.0, row)
        b_sh = _shift_rows(bcum, s, 0.0, row)
        bcum = acum * b_sh + bcum
        acum = acum * a_sh
        s *= 2
    h = acum * hst[0:1, :] + bcum
    y_ref[...] = (h * _gelu_tanh(gate_ref[...])).astype(y_ref.dtype)

    hist = u[BLK - 8:BLK, :]
    hlast = jnp.broadcast_to(h[BLK - 1:BLK, :], (8, WIDTH))
    ubuf[0:8, :] = hist
    hst[...] = hlast

    @pl.when(is_meta)
    def _():
        hist_meta[...] = hist
        h_meta[...] = hlast


def _lru(z, cw, cb, wa, ba, wx, bx, lam, nreal, nblk):
    rows = z.shape[0]
    return pl.pallas_call(
        functools.partial(_lru_kernel, nblk=nblk),
        out_shape=jax.ShapeDtypeStruct((rows, WIDTH), BF16),
        grid=(nreal + 1,),
        in_specs=[_zspec(0, nreal), _zspec(1, nreal),
                  _pspec((4, WIDTH)), _pspec((1, WIDTH)),
                  _pspec((HEADS, HEAD_W, HEAD_W)), _pspec((1, WIDTH)),
                  _pspec((HEADS, HEAD_W, HEAD_W)), _pspec((1, WIDTH)), _pspec((1, WIDTH))],
        out_specs=pl.BlockSpec((BLK, WIDTH), lambda i: (_row_block(i, nreal), 0)),
        scratch_shapes=[pltpu.VMEM((8 + BLK, WIDTH), F32), pltpu.VMEM((8, WIDTH), F32),
                        pltpu.VMEM((8, WIDTH), F32), pltpu.VMEM((8, WIDTH), F32)],
        compiler_params=_cparams(("arbitrary",)),
        name="rg_lru",
    )(z, z, cw, cb, wa, ba, wx, bx, lam)


POOL_HIST = 16


def _pool_kernel(u_ref, pw_ref, ps_ref, y_ref, ubuf, hist_meta, *, nblk):
    i = pl.program_id(0)
    is_meta = i == 0

    @pl.when(is_meta)
    def _():
        ubuf[0:POOL_HIST, :] = jnp.zeros((POOL_HIST, WIDTH), F32)

    @pl.when(jnp.logical_and(i >= 1, (i - 1) % nblk == 0))
    def _():
        ubuf[0:POOL_HIST, :] = hist_meta[...]

    u = u_ref[...]
    ubuf[POOL_HIST:POOL_HIST + BLK, :] = u
    row = lax.broadcasted_iota(jnp.int32, (BLK, 1), 0)
    meta_i = is_meta.astype(jnp.int32)
    pos1 = row + 1 - PAD_ROWS * meta_i + 2 * POOL_HIST * (1 - meta_i)
    outs = []
    for g, win in enumerate(POOL_WINDOWS):
        sl = slice(g * HEAD_W, (g + 1) * HEAD_W)
        acc = u[:, sl]
        for d in range(1, win):
            acc = acc + ubuf[POOL_HIST - d:POOL_HIST - d + BLK, sl]
        count = jnp.clip(pos1, 1, win).astype(F32)
        pooled = acc / count - u[:, sl]
        outs.append(jnp.dot(pooled.astype(BF16), pw_ref[g], preferred_element_type=F32))
    y_ref[...] = (jnp.concatenate(outs, axis=1) * ps_ref[...]).astype(y_ref.dtype)

    hist = u[BLK - POOL_HIST:BLK, :]
    ubuf[0:POOL_HIST, :] = hist

    @pl.when(is_meta)
    def _():
        hist_meta[...] = hist


def _pool(z, pw, ps, nreal, nblk):
    rows = z.shape[0]
    return pl.pallas_call(
        functools.partial(_pool_kernel, nblk=nblk),
        out_shape=jax.ShapeDtypeStruct((rows, WIDTH), BF16),
        grid=(nreal + 1,),
        in_specs=[_zspec(2, nreal), _pspec((4, HEAD_W, HEAD_W)), _pspec((1, WIDTH))],
        out_specs=pl.BlockSpec((BLK, WIDTH), lambda i: (_row_block(i, nreal), 0)),
        scratch_shapes=[pltpu.VMEM((POOL_HIST + BLK, WIDTH), F32), pltpu.VMEM((POOL_HIST, WIDTH), F32)],
        compiler_params=_cparams(("arbitrary",)),
        name="ms_pool",
    )(z, pw, ps)


def _hgrn_chunk(q, z, v, lbh, state_t, valid, ones_bf):
    ls = _log_sigmoid(z)
    x1 = jnp.log(lbh)
    x2 = jnp.log1p(-lbh) + ls
    mx = jnp.maximum(x1, x2)
    g = mx + jnp.log1p(jnp.exp(-jnp.abs(x1 - x2)))
    k = (1.0 - lbh) * _sigmoid(-z)
    if valid is not None:
        g = jnp.where(valid, g, 0.0)
    row = lax.broadcasted_iota(jnp.int32, (CHUNK, 1), 0)
    b = g
    s = 1
    while s < CHUNK:
        b = b + _shift_rows(b, s, 0.0, row)
        s *= 2
    b_last = b[CHUNK - 1:CHUNK, :]

    qe = (q * jnp.exp(b)).astype(BF16)
    o = lax.dot_general(qe, state_t.astype(BF16), (((1,), (1,)), ((), ())), preferred_element_type=F32)

    col = lax.broadcasted_iota(jnp.int32, (SUB, CHUNK), 1)
    rsub = lax.broadcasted_iota(jnp.int32, (SUB, CHUNK), 0)
    lane = lax.broadcasted_iota(jnp.int32, (SUB, HEAD_W), 1)
    s_rows = []
    for blk in range(CHUNK // SUB):
        lo = blk * SUB
        bi = b[lo:lo + SUB, :]
        qi = q[lo:lo + SUB, :]
        ki = k[lo:lo + SUB, :]
        parts = []
        for sr in range(SUB):
            e = jnp.exp(jnp.minimum(bi - bi[sr:sr + 1, :], 0.0))
            parts.append(qi * e * ki[sr:sr + 1, :])
        m3 = jnp.concatenate(parts, axis=0).astype(BF16)
        red = jnp.dot(m3, ones_bf, preferred_element_type=F32)
        diag = jnp.zeros((SUB, HEAD_W), F32)
        for sr in range(SUB):
            diag = diag + jnp.where(lane == lo + sr, red[sr * SUB:(sr + 1) * SUB, :], 0.0)
        diag = diag[:, :CHUNK]
        s_blk = jnp.where(jnp.logical_and(col >= lo, col - lo <= rsub), diag, 0.0)
        if blk > 0:
            b0 = b[lo - 1:lo, :]
            kt = (k * jnp.exp(jnp.minimum(b0 - b, 0.0))).astype(BF16)
            qd = (qi * jnp.exp(bi - b0)).astype(BF16)
            off = lax.dot_general(qd, kt, (((1,), (1,)), ((), ())), preferred_element_type=F32)
            s_blk = jnp.where(col < lo, off, s_blk)
        s_rows.append(s_blk)
    scores = jnp.concatenate(s_rows, axis=0).astype(BF16)
    vb = v.astype(BF16)
    o = o + jnp.dot(scores, vb, preferred_element_type=F32)

    kd = (k * jnp.exp(b_last - b)).astype(BF16)
    upd = lax.dot_general(vb, kd, (((0,), (0,)), ((), ())), preferred_element_type=F32)
    new_state_t = state_t * jnp.exp(b_last) + upd
    return o, new_state_t


def _hgrn_kernel(q_ref, f_ref, v_ref, og_ref, lb_ref, nw_ref, y_ref, state, state_meta, *, nblk):
    i = pl.program_id(0)
    is_meta = i == 0

    @pl.when(is_meta)
    def _():
        state[...] = jnp.zeros_like(state)

    @pl.when(jnp.logical_and(i >= 1, (i - 1) % nblk == 0))
    def _():
        state[...] = state_meta[...]

    ones_bf = jnp.ones((HEAD_W, HEAD_W), BF16)
    nw = nw_ref[...]
    for hd in range(HEADS):
        sl = slice(hd * HEAD_W, (hd + 1) * HEAD_W)
        lbh = lb_ref[:, sl]
        st = state[hd]
        for c in range(BLK // CHUNK):
            rs = slice(c * CHUNK, (c + 1) * CHUNK)
            rowg = lax.broadcasted_iota(jnp.int32, (CHUNK, 1), 0) + c * CHUNK
            valid = rowg >= PAD_ROWS * is_meta.astype(jnp.int32)
            o, st = _hgrn_chunk(q_ref[rs, sl], f_ref[rs, sl], v_ref[rs, sl], lbh, st, valid, ones_bf)
            og = og_ref[rs, sl]
            y_ref[rs, sl] = (_rms(o, nw) * (og * _sigmoid(og))).astype(y_ref.dtype)
        state[hd] = st

    @pl.when(is_meta)
    def _():
        state_meta[...] = state[...]


def _hgrn(z, lb, nw, nreal, nblk):
    rows = z.shape[0]
    return pl.pallas_call(
        functools.partial(_hgrn_kernel, nblk=nblk),
        out_shape=jax.ShapeDtypeStruct((rows, WIDTH), BF16),
        grid=(nreal + 1,),
        in_specs=[_zspec(3, nreal), _zspec(4, nreal), _zspec(5, nreal), _zspec(6, nreal),
                  _pspec((1, WIDTH)), _pspec((1, HEAD_W))],
        out_specs=pl.BlockSpec((BLK, WIDTH), lambda i: (_row_block(i, nreal), 0)),
        scratch_shapes=[pltpu.VMEM((HEADS, HEAD_W, HEAD_W), F32), pltpu.VMEM((HEADS, HEAD_W, HEAD_W), F32)],
        compiler_params=_cparams(("arbitrary",)),
        name="hgrn2",
    )(z, z, z, z, lb, nw)


def _attn_kernel(relb_ref, q_ref, k_ref, v_ref, km_ref, vm_ref, bias_ref, lamv_ref, sw_ref, y_ref,
                 m_sc, l_sc, acc_sc, *, nblk, lambda_init):
    hd = pl.program_id(0)
    i = pl.program_id(1)
    is_meta = i == 0
    j = jnp.where(is_meta, 0, (i - 1) % nblk)
    far = relb_ref[REL_BUCKETS - 1, hd]

    qf = (q_ref[...] * (DIFF_HEAD_DIM ** -0.5)).astype(BF16)
    qs = (qf[:, :DIFF_HEAD_DIM], qf[:, DIFF_HEAD_DIM:])
    m_sc[...] = jnp.full_like(m_sc, NEG)
    l_sc[...] = jnp.zeros_like(l_sc)
    acc_sc[...] = jnp.zeros_like(acc_sc)

    def tile(kt, vt, bias):
        kb = kt.astype(BF16)
        vb = vt.astype(BF16)
        for mp in range(2):
            s = lax.dot_general(qs[mp], kb[:, mp * DIFF_HEAD_DIM:(mp + 1) * DIFF_HEAD_DIM],
                                (((1,), (1,)), ((), ())), preferred_element_type=F32) + bias
            m_old = m_sc[mp]
            m_new = jnp.maximum(m_old, jnp.max(s, axis=-1, keepdims=True))
            alpha = jnp.exp(m_old - m_new)
            p = jnp.exp(s - m_new)
            l_sc[mp] = alpha * l_sc[mp] + jnp.sum(p, axis=-1, keepdims=True)
            acc_sc[mp] = alpha * acc_sc[mp] + jnp.dot(p.astype(BF16), vb, preferred_element_type=F32)
            m_sc[mp] = m_new

    colmask = jnp.where(lax.broadcasted_iota(jnp.int32, (1, BLK), 1) >= PAD_ROWS, 0.0, NEG)
    t0 = bias_ref[0]
    t1 = bias_ref[1]

    @pl.when(is_meta)
    def _():
        tile(km_ref[...], vm_ref[...], t0 + colmask)

    @pl.when(jnp.logical_not(is_meta))
    def _():
        first = jnp.full((1, BLK), j, jnp.int32) == 0
        tile(km_ref[...], vm_ref[...], jnp.where(first, t1, far) + colmask)

        def far_tile(kt, carry):
            r0 = pl.multiple_of(kt * BLK, BLK)
            tile(k_ref[pl.ds(r0, BLK), :], v_ref[pl.ds(r0, BLK), :], far)
            return carry

        lax.fori_loop(0, jnp.maximum(j - 1, 0), far_tile, 0)

        @pl.when(j >= 1)
        def _():
            r0 = pl.multiple_of((j - 1) * BLK, BLK)
            tile(k_ref[pl.ds(r0, BLK), :], v_ref[pl.ds(r0, BLK), :], t1)

        r0 = pl.multiple_of(j * BLK, BLK)
        tile(k_ref[pl.ds(r0, BLK), :], v_ref[pl.ds(r0, BLK), :], t0)

    lv = lamv_ref[...]
    lam = (jnp.exp(jnp.sum(lv[0:1] * lv[1:2], axis=-1, keepdims=True))
           - jnp.exp(jnp.sum(lv[2:3] * lv[3:4], axis=-1, keepdims=True)) + lambda_init)
    o = acc_sc[0] / l_sc[0] - lam * (acc_sc[1] / l_sc[1])
    o = _rms(o, sw_ref[...]) * (1.0 - lambda_init)
    row = lax.broadcasted_iota(jnp.int32, (BLK, 1), 0)
    o = jnp.where(row >= PAD_ROWS * is_meta.astype(jnp.int32), o, 0.0)
    y_ref[...] = o.astype(y_ref.dtype)


def _attn(z, rel_bias, bias_tiles, lam_vec, subln_w, nreal, nblk, lambda_init):
    rows = z.shape[0]
    qcol, kcol, vcol = 7 * HEADS, 8 * HEADS, 9 * HEADS

    def batch_of(i):
        return jnp.where(i == 0, 0, (i - 1) // nblk)

    return pl.pallas_call(
        functools.partial(_attn_kernel, nblk=nblk, lambda_init=lambda_init),
        out_shape=jax.ShapeDtypeStruct((rows, WIDTH), BF16),
        grid=(HEADS, nreal + 1),
        in_specs=[pl.BlockSpec(memory_space=pltpu.SMEM),
                  pl.BlockSpec((BLK, HEAD_W), lambda h, i: (_row_block(i, nreal), qcol + h)),
                  pl.BlockSpec((nblk * BLK, HEAD_W), lambda h, i: (batch_of(i), kcol + h)),
                  pl.BlockSpec((nblk * BLK, HEAD_W), lambda h, i: (batch_of(i), vcol + h)),
                  pl.BlockSpec((BLK, HEAD_W), lambda h, i: (nreal, kcol + h)),
                  pl.BlockSpec((BLK, HEAD_W), lambda h, i: (nreal, vcol + h)),
                  pl.BlockSpec((None, 2, BLK, BLK), lambda h, i: (h, 0, 0, 0)),
                  pl.BlockSpec((4, DIFF_HEAD_DIM), lambda h, i: (0, 0)),
                  pl.BlockSpec((1, HEAD_W), lambda h, i: (0, 0))],
        out_specs=pl.BlockSpec((BLK, HEAD_W), lambda h, i: (_row_block(i, nreal), h)),
        scratch_shapes=[pltpu.VMEM((2, BLK, 1), F32), pltpu.VMEM((2, BLK, 1), F32),
                        pltpu.VMEM((2, BLK, HEAD_W), F32)],
        compiler_params=_cparams(("arbitrary", "arbitrary")),
        name="diff_attn",
    )(rel_bias, z, z, z, z, z, bias_tiles, lam_vec, subln_w)


def _largest_tile(rows, cap, align=16):
    best = align
    for t in range(align, cap + 1, align):
        if rows % t == 0:
            best = t
    return best


def _forward(x, meta_tokens, rel_bias, hgrn_lower_bounds, norm_mix_pre, norm_mix_post, norm_ffn_pre,
             norm_ffn_post, w_in, lru_conv_w, lru_conv_b, lru_w_a, lru_b_a, lru_w_x, lru_b_x, lru_lambda,
             pool_w, pool_scale, hgrn_norm, diff_lambda, diff_subln, w_branch, w_out, ffn_w_gu, ffn_w_down):
    bsz, seq, _ = x.shape
    nblk = seq // BLK
    nreal = bsz * nblk
    rows = (nreal + 1) * BLK
    rows_real = nreal * BLK
    tm_big = _largest_tile(rows, 640)
    tm_epi = _largest_tile(rows, 320)
    tm_last = _largest_tile(rows_real, 512)

    def vec(a):
        return a.reshape(1, -1)

    lbs, bias_tiles = _prologue(hgrn_lower_bounds, rel_bias)
    h, hn = _embed(x.reshape(rows_real, D_MODEL), meta_tokens, vec(norm_mix_pre[0]), nreal)

    for layer in range(DEPTH):
        lambda_init = 0.8 - 0.6 * math.exp(-0.3 * layer)
        w_in_l = w_in[layer].astype(BF16)
        z = _mixer_in_proj(hn, w_in_l, tm_big, 1024)
        y_a = _lru(z, lru_conv_w[layer], vec(lru_conv_b[layer]), lru_w_a[layer].astype(BF16), vec(lru_b_a[layer]),
                   lru_w_x[layer].astype(BF16), vec(lru_b_x[layer]), vec(lru_lambda[layer]), nreal, nblk)
        y_b = _pool(z, pool_w[layer].astype(BF16), vec(pool_scale[layer]), nreal, nblk)
        y_c = _hgrn(z, lbs[layer:layer + 1], vec(hgrn_norm[layer]), nreal, nblk)
        y_d = _attn(z, rel_bias, bias_tiles, diff_lambda[layer], vec(diff_subln[layer]), nreal, nblk, lambda_init)
        merged = _gate_merge(hn, w_in_l, (y_a, y_b, y_c, y_d), w_branch[layer].astype(BF16), tm_big, 512)
        h, hn = _out_proj(merged, w_out[layer].astype(BF16), h, vec(norm_mix_post[layer]),
                          vec(norm_ffn_pre[layer]), tm_epi)
        a = _swiglu_up(hn, ffn_w_gu[layer].astype(BF16), tm_big, 512)
        last = layer == DEPTH - 1
        w_next = vec(norm_mix_pre[layer + 1]) if not last else vec(norm_mix_pre[layer])
        h, hn = _down_proj(a, ffn_w_down[layer].astype(BF16), h, vec(norm_ffn_post[layer]), w_next,
                           rows_real if last else rows, tm_last if last else tm_epi, 1408, not last)
    return h.reshape(bsz, seq, D_MODEL)


def kernel(x, meta_tokens, rel_bias, hgrn_lower_bounds, norm_mix_pre, norm_mix_post, norm_ffn_pre, norm_ffn_post, w_in, lru_conv_w, lru_conv_b, lru_w_a, lru_b_a, lru_w_x, lru_b_x, lru_lambda, pool_w, pool_scale, hgrn_norm, diff_lambda, diff_subln, w_branch, w_out, ffn_w_gu, ffn_w_down):
    return _forward(x, meta_tokens, rel_bias, hgrn_lower_bounds, norm_mix_pre, norm_mix_post, norm_ffn_pre,
                    norm_ffn_post, w_in, lru_conv_w, lru_conv_b, lru_w_a, lru_b_a, lru_w_x, lru_b_x, lru_lambda,
                    pool_w, pool_scale, hgrn_norm, diff_lambda, diff_subln, w_branch, w_out, ffn_w_gu, ffn_w_down)
```

```python
import functools
import math

import numpy as np
import jax
import jax.numpy as jnp
from jax import lax
from jax.experimental import pallas as pl
from jax.experimental.pallas import tpu as pltpu

F32 = jnp.float32
BF16 = jnp.bfloat16

D_MODEL = 2048
SEQ = 2048
DEPTH = 2
N_META = 16
BLK = 128
PAD_ROWS = BLK - N_META
WIDTH = 512
HEADS = 4
HEAD_W = 128
CHUNK = 64
SUB = 16
LRU_C = 8.0
POOL_WINDOWS = (2, 4, 8, 16)
DIFF_HEAD_DIM = 64
REL_BUCKETS = 32
REL_MAX_DIST = 128
FFN_HIDDEN = 5632
MIX_COLS = 10 * WIDTH
NEG = -1e30
EPS = 1e-6
VMEM_LIMIT = 56 * 1024 * 1024


def _cparams(sem):
    return pltpu.CompilerParams(dimension_semantics=sem, vmem_limit_bytes=VMEM_LIMIT)


def _rms(x, w):
    return x * lax.rsqrt(jnp.mean(x * x, axis=-1, keepdims=True) + EPS) * w


def _log_sigmoid(z):
    return -(jnp.maximum(-z, 0.0) + jnp.log1p(jnp.exp(-jnp.abs(z))))


def _sigmoid(z):
    return 1.0 / (1.0 + jnp.exp(-z))


def _gelu_tanh(x):
    c = math.sqrt(2.0 / math.pi)
    return 0.5 * x * (1.0 + jnp.tanh(c * (x + 0.044715 * (x * x * x))))


def _shift_rows(x, s, fill, row):
    return jnp.where(row >= s, pltpu.roll(x, s, axis=0), fill)


def _bucket_tiles():
    r = np.arange(BLK)[:, None]
    c = np.arange(BLK)[None, :]
    max_exact = REL_BUCKETS // 2

    def bucket(n):
        nf = np.maximum(n, 1).astype(np.float32)
        large = max_exact + (np.log(nf / np.float32(max_exact)) / np.float32(math.log(REL_MAX_DIST / max_exact))
                             * np.float32(REL_BUCKETS - max_exact)).astype(np.int32)
        large = np.minimum(large, REL_BUCKETS - 1)
        return np.where(n < max_exact, n, large).astype(np.int32)

    d0 = r - c
    t0 = np.where(d0 >= 0, bucket(np.maximum(d0, 0)), -1)
    t1 = bucket(BLK + r - c)
    return np.stack([t0, t1]).astype(np.int32)


def _prologue_kernel(lbraw_ref, relb_ref, idx_ref, lb_ref, bias_ref):
    raw = lbraw_ref[...]
    mx = jnp.max(raw, axis=0, keepdims=True)
    e = jnp.exp(raw - mx)
    sm = e / jnp.sum(e, axis=0, keepdims=True)
    cum = sm[0:1]
    lb_ref[0:1, :] = cum - sm[0:1]
    for l in range(1, DEPTH):
        cum = cum + sm[l:l + 1]
        lb_ref[l:l + 1, :] = cum - sm[0:1]
    for t in range(2):
        idx = idx_ref[t]
        for hd in range(HEADS):
            acc = jnp.zeros((BLK, BLK), F32)
            for bk in range(REL_BUCKETS):
                acc = jnp.where(idx == bk, relb_ref[bk, hd], acc)
            bias_ref[hd, t] = jnp.where(idx < 0, NEG, acc)


def _prologue(hgrn_lower_bounds, rel_bias):
    idx = jnp.asarray(_bucket_tiles())
    return pl.pallas_call(
        _prologue_kernel,
        out_shape=(jax.ShapeDtypeStruct((DEPTH, WIDTH), F32),
                   jax.ShapeDtypeStruct((HEADS, 2, BLK, BLK), F32)),
        in_specs=[pl.BlockSpec(memory_space=pltpu.VMEM),
                  pl.BlockSpec(memory_space=pltpu.SMEM),
                  pl.BlockSpec(memory_space=pltpu.VMEM)],
        out_specs=(pl.BlockSpec(memory_space=pltpu.VMEM), pl.BlockSpec(memory_space=pltpu.VMEM)),
        name="prologue",
    )(hgrn_lower_bounds, rel_bias, idx)


def _embed_kernel(x_ref, meta_ref, w_ref, h_ref, hn_ref, *, nreal):
    i = pl.program_id(0)

    @pl.when(i < nreal)
    def _():
        h_ref[...] = x_ref[...]

    @pl.when(i == nreal)
    def _():
        h_ref[0:PAD_ROWS, :] = jnp.zeros((PAD_ROWS, D_MODEL), F32)
        h_ref[PAD_ROWS:BLK, :] = meta_ref[...]

    hn_ref[...] = _rms(h_ref[...], w_ref[...]).astype(BF16)


def _embed(x2d, meta, w_pre, nreal):
    rows = (nreal + 1) * BLK
    return pl.pallas_call(
        functools.partial(_embed_kernel, nreal=nreal),
        out_shape=(jax.ShapeDtypeStruct((rows, D_MODEL), F32),
                   jax.ShapeDtypeStruct((rows, D_MODEL), BF16)),
        grid=(nreal + 1,),
        in_specs=[pl.BlockSpec((BLK, D_MODEL), lambda i: (jnp.minimum(i, nreal - 1), 0)),
                  pl.BlockSpec((N_META, D_MODEL), lambda i: (0, 0)),
                  pl.BlockSpec((1, D_MODEL), lambda i: (0, 0))],
        out_specs=(pl.BlockSpec((BLK, D_MODEL), lambda i: (i, 0)),
                   pl.BlockSpec((BLK, D_MODEL), lambda i: (i, 0))),
        compiler_params=_cparams(("arbitrary",)),
        name="embed",
    )(x2d, meta, w_pre)


def _cast_tiles_once(pairs):
    @pl.when(pl.program_id(1) == 0)
    def _():
        for src, dst in pairs:
            dst[...] = src[...].astype(BF16)


def _matmul_kernel(x_ref, w_ref, o_ref, wbf):
    _cast_tiles_once([(w_ref, wbf)])
    o_ref[...] = jnp.dot(x_ref[...], wbf[...], preferred_element_type=F32).astype(o_ref.dtype)


def _mixer_in_proj(hn, w_in, layer, col0, ncols, out_dtype, tm, tn):
    rows, k = hn.shape
    return pl.pallas_call(
        _matmul_kernel,
        out_shape=jax.ShapeDtypeStruct((rows, ncols), out_dtype),
        grid=(ncols // tn, rows // tm),
        in_specs=[pl.BlockSpec((tm, k), lambda n, m: (m, 0)),
                  pl.BlockSpec((None, k, tn), lambda n, m: (layer, 0, col0 // tn + n))],
        out_specs=pl.BlockSpec((tm, tn), lambda n, m: (m, n)),
        scratch_shapes=[pltpu.VMEM((k, tn), BF16)],
        compiler_params=_cparams(("arbitrary", "arbitrary")),
        name="mixer_in_proj",
    )(hn, w_in)


def _gate_merge_kernel(hn_ref, g0, g1, g2, g3, y0, y1, y2, y3, wb_ref, o_ref, gbf, wbbf):
    _cast_tiles_once([(g, gbf.at[k]) for k, g in enumerate((g0, g1, g2, g3))] + [(wb_ref, wbbf)])
    hn = hn_ref[...]
    acc = None
    for k, y_ref in enumerate((y0, y1, y2, y3)):
        gate = _sigmoid(jnp.dot(hn, gbf[k], preferred_element_type=F32))
        proj = jnp.dot(y_ref[...], wbbf[k], preferred_element_type=F32)
        acc = gate * proj if acc is None else acc + gate * proj
    o_ref[...] = acc.astype(o_ref.dtype)


def _gate_merge(hn, w_in, ys, w_branch, layer, tm, tn):
    rows, k = hn.shape
    gate_specs = [
        pl.BlockSpec((None, k, tn), functools.partial(
            lambda n, m, base: (layer, 0, base + n), base=(MIX_COLS + br * D_MODEL) // tn))
        for br in range(4)]
    y_specs = [pl.BlockSpec((tm, WIDTH), lambda n, m: (m, 0)) for _ in range(4)]
    return pl.pallas_call(
        _gate_merge_kernel,
        out_shape=jax.ShapeDtypeStruct((rows, D_MODEL), BF16),
        grid=(D_MODEL // tn, rows // tm),
        in_specs=[pl.BlockSpec((tm, k), lambda n, m: (m, 0))] + gate_specs + y_specs
                 + [pl.BlockSpec((None, 4, WIDTH, tn), lambda n, m: (layer, 0, 0, n))],
        out_specs=pl.BlockSpec((tm, tn), lambda n, m: (m, n)),
        scratch_shapes=[pltpu.VMEM((4, k, tn), BF16), pltpu.VMEM((4, WIDTH, tn), BF16)],
        compiler_params=_cparams(("arbitrary", "arbitrary")),
        name="gate_merge",
    )(hn, w_in, w_in, w_in, w_in, *ys, w_branch)


def _cast_kernel(w_ref, o_ref):
    o_ref[...] = w_ref[...].astype(BF16)


def _cast_layer_bf16(w, layer, tr):
    _, r, c = w.shape
    return pl.pallas_call(
        _cast_kernel,
        out_shape=jax.ShapeDtypeStruct((r, c), BF16),
        grid=(r // tr,),
        in_specs=[pl.BlockSpec((None, tr, c), lambda i: (layer, i, 0))],
        out_specs=pl.BlockSpec((tr, c), lambda i: (i, 0)),
        compiler_params=_cparams(("arbitrary",)),
        name="cast_bf16",
    )(w)


def _residual_epilogue(acc, h_ref, wpost_ref, wnext_ref, hnew_ref, hn_ref):
    h_new = h_ref[...] + _rms(acc, wpost_ref[...])
    hnew_ref[...] = h_new
    if hn_ref is not None:
        hn_ref[...] = _rms(h_new, wnext_ref[...]).astype(BF16)


def _out_proj_kernel(x_ref, w_ref, h_ref, wpost_ref, wnext_ref, hnew_ref, hn_ref):
    acc = jnp.dot(x_ref[...], w_ref[...], preferred_element_type=F32)
    _residual_epilogue(acc, h_ref, wpost_ref, wnext_ref, hnew_ref, hn_ref)


def _out_proj(merged, w_out, h, w_post, w_next, tm):
    rows = h.shape[0]
    return pl.pallas_call(
        _out_proj_kernel,
        out_shape=(jax.ShapeDtypeStruct((rows, D_MODEL), F32),
                   jax.ShapeDtypeStruct((rows, D_MODEL), BF16)),
        grid=(rows // tm,),
        in_specs=[pl.BlockSpec((tm, D_MODEL), lambda m: (m, 0)),
                  pl.BlockSpec((D_MODEL, D_MODEL), lambda m: (0, 0)),
                  pl.BlockSpec((tm, D_MODEL), lambda m: (m, 0)),
                  pl.BlockSpec((1, D_MODEL), lambda m: (0, 0)),
                  pl.BlockSpec((1, D_MODEL), lambda m: (0, 0))],
        out_specs=(pl.BlockSpec((tm, D_MODEL), lambda m: (m, 0)),
                   pl.BlockSpec((tm, D_MODEL), lambda m: (m, 0))),
        compiler_params=_cparams(("arbitrary",)),
        name="out_proj",
    )(merged, w_out, h, w_post, w_next)


def _swiglu_up_kernel(x_ref, wg_ref, wu_ref, o_ref, wgbf, wubf):
    _cast_tiles_once([(wg_ref, wgbf), (wu_ref, wubf)])
    x = x_ref[...]
    g = jnp.dot(x, wgbf[...], preferred_element_type=F32)
    u = jnp.dot(x, wubf[...], preferred_element_type=F32)
    o_ref[...] = (g * _sigmoid(g) * u).astype(o_ref.dtype)


def _swiglu_up(hn, w_gu, layer, tm, tn):
    rows, k = hn.shape
    nt = FFN_HIDDEN // tn
    return pl.pallas_call(
        _swiglu_up_kernel,
        out_shape=jax.ShapeDtypeStruct((rows, FFN_HIDDEN), BF16),
        grid=(nt, rows // tm),
        in_specs=[pl.BlockSpec((tm, k), lambda n, m: (m, 0)),
                  pl.BlockSpec((None, k, tn), lambda n, m: (layer, 0, n)),
                  pl.BlockSpec((None, k, tn), lambda n, m: (layer, 0, nt + n))],
        out_specs=pl.BlockSpec((tm, tn), lambda n, m: (m, n)),
        scratch_shapes=[pltpu.VMEM((k, tn), BF16), pltpu.VMEM((k, tn), BF16)],
        compiler_params=_cparams(("arbitrary", "arbitrary")),
        name="swiglu_up",
    )(hn, w_gu, w_gu)


def _down_proj_kernel(a_ref, w_ref, h_ref, wpost_ref, wnext_ref, *rest, emit_hn):
    if emit_hn:
        hnew_ref, hn_ref, acc_ref = rest
    else:
        hnew_ref, acc_ref = rest
        hn_ref = None
    kk = pl.program_id(1)

    @pl.when(kk == 0)
    def _():
        acc_ref[...] = jnp.zeros_like(acc_ref)

    acc_ref[...] += jnp.dot(a_ref[...], w_ref[...], preferred_element_type=F32)

    @pl.when(kk == pl.num_programs(1) - 1)
    def _():
        _residual_epilogue(acc_ref[...], h_ref, wpost_ref, wnext_ref, hnew_ref, hn_ref)


def _down_proj(a, w_down, h, w_post, w_next, rows_out, tm, tk, emit_hn):
    out_shape = [jax.ShapeDtypeStruct((rows_out, D_MODEL), F32)]
    out_specs = [pl.BlockSpec((tm, D_MODEL), lambda m, k: (m, 0))]
    if emit_hn:
        out_shape.append(jax.ShapeDtypeStruct((rows_out, D_MODEL), BF16))
        out_specs.append(pl.BlockSpec((tm, D_MODEL), lambda m, k: (m, 0)))
    res = pl.pallas_call(
        functools.partial(_down_proj_kernel, emit_hn=emit_hn),
        out_shape=tuple(out_shape),
        grid=(rows_out // tm, FFN_HIDDEN // tk),
        in_specs=[pl.BlockSpec((tm, tk), lambda m, k: (m, k)),
                  pl.BlockSpec((tk, D_MODEL), lambda m, k: (k, 0)),
                  pl.BlockSpec((tm, D_MODEL), lambda m, k: (m, 0)),
                  pl.BlockSpec((1, D_MODEL), lambda m, k: (0, 0)),
                  pl.BlockSpec((1, D_MODEL), lambda m, k: (0, 0))],
        out_specs=tuple(out_specs),
        scratch_shapes=[pltpu.VMEM((tm, D_MODEL), F32)],
        compiler_params=_cparams(("arbitrary", "arbitrary")),
        name="down_proj",
    )(a, w_down, h, w_post, w_next)
    return res if emit_hn else (res[0], None)


def _row_block(i, nreal):
    return (i + nreal) % (nreal + 1)


def _zspec(col_block, nreal):
    return pl.BlockSpec((BLK, WIDTH), lambda i: (_row_block(i, nreal), col_block))


def _pspec(shape):
    nd = len(shape)
    return pl.BlockSpec(shape, lambda *_: (0,) * nd)


def _lru_kernel(u_ref, gate_ref, cw_ref, cb_ref, wa_ref, ba_ref, wx_ref, bx_ref, lam_ref, y_ref,
                ubuf, hst, hist_meta, h_meta, *, nblk):
    i = pl.program_id(0)
    is_meta = i == 0

    @pl.when(is_meta)
    def _():
        ubuf[0:8, :] = jnp.zeros((8, WIDTH), F32)
        hst[...] = jnp.zeros_like(hst)

    @pl.when(jnp.logical_and(i >= 1, (i - 1) % nblk == 0))
    def _():
        ubuf[0:8, :] = hist_meta[...]
        hst[...] = h_meta[...]

    u = u_ref[...]
    ubuf[8:8 + BLK, :] = u
    cw = cw_ref[...]
    xc = (cb_ref[...] + cw[3:4] * u + cw[2:3] * ubuf[7:7 + BLK, :]
          + cw[1:2] * ubuf[6:6 + BLK, :] + cw[0:1] * ubuf[5:5 + BLK, :])
    xb = xc.astype(BF16)
    ra, ia = [], []
    for hd in range(HEADS):
        sl = slice(hd * HEAD_W, (hd + 1) * HEAD_W)
        ra.append(jnp.dot(xb[:, sl], wa_ref[hd].astype(BF16), preferred_element_type=F32))
        ia.append(jnp.dot(xb[:, sl], wx_ref[hd].astype(BF16), preferred_element_type=F32))
    r = _sigmoid(jnp.concatenate(ra, axis=1) + ba_ref[...])
    ig = _sigmoid(jnp.concatenate(ia, axis=1) + bx_ref[...])
    lam = lam_ref[...]
    softplus_neg_lam = jnp.maximum(-lam, 0.0) + jnp.log1p(jnp.exp(-jnp.abs(lam)))
    log_a = -LRU_C * r * softplus_neg_lam
    a = jnp.exp(log_a)
    bb = jnp.sqrt(-jnp.tanh(log_a) * (a * a + 1.0)) * (ig * xc)
    row = lax.broadcasted_iota(jnp.int32, (BLK, 1), 0)
    bb = jnp.where(row >= PAD_ROWS * is_meta.astype(jnp.int32), bb, 0.0)

    acum, bcum = a, bb
    s = 1
    while s < BLK:
        a_sh = _shift_rows(acum, s, 1.0, row)
        b_sh = _shift_rows(bcum, s, 0.0, row)
        bcum = acum * b_sh + bcum
        acum = acum * a_sh
        s *= 2
    h = acum * hst[0:1, :] + bcum
    y_ref[...] = (h * _gelu_tanh(gate_ref[...])).astype(y_ref.dtype)

    hist = u[BLK - 8:BLK, :]
    hlast = jnp.broadcast_to(h[BLK - 1:BLK, :], (8, WIDTH))
    ubuf[0:8, :] = hist
    hst[...] = hlast

    @pl.when(is_meta)
    def _():
        hist_meta[...] = hist
        h_meta[...] = hlast


def _lru(z, cw, cb, wa, ba, wx, bx, lam, nreal, nblk):
    rows = z.shape[0]
    return pl.pallas_call(
        functools.partial(_lru_kernel, nblk=nblk),
        out_shape=jax.ShapeDtypeStruct((rows, WIDTH), BF16),
        grid=(nreal + 1,),
        in_specs=[_zspec(0, nreal), _zspec(1, nreal),
                  _pspec((4, WIDTH)), _pspec((1, WIDTH)),
                  _pspec((HEADS, HEAD_W, HEAD_W)), _pspec((1, WIDTH)),
                  _pspec((HEADS, HEAD_W, HEAD_W)), _pspec((1, WIDTH)), _pspec((1, WIDTH))],
        out_specs=pl.BlockSpec((BLK, WIDTH), lambda i: (_row_block(i, nreal), 0)),
        scratch_shapes=[pltpu.VMEM((8 + BLK, WIDTH), F32), pltpu.VMEM((8, WIDTH), F32),
                        pltpu.VMEM((8, WIDTH), F32), pltpu.VMEM((8, WIDTH), F32)],
        compiler_params=_cparams(("arbitrary",)),
        name="rg_lru",
    )(z, z, cw, cb, wa, ba, wx, bx, lam)


POOL_HIST = 16


def _pool_kernel(u_ref, pw_ref, ps_ref, y_ref, ubuf, hist_meta, *, nblk):
    i = pl.program_id(0)
    is_meta = i == 0

    @pl.when(is_meta)
    def _():
        ubuf[0:POOL_HIST, :] = jnp.zeros((POOL_HIST, WIDTH), F32)

    @pl.when(jnp.logical_and(i >= 1, (i - 1) % nblk == 0))
    def _():
        ubuf[0:POOL_HIST, :] = hist_meta[...]

    u = u_ref[...]
    ubuf[POOL_HIST:POOL_HIST + BLK, :] = u
    row = lax.broadcasted_iota(jnp.int32, (BLK, 1), 0)
    meta_i = is_meta.astype(jnp.int32)
    pos1 = row + 1 - PAD_ROWS * meta_i + 2 * POOL_HIST * (1 - meta_i)
    outs = []
    for g, win in enumerate(POOL_WINDOWS):
        sl = slice(g * HEAD_W, (g + 1) * HEAD_W)
        acc = u[:, sl]
        for d in range(1, win):
            acc = acc + ubuf[POOL_HIST - d:POOL_HIST - d + BLK, sl]
        count = jnp.clip(pos1, 1, win).astype(F32)
        pooled = acc / count - u[:, sl]
        outs.append(jnp.dot(pooled.astype(BF16), pw_ref[g].astype(BF16), preferred_element_type=F32))
    y_ref[...] = (jnp.concatenate(outs, axis=1) * ps_ref[...]).astype(y_ref.dtype)

    hist = u[BLK - POOL_HIST:BLK, :]
    ubuf[0:POOL_HIST, :] = hist

    @pl.when(is_meta)
    def _():
        hist_meta[...] = hist


def _pool(z, pw, ps, nreal, nblk):
    rows = z.shape[0]
    return pl.pallas_call(
        functools.partial(_pool_kernel, nblk=nblk),
        out_shape=jax.ShapeDtypeStruct((rows, WIDTH), BF16),
        grid=(nreal + 1,),
        in_specs=[_zspec(2, nreal), _pspec((4, HEAD_W, HEAD_W)), _pspec((1, WIDTH))],
        out_specs=pl.BlockSpec((BLK, WIDTH), lambda i: (_row_block(i, nreal), 0)),
        scratch_shapes=[pltpu.VMEM((POOL_HIST + BLK, WIDTH), F32), pltpu.VMEM((POOL_HIST, WIDTH), F32)],
        compiler_params=_cparams(("arbitrary",)),
        name="ms_pool",
    )(z, pw, ps)


def _hgrn_chunk(q, z, v, lbh, state_t, valid, ones_bf):
    ls = _log_sigmoid(z)
    x1 = jnp.log(lbh)
    x2 = jnp.log1p(-lbh) + ls
    mx = jnp.maximum(x1, x2)
    g = mx + jnp.log1p(jnp.exp(-jnp.abs(x1 - x2)))
    k = (1.0 - lbh) * _sigmoid(-z)
    if valid is not None:
        g = jnp.where(valid, g, 0.0)
    row = lax.broadcasted_iota(jnp.int32, (CHUNK, 1), 0)
    b = g
    s = 1
    while s < CHUNK:
        b = b + _shift_rows(b, s, 0.0, row)
        s *= 2
    b_last = b[CHUNK - 1:CHUNK, :]

    qe = (q * jnp.exp(b)).astype(BF16)
    o = lax.dot_general(qe, state_t.astype(BF16), (((1,), (1,)), ((), ())), preferred_element_type=F32)

    col = lax.broadcasted_iota(jnp.int32, (SUB, CHUNK), 1)
    rsub = lax.broadcasted_iota(jnp.int32, (SUB, CHUNK), 0)
    lane = lax.broadcasted_iota(jnp.int32, (SUB, HEAD_W), 1)
    s_rows = []
    for blk in range(CHUNK // SUB):
        lo = blk * SUB
        bi = b[lo:lo + SUB, :]
        qi = q[lo:lo + SUB, :]
        ki = k[lo:lo + SUB, :]
        parts = []
        for sr in range(SUB):
            e = jnp.exp(jnp.minimum(bi - bi[sr:sr + 1, :], 0.0))
            parts.append(qi * e * ki[sr:sr + 1, :])
        m3 = jnp.concatenate(parts, axis=0).astype(BF16)
        red = jnp.dot(m3, ones_bf, preferred_element_type=F32)
        diag = jnp.zeros((SUB, HEAD_W), F32)
        for sr in range(SUB):
            diag = diag + jnp.where(lane == lo + sr, red[sr * SUB:(sr + 1) * SUB, :], 0.0)
        diag = diag[:, :CHUNK]
        s_blk = jnp.where(jnp.logical_and(col >= lo, col - lo <= rsub), diag, 0.0)
        if blk > 0:
            b0 = b[lo - 1:lo, :]
            kt = (k * jnp.exp(jnp.minimum(b0 - b, 0.0))).astype(BF16)
            qd = (qi * jnp.exp(bi - b0)).astype(BF16)
            off = lax.dot_general(qd, kt, (((1,), (1,)), ((), ())), preferred_element_type=F32)
            s_blk = jnp.where(col < lo, off, s_blk)
        s_rows.append(s_blk)
    scores = jnp.concatenate(s_rows, axis=0).astype(BF16)
    vb = v.astype(BF16)
    o = o + jnp.dot(scores, vb, preferred_element_type=F32)

    kd = (k * jnp.exp(b_last - b)).astype(BF16)
    upd = lax.dot_general(vb, kd, (((0,), (0,)), ((), ())), preferred_element_type=F32)
    new_state_t = state_t * jnp.exp(b_last) + upd
    return o, new_state_t


def _hgrn_kernel(q_ref, f_ref, v_ref, og_ref, lb_ref, nw_ref, y_ref, state, state_meta, *, nblk):
    i = pl.program_id(0)
    is_meta = i == 0

    @pl.when(is_meta)
    def _():
        state[...] = jnp.zeros_like(state)

    @pl.when(jnp.logical_and(i >= 1, (i - 1) % nblk == 0))
    def _():
        state[...] = state_meta[...]

    ones_bf = jnp.ones((HEAD_W, HEAD_W), BF16)
    nw = nw_ref[...]
    for hd in range(HEADS):
        sl = slice(hd * HEAD_W, (hd + 1) * HEAD_W)
        lbh = lb_ref[:, sl]
        st = state[hd]
        for c in range(BLK // CHUNK):
            rs = slice(c * CHUNK, (c + 1) * CHUNK)
            rowg = lax.broadcasted_iota(jnp.int32, (CHUNK, 1), 0) + c * CHUNK
            valid = rowg >= PAD_ROWS * is_meta.astype(jnp.int32)
            o, st = _hgrn_chunk(q_ref[rs, sl], f_ref[rs, sl], v_ref[rs, sl], lbh, st, valid, ones_bf)
            og = og_ref[rs, sl]
            y_ref[rs, sl] = (_rms(o, nw) * (og * _sigmoid(og))).astype(y_ref.dtype)
        state[hd] = st

    @pl.when(is_meta)
    def _():
        state_meta[...] = state[...]


def _hgrn(z, lb, nw, nreal, nblk):
    rows = z.shape[0]
    return pl.pallas_call(
        functools.partial(_hgrn_kernel, nblk=nblk),
        out_shape=jax.ShapeDtypeStruct((rows, WIDTH), BF16),
        grid=(nreal + 1,),
        in_specs=[_zspec(3, nreal), _zspec(4, nreal), _zspec(5, nreal), _zspec(6, nreal),
                  _pspec((1, WIDTH)), _pspec((1, HEAD_W))],
        out_specs=pl.BlockSpec((BLK, WIDTH), lambda i: (_row_block(i, nreal), 0)),
        scratch_shapes=[pltpu.VMEM((HEADS, HEAD_W, HEAD_W), F32), pltpu.VMEM((HEADS, HEAD_W, HEAD_W), F32)],
        compiler_params=_cparams(("arbitrary",)),
        name="hgrn2",
    )(z, z, z, z, lb, nw)


QB = 512
NSUB = QB // BLK


def _attn_lambda(lamv_ref, lambda_init):
    lv = lamv_ref[...]
    return (jnp.exp(jnp.sum(lv[0:1] * lv[1:2], axis=-1, keepdims=True))
            - jnp.exp(jnp.sum(lv[2:3] * lv[3:4], axis=-1, keepdims=True)) + lambda_init)


def _scaled_q(q_ref):
    qf = (q_ref[...].astype(F32) * (DIFF_HEAD_DIM ** -0.5)).astype(BF16)
    return qf[:, :DIFF_HEAD_DIM], qf[:, DIFF_HEAD_DIM:]


def _attn_kernel(relb_ref, q_ref, k_ref, v_ref, km_ref, vm_ref, bias_ref, lamv_ref, sw_ref, y_ref,
                 s_sc, mx_sc, mb_sc, ls_sc, acc_sc, *, nqb, lambda_init):
    hd = pl.program_id(0)
    jq = pl.program_id(1) % nqb
    far = relb_ref[REL_BUCKETS - 1, hd]
    t0 = bias_ref[0]
    t1 = bias_ref[1]
    qs = _scaled_q(q_ref)
    colmask = jnp.where(lax.broadcasted_iota(jnp.int32, (1, BLK), 1) >= PAD_ROWS, 0.0, NEG)

    def key_rows(ref, kt):
        return ref[pl.ds(pl.multiple_of(kt * BLK, BLK), BLK), :]

    def score_tile(kt_rows, slot, sub0, biases):
        r0 = sub0 * BLK
        for mp in range(2):
            kk = kt_rows[:, mp * DIFF_HEAD_DIM:(mp + 1) * DIFF_HEAD_DIM]
            s = lax.dot_general(qs[mp][r0:, :], kk, (((1,), (1,)), ((), ())), preferred_element_type=F32)
            for n, bias in enumerate(biases):
                lo = n * BLK
                sb = s[lo:lo + BLK, :] + bias
                s_sc[mp, slot, r0 + lo:r0 + lo + BLK, :] = sb
                mx_sc[mp, r0 + lo:r0 + lo + BLK, :] = jnp.maximum(mx_sc[mp, r0 + lo:r0 + lo + BLK, :], sb)

    def value_tile(vt_rows, slot, sub0):
        r0 = sub0 * BLK
        for mp in range(2):
            p = jnp.exp(s_sc[mp, slot, r0:, :] - mb_sc[mp, r0:, :])
            ls_sc[mp, r0:, :] += p
            acc_sc[mp, r0:, :] += jnp.dot(p.astype(BF16), vt_rows, preferred_element_type=F32)

    def band_biases(t):
        out = []
        for qi in range(max(t - 1, 0), NSUB):
            delta = qi - t + 1
            out.append(t0 if delta == 0 else (t1 if delta == 1 else far))
        return out

    def walk(fn_far, fn_meta, fn_prev, fn_band):
        n_far = jnp.maximum(NSUB * jq - 1, 0)

        def body(kt, carry):
            fn_far(kt)
            return carry

        lax.fori_loop(0, n_far, body, 0)
        fn_meta()

        @pl.when(jq >= 1)
        def _():
            fn_prev(NSUB * jq - 1)

        for t in range(1, NSUB + 1):
            fn_band(NSUB * jq + t - 1, t)

    mx_sc[...] = jnp.full_like(mx_sc, NEG)
    first = jnp.full((1, BLK), jq, jnp.int32) == 0
    meta_biases = [jnp.where(first, t1, far) + colmask] + [far + colmask] * (NSUB - 1)
    walk(lambda kt: score_tile(key_rows(k_ref, kt), kt + 1, 0, [far] * NSUB),
         lambda: score_tile(km_ref[...], 0, 0, meta_biases),
         lambda kt: score_tile(key_rows(k_ref, kt), kt + 1, 0, [t1] + [far] * (NSUB - 1)),
         lambda kt, t: score_tile(key_rows(k_ref, kt), kt + 1, t - 1, band_biases(t)))

    for mp in range(2):
        mb_sc[mp] = jnp.broadcast_to(jnp.max(mx_sc[mp], axis=-1, keepdims=True), (QB, BLK))
    ls_sc[...] = jnp.zeros_like(ls_sc)
    acc_sc[...] = jnp.zeros_like(acc_sc)

    walk(lambda kt: value_tile(key_rows(v_ref, kt), kt + 1, 0),
         lambda: value_tile(vm_ref[...], 0, 0),
         lambda kt: value_tile(key_rows(v_ref, kt), kt + 1, 0),
         lambda kt, t: value_tile(key_rows(v_ref, kt), kt + 1, t - 1))

    lam = _attn_lambda(lamv_ref, lambda_init)
    l0 = jnp.sum(ls_sc[0], axis=-1, keepdims=True)
    l1 = jnp.sum(ls_sc[1], axis=-1, keepdims=True)
    o = acc_sc[0] / l0 - lam * (acc_sc[1] / l1)
    y_ref[...] = (_rms(o, sw_ref[...]) * (1.0 - lambda_init)).astype(y_ref.dtype)


def _attn_meta_kernel(q_ref, k_ref, v_ref, bias_ref, lamv_ref, sw_ref, ymain_ref, y_ref, *, lambda_init):
    del ymain_ref
    qs = _scaled_q(q_ref)
    kb = k_ref[...]
    colmask = jnp.where(lax.broadcasted_iota(jnp.int32, (1, BLK), 1) >= PAD_ROWS, 0.0, NEG)
    outs = []
    for mp in range(2):
        s = lax.dot_general(qs[mp], kb[:, mp * DIFF_HEAD_DIM:(mp + 1) * DIFF_HEAD_DIM],
                            (((1,), (1,)), ((), ())), preferred_element_type=F32) + bias_ref[0] + colmask
        p = jnp.exp(s - jnp.max(s, axis=-1, keepdims=True))
        acc = jnp.dot(p.astype(BF16), v_ref[...], preferred_element_type=F32)
        outs.append(acc / jnp.sum(p, axis=-1, keepdims=True))
    o = outs[0] - _attn_lambda(lamv_ref, lambda_init) * outs[1]
    o = _rms(o, sw_ref[...]) * (1.0 - lambda_init)
    row = lax.broadcasted_iota(jnp.int32, (BLK, 1), 0)
    y_ref[...] = jnp.where(row >= PAD_ROWS, o, 0.0).astype(y_ref.dtype)


def _attn(zd, rel_bias, bias_tiles, lam_vec, subln_w, nreal, nblk, lambda_init):
    rows = zd.shape[0]
    nqb = nblk * BLK // QB
    nq_total = nreal * BLK // QB
    seq = nblk * BLK
    kcol, vcol = HEADS, 2 * HEADS
    y_main = pl.pallas_call(
        functools.partial(_attn_kernel, nqb=nqb, lambda_init=lambda_init),
        out_shape=jax.ShapeDtypeStruct((rows, WIDTH), BF16),
        grid=(HEADS, nq_total),
        in_specs=[pl.BlockSpec(memory_space=pltpu.SMEM),
                  pl.BlockSpec((QB, HEAD_W), lambda h, i: (i, h)),
                  pl.BlockSpec((seq, HEAD_W), lambda h, i: (i // nqb, kcol + h)),
                  pl.BlockSpec((seq, HEAD_W), lambda h, i: (i // nqb, vcol + h)),
                  pl.BlockSpec((BLK, HEAD_W), lambda h, i: (nreal, kcol + h)),
                  pl.BlockSpec((BLK, HEAD_W), lambda h, i: (nreal, vcol + h)),
                  pl.BlockSpec((None, 2, BLK, BLK), lambda h, i: (h, 0, 0, 0)),
                  pl.BlockSpec((4, DIFF_HEAD_DIM), lambda h, i: (0, 0)),
                  pl.BlockSpec((1, HEAD_W), lambda h, i: (0, 0))],
        out_specs=pl.BlockSpec((QB, HEAD_W), lambda h, i: (i, h)),
        scratch_shapes=[pltpu.VMEM((2, nblk + 1, QB, BLK), F32),
                        pltpu.VMEM((2, QB, BLK), F32),
                        pltpu.VMEM((2, QB, BLK), F32),
                        pltpu.VMEM((2, QB, BLK), F32),
                        pltpu.VMEM((2, QB, HEAD_W), F32)],
        compiler_params=_cparams(("arbitrary", "arbitrary")),
        name="diff_attn",
    )(rel_bias, zd, zd, zd, zd, zd, bias_tiles, lam_vec, subln_w)
    return pl.pallas_call(
        functools.partial(_attn_meta_kernel, lambda_init=lambda_init),
        out_shape=jax.ShapeDtypeStruct((rows, WIDTH), BF16),
        grid=(HEADS,),
        in_specs=[pl.BlockSpec((BLK, HEAD_W), lambda h: (nreal, h)),
                  pl.BlockSpec((BLK, HEAD_W), lambda h: (nreal, kcol + h)),
                  pl.BlockSpec((BLK, HEAD_W), lambda h: (nreal, vcol + h)),
                  pl.BlockSpec((None, 2, BLK, BLK), lambda h: (h, 0, 0, 0)),
                  pl.BlockSpec((4, DIFF_HEAD_DIM), lambda h: (0, 0)),
                  pl.BlockSpec((1, HEAD_W), lambda h: (0, 0)),
                  pl.BlockSpec(memory_space=pl.ANY)],
        out_specs=pl.BlockSpec((BLK, HEAD_W), lambda h: (nreal, h)),
        input_output_aliases={6: 0},
        compiler_params=_cparams(("arbitrary",)),
        name="diff_attn_meta",
    )(zd, zd, zd, bias_tiles, lam_vec, subln_w, y_main)


def _largest_tile(rows, cap, align=16):
    best = align
    for t in range(align, cap + 1, align):
        if rows % t == 0:
            best = t
    return best


def _forward(x, meta_tokens, rel_bias, hgrn_lower_bounds, norm_mix_pre, norm_mix_post, norm_ffn_pre,
             norm_ffn_post, w_in, lru_conv_w, lru_conv_b, lru_w_a, lru_b_a, lru_w_x, lru_b_x, lru_lambda,
             pool_w, pool_scale, hgrn_norm, diff_lambda, diff_subln, w_branch, w_out, ffn_w_gu, ffn_w_down):
    bsz, seq, _ = x.shape
    nblk = seq // BLK
    nreal = bsz * nblk
    rows = (nreal + 1) * BLK
    rows_real = nreal * BLK
    tm_big = _largest_tile(rows, 640)
    tm_epi = _largest_tile(rows, 320)
    tm_last = _largest_tile(rows_real, 512)

    def vec(a):
        return a.reshape(1, -1)

    lbs, bias_tiles = _prologue(hgrn_lower_bounds, rel_bias)
    h, hn = _embed(x.reshape(rows_real, D_MODEL), meta_tokens, vec(norm_mix_pre[0]), nreal)

    for layer in range(DEPTH):
        lambda_init = 0.8 - 0.6 * math.exp(-0.3 * layer)
        z = _mixer_in_proj(hn, w_in, layer, 0, 7 * WIDTH, F32, tm_big, 512)
        zd = _mixer_in_proj(hn, w_in, layer, 7 * WIDTH, 3 * WIDTH, BF16, tm_big, 512)
        y_a = _lru(z, lru_conv_w[layer], vec(lru_conv_b[layer]), lru_w_a[layer], vec(lru_b_a[layer]),
                   lru_w_x[layer], vec(lru_b_x[layer]), vec(lru_lambda[layer]), nreal, nblk)
        y_b = _pool(z, pool_w[layer], vec(pool_scale[layer]), nreal, nblk)
        y_c = _hgrn(z, lbs[layer:layer + 1], vec(hgrn_norm[layer]), nreal, nblk)
        y_d = _attn(zd, rel_bias, bias_tiles, diff_lambda[layer], vec(diff_subln[layer]), nreal, nblk, lambda_init)
        merged = _gate_merge(hn, w_in, (y_a, y_b, y_c, y_d), w_branch, layer, tm_big, 256)
        h, hn = _out_proj(merged, _cast_layer_bf16(w_out, layer, 512), h, vec(norm_mix_post[layer]),
                          vec(norm_ffn_pre[layer]), tm_epi)
        a = _swiglu_up(hn, ffn_w_gu, layer, tm_big, 512)
        last = layer == DEPTH - 1
        w_next = vec(norm_mix_pre[layer + 1]) if not last else vec(norm_mix_pre[layer])
        h, hn = _down_proj(a, _cast_layer_bf16(ffn_w_down, layer, 512), h, vec(norm_ffn_post[layer]), w_next,
                           rows_real if last else rows, tm_last if last else tm_epi, 1408, not last)
    return h.reshape(bsz, seq, D_MODEL)


def kernel(x, meta_tokens, rel_bias, hgrn_lower_bounds, norm_mix_pre, norm_mix_post, norm_ffn_pre, norm_ffn_post, w_in, lru_conv_w, lru_conv_b, lru_w_a, lru_b_a, lru_w_x, lru_b_x, lru_lambda, pool_w, pool_scale, hgrn_norm, diff_lambda, diff_subln, w_branch, w_out, ffn_w_gu, ffn_w_down):
    return _forward(x, meta_tokens, rel_bias, hgrn_lower_bounds, norm_mix_pre, norm_mix_post, norm_ffn_pre,
                    norm_ffn_post, w_in, lru_conv_w, lru_conv_b, lru_w_a, lru_b_a, lru_w_x, lru_b_x, lru_lambda,
                    pool_w, pool_scale, hgrn_norm, diff_lambda, diff_subln, w_branch, w_out, ffn_w_gu, ffn_w_down)
```

```python
import functools
import math

import numpy as np
import jax
import jax.numpy as jnp
from jax import lax
from jax.experimental import pallas as pl
from jax.experimental.pallas import tpu as pltpu

F32 = jnp.float32
BF16 = jnp.bfloat16

D_MODEL = 2048
SEQ = 2048
DEPTH = 2
N_META = 16
BLK = 128
PAD_ROWS = BLK - N_META
WIDTH = 512
HEADS = 4
HEAD_W = 128
CHUNK = 64
SUB = 16
LRU_C = 8.0
POOL_WINDOWS = (2, 4, 8, 16)
DIFF_HEAD_DIM = 64
REL_BUCKETS = 32
REL_MAX_DIST = 128
FFN_HIDDEN = 5632
MIX_COLS = 10 * WIDTH
NEG = -1e30
EPS = 1e-6
VMEM_LIMIT = 56 * 1024 * 1024


def _cparams(sem):
    return pltpu.CompilerParams(dimension_semantics=sem, vmem_limit_bytes=VMEM_LIMIT)


def _rms(x, w):
    return x * lax.rsqrt(jnp.mean(x * x, axis=-1, keepdims=True) + EPS) * w


def _log_sigmoid(z):
    return -(jnp.maximum(-z, 0.0) + jnp.log1p(jnp.exp(-jnp.abs(z))))


def _sigmoid(z):
    return 1.0 / (1.0 + jnp.exp(-z))


def _gelu_tanh(x):
    c = math.sqrt(2.0 / math.pi)
    return 0.5 * x * (1.0 + jnp.tanh(c * (x + 0.044715 * (x * x * x))))


def _shift_rows(x, s, fill, row):
    return jnp.where(row >= s, pltpu.roll(x, s, axis=0), fill)


def _bucket_tiles():
    r = np.arange(BLK)[:, None]
    c = np.arange(BLK)[None, :]
    max_exact = REL_BUCKETS // 2

    def bucket(n):
        nf = np.maximum(n, 1).astype(np.float32)
        large = max_exact + (np.log(nf / np.float32(max_exact)) / np.float32(math.log(REL_MAX_DIST / max_exact))
                             * np.float32(REL_BUCKETS - max_exact)).astype(np.int32)
        large = np.minimum(large, REL_BUCKETS - 1)
        return np.where(n < max_exact, n, large).astype(np.int32)

    d0 = r - c
    t0 = np.where(d0 >= 0, bucket(np.maximum(d0, 0)), -1)
    t1 = bucket(BLK + r - c)
    return np.stack([t0, t1]).astype(np.int32)


def _prologue_kernel(lbraw_ref, relb_ref, idx_ref, lb_ref, bias_ref):
    raw = lbraw_ref[...]
    mx = jnp.max(raw, axis=0, keepdims=True)
    e = jnp.exp(raw - mx)
    sm = e / jnp.sum(e, axis=0, keepdims=True)
    cum = sm[0:1]
    lb_ref[0:1, :] = cum - sm[0:1]
    for l in range(1, DEPTH):
        cum = cum + sm[l:l + 1]
        lb_ref[l:l + 1, :] = cum - sm[0:1]
    for t in range(2):
        idx = idx_ref[t]
        for hd in range(HEADS):
            acc = jnp.zeros((BLK, BLK), F32)
            for bk in range(REL_BUCKETS):
                acc = jnp.where(idx == bk, relb_ref[bk, hd], acc)
            bias_ref[hd, t] = jnp.where(idx < 0, NEG, acc)


def _prologue(hgrn_lower_bounds, rel_bias):
    idx = jnp.asarray(_bucket_tiles())
    return pl.pallas_call(
        _prologue_kernel,
        out_shape=(jax.ShapeDtypeStruct((DEPTH, WIDTH), F32),
                   jax.ShapeDtypeStruct((HEADS, 2, BLK, BLK), F32)),
        in_specs=[pl.BlockSpec(memory_space=pltpu.VMEM),
                  pl.BlockSpec(memory_space=pltpu.SMEM),
                  pl.BlockSpec(memory_space=pltpu.VMEM)],
        out_specs=(pl.BlockSpec(memory_space=pltpu.VMEM), pl.BlockSpec(memory_space=pltpu.VMEM)),
        name="prologue",
    )(hgrn_lower_bounds, rel_bias, idx)


def _embed_kernel(x_ref, meta_ref, w_ref, h_ref, hn_ref, *, nreal):
    i = pl.program_id(0)

    @pl.when(i < nreal)
    def _():
        h_ref[...] = x_ref[...]

    @pl.when(i == nreal)
    def _():
        h_ref[0:PAD_ROWS, :] = jnp.zeros((PAD_ROWS, D_MODEL), F32)
        h_ref[PAD_ROWS:BLK, :] = meta_ref[...]

    hn_ref[...] = _rms(h_ref[...], w_ref[...]).astype(BF16)


def _embed(x2d, meta, w_pre, nreal):
    rows = (nreal + 1) * BLK
    return pl.pallas_call(
        functools.partial(_embed_kernel, nreal=nreal),
        out_shape=(jax.ShapeDtypeStruct((rows, D_MODEL), F32),
                   jax.ShapeDtypeStruct((rows, D_MODEL), BF16)),
        grid=(nreal + 1,),
        in_specs=[pl.BlockSpec((BLK, D_MODEL), lambda i: (jnp.minimum(i, nreal - 1), 0)),
                  pl.BlockSpec((N_META, D_MODEL), lambda i: (0, 0)),
                  pl.BlockSpec((1, D_MODEL), lambda i: (0, 0))],
        out_specs=(pl.BlockSpec((BLK, D_MODEL), lambda i: (i, 0)),
                   pl.BlockSpec((BLK, D_MODEL), lambda i: (i, 0))),
        compiler_params=_cparams(("arbitrary",)),
        name="embed",
    )(x2d, meta, w_pre)


def _cast_tiles_once(pairs):
    @pl.when(pl.program_id(1) == 0)
    def _():
        for src, dst in pairs:
            dst[...] = src[...].astype(BF16)


def _matmul_kernel(x_ref, *rest):
    *w_refs, o_ref, wbf = rest
    wblk = w_refs[0].shape[-1]
    _cast_tiles_once([(w, wbf.at[:, j * wblk:(j + 1) * wblk]) for j, w in enumerate(w_refs)])
    o_ref[...] = jnp.dot(x_ref[...], wbf[...], preferred_element_type=F32).astype(o_ref.dtype)


def _mixer_in_proj(hn, w_in, layer, col0, ncols, out_dtype, tm, tn, wblk):
    rows, k = hn.shape
    nw = tn // wblk
    w_specs = [pl.BlockSpec((None, k, wblk), functools.partial(
        lambda n, m, j: (layer, 0, col0 // wblk + n * nw + j), j=j)) for j in range(nw)]
    return pl.pallas_call(
        _matmul_kernel,
        out_shape=jax.ShapeDtypeStruct((rows, ncols), out_dtype),
        grid=(ncols // tn, rows // tm),
        in_specs=[pl.BlockSpec((tm, k), lambda n, m: (m, 0))] + w_specs,
        out_specs=pl.BlockSpec((tm, tn), lambda n, m: (m, n)),
        scratch_shapes=[pltpu.VMEM((k, tn), BF16)],
        compiler_params=_cparams(("arbitrary", "arbitrary")),
        name="mixer_in_proj",
    )(hn, *([w_in] * nw))


def _gate_merge_kernel(hn_ref, g0, g1, g2, g3, y0, y1, y2, y3, wb_ref, o_ref, gbf, wbbf):
    _cast_tiles_once([(g, gbf.at[k]) for k, g in enumerate((g0, g1, g2, g3))] + [(wb_ref, wbbf)])
    hn = hn_ref[...]
    acc = None
    for k, y_ref in enumerate((y0, y1, y2, y3)):
        gate = _sigmoid(jnp.dot(hn, gbf[k], preferred_element_type=F32))
        proj = jnp.dot(y_ref[...], wbbf[k], preferred_element_type=F32)
        acc = gate * proj if acc is None else acc + gate * proj
    o_ref[...] = acc.astype(o_ref.dtype)


def _gate_merge(hn, w_in, ys, w_branch, layer, tm, tn):
    rows, k = hn.shape
    gate_specs = [
        pl.BlockSpec((None, k, tn), functools.partial(
            lambda n, m, base: (layer, 0, base + n), base=(MIX_COLS + br * D_MODEL) // tn))
        for br in range(4)]
    y_specs = [pl.BlockSpec((tm, WIDTH), lambda n, m: (m, 0)) for _ in range(4)]
    return pl.pallas_call(
        _gate_merge_kernel,
        out_shape=jax.ShapeDtypeStruct((rows, D_MODEL), BF16),
        grid=(D_MODEL // tn, rows // tm),
        in_specs=[pl.BlockSpec((tm, k), lambda n, m: (m, 0))] + gate_specs + y_specs
                 + [pl.BlockSpec((None, 4, WIDTH, tn), lambda n, m: (layer, 0, 0, n))],
        out_specs=pl.BlockSpec((tm, tn), lambda n, m: (m, n)),
        scratch_shapes=[pltpu.VMEM((4, k, tn), BF16), pltpu.VMEM((4, WIDTH, tn), BF16)],
        compiler_params=_cparams(("arbitrary", "arbitrary")),
        name="gate_merge",
    )(hn, w_in, w_in, w_in, w_in, *ys, w_branch)


def _cast_kernel(w_ref, o_ref):
    o_ref[...] = w_ref[...].astype(BF16)


def _cast_layer_bf16(w, layer, tr):
    _, r, c = w.shape
    return pl.pallas_call(
        _cast_kernel,
        out_shape=jax.ShapeDtypeStruct((r, c), BF16),
        grid=(r // tr,),
        in_specs=[pl.BlockSpec((None, tr, c), lambda i: (layer, i, 0))],
        out_specs=pl.BlockSpec((tr, c), lambda i: (i, 0)),
        compiler_params=_cparams(("arbitrary",)),
        name="cast_bf16",
    )(w)


def _residual_epilogue(acc, h_ref, wpost_ref, wnext_ref, hnew_ref, hn_ref):
    h_new = h_ref[...] + _rms(acc, wpost_ref[...])
    hnew_ref[...] = h_new
    if hn_ref is not None:
        hn_ref[...] = _rms(h_new, wnext_ref[...]).astype(BF16)


EPI_ROWS = 160


def _row_subtiles(tm):
    sub = next(s for s in (EPI_ROWS, 128, 64, 32, 16) if tm % s == 0)
    return [slice(r, r + sub) for r in range(0, tm, sub)]


def _out_proj_kernel(x_ref, w_ref, h_ref, wpost_ref, wnext_ref, hnew_ref, hn_ref):
    for rs in _row_subtiles(x_ref.shape[0]):
        acc = jnp.dot(x_ref[rs, :], w_ref[...], preferred_element_type=F32)
        _residual_epilogue(acc, h_ref.at[rs, :], wpost_ref, wnext_ref, hnew_ref.at[rs, :], hn_ref.at[rs, :])


def _out_proj(merged, w_out, h, w_post, w_next, tm):
    rows = h.shape[0]
    return pl.pallas_call(
        _out_proj_kernel,
        out_shape=(jax.ShapeDtypeStruct((rows, D_MODEL), F32),
                   jax.ShapeDtypeStruct((rows, D_MODEL), BF16)),
        grid=(rows // tm,),
        in_specs=[pl.BlockSpec((tm, D_MODEL), lambda m: (m, 0)),
                  pl.BlockSpec((D_MODEL, D_MODEL), lambda m: (0, 0)),
                  pl.BlockSpec((tm, D_MODEL), lambda m: (m, 0)),
                  pl.BlockSpec((1, D_MODEL), lambda m: (0, 0)),
                  pl.BlockSpec((1, D_MODEL), lambda m: (0, 0))],
        out_specs=(pl.BlockSpec((tm, D_MODEL), lambda m: (m, 0)),
                   pl.BlockSpec((tm, D_MODEL), lambda m: (m, 0))),
        compiler_params=_cparams(("arbitrary",)),
        name="out_proj",
    )(merged, w_out, h, w_post, w_next)


def _swiglu_up_kernel(x_ref, wg_ref, wu_ref, o_ref, wgbf, wubf):
    _cast_tiles_once([(wg_ref, wgbf), (wu_ref, wubf)])
    x = x_ref[...]
    g = jnp.dot(x, wgbf[...], preferred_element_type=F32)
    u = jnp.dot(x, wubf[...], preferred_element_type=F32)
    o_ref[...] = (g * _sigmoid(g) * u).astype(o_ref.dtype)


def _swiglu_up(hn, w_gu, layer, tm, tn):
    rows, k = hn.shape
    nt = FFN_HIDDEN // tn
    return pl.pallas_call(
        _swiglu_up_kernel,
        out_shape=jax.ShapeDtypeStruct((rows, FFN_HIDDEN), BF16),
        grid=(nt, rows // tm),
        in_specs=[pl.BlockSpec((tm, k), lambda n, m: (m, 0)),
                  pl.BlockSpec((None, k, tn), lambda n, m: (layer, 0, n)),
                  pl.BlockSpec((None, k, tn), lambda n, m: (layer, 0, nt + n))],
        out_specs=pl.BlockSpec((tm, tn), lambda n, m: (m, n)),
        scratch_shapes=[pltpu.VMEM((k, tn), BF16), pltpu.VMEM((k, tn), BF16)],
        compiler_params=_cparams(("arbitrary", "arbitrary")),
        name="swiglu_up",
    )(hn, w_gu, w_gu)


def _down_proj_kernel(a_ref, w_ref, h_ref, wpost_ref, wnext_ref, *rest, emit_hn):
    if emit_hn:
        hnew_ref, hn_ref, acc_ref = rest
    else:
        hnew_ref, acc_ref = rest
        hn_ref = None
    kk = pl.program_id(1)
    last = pl.num_programs(1) - 1

    @pl.when(kk == 0)
    def _():
        acc_ref[...] = jnp.dot(a_ref[...], w_ref[...], preferred_element_type=F32)

    @pl.when(jnp.logical_and(kk > 0, kk < last))
    def _():
        acc_ref[...] += jnp.dot(a_ref[...], w_ref[...], preferred_element_type=F32)

    @pl.when(kk == last)
    def _():
        for rs in _row_subtiles(a_ref.shape[0]):
            acc = acc_ref[rs, :] + jnp.dot(a_ref[rs, :], w_ref[...], preferred_element_type=F32)
            _residual_epilogue(acc, h_ref.at[rs, :], wpost_ref, wnext_ref, hnew_ref.at[rs, :],
                               None if hn_ref is None else hn_ref.at[rs, :])


def _down_proj(a, w_down, h, w_post, w_next, rows_out, tm, tk, emit_hn):
    out_shape = [jax.ShapeDtypeStruct((rows_out, D_MODEL), F32)]
    out_specs = [pl.BlockSpec((tm, D_MODEL), lambda m, k: (m, 0))]
    if emit_hn:
        out_shape.append(jax.ShapeDtypeStruct((rows_out, D_MODEL), BF16))
        out_specs.append(pl.BlockSpec((tm, D_MODEL), lambda m, k: (m, 0)))
    res = pl.pallas_call(
        functools.partial(_down_proj_kernel, emit_hn=emit_hn),
        out_shape=tuple(out_shape),
        grid=(rows_out // tm, FFN_HIDDEN // tk),
        in_specs=[pl.BlockSpec((tm, tk), lambda m, k: (m, k)),
                  pl.BlockSpec((tk, D_MODEL), lambda m, k: (k, 0)),
                  pl.BlockSpec((tm, D_MODEL), lambda m, k: (m, 0)),
                  pl.BlockSpec((1, D_MODEL), lambda m, k: (0, 0)),
                  pl.BlockSpec((1, D_MODEL), lambda m, k: (0, 0))],
        out_specs=tuple(out_specs),
        scratch_shapes=[pltpu.VMEM((tm, D_MODEL), F32)],
        compiler_params=_cparams(("arbitrary", "arbitrary")),
        name="down_proj",
    )(a, w_down, h, w_post, w_next)
    return res if emit_hn else (res[0], None)


def _row_block(i, nreal):
    return (i + nreal) % (nreal + 1)


def _zspec(col_block, nreal):
    return pl.BlockSpec((BLK, WIDTH), lambda i: (_row_block(i, nreal), col_block))


def _pspec(shape):
    nd = len(shape)
    return pl.BlockSpec(shape, lambda *_: (0,) * nd)


def _lru_kernel(u_ref, gate_ref, cw_ref, cb_ref, wa_ref, ba_ref, wx_ref, bx_ref, lam_ref, y_ref,
                ubuf, hst, hist_meta, h_meta, *, nblk):
    i = pl.program_id(0)
    is_meta = i == 0

    @pl.when(is_meta)
    def _():
        ubuf[0:8, :] = jnp.zeros((8, WIDTH), F32)
        hst[...] = jnp.zeros_like(hst)

    @pl.when(jnp.logical_and(i >= 1, (i - 1) % nblk == 0))
    def _():
        ubuf[0:8, :] = hist_meta[...]
        hst[...] = h_meta[...]

    u = u_ref[...]
    ubuf[8:8 + BLK, :] = u
    cw = cw_ref[...]
    xc = (cb_ref[...] + cw[3:4] * u + cw[2:3] * ubuf[7:7 + BLK, :]
          + cw[1:2] * ubuf[6:6 + BLK, :] + cw[0:1] * ubuf[5:5 + BLK, :])
    xb = xc.astype(BF16)
    ra, ia = [], []
    for hd in range(HEADS):
        sl = slice(hd * HEAD_W, (hd + 1) * HEAD_W)
        ra.append(jnp.dot(xb[:, sl], wa_ref[hd].astype(BF16), preferred_element_type=F32))
        ia.append(jnp.dot(xb[:, sl], wx_ref[hd].astype(BF16), preferred_element_type=F32))
    r = _sigmoid(jnp.concatenate(ra, axis=1) + ba_ref[...])
    ig = _sigmoid(jnp.concatenate(ia, axis=1) + bx_ref[...])
    lam = lam_ref[...]
    softplus_neg_lam = jnp.maximum(-lam, 0.0) + jnp.log1p(jnp.exp(-jnp.abs(lam)))
    log_a = -LRU_C * r * softplus_neg_lam
    a = jnp.exp(log_a)
    bb = jnp.sqrt(-jnp.tanh(log_a) * (a * a + 1.0)) * (ig * xc)
    row = lax.broadcasted_iota(jnp.int32, (BLK, 1), 0)
    bb = jnp.where(row >= PAD_ROWS * is_meta.astype(jnp.int32), bb, 0.0)

    acum, bcum = a, bb
    s = 1
    while s < BLK:
        a_sh = _shift_rows(acum, s, 1.0, row)
        b_sh = _shift_rows(bcum, s, 0.0, row)
        bcum = acum * b_sh + bcum
        acum = acum * a_sh
        s *= 2
    h = acum * hst[0:1, :] + bcum
    y_ref[...] = (h * _gelu_tanh(gate_ref[...])).astype(y_ref.dtype)

    hist = u[BLK - 8:BLK, :]
    hlast = jnp.broadcast_to(h[BLK - 1:BLK, :], (8, WIDTH))
    ubuf[0:8, :] = hist
    hst[...] = hlast

    @pl.when(is_meta)
    def _():
        hist_meta[...] = hist
        h_meta[...] = hlast


def _lru(z, cw, cb, wa, ba, wx, bx, lam, nreal, nblk):
    rows = z.shape[0]
    return pl.pallas_call(
        functools.partial(_lru_kernel, nblk=nblk),
        out_shape=jax.ShapeDtypeStruct((rows, WIDTH), BF16),
        grid=(nreal + 1,),
        in_specs=[_zspec(0, nreal), _zspec(1, nreal),
                  _pspec((4, WIDTH)), _pspec((1, WIDTH)),
                  _pspec((HEADS, HEAD_W, HEAD_W)), _pspec((1, WIDTH)),
                  _pspec((HEADS, HEAD_W, HEAD_W)), _pspec((1, WIDTH)), _pspec((1, WIDTH))],
        out_specs=pl.BlockSpec((BLK, WIDTH), lambda i: (_row_block(i, nreal), 0)),
        scratch_shapes=[pltpu.VMEM((8 + BLK, WIDTH), F32), pltpu.VMEM((8, WIDTH), F32),
                        pltpu.VMEM((8, WIDTH), F32), pltpu.VMEM((8, WIDTH), F32)],
        compiler_params=_cparams(("arbitrary",)),
        name="rg_lru",
    )(z, z, cw, cb, wa, ba, wx, bx, lam)


POOL_HIST = 16


def _pool_kernel(u_ref, pw_ref, ps_ref, y_ref, ubuf, hist_meta, *, nblk):
    i = pl.program_id(0)
    is_meta = i == 0

    @pl.when(is_meta)
    def _():
        ubuf[0:POOL_HIST, :] = jnp.zeros((POOL_HIST, WIDTH), F32)

    @pl.when(jnp.logical_and(i >= 1, (i - 1) % nblk == 0))
    def _():
        ubuf[0:POOL_HIST, :] = hist_meta[...]

    u = u_ref[...]
    ubuf[POOL_HIST:POOL_HIST + BLK, :] = u
    row = lax.broadcasted_iota(jnp.int32, (BLK, 1), 0)
    meta_i = is_meta.astype(jnp.int32)
    pos1 = row + 1 - PAD_ROWS * meta_i + 2 * POOL_HIST * (1 - meta_i)
    outs = []
    for g, win in enumerate(POOL_WINDOWS):
        sl = slice(g * HEAD_W, (g + 1) * HEAD_W)
        acc = u[:, sl]
        for d in range(1, win):
            acc = acc + ubuf[POOL_HIST - d:POOL_HIST - d + BLK, sl]
        count = jnp.clip(pos1, 1, win).astype(F32)
        pooled = acc / count - u[:, sl]
        outs.append(jnp.dot(pooled.astype(BF16), pw_ref[g].astype(BF16), preferred_element_type=F32))
    y_ref[...] = (jnp.concatenate(outs, axis=1) * ps_ref[...]).astype(y_ref.dtype)

    hist = u[BLK - POOL_HIST:BLK, :]
    ubuf[0:POOL_HIST, :] = hist

    @pl.when(is_meta)
    def _():
        hist_meta[...] = hist


def _pool(z, pw, ps, nreal, nblk):
    rows = z.shape[0]
    return pl.pallas_call(
        functools.partial(_pool_kernel, nblk=nblk),
        out_shape=jax.ShapeDtypeStruct((rows, WIDTH), BF16),
        grid=(nreal + 1,),
        in_specs=[_zspec(2, nreal), _pspec((4, HEAD_W, HEAD_W)), _pspec((1, WIDTH))],
        out_specs=pl.BlockSpec((BLK, WIDTH), lambda i: (_row_block(i, nreal), 0)),
        scratch_shapes=[pltpu.VMEM((POOL_HIST + BLK, WIDTH), F32), pltpu.VMEM((POOL_HIST, WIDTH), F32)],
        compiler_params=_cparams(("arbitrary",)),
        name="ms_pool",
    )(z, pw, ps)


def _hgrn_chunk(q, z, v, lbh, state_t, valid, ones_bf):
    ls = _log_sigmoid(z)
    x1 = jnp.log(lbh)
    x2 = jnp.log1p(-lbh) + ls
    mx = jnp.maximum(x1, x2)
    g = mx + jnp.log1p(jnp.exp(-jnp.abs(x1 - x2)))
    k = (1.0 - lbh) * _sigmoid(-z)
    if valid is not None:
        g = jnp.where(valid, g, 0.0)
    row = lax.broadcasted_iota(jnp.int32, (CHUNK, 1), 0)
    b = g
    s = 1
    while s < CHUNK:
        b = b + _shift_rows(b, s, 0.0, row)
        s *= 2
    b_last = b[CHUNK - 1:CHUNK, :]

    qe = (q * jnp.exp(b)).astype(BF16)
    o = lax.dot_general(qe, state_t.astype(BF16), (((1,), (1,)), ((), ())), preferred_element_type=F32)

    col = lax.broadcasted_iota(jnp.int32, (SUB, CHUNK), 1)
    rsub = lax.broadcasted_iota(jnp.int32, (SUB, CHUNK), 0)
    lane = lax.broadcasted_iota(jnp.int32, (SUB, HEAD_W), 1)
    s_rows = []
    for blk in range(CHUNK // SUB):
        lo = blk * SUB
        bi = b[lo:lo + SUB, :]
        qi = q[lo:lo + SUB, :]
        ki = k[lo:lo + SUB, :]
        parts = []
        for sr in range(SUB):
            e = jnp.exp(jnp.minimum(bi - bi[sr:sr + 1, :], 0.0))
            parts.append(qi * e * ki[sr:sr + 1, :])
        m3 = jnp.concatenate(parts, axis=0).astype(BF16)
        red = jnp.dot(m3, ones_bf, preferred_element_type=F32)
        diag = jnp.zeros((SUB, HEAD_W), F32)
        for sr in range(SUB):
            diag = diag + jnp.where(lane == lo + sr, red[sr * SUB:(sr + 1) * SUB, :], 0.0)
        diag = diag[:, :CHUNK]
        s_blk = jnp.where(jnp.logical_and(col >= lo, col - lo <= rsub), diag, 0.0)
        if blk > 0:
            b0 = b[lo - 1:lo, :]
            kt = (k * jnp.exp(jnp.minimum(b0 - b, 0.0))).astype(BF16)
            qd = (qi * jnp.exp(bi - b0)).astype(BF16)
            off = lax.dot_general(qd, kt, (((1,), (1,)), ((), ())), preferred_element_type=F32)
            s_blk = jnp.where(col < lo, off, s_blk)
        s_rows.append(s_blk)
    scores = jnp.concatenate(s_rows, axis=0).astype(BF16)
    vb = v.astype(BF16)
    o = o + jnp.dot(scores, vb, preferred_element_type=F32)

    kd = (k * jnp.exp(b_last - b)).astype(BF16)
    upd = lax.dot_general(vb, kd, (((0,), (0,)), ((), ())), preferred_element_type=F32)
    new_state_t = state_t * jnp.exp(b_last) + upd
    return o, new_state_t


def _hgrn_kernel(q_ref, f_ref, v_ref, og_ref, lb_ref, nw_ref, y_ref, state, state_meta, *, nblk):
    i = pl.program_id(0)
    is_meta = i == 0

    @pl.when(is_meta)
    def _():
        state[...] = jnp.zeros_like(state)

    @pl.when(jnp.logical_and(i >= 1, (i - 1) % nblk == 0))
    def _():
        state[...] = state_meta[...]

    ones_bf = jnp.ones((HEAD_W, HEAD_W), BF16)
    nw = nw_ref[...]
    for hd in range(HEADS):
        sl = slice(hd * HEAD_W, (hd + 1) * HEAD_W)
        lbh = lb_ref[:, sl]
        st = state[hd]
        for c in range(BLK // CHUNK):
            rs = slice(c * CHUNK, (c + 1) * CHUNK)
            rowg = lax.broadcasted_iota(jnp.int32, (CHUNK, 1), 0) + c * CHUNK
            valid = rowg >= PAD_ROWS * is_meta.astype(jnp.int32)
            o, st = _hgrn_chunk(q_ref[rs, sl], f_ref[rs, sl], v_ref[rs, sl], lbh, st, valid, ones_bf)
            og = og_ref[rs, sl]
            y_ref[rs, sl] = (_rms(o, nw) * (og * _sigmoid(og))).astype(y_ref.dtype)
        state[hd] = st

    @pl.when(is_meta)
    def _():
        state_meta[...] = state[...]


def _hgrn(z_a, z_b, lb, nw, nreal, nblk):
    rows = z_a.shape[0]
    return pl.pallas_call(
        functools.partial(_hgrn_kernel, nblk=nblk),
        out_shape=jax.ShapeDtypeStruct((rows, WIDTH), BF16),
        grid=(nreal + 1,),
        in_specs=[_zspec(3, nreal), _zspec(0, nreal), _zspec(1, nreal), _zspec(2, nreal),
                  _pspec((1, WIDTH)), _pspec((1, HEAD_W))],
        out_specs=pl.BlockSpec((BLK, WIDTH), lambda i: (_row_block(i, nreal), 0)),
        scratch_shapes=[pltpu.VMEM((HEADS, HEAD_W, HEAD_W), F32), pltpu.VMEM((HEADS, HEAD_W, HEAD_W), F32)],
        compiler_params=_cparams(("arbitrary",)),
        name="hgrn2",
    )(z_a, z_b, z_b, z_b, lb, nw)


QB = 512
NSUB = QB // BLK


def _attn_lambda(lamv_ref, lambda_init):
    lv = lamv_ref[...]
    return (jnp.exp(jnp.sum(lv[0:1] * lv[1:2], axis=-1, keepdims=True))
            - jnp.exp(jnp.sum(lv[2:3] * lv[3:4], axis=-1, keepdims=True)) + lambda_init)


def _scaled_q(q_ref):
    qf = (q_ref[...].astype(F32) * (DIFF_HEAD_DIM ** -0.5)).astype(BF16)
    return qf[:, :DIFF_HEAD_DIM], qf[:, DIFF_HEAD_DIM:]


def _attn_kernel(relb_ref, q_ref, k_ref, v_ref, km_ref, vm_ref, bias_ref, lamv_ref, sw_ref, y_ref,
                 s_sc, mx_sc, mb_sc, ls_sc, acc_sc, *, nqb, lambda_init):
    hd = pl.program_id(0)
    jq = pl.program_id(1) % nqb
    far = relb_ref[REL_BUCKETS - 1, hd]
    t0 = bias_ref[0]
    t1 = bias_ref[1]
    qs = _scaled_q(q_ref)
    colmask = jnp.where(lax.broadcasted_iota(jnp.int32, (1, BLK), 1) >= PAD_ROWS, 0.0, NEG)

    def key_rows(ref, kt):
        return ref[pl.ds(pl.multiple_of(kt * BLK, BLK), BLK), :]

    def score_tile(kt_rows, slot, sub0, biases):
        r0 = sub0 * BLK
        for mp in range(2):
            kk = kt_rows[:, mp * DIFF_HEAD_DIM:(mp + 1) * DIFF_HEAD_DIM]
            s = lax.dot_general(qs[mp][r0:, :], kk, (((1,), (1,)), ((), ())), preferred_element_type=F32)
            for n, bias in enumerate(biases):
                lo = n * BLK
                sb = s[lo:lo + BLK, :] + bias
                s_sc[mp, slot, r0 + lo:r0 + lo + BLK, :] = sb
                mx_sc[mp, r0 + lo:r0 + lo + BLK, :] = jnp.maximum(mx_sc[mp, r0 + lo:r0 + lo + BLK, :], sb)

    def value_tile(vt_rows, slot, sub0):
        r0 = sub0 * BLK
        for mp in range(2):
            p = jnp.exp(s_sc[mp, slot, r0:, :] - mb_sc[mp, r0:, :])
            ls_sc[mp, r0:, :] += p
            acc_sc[mp, r0:, :] += jnp.dot(p.astype(BF16), vt_rows, preferred_element_type=F32)

    def band_biases(t):
        out = []
        for qi in range(max(t - 1, 0), NSUB):
            delta = qi - t + 1
            out.append(t0 if delta == 0 else (t1 if delta == 1 else far))
        return out

    def walk(fn_far, fn_meta, fn_prev, fn_band):
        n_far = jnp.maximum(NSUB * jq - 1, 0)

        def body(kt, carry):
            fn_far(kt)
            return carry

        lax.fori_loop(0, n_far, body, 0)
        fn_meta()

        @pl.when(jq >= 1)
        def _():
            fn_prev(NSUB * jq - 1)

        for t in range(1, NSUB + 1):
            fn_band(NSUB * jq + t - 1, t)

    mx_sc[...] = jnp.full_like(mx_sc, NEG)
    first = jnp.full((1, BLK), jq, jnp.int32) == 0
    meta_biases = [jnp.where(first, t1, far) + colmask] + [far + colmask] * (NSUB - 1)
    walk(lambda kt: score_tile(key_rows(k_ref, kt), kt + 1, 0, [far] * NSUB),
         lambda: score_tile(km_ref[...], 0, 0, meta_biases),
         lambda kt: score_tile(key_rows(k_ref, kt), kt + 1, 0, [t1] + [far] * (NSUB - 1)),
         lambda kt, t: score_tile(key_rows(k_ref, kt), kt + 1, t - 1, band_biases(t)))

    for mp in range(2):
        mb_sc[mp] = jnp.broadcast_to(jnp.max(mx_sc[mp], axis=-1, keepdims=True), (QB, BLK))
    ls_sc[...] = jnp.zeros_like(ls_sc)
    acc_sc[...] = jnp.zeros_like(acc_sc)

    walk(lambda kt: value_tile(key_rows(v_ref, kt), kt + 1, 0),
         lambda: value_tile(vm_ref[...], 0, 0),
         lambda kt: value_tile(key_rows(v_ref, kt), kt + 1, 0),
         lambda kt, t: value_tile(key_rows(v_ref, kt), kt + 1, t - 1))

    lam = _attn_lambda(lamv_ref, lambda_init)
    l0 = jnp.sum(ls_sc[0], axis=-1, keepdims=True)
    l1 = jnp.sum(ls_sc[1], axis=-1, keepdims=True)
    o = acc_sc[0] / l0 - lam * (acc_sc[1] / l1)
    y_ref[...] = (_rms(o, sw_ref[...]) * (1.0 - lambda_init)).astype(y_ref.dtype)


def _attn_meta_kernel(q_ref, k_ref, v_ref, bias_ref, lamv_ref, sw_ref, ymain_ref, y_ref, *, lambda_init):
    del ymain_ref
    qs = _scaled_q(q_ref)
    kb = k_ref[...]
    colmask = jnp.where(lax.broadcasted_iota(jnp.int32, (1, BLK), 1) >= PAD_ROWS, 0.0, NEG)
    outs = []
    for mp in range(2):
        s = lax.dot_general(qs[mp], kb[:, mp * DIFF_HEAD_DIM:(mp + 1) * DIFF_HEAD_DIM],
                            (((1,), (1,)), ((), ())), preferred_element_type=F32) + bias_ref[0] + colmask
        p = jnp.exp(s - jnp.max(s, axis=-1, keepdims=True))
        acc = jnp.dot(p.astype(BF16), v_ref[...], preferred_element_type=F32)
        outs.append(acc / jnp.sum(p, axis=-1, keepdims=True))
    o = outs[0] - _attn_lambda(lamv_ref, lambda_init) * outs[1]
    o = _rms(o, sw_ref[...]) * (1.0 - lambda_init)
    row = lax.broadcasted_iota(jnp.int32, (BLK, 1), 0)
    y_ref[...] = jnp.where(row >= PAD_ROWS, o, 0.0).astype(y_ref.dtype)


def _attn(zd, rel_bias, bias_tiles, lam_vec, subln_w, nreal, nblk, lambda_init):
    rows = zd.shape[0]
    nqb = nblk * BLK // QB
    nq_total = nreal * BLK // QB
    seq = nblk * BLK
    kcol, vcol = HEADS, 2 * HEADS
    y_main = pl.pallas_call(
        functools.partial(_attn_kernel, nqb=nqb, lambda_init=lambda_init),
        out_shape=jax.ShapeDtypeStruct((rows, WIDTH), BF16),
        grid=(HEADS, nq_total),
        in_specs=[pl.BlockSpec(memory_space=pltpu.SMEM),
                  pl.BlockSpec((QB, HEAD_W), lambda h, i: (i, h)),
                  pl.BlockSpec((seq, HEAD_W), lambda h, i: (i // nqb, kcol + h)),
                  pl.BlockSpec((seq, HEAD_W), lambda h, i: (i // nqb, vcol + h)),
                  pl.BlockSpec((BLK, HEAD_W), lambda h, i: (nreal, kcol + h)),
                  pl.BlockSpec((BLK, HEAD_W), lambda h, i: (nreal, vcol + h)),
                  pl.BlockSpec((None, 2, BLK, BLK), lambda h, i: (h, 0, 0, 0)),
                  pl.BlockSpec((4, DIFF_HEAD_DIM), lambda h, i: (0, 0)),
                  pl.BlockSpec((1, HEAD_W), lambda h, i: (0, 0))],
        out_specs=pl.BlockSpec((QB, HEAD_W), lambda h, i: (i, h)),
        scratch_shapes=[pltpu.VMEM((2, nblk + 1, QB, BLK), F32),
                        pltpu.VMEM((2, QB, BLK), F32),
                        pltpu.VMEM((2, QB, BLK), F32),
                        pltpu.VMEM((2, QB, BLK), F32),
                        pltpu.VMEM((2, QB, HEAD_W), F32)],
        compiler_params=_cparams(("arbitrary", "arbitrary")),
        name="diff_attn",
    )(rel_bias, zd, zd, zd, zd, zd, bias_tiles, lam_vec, subln_w)
    return pl.pallas_call(
        functools.partial(_attn_meta_kernel, lambda_init=lambda_init),
        out_shape=jax.ShapeDtypeStruct((rows, WIDTH), BF16),
        grid=(HEADS,),
        in_specs=[pl.BlockSpec((BLK, HEAD_W), lambda h: (nreal, h)),
                  pl.BlockSpec((BLK, HEAD_W), lambda h: (nreal, kcol + h)),
                  pl.BlockSpec((BLK, HEAD_W), lambda h: (nreal, vcol + h)),
                  pl.BlockSpec((None, 2, BLK, BLK), lambda h: (h, 0, 0, 0)),
                  pl.BlockSpec((4, DIFF_HEAD_DIM), lambda h: (0, 0)),
                  pl.BlockSpec((1, HEAD_W), lambda h: (0, 0)),
                  pl.BlockSpec(memory_space=pl.ANY)],
        out_specs=pl.BlockSpec((BLK, HEAD_W), lambda h: (nreal, h)),
        input_output_aliases={6: 0},
        compiler_params=_cparams(("arbitrary",)),
        name="diff_attn_meta",
    )(zd, zd, zd, bias_tiles, lam_vec, subln_w, y_main)


def _largest_tile(rows, cap, align=16):
    best = align
    for t in range(align, cap + 1, align):
        if rows % t == 0:
            best = t
    return best


def _forward(x, meta_tokens, rel_bias, hgrn_lower_bounds, norm_mix_pre, norm_mix_post, norm_ffn_pre,
             norm_ffn_post, w_in, lru_conv_w, lru_conv_b, lru_w_a, lru_b_a, lru_w_x, lru_b_x, lru_lambda,
             pool_w, pool_scale, hgrn_norm, diff_lambda, diff_subln, w_branch, w_out, ffn_w_gu, ffn_w_down):
    bsz, seq, _ = x.shape
    nblk = seq // BLK
    nreal = bsz * nblk
    rows = (nreal + 1) * BLK
    rows_real = nreal * BLK
    tm_big = _largest_tile(rows, 640)
    tm_epi = _largest_tile(rows, 640)
    tm_last = _largest_tile(rows_real, 512)

    def vec(a):
        return a.reshape(1, -1)

    lbs, bias_tiles = _prologue(hgrn_lower_bounds, rel_bias)
    h, hn = _embed(x.reshape(rows_real, D_MODEL), meta_tokens, vec(norm_mix_pre[0]), nreal)

    for layer in range(DEPTH):
        lambda_init = 0.8 - 0.6 * math.exp(-0.3 * layer)
        z = _mixer_in_proj(hn, w_in, layer, 0, 4 * WIDTH, F32, tm_big, 1024, 1024)
        z_b = _mixer_in_proj(hn, w_in, layer, 4 * WIDTH, 3 * WIDTH, F32, tm_big, 3 * WIDTH, WIDTH)
        zd = _mixer_in_proj(hn, w_in, layer, 7 * WIDTH, 3 * WIDTH, BF16, tm_big, 3 * WIDTH, WIDTH)
        y_a = _lru(z, lru_conv_w[layer], vec(lru_conv_b[layer]), lru_w_a[layer], vec(lru_b_a[layer]),
                   lru_w_x[layer], vec(lru_b_x[layer]), vec(lru_lambda[layer]), nreal, nblk)
        y_b = _pool(z, pool_w[layer], vec(pool_scale[layer]), nreal, nblk)
        y_c = _hgrn(z, z_b, lbs[layer:layer + 1], vec(hgrn_norm[layer]), nreal, nblk)
        y_d = _attn(zd, rel_bias, bias_tiles, diff_lambda[layer], vec(diff_subln[layer]), nreal, nblk, lambda_init)
        merged = _gate_merge(hn, w_in, (y_a, y_b, y_c, y_d), w_branch, layer, tm_big, 256)
        h, hn = _out_proj(merged, _cast_layer_bf16(w_out, layer, 512), h, vec(norm_mix_post[layer]),
                          vec(norm_ffn_pre[layer]), tm_epi)
        a = _swiglu_up(hn, ffn_w_gu, layer, tm_big, 512)
        last = layer == DEPTH - 1
        w_next = vec(norm_mix_pre[layer + 1]) if not last else vec(norm_mix_pre[layer])
        h, hn = _down_proj(a, _cast_layer_bf16(ffn_w_down, layer, 512), h, vec(norm_ffn_post[layer]), w_next,
                           rows_real if last else rows, tm_last if last else tm_epi, 1408, not last)
    return h.reshape(bsz, seq, D_MODEL)


def kernel(x, meta_tokens, rel_bias, hgrn_lower_bounds, norm_mix_pre, norm_mix_post, norm_ffn_pre, norm_ffn_post, w_in, lru_conv_w, lru_conv_b, lru_w_a, lru_b_a, lru_w_x, lru_b_x, lru_lambda, pool_w, pool_scale, hgrn_norm, diff_lambda, diff_subln, w_branch, w_out, ffn_w_gu, ffn_w_down):
    return _forward(x, meta_tokens, rel_bias, hgrn_lower_bounds, norm_mix_pre, norm_mix_post, norm_ffn_pre,
                    norm_ffn_post, w_in, lru_conv_w, lru_conv_b, lru_w_a, lru_b_a, lru_w_x, lru_b_x, lru_lambda,
                    pool_w, pool_scale, hgrn_norm, diff_lambda, diff_subln, w_branch, w_out, ffn_w_gu, ffn_w_down)
```

```python
import functools
import math

import numpy as np
import jax
import jax.numpy as jnp
from jax import lax
from jax.experimental import pallas as pl
from jax.experimental.pallas import tpu as pltpu

F32 = jnp.float32
BF16 = jnp.bfloat16

D_MODEL = 2048
SEQ = 2048
DEPTH = 2
N_META = 16
BLK = 128
PAD_ROWS = BLK - N_META
QB = 512
NSUB = QB // BLK
WIDTH = 512
HEADS = 4
HEAD_W = 128
CHUNK = 64
SUB = 16
LRU_C = 8.0
POOL_WINDOWS = (2, 4, 8, 16)
DIFF_HEAD_DIM = 64
REL_BUCKETS = 32
REL_MAX_DIST = 128
FFN_HIDDEN = 5632
MIX_COLS = 10 * WIDTH
NEG = -1e30
EPS = 1e-6
VMEM_LIMIT = 56 * 1024 * 1024


def _cparams(sem):
    return pltpu.CompilerParams(dimension_semantics=sem, vmem_limit_bytes=VMEM_LIMIT)


def _rms(x, w):
    return x * lax.rsqrt(jnp.mean(x * x, axis=-1, keepdims=True) + EPS) * w


def _log_sigmoid(z):
    return -(jnp.maximum(-z, 0.0) + jnp.log1p(jnp.exp(-jnp.abs(z))))


def _sigmoid(z):
    return 1.0 / (1.0 + jnp.exp(-z))


def _gelu_tanh(x):
    c = math.sqrt(2.0 / math.pi)
    return 0.5 * x * (1.0 + jnp.tanh(c * (x + 0.044715 * (x * x * x))))


def _shift_rows(x, s, fill, row):
    return jnp.where(row >= s, pltpu.roll(x, s, axis=0), fill)


def _bucket_tiles():
    r = np.arange(BLK)[None, :]
    c = np.arange(BLK)[:, None]
    max_exact = REL_BUCKETS // 2

    def bucket(n):
        nf = np.maximum(n, 1).astype(np.float32)
        large = max_exact + (np.log(nf / np.float32(max_exact)) / np.float32(math.log(REL_MAX_DIST / max_exact))
                             * np.float32(REL_BUCKETS - max_exact)).astype(np.int32)
        large = np.minimum(large, REL_BUCKETS - 1)
        return np.where(n < max_exact, n, large).astype(np.int32)

    d0 = r - c
    t0 = np.where(d0 >= 0, bucket(np.maximum(d0, 0)), -1)
    t1 = bucket(BLK + r - c)
    return np.stack([t0, t1]).astype(np.int32)


def _prologue_kernel(lbraw_ref, relb_ref, idx_ref, lb_ref, bias_ref, diag_ref):
    raw = lbraw_ref[...]
    mx = jnp.max(raw, axis=0, keepdims=True)
    e = jnp.exp(raw - mx)
    sm = e / jnp.sum(e, axis=0, keepdims=True)
    cum = sm[0:1]
    lb_ref[0:1, :] = cum - sm[0:1]
    for l in range(1, DEPTH):
        cum = cum + sm[l:l + 1]
        lb_ref[l:l + 1, :] = cum - sm[0:1]
    for t in range(2):
        idx = idx_ref[t]
        for hd in range(HEADS):
            acc = jnp.zeros((BLK, BLK), F32)
            for bk in range(REL_BUCKETS):
                acc = jnp.where(idx == bk, relb_ref[bk, hd], acc)
            bias_ref[hd, t] = jnp.where(idx < 0, NEG, acc)
    for hd in range(HEADS):
        far = relb_ref[REL_BUCKETS - 1, hd]
        for kb in range(NSUB):
            for qb in range(NSUB):
                delta = qb - kb
                if delta == 0:
                    blk = bias_ref[hd, 0]
                elif delta == 1:
                    blk = bias_ref[hd, 1]
                else:
                    blk = jnp.full((BLK, BLK), far if delta > 1 else NEG, F32)
                diag_ref[hd, kb * BLK:(kb + 1) * BLK, qb * BLK:(qb + 1) * BLK] = blk


def _prologue(hgrn_lower_bounds, rel_bias):
    idx = jnp.asarray(_bucket_tiles())
    vmem = pl.BlockSpec(memory_space=pltpu.VMEM)
    return pl.pallas_call(
        _prologue_kernel,
        out_shape=(jax.ShapeDtypeStruct((DEPTH, WIDTH), F32),
                   jax.ShapeDtypeStruct((HEADS, 2, BLK, BLK), F32),
                   jax.ShapeDtypeStruct((HEADS, QB, QB), F32)),
        in_specs=[vmem, pl.BlockSpec(memory_space=pltpu.SMEM), vmem],
        out_specs=(vmem, vmem, vmem),
        name="prologue",
    )(hgrn_lower_bounds, rel_bias, idx)


def _embed_kernel(x_ref, meta_ref, w_ref, h_ref, hn_ref, *, nreal):
    i = pl.program_id(0)

    @pl.when(i < nreal)
    def _():
        h_ref[...] = x_ref[...]

    @pl.when(i == nreal)
    def _():
        h_ref[0:PAD_ROWS, :] = jnp.zeros((PAD_ROWS, D_MODEL), F32)
        h_ref[PAD_ROWS:BLK, :] = meta_ref[...]

    hn_ref[...] = _rms(h_ref[...], w_ref[...]).astype(BF16)


def _embed(x2d, meta, w_pre, nreal):
    rows = (nreal + 1) * BLK
    return pl.pallas_call(
        functools.partial(_embed_kernel, nreal=nreal),
        out_shape=(jax.ShapeDtypeStruct((rows, D_MODEL), F32),
                   jax.ShapeDtypeStruct((rows, D_MODEL), BF16)),
        grid=(nreal + 1,),
        in_specs=[pl.BlockSpec((BLK, D_MODEL), lambda i: (jnp.minimum(i, nreal - 1), 0)),
                  pl.BlockSpec((N_META, D_MODEL), lambda i: (0, 0)),
                  pl.BlockSpec((1, D_MODEL), lambda i: (0, 0))],
        out_specs=(pl.BlockSpec((BLK, D_MODEL), lambda i: (i, 0)),
                   pl.BlockSpec((BLK, D_MODEL), lambda i: (i, 0))),
        compiler_params=_cparams(("arbitrary",)),
        name="embed",
    )(x2d, meta, w_pre)


def _cast_tiles_once(pairs):
    @pl.when(pl.program_id(1) == 0)
    def _():
        for src, dst in pairs:
            dst[...] = src[...].astype(BF16)


def _matmul_kernel(x_ref, *rest):
    *w_refs, o_ref, wbf = rest
    wblk = w_refs[0].shape[-1]
    _cast_tiles_once([(w, wbf.at[:, j * wblk:(j + 1) * wblk]) for j, w in enumerate(w_refs)])
    o_ref[...] = jnp.dot(x_ref[...], wbf[...], preferred_element_type=F32).astype(o_ref.dtype)


def _mixer_in_proj(hn, w_in, layer, col0, ncols, out_dtype, tm, tn, wblk):
    rows, k = hn.shape
    nw = tn // wblk
    w_specs = [pl.BlockSpec((None, k, wblk), functools.partial(
        lambda n, m, j: (layer, 0, col0 // wblk + n * nw + j), j=j)) for j in range(nw)]
    return pl.pallas_call(
        _matmul_kernel,
        out_shape=jax.ShapeDtypeStruct((rows, ncols), out_dtype),
        grid=(ncols // tn, rows // tm),
        in_specs=[pl.BlockSpec((tm, k), lambda n, m: (m, 0))] + w_specs,
        out_specs=pl.BlockSpec((tm, tn), lambda n, m: (m, n)),
        scratch_shapes=[pltpu.VMEM((k, tn), BF16)],
        compiler_params=_cparams(("arbitrary", "arbitrary")),
        name="mixer_in_proj",
    )(hn, *([w_in] * nw))


def _gate_merge_kernel(hn_ref, g0, g1, g2, g3, y0, y1, y2, y3, wb_ref, o_ref, gbf, wbbf):
    _cast_tiles_once([(g, gbf.at[k]) for k, g in enumerate((g0, g1, g2, g3))] + [(wb_ref, wbbf)])
    hn = hn_ref[...]
    acc = None
    for k, y_ref in enumerate((y0, y1, y2, y3)):
        gate = _sigmoid(jnp.dot(hn, gbf[k], preferred_element_type=F32))
        proj = jnp.dot(y_ref[...], wbbf[k], preferred_element_type=F32)
        acc = gate * proj if acc is None else acc + gate * proj
    o_ref[...] = acc.astype(o_ref.dtype)


def _gate_merge(hn, w_in, ys, w_branch, layer, tm, tn):
    rows, k = hn.shape
    gate_specs = [
        pl.BlockSpec((None, k, tn), functools.partial(
            lambda n, m, base: (layer, 0, base + n), base=(MIX_COLS + br * D_MODEL) // tn))
        for br in range(4)]
    y_specs = [pl.BlockSpec((tm, WIDTH), lambda n, m: (m, 0)) for _ in range(4)]
    return pl.pallas_call(
        _gate_merge_kernel,
        out_shape=jax.ShapeDtypeStruct((rows, D_MODEL), BF16),
        grid=(D_MODEL // tn, rows // tm),
        in_specs=[pl.BlockSpec((tm, k), lambda n, m: (m, 0))] + gate_specs + y_specs
                 + [pl.BlockSpec((None, 4, WIDTH, tn), lambda n, m: (layer, 0, 0, n))],
        out_specs=pl.BlockSpec((tm, tn), lambda n, m: (m, n)),
        scratch_shapes=[pltpu.VMEM((4, k, tn), BF16), pltpu.VMEM((4, WIDTH, tn), BF16)],
        compiler_params=_cparams(("arbitrary", "arbitrary")),
        name="gate_merge",
    )(hn, w_in, w_in, w_in, w_in, *ys, w_branch)


def _cast_kernel(w_ref, o_ref):
    o_ref[...] = w_ref[...].astype(BF16)


def _cast_layer_bf16(w, layer, tr):
    _, r, c = w.shape
    return pl.pallas_call(
        _cast_kernel,
        out_shape=jax.ShapeDtypeStruct((r, c), BF16),
        grid=(r // tr,),
        in_specs=[pl.BlockSpec((None, tr, c), lambda i: (layer, i, 0))],
        out_specs=pl.BlockSpec((tr, c), lambda i: (i, 0)),
        compiler_params=_cparams(("arbitrary",)),
        name="cast_bf16",
    )(w)


def _residual_epilogue(acc, h_ref, wpost_ref, wnext_ref, hnew_ref, hn_ref):
    h_new = h_ref[...] + _rms(acc, wpost_ref[...])
    hnew_ref[...] = h_new
    if hn_ref is not None:
        hn_ref[...] = _rms(h_new, wnext_ref[...]).astype(BF16)


EPI_ROWS = 160


def _row_subtiles(tm):
    sub = next(s for s in (EPI_ROWS, 128, 64, 32, 16) if tm % s == 0)
    return [slice(r, r + sub) for r in range(0, tm, sub)]


def _out_proj_kernel(x_ref, w_ref, h_ref, wpost_ref, wnext_ref, hnew_ref, hn_ref):
    for rs in _row_subtiles(x_ref.shape[0]):
        acc = jnp.dot(x_ref[rs, :], w_ref[...], preferred_element_type=F32)
        _residual_epilogue(acc, h_ref.at[rs, :], wpost_ref, wnext_ref, hnew_ref.at[rs, :], hn_ref.at[rs, :])


def _out_proj(merged, w_out, h, w_post, w_next, tm):
    rows = h.shape[0]
    return pl.pallas_call(
        _out_proj_kernel,
        out_shape=(jax.ShapeDtypeStruct((rows, D_MODEL), F32),
                   jax.ShapeDtypeStruct((rows, D_MODEL), BF16)),
        grid=(rows // tm,),
        in_specs=[pl.BlockSpec((tm, D_MODEL), lambda m: (m, 0)),
                  pl.BlockSpec((D_MODEL, D_MODEL), lambda m: (0, 0)),
                  pl.BlockSpec((tm, D_MODEL), lambda m: (m, 0)),
                  pl.BlockSpec((1, D_MODEL), lambda m: (0, 0)),
                  pl.BlockSpec((1, D_MODEL), lambda m: (0, 0))],
        out_specs=(pl.BlockSpec((tm, D_MODEL), lambda m: (m, 0)),
                   pl.BlockSpec((tm, D_MODEL), lambda m: (m, 0))),
        compiler_params=_cparams(("arbitrary",)),
        name="out_proj",
    )(merged, w_out, h, w_post, w_next)


def _swiglu_up_kernel(x_ref, wg_ref, wu_ref, o_ref, wgbf, wubf):
    _cast_tiles_once([(wg_ref, wgbf), (wu_ref, wubf)])
    x = x_ref[...]
    g = jnp.dot(x, wgbf[...], preferred_element_type=F32)
    u = jnp.dot(x, wubf[...], preferred_element_type=F32)
    o_ref[...] = (g * _sigmoid(g) * u).astype(o_ref.dtype)


def _swiglu_up(hn, w_gu, layer, tm, tn):
    rows, k = hn.shape
    nt = FFN_HIDDEN // tn
    return pl.pallas_call(
        _swiglu_up_kernel,
        out_shape=jax.ShapeDtypeStruct((rows, FFN_HIDDEN), BF16),
        grid=(nt, rows // tm),
        in_specs=[pl.BlockSpec((tm, k), lambda n, m: (m, 0)),
                  pl.BlockSpec((None, k, tn), lambda n, m: (layer, 0, n)),
                  pl.BlockSpec((None, k, tn), lambda n, m: (layer, 0, nt + n))],
        out_specs=pl.BlockSpec((tm, tn), lambda n, m: (m, n)),
        scratch_shapes=[pltpu.VMEM((k, tn), BF16), pltpu.VMEM((k, tn), BF16)],
        compiler_params=_cparams(("arbitrary", "arbitrary")),
        name="swiglu_up",
    )(hn, w_gu, w_gu)


def _down_proj_kernel(a_ref, w_ref, h_ref, wpost_ref, wnext_ref, *rest, emit_hn):
    if emit_hn:
        hnew_ref, hn_ref, acc_ref = rest
    else:
        hnew_ref, acc_ref = rest
        hn_ref = None
    kk = pl.program_id(1)
    last = pl.num_programs(1) - 1

    @pl.when(kk == 0)
    def _():
        acc_ref[...] = jnp.dot(a_ref[...], w_ref[...], preferred_element_type=F32)

    @pl.when(jnp.logical_and(kk > 0, kk < last))
    def _():
        acc_ref[...] += jnp.dot(a_ref[...], w_ref[...], preferred_element_type=F32)

    @pl.when(kk == last)
    def _():
        for rs in _row_subtiles(a_ref.shape[0]):
            acc = acc_ref[rs, :] + jnp.dot(a_ref[rs, :], w_ref[...], preferred_element_type=F32)
            _residual_epilogue(acc, h_ref.at[rs, :], wpost_ref, wnext_ref, hnew_ref.at[rs, :],
                               None if hn_ref is None else hn_ref.at[rs, :])


def _down_proj(a, w_down, h, w_post, w_next, rows_out, tm, tk, emit_hn):
    out_shape = [jax.ShapeDtypeStruct((rows_out, D_MODEL), F32)]
    out_specs = [pl.BlockSpec((tm, D_MODEL), lambda m, k: (m, 0))]
    if emit_hn:
        out_shape.append(jax.ShapeDtypeStruct((rows_out, D_MODEL), BF16))
        out_specs.append(pl.BlockSpec((tm, D_MODEL), lambda m, k: (m, 0)))
    res = pl.pallas_call(
        functools.partial(_down_proj_kernel, emit_hn=emit_hn),
        out_shape=tuple(out_shape),
        grid=(rows_out // tm, FFN_HIDDEN // tk),
        in_specs=[pl.BlockSpec((tm, tk), lambda m, k: (m, k)),
                  pl.BlockSpec((tk, D_MODEL), lambda m, k: (k, 0)),
                  pl.BlockSpec((tm, D_MODEL), lambda m, k: (m, 0)),
                  pl.BlockSpec((1, D_MODEL), lambda m, k: (0, 0)),
                  pl.BlockSpec((1, D_MODEL), lambda m, k: (0, 0))],
        out_specs=tuple(out_specs),
        scratch_shapes=[pltpu.VMEM((tm, D_MODEL), F32)],
        compiler_params=_cparams(("arbitrary", "arbitrary")),
        name="down_proj",
    )(a, w_down, h, w_post, w_next)
    return res if emit_hn else (res[0], None)


def _row_block(i, nreal):
    return (i + nreal) % (nreal + 1)


def _zspec(col_block, nreal):
    return pl.BlockSpec((BLK, WIDTH), lambda i: (_row_block(i, nreal), col_block))


def _pspec(shape):
    nd = len(shape)
    return pl.BlockSpec(shape, lambda *_: (0,) * nd)


def _lru_kernel(u_ref, gate_ref, cw_ref, cb_ref, wa_ref, ba_ref, wx_ref, bx_ref, lam_ref, y_ref,
                ubuf, hst, hist_meta, h_meta, *, nblk):
    i = pl.program_id(0)
    is_meta = i == 0

    @pl.when(is_meta)
    def _():
        ubuf[0:8, :] = jnp.zeros((8, WIDTH), F32)
        hst[...] = jnp.zeros_like(hst)

    @pl.when(jnp.logical_and(i >= 1, (i - 1) % nblk == 0))
    def _():
        ubuf[0:8, :] = hist_meta[...]
        hst[...] = h_meta[...]

    u = u_ref[...]
    ubuf[8:8 + BLK, :] = u
    cw = cw_ref[...]
    xc = (cb_ref[...] + cw[3:4] * u + cw[2:3] * ubuf[7:7 + BLK, :]
          + cw[1:2] * ubuf[6:6 + BLK, :] + cw[0:1] * ubuf[5:5 + BLK, :])
    xb = xc.astype(BF16)
    ra, ia = [], []
    for hd in range(HEADS):
        sl = slice(hd * HEAD_W, (hd + 1) * HEAD_W)
        ra.append(jnp.dot(xb[:, sl], wa_ref[hd].astype(BF16), preferred_element_type=F32))
        ia.append(jnp.dot(xb[:, sl], wx_ref[hd].astype(BF16), preferred_element_type=F32))
    r = _sigmoid(jnp.concatenate(ra, axis=1) + ba_ref[...])
    ig = _sigmoid(jnp.concatenate(ia, axis=1) + bx_ref[...])
    lam = lam_ref[...]
    softplus_neg_lam = jnp.maximum(-lam, 0.0) + jnp.log1p(jnp.exp(-jnp.abs(lam)))
    log_a = -LRU_C * r * softplus_neg_lam
    a = jnp.exp(log_a)
    bb = jnp.sqrt(-jnp.tanh(log_a) * (a * a + 1.0)) * (ig * xc)
    row = lax.broadcasted_iota(jnp.int32, (BLK, 1), 0)
    bb = jnp.where(row >= PAD_ROWS * is_meta.astype(jnp.int32), bb, 0.0)

    acum, bcum = a, bb
    s = 1
    while s < BLK:
        a_sh = _shift_rows(acum, s, 1.0, row)
        b_sh = _shift_rows(bcum, s, 0.0, row)
        bcum = acum * b_sh + bcum
        acum = acum * a_sh
        s *= 2
    h = acum * hst[0:1, :] + bcum
    y_ref[...] = (h * _gelu_tanh(gate_ref[...])).astype(y_ref.dtype)

    hist = u[BLK - 8:BLK, :]
    hlast = jnp.broadcast_to(h[BLK - 1:BLK, :], (8, WIDTH))
    ubuf[0:8, :] = hist
    hst[...] = hlast

    @pl.when(is_meta)
    def _():
        hist_meta[...] = hist
        h_meta[...] = hlast


def _lru(z, cw, cb, wa, ba, wx, bx, lam, nreal, nblk):
    rows = z.shape[0]
    return pl.pallas_call(
        functools.partial(_lru_kernel, nblk=nblk),
        out_shape=jax.ShapeDtypeStruct((rows, WIDTH), BF16),
        grid=(nreal + 1,),
        in_specs=[_zspec(0, nreal), _zspec(1, nreal),
                  _pspec((4, WIDTH)), _pspec((1, WIDTH)),
                  _pspec((HEADS, HEAD_W, HEAD_W)), _pspec((1, WIDTH)),
                  _pspec((HEADS, HEAD_W, HEAD_W)), _pspec((1, WIDTH)), _pspec((1, WIDTH))],
        out_specs=pl.BlockSpec((BLK, WIDTH), lambda i: (_row_block(i, nreal), 0)),
        scratch_shapes=[pltpu.VMEM((8 + BLK, WIDTH), F32), pltpu.VMEM((8, WIDTH), F32),
                        pltpu.VMEM((8, WIDTH), F32), pltpu.VMEM((8, WIDTH), F32)],
        compiler_params=_cparams(("arbitrary",)),
        name="rg_lru",
    )(z, z, cw, cb, wa, ba, wx, bx, lam)


POOL_HIST = 16


def _pool_kernel(u_ref, pw_ref, ps_ref, y_ref, ubuf, hist_meta, *, nblk):
    i = pl.program_id(0)
    is_meta = i == 0

    @pl.when(is_meta)
    def _():
        ubuf[0:POOL_HIST, :] = jnp.zeros((POOL_HIST, WIDTH), F32)

    @pl.when(jnp.logical_and(i >= 1, (i - 1) % nblk == 0))
    def _():
        ubuf[0:POOL_HIST, :] = hist_meta[...]

    u = u_ref[...]
    ubuf[POOL_HIST:POOL_HIST + BLK, :] = u
    row = lax.broadcasted_iota(jnp.int32, (BLK, 1), 0)
    meta_i = is_meta.astype(jnp.int32)
    pos1 = row + 1 - PAD_ROWS * meta_i + 2 * POOL_HIST * (1 - meta_i)
    outs = []
    for g, win in enumerate(POOL_WINDOWS):
        sl = slice(g * HEAD_W, (g + 1) * HEAD_W)
        acc = u[:, sl]
        for d in range(1, win):
            acc = acc + ubuf[POOL_HIST - d:POOL_HIST - d + BLK, sl]
        count = jnp.clip(pos1, 1, win).astype(F32)
        pooled = acc / count - u[:, sl]
        outs.append(jnp.dot(pooled.astype(BF16), pw_ref[g].astype(BF16), preferred_element_type=F32))
    y_ref[...] = (jnp.concatenate(outs, axis=1) * ps_ref[...]).astype(y_ref.dtype)

    hist = u[BLK - POOL_HIST:BLK, :]
    ubuf[0:POOL_HIST, :] = hist

    @pl.when(is_meta)
    def _():
        hist_meta[...] = hist


def _pool(z, pw, ps, nreal, nblk):
    rows = z.shape[0]
    return pl.pallas_call(
        functools.partial(_pool_kernel, nblk=nblk),
        out_shape=jax.ShapeDtypeStruct((rows, WIDTH), BF16),
        grid=(nreal + 1,),
        in_specs=[_zspec(2, nreal), _pspec((4, HEAD_W, HEAD_W)), _pspec((1, WIDTH))],
        out_specs=pl.BlockSpec((BLK, WIDTH), lambda i: (_row_block(i, nreal), 0)),
        scratch_shapes=[pltpu.VMEM((POOL_HIST + BLK, WIDTH), F32), pltpu.VMEM((POOL_HIST, WIDTH), F32)],
        compiler_params=_cparams(("arbitrary",)),
        name="ms_pool",
    )(z, pw, ps)


def _hgrn_chunk(q, z, v, lbh, state_t, valid, ones_bf):
    ls = _log_sigmoid(z)
    x1 = jnp.log(lbh)
    x2 = jnp.log1p(-lbh) + ls
    mx = jnp.maximum(x1, x2)
    g = mx + jnp.log1p(jnp.exp(-jnp.abs(x1 - x2)))
    k = (1.0 - lbh) * _sigmoid(-z)
    if valid is not None:
        g = jnp.where(valid, g, 0.0)
    row = lax.broadcasted_iota(jnp.int32, (CHUNK, 1), 0)
    b = g
    s = 1
    while s < CHUNK:
        b = b + _shift_rows(b, s, 0.0, row)
        s *= 2
    b_last = b[CHUNK - 1:CHUNK, :]

    qe = (q * jnp.exp(b)).astype(BF16)
    o = lax.dot_general(qe, state_t.astype(BF16), (((1,), (1,)), ((), ())), preferred_element_type=F32)

    col = lax.broadcasted_iota(jnp.int32, (SUB, CHUNK), 1)
    rsub = lax.broadcasted_iota(jnp.int32, (SUB, CHUNK), 0)
    lane = lax.broadcasted_iota(jnp.int32, (SUB, HEAD_W), 1)
    s_rows = []
    for blk in range(CHUNK // SUB):
        lo = blk * SUB
        bi = b[lo:lo + SUB, :]
        qi = q[lo:lo + SUB, :]
        ki = k[lo:lo + SUB, :]
        parts = []
        for sr in range(SUB):
            e = jnp.exp(jnp.minimum(bi - bi[sr:sr + 1, :], 0.0))
            parts.append(qi * e * ki[sr:sr + 1, :])
        m3 = jnp.concatenate(parts, axis=0).astype(BF16)
        red = jnp.dot(m3, ones_bf, preferred_element_type=F32)
        diag = jnp.zeros((SUB, HEAD_W), F32)
        for sr in range(SUB):
            diag = diag + jnp.where(lane == lo + sr, red[sr * SUB:(sr + 1) * SUB, :], 0.0)
        diag = diag[:, :CHUNK]
        s_blk = jnp.where(jnp.logical_and(col >= lo, col - lo <= rsub), diag, 0.0)
        if blk > 0:
            b0 = b[lo - 1:lo, :]
            kt = (k * jnp.exp(jnp.minimum(b0 - b, 0.0))).astype(BF16)
            qd = (qi * jnp.exp(bi - b0)).astype(BF16)
            off = lax.dot_general(qd, kt, (((1,), (1,)), ((), ())), preferred_element_type=F32)
            s_blk = jnp.where(col < lo, off, s_blk)
        s_rows.append(s_blk)
    scores = jnp.concatenate(s_rows, axis=0).astype(BF16)
    vb = v.astype(BF16)
    o = o + jnp.dot(scores, vb, preferred_element_type=F32)

    kd = (k * jnp.exp(b_last - b)).astype(BF16)
    upd = lax.dot_general(vb, kd, (((0,), (0,)), ((), ())), preferred_element_type=F32)
    new_state_t = state_t * jnp.exp(b_last) + upd
    return o, new_state_t


def _hgrn_kernel(q_ref, f_ref, v_ref, og_ref, lb_ref, nw_ref, y_ref, state, state_meta, *, nblk):
    i = pl.program_id(0)
    is_meta = i == 0

    @pl.when(is_meta)
    def _():
        state[...] = jnp.zeros_like(state)

    @pl.when(jnp.logical_and(i >= 1, (i - 1) % nblk == 0))
    def _():
        state[...] = state_meta[...]

    ones_bf = jnp.ones((HEAD_W, HEAD_W), BF16)
    nw = nw_ref[...]
    for hd in range(HEADS):
        sl = slice(hd * HEAD_W, (hd + 1) * HEAD_W)
        lbh = lb_ref[:, sl]
        st = state[hd]
        for c in range(BLK // CHUNK):
            rs = slice(c * CHUNK, (c + 1) * CHUNK)
            rowg = lax.broadcasted_iota(jnp.int32, (CHUNK, 1), 0) + c * CHUNK
            valid = rowg >= PAD_ROWS * is_meta.astype(jnp.int32)
            o, st = _hgrn_chunk(q_ref[rs, sl], f_ref[rs, sl], v_ref[rs, sl], lbh, st, valid, ones_bf)
            og = og_ref[rs, sl]
            y_ref[rs, sl] = (_rms(o, nw) * (og * _sigmoid(og))).astype(y_ref.dtype)
        state[hd] = st

    @pl.when(is_meta)
    def _():
        state_meta[...] = state[...]


def _hgrn(z_a, z_b, lb, nw, nreal, nblk):
    rows = z_a.shape[0]
    return pl.pallas_call(
        functools.partial(_hgrn_kernel, nblk=nblk),
        out_shape=jax.ShapeDtypeStruct((rows, WIDTH), BF16),
        grid=(nreal + 1,),
        in_specs=[_zspec(3, nreal), _zspec(0, nreal), _zspec(1, nreal), _zspec(2, nreal),
                  _pspec((1, WIDTH)), _pspec((1, HEAD_W))],
        out_specs=pl.BlockSpec((BLK, WIDTH), lambda i: (_row_block(i, nreal), 0)),
        scratch_shapes=[pltpu.VMEM((HEADS, HEAD_W, HEAD_W), F32), pltpu.VMEM((HEADS, HEAD_W, HEAD_W), F32)],
        compiler_params=_cparams(("arbitrary",)),
        name="hgrn2",
    )(z_a, z_b, z_b, z_b, lb, nw)


def _attn_lambda(lamv_ref, lambda_init):
    lv = lamv_ref[...]
    return (jnp.exp(jnp.sum(lv[0:1] * lv[1:2], axis=-1, keepdims=True))
            - jnp.exp(jnp.sum(lv[2:3] * lv[3:4], axis=-1, keepdims=True)) + lambda_init)


def _scaled_q(q_ref):
    qf = (q_ref[...].astype(F32) * (DIFF_HEAD_DIM ** -0.5)).astype(BF16)
    return qf[:, :DIFF_HEAD_DIM], qf[:, DIFF_HEAD_DIM:]


def _fold8(x, op):
    r, c = x.shape
    return op(x.reshape(r // 8, 8, c), axis=0)


def _attn_kernel(relb_ref, q_ref, k_ref, v_ref, km_ref, vm_ref, bias_ref, diag_ref, lamv_ref, sw_ref, y_ref,
                 s_sc, vt_sc, vtm_sc, m_sc, l_sc, acc_sc, *, nqb, nblk, lambda_init):
    hd = pl.program_id(0)
    jq = pl.program_id(1) % nqb
    far = relb_ref[REL_BUCKETS - 1, hd]
    t1 = bias_ref[1]
    slot_diag, slot_meta = nqb - 1, nqb

    @pl.when(jq == 0)
    def _():
        for t in range(nblk):
            vt_sc[t // NSUB, :, (t % NSUB) * BLK:(t % NSUB + 1) * BLK] = (
                v_ref[t * BLK:(t + 1) * BLK, :].astype(F32).T.astype(BF16))
        vtm_sc[...] = vm_ref[...].astype(F32).T.astype(BF16)

    qt = (q_ref[...].astype(F32) * (DIFF_HEAD_DIM ** -0.5)).T
    zero = jnp.zeros((DIFF_HEAD_DIM, QB), F32)
    qtp = (jnp.concatenate([qt[:DIFF_HEAD_DIM], zero], axis=0).astype(BF16),
           jnp.concatenate([zero, qt[DIFF_HEAD_DIM:]], axis=0).astype(BF16))

    def key_chunk(c):
        return k_ref[pl.ds(pl.multiple_of(c * QB, QB), QB), :]

    def score_chunk(k_rows, slot, add_bias):
        r = k_rows.shape[0]
        for mp in range(2):
            s = add_bias(jnp.dot(k_rows, qtp[mp], preferred_element_type=F32))
            s_sc[mp, slot, 0:r, :] = s
            m_sc[mp] = jnp.maximum(m_sc[mp], _fold8(s, jnp.max))

    def value_chunk(vt_cols, slot, m8):
        r = vt_cols.shape[1]
        for mp in range(2):
            p = jnp.exp(s_sc[mp, slot, 0:r, :].reshape(r // 8, 8, QB) - m8[mp][None])
            l_sc[mp] += jnp.sum(p, axis=0)
            acc_sc[mp] += jnp.dot(vt_cols, p.reshape(r, QB).astype(BF16), preferred_element_type=F32)

    def walk(fn_far, fn_meta, fn_diag):
        def body(c, carry):
            fn_far(c)
            return carry

        lax.fori_loop(0, jq, body, 0)
        fn_meta()
        fn_diag()

    near = t1 - far
    krow = lax.broadcasted_iota(jnp.int32, (BLK, 1), 0)
    meta_rows = jnp.where(krow >= PAD_ROWS, far, NEG)
    first = jnp.full((1, BLK), jq, jnp.int32) == 0

    def meta_bias(s):
        head = s[:, 0:BLK] + (meta_rows + jnp.where(first, near, 0.0))
        return jnp.concatenate([head, s[:, BLK:] + meta_rows], axis=1)

    m_sc[...] = jnp.full_like(m_sc, NEG)
    walk(lambda c: score_chunk(key_chunk(c), c, lambda s: s + far),
         lambda: score_chunk(km_ref[...], slot_meta, meta_bias),
         lambda: score_chunk(key_chunk(jq), slot_diag, lambda s: s + diag_ref[...]))

    @pl.when(jq >= 1)
    def _():
        for mp in range(2):
            fixed = s_sc[mp, jq - 1, QB - BLK:QB, 0:BLK] + near
            s_sc[mp, jq - 1, QB - BLK:QB, 0:BLK] = fixed
            m_sc[mp, :, 0:BLK] = jnp.maximum(m_sc[mp, :, 0:BLK], _fold8(fixed, jnp.max))

    m8 = [jnp.broadcast_to(jnp.max(m_sc[mp], axis=0, keepdims=True), (8, QB)) for mp in range(2)]
    l_sc[...] = jnp.zeros_like(l_sc)
    acc_sc[...] = jnp.zeros_like(acc_sc)

    walk(lambda c: value_chunk(vt_sc[c], c, m8),
         lambda: value_chunk(vtm_sc[...], slot_meta, m8),
         lambda: value_chunk(vt_sc[jq], slot_diag, m8))

    lam = _attn_lambda(lamv_ref, lambda_init)
    l0 = jnp.sum(l_sc[0], axis=0, keepdims=True)
    l1 = jnp.sum(l_sc[1], axis=0, keepdims=True)
    ot = acc_sc[0] / l0 - lam * (acc_sc[1] / l1)
    ot = ot * lax.rsqrt(jnp.mean(ot * ot, axis=0, keepdims=True) + EPS)
    y_ref[...] = (ot.T * sw_ref[...] * (1.0 - lambda_init)).astype(y_ref.dtype)


def _attn_meta_kernel(q_ref, k_ref, v_ref, bias_ref, lamv_ref, sw_ref, ymain_ref, y_ref, *, lambda_init):
    del ymain_ref
    qs = _scaled_q(q_ref)
    kb = k_ref[...]
    colmask = jnp.where(lax.broadcasted_iota(jnp.int32, (1, BLK), 1) >= PAD_ROWS, 0.0, NEG)
    outs = []
    bias = bias_ref[0].T + colmask
    for mp in range(2):
        s = lax.dot_general(qs[mp], kb[:, mp * DIFF_HEAD_DIM:(mp + 1) * DIFF_HEAD_DIM],
                            (((1,), (1,)), ((), ())), preferred_element_type=F32) + bias
        p = jnp.exp(s - jnp.max(s, axis=-1, keepdims=True))
        acc = jnp.dot(p.astype(BF16), v_ref[...], preferred_element_type=F32)
        outs.append(acc / jnp.sum(p, axis=-1, keepdims=True))
    o = outs[0] - _attn_lambda(lamv_ref, lambda_init) * outs[1]
    o = _rms(o, sw_ref[...]) * (1.0 - lambda_init)
    row = lax.broadcasted_iota(jnp.int32, (BLK, 1), 0)
    y_ref[...] = jnp.where(row >= PAD_ROWS, o, 0.0).astype(y_ref.dtype)


def _attn(zd, rel_bias, bias_tiles, bias_diag, lam_vec, subln_w, nreal, nblk, lambda_init):
    rows = zd.shape[0]
    nqb = nblk * BLK // QB
    nq_total = nreal * BLK // QB
    seq = nblk * BLK
    kcol, vcol = HEADS, 2 * HEADS
    y_main = pl.pallas_call(
        functools.partial(_attn_kernel, nqb=nqb, nblk=nblk, lambda_init=lambda_init),
        out_shape=jax.ShapeDtypeStruct((rows, WIDTH), BF16),
        grid=(HEADS, nq_total),
        in_specs=[pl.BlockSpec(memory_space=pltpu.SMEM),
                  pl.BlockSpec((QB, HEAD_W), lambda h, i: (i, h)),
                  pl.BlockSpec((seq, HEAD_W), lambda h, i: (i // nqb, kcol + h)),
                  pl.BlockSpec((seq, HEAD_W), lambda h, i: (i // nqb, vcol + h)),
                  pl.BlockSpec((BLK, HEAD_W), lambda h, i: (nreal, kcol + h)),
                  pl.BlockSpec((BLK, HEAD_W), lambda h, i: (nreal, vcol + h)),
                  pl.BlockSpec((None, 2, BLK, BLK), lambda h, i: (h, 0, 0, 0)),
                  pl.BlockSpec((None, QB, QB), lambda h, i: (h, 0, 0)),
                  pl.BlockSpec((4, DIFF_HEAD_DIM), lambda h, i: (0, 0)),
                  pl.BlockSpec((1, HEAD_W), lambda h, i: (0, 0))],
        out_specs=pl.BlockSpec((QB, HEAD_W), lambda h, i: (i, h)),
        scratch_shapes=[pltpu.VMEM((2, nqb + 1, QB, QB), F32),
                        pltpu.VMEM((nqb, HEAD_W, QB), BF16),
                        pltpu.VMEM((HEAD_W, BLK), BF16),
                        pltpu.VMEM((2, 8, QB), F32),
                        pltpu.VMEM((2, 8, QB), F32),
                        pltpu.VMEM((2, HEAD_W, QB), F32)],
        compiler_params=_cparams(("arbitrary", "arbitrary")),
        name="diff_attn",
    )(rel_bias, zd, zd, zd, zd, zd, bias_tiles, bias_diag, lam_vec, subln_w)
    return pl.pallas_call(
        functools.partial(_attn_meta_kernel, lambda_init=lambda_init),
        out_shape=jax.ShapeDtypeStruct((rows, WIDTH), BF16),
        grid=(HEADS,),
        in_specs=[pl.BlockSpec((BLK, HEAD_W), lambda h: (nreal, h)),
                  pl.BlockSpec((BLK, HEAD_W), lambda h: (nreal, kcol + h)),
                  pl.BlockSpec((BLK, HEAD_W), lambda h: (nreal, vcol + h)),
                  pl.BlockSpec((None, 2, BLK, BLK), lambda h: (h, 0, 0, 0)),
                  pl.BlockSpec((4, DIFF_HEAD_DIM), lambda h: (0, 0)),
                  pl.BlockSpec((1, HEAD_W), lambda h: (0, 0)),
                  pl.BlockSpec(memory_space=pl.ANY)],
        out_specs=pl.BlockSpec((BLK, HEAD_W), lambda h: (nreal, h)),
        input_output_aliases={6: 0},
        compiler_params=_cparams(("arbitrary",)),
        name="diff_attn_meta",
    )(zd, zd, zd, bias_tiles, lam_vec, subln_w, y_main)


def _largest_tile(rows, cap, align=16):
    best = align
    for t in range(align, cap + 1, align):
        if rows % t == 0:
            best = t
    return best


def _forward(x, meta_tokens, rel_bias, hgrn_lower_bounds, norm_mix_pre, norm_mix_post, norm_ffn_pre,
             norm_ffn_post, w_in, lru_conv_w, lru_conv_b, lru_w_a, lru_b_a, lru_w_x, lru_b_x, lru_lambda,
             pool_w, pool_scale, hgrn_norm, diff_lambda, diff_subln, w_branch, w_out, ffn_w_gu, ffn_w_down):
    bsz, seq, _ = x.shape
    nblk = seq // BLK
    nreal = bsz * nblk
    rows = (nreal + 1) * BLK
    rows_real = nreal * BLK
    tm_big = _largest_tile(rows, 640)
    tm_epi = _largest_tile(rows, 640)
    tm_last = _largest_tile(rows_real, 512)

    def vec(a):
        return a.reshape(1, -1)

    lbs, bias_tiles, bias_diag = _prologue(hgrn_lower_bounds, rel_bias)
    h, hn = _embed(x.reshape(rows_real, D_MODEL), meta_tokens, vec(norm_mix_pre[0]), nreal)

    for layer in range(DEPTH):
        lambda_init = 0.8 - 0.6 * math.exp(-0.3 * layer)
        z = _mixer_in_proj(hn, w_in, layer, 0, 4 * WIDTH, F32, tm_big, 1024, 1024)
        z_b = _mixer_in_proj(hn, w_in, layer, 4 * WIDTH, 3 * WIDTH, F32, tm_big, 3 * WIDTH, WIDTH)
        zd = _mixer_in_proj(hn, w_in, layer, 7 * WIDTH, 3 * WIDTH, BF16, tm_big, 3 * WIDTH, WIDTH)
        y_a = _lru(z, lru_conv_w[layer], vec(lru_conv_b[layer]), lru_w_a[layer], vec(lru_b_a[layer]),
                   lru_w_x[layer], vec(lru_b_x[layer]), vec(lru_lambda[layer]), nreal, nblk)
        y_b = _pool(z, pool_w[layer], vec(pool_scale[layer]), nreal, nblk)
        y_c = _hgrn(z, z_b, lbs[layer:layer + 1], vec(hgrn_norm[layer]), nreal, nblk)
        y_d = _attn(zd, rel_bias, bias_tiles, bias_diag, diff_lambda[layer], vec(diff_subln[layer]), nreal, nblk, lambda_init)
        merged = _gate_merge(hn, w_in, (y_a, y_b, y_c, y_d), w_branch, layer, tm_big, 256)
        h, hn = _out_proj(merged, _cast_layer_bf16(w_out, layer, 512), h, vec(norm_mix_post[layer]),
                          vec(norm_ffn_pre[layer]), tm_epi)
        a = _swiglu_up(hn, ffn_w_gu, layer, tm_big, 512)
        last = layer == DEPTH - 1
        w_next = vec(norm_mix_pre[layer + 1]) if not last else vec(norm_mix_pre[layer])
        h, hn = _down_proj(a, _cast_layer_bf16(ffn_w_down, layer, 512), h, vec(norm_ffn_post[layer]), w_next,
                           rows_real if last else rows, tm_last if last else tm_epi, 1408, not last)
    return h.reshape(bsz, seq, D_MODEL)


def kernel(x, meta_tokens, rel_bias, hgrn_lower_bounds, norm_mix_pre, norm_mix_post, norm_ffn_pre, norm_ffn_post, w_in, lru_conv_w, lru_conv_b, lru_w_a, lru_b_a, lru_w_x, lru_b_x, lru_lambda, pool_w, pool_scale, hgrn_norm, diff_lambda, diff_subln, w_branch, w_out, ffn_w_gu, ffn_w_down):
    return _forward(x, meta_tokens, rel_bias, hgrn_lower_bounds, norm_mix_pre, norm_mix_post, norm_ffn_pre,
                    norm_ffn_post, w_in, lru_conv_w, lru_conv_b, lru_w_a, lru_b_a, lru_w_x, lru_b_x, lru_lambda,
                    pool_w, pool_scale, hgrn_norm, diff_lambda, diff_subln, w_branch, w_out, ffn_w_gu, ffn_w_down)
```

```python
import functools
import math

import numpy as np
import jax
import jax.numpy as jnp
from jax import lax
from jax.experimental import pallas as pl
from jax.experimental.pallas import tpu as pltpu

F32 = jnp.float32
BF16 = jnp.bfloat16

D_MODEL = 2048
SEQ = 2048
DEPTH = 2
N_META = 16
BLK = 128
PAD_ROWS = BLK - N_META
QB = 512
NSUB = QB // BLK
WIDTH = 512
HEADS = 4
HEAD_W = 128
CHUNK = 64
SUB = 16
LRU_C = 8.0
POOL_WINDOWS = (2, 4, 8, 16)
DIFF_HEAD_DIM = 64
REL_BUCKETS = 32
REL_MAX_DIST = 128
FFN_HIDDEN = 5632
MIX_COLS = 10 * WIDTH
NEG = -1e30
EPS = 1e-6
VMEM_LIMIT = 56 * 1024 * 1024


def _cparams(sem):
    return pltpu.CompilerParams(dimension_semantics=sem, vmem_limit_bytes=VMEM_LIMIT)


def _rms(x, w):
    return x * lax.rsqrt(jnp.mean(x * x, axis=-1, keepdims=True) + EPS) * w


def _log_sigmoid(z):
    return -(jnp.maximum(-z, 0.0) + jnp.log1p(jnp.exp(-jnp.abs(z))))


def _sigmoid(z):
    return 1.0 / (1.0 + jnp.exp(-z))


def _gelu_tanh(x):
    c = math.sqrt(2.0 / math.pi)
    return 0.5 * x * (1.0 + jnp.tanh(c * (x + 0.044715 * (x * x * x))))


def _shift_rows(x, s, fill, row):
    return jnp.where(row >= s, pltpu.roll(x, s, axis=0), fill)


def _bucket_tiles():
    r = np.arange(BLK)[None, :]
    c = np.arange(BLK)[:, None]
    max_exact = REL_BUCKETS // 2

    def bucket(n):
        nf = np.maximum(n, 1).astype(np.float32)
        large = max_exact + (np.log(nf / np.float32(max_exact)) / np.float32(math.log(REL_MAX_DIST / max_exact))
                             * np.float32(REL_BUCKETS - max_exact)).astype(np.int32)
        large = np.minimum(large, REL_BUCKETS - 1)
        return np.where(n < max_exact, n, large).astype(np.int32)

    d0 = r - c
    t0 = np.where(d0 >= 0, bucket(np.maximum(d0, 0)), -1)
    t1 = bucket(BLK + r - c)
    return np.stack([t0, t1]).astype(np.int32)


def _prologue_kernel(lbraw_ref, relb_ref, idx_ref, lb_ref, bias_ref, diag_ref):
    raw = lbraw_ref[...]
    mx = jnp.max(raw, axis=0, keepdims=True)
    e = jnp.exp(raw - mx)
    sm = e / jnp.sum(e, axis=0, keepdims=True)
    cum = sm[0:1]
    lb_ref[0:1, :] = cum - sm[0:1]
    for l in range(1, DEPTH):
        cum = cum + sm[l:l + 1]
        lb_ref[l:l + 1, :] = cum - sm[0:1]
    for t in range(2):
        idx = idx_ref[t]
        for hd in range(HEADS):
            acc = jnp.zeros((BLK, BLK), F32)
            for bk in range(REL_BUCKETS):
                acc = jnp.where(idx == bk, relb_ref[bk, hd], acc)
            bias_ref[hd, t] = jnp.where(idx < 0, NEG, acc)
    for hd in range(HEADS):
        far = relb_ref[REL_BUCKETS - 1, hd]
        for kb in range(NSUB):
            for qb in range(NSUB):
                delta = qb - kb
                if delta == 0:
                    blk = bias_ref[hd, 0]
                elif delta == 1:
                    blk = bias_ref[hd, 1]
                else:
                    blk = jnp.full((BLK, BLK), far if delta > 1 else NEG, F32)
                diag_ref[hd, kb * BLK:(kb + 1) * BLK, qb * BLK:(qb + 1) * BLK] = blk


def _prologue(hgrn_lower_bounds, rel_bias):
    idx = jnp.asarray(_bucket_tiles())
    vmem = pl.BlockSpec(memory_space=pltpu.VMEM)
    return pl.pallas_call(
        _prologue_kernel,
        out_shape=(jax.ShapeDtypeStruct((DEPTH, WIDTH), F32),
                   jax.ShapeDtypeStruct((HEADS, 2, BLK, BLK), F32),
                   jax.ShapeDtypeStruct((HEADS, QB, QB), F32)),
        in_specs=[vmem, pl.BlockSpec(memory_space=pltpu.SMEM), vmem],
        out_specs=(vmem, vmem, vmem),
        name="prologue",
    )(hgrn_lower_bounds, rel_bias, idx)


def _embed_kernel(x_ref, meta_ref, w_ref, h_ref, hn_ref, *, nreal):
    i = pl.program_id(0)

    @pl.when(i < nreal)
    def _():
        h_ref[...] = x_ref[...]

    @pl.when(i == nreal)
    def _():
        h_ref[0:PAD_ROWS, :] = jnp.zeros((PAD_ROWS, D_MODEL), F32)
        h_ref[PAD_ROWS:BLK, :] = meta_ref[...]

    hn_ref[...] = _rms(h_ref[...], w_ref[...]).astype(BF16)


def _embed(x2d, meta, w_pre, nreal):
    rows = (nreal + 1) * BLK
    return pl.pallas_call(
        functools.partial(_embed_kernel, nreal=nreal),
        out_shape=(jax.ShapeDtypeStruct((rows, D_MODEL), F32),
                   jax.ShapeDtypeStruct((rows, D_MODEL), BF16)),
        grid=(nreal + 1,),
        in_specs=[pl.BlockSpec((BLK, D_MODEL), lambda i: (jnp.minimum(i, nreal - 1), 0)),
                  pl.BlockSpec((N_META, D_MODEL), lambda i: (0, 0)),
                  pl.BlockSpec((1, D_MODEL), lambda i: (0, 0))],
        out_specs=(pl.BlockSpec((BLK, D_MODEL), lambda i: (i, 0)),
                   pl.BlockSpec((BLK, D_MODEL), lambda i: (i, 0))),
        compiler_params=_cparams(("arbitrary",)),
        name="embed",
    )(x2d, meta, w_pre)


def _cast_tiles_once(pairs):
    @pl.when(pl.program_id(1) == 0)
    def _():
        for src, dst in pairs:
            dst[...] = src[...].astype(BF16)


def _matmul_kernel(x_ref, *rest):
    *w_refs, o_ref, wbf = rest
    wblk = w_refs[0].shape[-1]
    _cast_tiles_once([(w, wbf.at[:, j * wblk:(j + 1) * wblk]) for j, w in enumerate(w_refs)])
    o_ref[...] = jnp.dot(x_ref[...], wbf[...], preferred_element_type=F32).astype(o_ref.dtype)


def _mixer_in_proj(hn, w_in, layer, col0, ncols, out_dtype, tm, tn, wblk):
    rows, k = hn.shape
    nw = tn // wblk
    w_specs = [pl.BlockSpec((None, k, wblk), functools.partial(
        lambda n, m, j: (layer, 0, col0 // wblk + n * nw + j), j=j)) for j in range(nw)]
    return pl.pallas_call(
        _matmul_kernel,
        out_shape=jax.ShapeDtypeStruct((rows, ncols), out_dtype),
        grid=(ncols // tn, rows // tm),
        in_specs=[pl.BlockSpec((tm, k), lambda n, m: (m, 0))] + w_specs,
        out_specs=pl.BlockSpec((tm, tn), lambda n, m: (m, n)),
        scratch_shapes=[pltpu.VMEM((k, tn), BF16)],
        compiler_params=_cparams(("arbitrary", "arbitrary")),
        name="mixer_in_proj",
    )(hn, *([w_in] * nw))


def _cast_kernel(w_ref, o_ref):
    o_ref[...] = w_ref[...].astype(BF16)


def _cast_layer_bf16(w, layer, tr, col0=0, ncols=None, tc=None):
    _, r, c = w.shape
    ncols = c if ncols is None else ncols
    tc = ncols if tc is None else tc
    return pl.pallas_call(
        _cast_kernel,
        out_shape=jax.ShapeDtypeStruct((r, ncols), BF16),
        grid=(r // tr, ncols // tc),
        in_specs=[pl.BlockSpec((None, tr, tc), lambda i, j: (layer, i, col0 // tc + j))],
        out_specs=pl.BlockSpec((tr, tc), lambda i, j: (i, j)),
        compiler_params=_cparams(("arbitrary", "arbitrary")),
        name="cast_bf16",
    )(w)


def _residual_epilogue(acc, h_ref, wpost_ref, wnext_ref, hnew_ref, hn_ref):
    h_new = h_ref[...] + _rms(acc, wpost_ref[...])
    hnew_ref[...] = h_new
    if hn_ref is not None:
        hn_ref[...] = _rms(h_new, wnext_ref[...]).astype(BF16)


EPI_ROWS = 160


def _row_subtiles(tm):
    sub = next(s for s in (EPI_ROWS, 128, 64, 32, 16) if tm % s == 0)
    return [slice(r, r + sub) for r in range(0, tm, sub)]


def _gate_merge_kernel(hn_ref, g0, g1, g2, g3, y0, y1, y2, y3, wb_ref, o_ref, gbf, wbbf):
    _cast_tiles_once([(g, gbf.at[k]) for k, g in enumerate((g0, g1, g2, g3))] + [(wb_ref, wbbf)])
    hn = hn_ref[...]
    acc = None
    for k, y_ref in enumerate((y0, y1, y2, y3)):
        gate = _sigmoid(jnp.dot(hn, gbf[k], preferred_element_type=F32))
        proj = jnp.dot(y_ref[...], wbbf[k], preferred_element_type=F32)
        acc = gate * proj if acc is None else acc + gate * proj
    o_ref[...] = acc.astype(o_ref.dtype)


def _gate_merge(hn, w_in, ys, w_branch, layer, tm, tn):
    rows, k = hn.shape
    gate_specs = [
        pl.BlockSpec((None, k, tn), functools.partial(
            lambda n, m, base: (layer, 0, base + n), base=(MIX_COLS + br * D_MODEL) // tn))
        for br in range(4)]
    y_specs = [pl.BlockSpec((tm, WIDTH), lambda n, m: (m, 0)) for _ in range(4)]
    return pl.pallas_call(
        _gate_merge_kernel,
        out_shape=jax.ShapeDtypeStruct((rows, D_MODEL), BF16),
        grid=(D_MODEL // tn, rows // tm),
        in_specs=[pl.BlockSpec((tm, k), lambda n, m: (m, 0))] + gate_specs + y_specs
                 + [pl.BlockSpec((None, 4, WIDTH, tn), lambda n, m: (layer, 0, 0, n))],
        out_specs=pl.BlockSpec((tm, tn), lambda n, m: (m, n)),
        scratch_shapes=[pltpu.VMEM((4, k, tn), BF16), pltpu.VMEM((4, WIDTH, tn), BF16)],
        compiler_params=_cparams(("arbitrary", "arbitrary")),
        name="gate_merge",
    )(hn, w_in, w_in, w_in, w_in, *ys, w_branch)


def _out_proj_kernel(x_ref, w_ref, h_ref, wpost_ref, wnext_ref, hnew_ref, hn_ref):
    for rs in _row_subtiles(x_ref.shape[0]):
        acc = jnp.dot(x_ref[rs, :], w_ref[...], preferred_element_type=F32)
        _residual_epilogue(acc, h_ref.at[rs, :], wpost_ref, wnext_ref, hnew_ref.at[rs, :], hn_ref.at[rs, :])


def _out_proj(merged, w_out, h, w_post, w_next, tm):
    rows = h.shape[0]
    row_spec = pl.BlockSpec((tm, D_MODEL), lambda m: (m, 0))
    return pl.pallas_call(
        _out_proj_kernel,
        out_shape=(jax.ShapeDtypeStruct((rows, D_MODEL), F32),
                   jax.ShapeDtypeStruct((rows, D_MODEL), BF16)),
        grid=(rows // tm,),
        in_specs=[row_spec, _resident((D_MODEL, D_MODEL), (0, 0)), row_spec,
                  _pspec((1, D_MODEL)), _pspec((1, D_MODEL))],
        out_specs=(row_spec, row_spec),
        compiler_params=_cparams(("arbitrary",)),
        name="out_proj",
    )(merged, w_out, h, w_post, w_next)


def _swiglu_up_kernel(x_ref, wg_ref, wu_ref, o_ref, wgbf, wubf):
    _cast_tiles_once([(wg_ref, wgbf), (wu_ref, wubf)])
    x = x_ref[...]
    g = jnp.dot(x, wgbf[...], preferred_element_type=F32)
    u = jnp.dot(x, wubf[...], preferred_element_type=F32)
    o_ref[...] = (g * _sigmoid(g) * u).astype(o_ref.dtype)


def _swiglu_up(hn, w_gu, layer, tm, tn):
    rows, k = hn.shape
    nt = FFN_HIDDEN // tn
    return pl.pallas_call(
        _swiglu_up_kernel,
        out_shape=jax.ShapeDtypeStruct((rows, FFN_HIDDEN), BF16),
        grid=(nt, rows // tm),
        in_specs=[pl.BlockSpec((tm, k), lambda n, m: (m, 0)),
                  pl.BlockSpec((None, k, tn), lambda n, m: (layer, 0, n)),
                  pl.BlockSpec((None, k, tn), lambda n, m: (layer, 0, nt + n))],
        out_specs=pl.BlockSpec((tm, tn), lambda n, m: (m, n)),
        scratch_shapes=[pltpu.VMEM((k, tn), BF16), pltpu.VMEM((k, tn), BF16)],
        compiler_params=_cparams(("arbitrary", "arbitrary")),
        name="swiglu_up",
    )(hn, w_gu, w_gu)


def _down_proj_kernel(a_ref, w_ref, h_ref, wpost_ref, wnext_ref, hnew_ref, hn_ref=None):
    for rs in _row_subtiles(a_ref.shape[0]):
        acc = jnp.dot(a_ref[rs, :], w_ref[...], preferred_element_type=F32)
        _residual_epilogue(acc, h_ref.at[rs, :], wpost_ref, wnext_ref, hnew_ref.at[rs, :],
                           None if hn_ref is None else hn_ref.at[rs, :])


def _down_proj(a, w_down, h, w_post, w_next, rows_out, tm, emit_hn):
    row_spec = lambda w: pl.BlockSpec((tm, w), lambda m: (m, 0))
    out_shape = [jax.ShapeDtypeStruct((rows_out, D_MODEL), F32)]
    out_specs = [row_spec(D_MODEL)]
    if emit_hn:
        out_shape.append(jax.ShapeDtypeStruct((rows_out, D_MODEL), BF16))
        out_specs.append(row_spec(D_MODEL))
    res = pl.pallas_call(
        _down_proj_kernel,
        out_shape=tuple(out_shape),
        grid=(rows_out // tm,),
        in_specs=[row_spec(FFN_HIDDEN), _resident((FFN_HIDDEN, D_MODEL), (0, 0)), row_spec(D_MODEL),
                  _pspec((1, D_MODEL)), _pspec((1, D_MODEL))],
        out_specs=tuple(out_specs),
        compiler_params=_cparams(("arbitrary",)),
        name="down_proj",
    )(a, w_down, h, w_post, w_next)
    return res if emit_hn else (res[0], None)


def _row_block(i, nreal):
    return (i + nreal) % (nreal + 1)


def _zspec(col_block, nreal):
    return pl.BlockSpec((BLK, WIDTH), lambda i: (_row_block(i, nreal), col_block))


def _pspec(shape):
    nd = len(shape)
    return pl.BlockSpec(shape, lambda *_: (0,) * nd)


def _resident(shape, index):
    return pl.BlockSpec(shape, lambda *_: index, pipeline_mode=pl.Buffered(1))


def _lru_kernel(u_ref, gate_ref, cw_ref, cb_ref, wa_ref, ba_ref, wx_ref, bx_ref, lam_ref, y_ref,
                ubuf, hst, hist_meta, h_meta, *, nblk):
    i = pl.program_id(0)
    is_meta = i == 0

    @pl.when(is_meta)
    def _():
        ubuf[0:8, :] = jnp.zeros((8, WIDTH), F32)
        hst[...] = jnp.zeros_like(hst)

    @pl.when(jnp.logical_and(i >= 1, (i - 1) % nblk == 0))
    def _():
        ubuf[0:8, :] = hist_meta[...]
        hst[...] = h_meta[...]

    u = u_ref[...]
    ubuf[8:8 + BLK, :] = u
    cw = cw_ref[...]
    xc = (cb_ref[...] + cw[3:4] * u + cw[2:3] * ubuf[7:7 + BLK, :]
          + cw[1:2] * ubuf[6:6 + BLK, :] + cw[0:1] * ubuf[5:5 + BLK, :])
    xb = xc.astype(BF16)
    ra, ia = [], []
    for hd in range(HEADS):
        sl = slice(hd * HEAD_W, (hd + 1) * HEAD_W)
        ra.append(jnp.dot(xb[:, sl], wa_ref[hd].astype(BF16), preferred_element_type=F32))
        ia.append(jnp.dot(xb[:, sl], wx_ref[hd].astype(BF16), preferred_element_type=F32))
    r = _sigmoid(jnp.concatenate(ra, axis=1) + ba_ref[...])
    ig = _sigmoid(jnp.concatenate(ia, axis=1) + bx_ref[...])
    lam = lam_ref[...]
    softplus_neg_lam = jnp.maximum(-lam, 0.0) + jnp.log1p(jnp.exp(-jnp.abs(lam)))
    log_a = -LRU_C * r * softplus_neg_lam
    a = jnp.exp(log_a)
    bb = jnp.sqrt(-jnp.tanh(log_a) * (a * a + 1.0)) * (ig * xc)
    row = lax.broadcasted_iota(jnp.int32, (BLK, 1), 0)
    bb = jnp.where(row >= PAD_ROWS * is_meta.astype(jnp.int32), bb, 0.0)

    acum, bcum = a, bb
    s = 1
    while s < BLK:
        a_sh = _shift_rows(acum, s, 1.0, row)
        b_sh = _shift_rows(bcum, s, 0.0, row)
        bcum = acum * b_sh + bcum
        acum = acum * a_sh
        s *= 2
    h = acum * hst[0:1, :] + bcum
    y_ref[...] = (h * _gelu_tanh(gate_ref[...])).astype(y_ref.dtype)

    hist = u[BLK - 8:BLK, :]
    hlast = jnp.broadcast_to(h[BLK - 1:BLK, :], (8, WIDTH))
    ubuf[0:8, :] = hist
    hst[...] = hlast

    @pl.when(is_meta)
    def _():
        hist_meta[...] = hist
        h_meta[...] = hlast


def _lru(z, cw, cb, wa, ba, wx, bx, lam, nreal, nblk):
    rows = z.shape[0]
    return pl.pallas_call(
        functools.partial(_lru_kernel, nblk=nblk),
        out_shape=jax.ShapeDtypeStruct((rows, WIDTH), BF16),
        grid=(nreal + 1,),
        in_specs=[_zspec(0, nreal), _zspec(1, nreal),
                  _pspec((4, WIDTH)), _pspec((1, WIDTH)),
                  _pspec((HEADS, HEAD_W, HEAD_W)), _pspec((1, WIDTH)),
                  _pspec((HEADS, HEAD_W, HEAD_W)), _pspec((1, WIDTH)), _pspec((1, WIDTH))],
        out_specs=pl.BlockSpec((BLK, WIDTH), lambda i: (_row_block(i, nreal), 0)),
        scratch_shapes=[pltpu.VMEM((8 + BLK, WIDTH), F32), pltpu.VMEM((8, WIDTH), F32),
                        pltpu.VMEM((8, WIDTH), F32), pltpu.VMEM((8, WIDTH), F32)],
        compiler_params=_cparams(("arbitrary",)),
        name="rg_lru",
    )(z, z, cw, cb, wa, ba, wx, bx, lam)


POOL_HIST = 16


def _pool_kernel(u_ref, pw_ref, ps_ref, y_ref, ubuf, hist_meta, *, nblk):
    i = pl.program_id(0)
    is_meta = i == 0

    @pl.when(is_meta)
    def _():
        ubuf[0:POOL_HIST, :] = jnp.zeros((POOL_HIST, WIDTH), F32)

    @pl.when(jnp.logical_and(i >= 1, (i - 1) % nblk == 0))
    def _():
        ubuf[0:POOL_HIST, :] = hist_meta[...]

    u = u_ref[...]
    ubuf[POOL_HIST:POOL_HIST + BLK, :] = u
    row = lax.broadcasted_iota(jnp.int32, (BLK, 1), 0)
    meta_i = is_meta.astype(jnp.int32)
    pos1 = row + 1 - PAD_ROWS * meta_i + 2 * POOL_HIST * (1 - meta_i)
    outs = []
    for g, win in enumerate(POOL_WINDOWS):
        sl = slice(g * HEAD_W, (g + 1) * HEAD_W)
        acc = u[:, sl]
        for d in range(1, win):
            acc = acc + ubuf[POOL_HIST - d:POOL_HIST - d + BLK, sl]
        count = jnp.clip(pos1, 1, win).astype(F32)
        pooled = acc / count - u[:, sl]
        outs.append(jnp.dot(pooled.astype(BF16), pw_ref[g].astype(BF16), preferred_element_type=F32))
    y_ref[...] = (jnp.concatenate(outs, axis=1) * ps_ref[...]).astype(y_ref.dtype)

    hist = u[BLK - POOL_HIST:BLK, :]
    ubuf[0:POOL_HIST, :] = hist

    @pl.when(is_meta)
    def _():
        hist_meta[...] = hist


def _pool(z, pw, ps, nreal, nblk):
    rows = z.shape[0]
    return pl.pallas_call(
        functools.partial(_pool_kernel, nblk=nblk),
        out_shape=jax.ShapeDtypeStruct((rows, WIDTH), BF16),
        grid=(nreal + 1,),
        in_specs=[_zspec(2, nreal), _pspec((4, HEAD_W, HEAD_W)), _pspec((1, WIDTH))],
        out_specs=pl.BlockSpec((BLK, WIDTH), lambda i: (_row_block(i, nreal), 0)),
        scratch_shapes=[pltpu.VMEM((POOL_HIST + BLK, WIDTH), F32), pltpu.VMEM((POOL_HIST, WIDTH), F32)],
        compiler_params=_cparams(("arbitrary",)),
        name="ms_pool",
    )(z, pw, ps)


def _hgrn_chunk(q, z, v, lbh, state_t, valid, ones_bf):
    ls = _log_sigmoid(z)
    x1 = jnp.log(lbh)
    x2 = jnp.log1p(-lbh) + ls
    mx = jnp.maximum(x1, x2)
    g = mx + jnp.log1p(jnp.exp(-jnp.abs(x1 - x2)))
    k = (1.0 - lbh) * _sigmoid(-z)
    if valid is not None:
        g = jnp.where(valid, g, 0.0)
    row = lax.broadcasted_iota(jnp.int32, (CHUNK, 1), 0)
    b = g
    s = 1
    while s < CHUNK:
        b = b + _shift_rows(b, s, 0.0, row)
        s *= 2
    b_last = b[CHUNK - 1:CHUNK, :]

    qe = (q * jnp.exp(b)).astype(BF16)
    o = lax.dot_general(qe, state_t.astype(BF16), (((1,), (1,)), ((), ())), preferred_element_type=F32)

    col = lax.broadcasted_iota(jnp.int32, (SUB, CHUNK), 1)
    rsub = lax.broadcasted_iota(jnp.int32, (SUB, CHUNK), 0)
    lane = lax.broadcasted_iota(jnp.int32, (SUB, HEAD_W), 1)
    s_rows = []
    for blk in range(CHUNK // SUB):
        lo = blk * SUB
        bi = b[lo:lo + SUB, :]
        qi = q[lo:lo + SUB, :]
        ki = k[lo:lo + SUB, :]
        parts = []
        for sr in range(SUB):
            e = jnp.exp(jnp.minimum(bi - bi[sr:sr + 1, :], 0.0))
            parts.append(qi * e * ki[sr:sr + 1, :])
        m3 = jnp.concatenate(parts, axis=0).astype(BF16)
        red = jnp.dot(m3, ones_bf, preferred_element_type=F32)
        diag = jnp.zeros((SUB, HEAD_W), F32)
        for sr in range(SUB):
            diag = diag + jnp.where(lane == lo + sr, red[sr * SUB:(sr + 1) * SUB, :], 0.0)
        diag = diag[:, :CHUNK]
        s_blk = jnp.where(jnp.logical_and(col >= lo, col - lo <= rsub), diag, 0.0)
        if blk > 0:
            b0 = b[lo - 1:lo, :]
            kt = (k * jnp.exp(jnp.minimum(b0 - b, 0.0))).astype(BF16)
            qd = (qi * jnp.exp(bi - b0)).astype(BF16)
            off = lax.dot_general(qd, kt, (((1,), (1,)), ((), ())), preferred_element_type=F32)
            s_blk = jnp.where(col < lo, off, s_blk)
        s_rows.append(s_blk)
    scores = jnp.concatenate(s_rows, axis=0).astype(BF16)
    vb = v.astype(BF16)
    o = o + jnp.dot(scores, vb, preferred_element_type=F32)

    kd = (k * jnp.exp(b_last - b)).astype(BF16)
    upd = lax.dot_general(vb, kd, (((0,), (0,)), ((), ())), preferred_element_type=F32)
    new_state_t = state_t * jnp.exp(b_last) + upd
    return o, new_state_t


def _hgrn_kernel(q_ref, f_ref, v_ref, og_ref, lb_ref, nw_ref, y_ref, state, state_meta, *, nblk):
    i = pl.program_id(0)
    is_meta = i == 0

    @pl.when(is_meta)
    def _():
        state[...] = jnp.zeros_like(state)

    @pl.when(jnp.logical_and(i >= 1, (i - 1) % nblk == 0))
    def _():
        state[...] = state_meta[...]

    ones_bf = jnp.ones((HEAD_W, HEAD_W), BF16)
    nw = nw_ref[...]
    for hd in range(HEADS):
        sl = slice(hd * HEAD_W, (hd + 1) * HEAD_W)
        lbh = lb_ref[:, sl]
        st = state[hd]
        for c in range(BLK // CHUNK):
            rs = slice(c * CHUNK, (c + 1) * CHUNK)
            rowg = lax.broadcasted_iota(jnp.int32, (CHUNK, 1), 0) + c * CHUNK
            valid = rowg >= PAD_ROWS * is_meta.astype(jnp.int32)
            o, st = _hgrn_chunk(q_ref[rs, sl], f_ref[rs, sl], v_ref[rs, sl], lbh, st, valid, ones_bf)
            og = og_ref[rs, sl]
            y_ref[rs, sl] = (_rms(o, nw) * (og * _sigmoid(og))).astype(y_ref.dtype)
        state[hd] = st

    @pl.when(is_meta)
    def _():
        state_meta[...] = state[...]


def _hgrn(z_a, z_b, lb, nw, nreal, nblk):
    rows = z_a.shape[0]
    return pl.pallas_call(
        functools.partial(_hgrn_kernel, nblk=nblk),
        out_shape=jax.ShapeDtypeStruct((rows, WIDTH), BF16),
        grid=(nreal + 1,),
        in_specs=[_zspec(3, nreal), _zspec(0, nreal), _zspec(1, nreal), _zspec(2, nreal),
                  _pspec((1, WIDTH)), _pspec((1, HEAD_W))],
        out_specs=pl.BlockSpec((BLK, WIDTH), lambda i: (_row_block(i, nreal), 0)),
        scratch_shapes=[pltpu.VMEM((HEADS, HEAD_W, HEAD_W), F32), pltpu.VMEM((HEADS, HEAD_W, HEAD_W), F32)],
        compiler_params=_cparams(("arbitrary",)),
        name="hgrn2",
    )(z_a, z_b, z_b, z_b, lb, nw)


def _attn_lambda(lamv_ref, lambda_init):
    lv = lamv_ref[...]
    return (jnp.exp(jnp.sum(lv[0:1] * lv[1:2], axis=-1, keepdims=True))
            - jnp.exp(jnp.sum(lv[2:3] * lv[3:4], axis=-1, keepdims=True)) + lambda_init)


def _scaled_q(q_ref):
    qf = (q_ref[...].astype(F32) * (DIFF_HEAD_DIM ** -0.5)).astype(BF16)
    return qf[:, :DIFF_HEAD_DIM], qf[:, DIFF_HEAD_DIM:]


def _fold8(x, op):
    r, c = x.shape
    return op(x.reshape(r // 8, 8, c), axis=0)


def _attn_kernel(relb_ref, q_ref, k_ref, v_ref, km_ref, vm_ref, bias_ref, diag_ref, lamv_ref, sw_ref, y_ref,
                 s_sc, vt_sc, vtm_sc, m_sc, l_sc, acc_sc, *, nqb, nblk, lambda_init):
    hd = pl.program_id(0)
    jq = pl.program_id(1) % nqb
    far = relb_ref[REL_BUCKETS - 1, hd]
    t1 = bias_ref[1]
    slot_diag, slot_meta = nqb - 1, nqb

    @pl.when(jq == 0)
    def _():
        for t in range(nblk):
            vt_sc[t // NSUB, :, (t % NSUB) * BLK:(t % NSUB + 1) * BLK] = (
                v_ref[t * BLK:(t + 1) * BLK, :].astype(F32).T.astype(BF16))
        vtm_sc[...] = vm_ref[...].astype(F32).T.astype(BF16)

    qt = (q_ref[...].astype(F32) * (DIFF_HEAD_DIM ** -0.5)).T
    zero = jnp.zeros((DIFF_HEAD_DIM, QB), F32)
    qtp = (jnp.concatenate([qt[:DIFF_HEAD_DIM], zero], axis=0).astype(BF16),
           jnp.concatenate([zero, qt[DIFF_HEAD_DIM:]], axis=0).astype(BF16))

    def key_chunk(c):
        return k_ref[pl.ds(pl.multiple_of(c * QB, QB), QB), :]

    def score_chunk(k_rows, slot, add_bias):
        r = k_rows.shape[0]
        for mp in range(2):
            s = add_bias(jnp.dot(k_rows, qtp[mp], preferred_element_type=F32))
            s_sc[mp, slot, 0:r, :] = s
            m_sc[mp] = jnp.maximum(m_sc[mp], _fold8(s, jnp.max))

    def value_chunk(vt_cols, slot, m8):
        r = vt_cols.shape[1]
        for mp in range(2):
            p = jnp.exp(s_sc[mp, slot, 0:r, :].reshape(r // 8, 8, QB) - m8[mp][None])
            l_sc[mp] += jnp.sum(p, axis=0)
            acc_sc[mp] += jnp.dot(vt_cols, p.reshape(r, QB).astype(BF16), preferred_element_type=F32)

    def walk(fn_far, fn_meta, fn_diag):
        def body(c, carry):
            fn_far(c)
            return carry

        lax.fori_loop(0, jq, body, 0)
        fn_meta()
        fn_diag()

    near = t1 - far
    krow = lax.broadcasted_iota(jnp.int32, (BLK, 1), 0)
    meta_rows = jnp.where(krow >= PAD_ROWS, far, NEG)
    first = jnp.full((1, BLK), jq, jnp.int32) == 0

    def meta_bias(s):
        head = s[:, 0:BLK] + (meta_rows + jnp.where(first, near, 0.0))
        return jnp.concatenate([head, s[:, BLK:] + meta_rows], axis=1)

    m_sc[...] = jnp.full_like(m_sc, NEG)
    walk(lambda c: score_chunk(key_chunk(c), c, lambda s: s + far),
         lambda: score_chunk(km_ref[...], slot_meta, meta_bias),
         lambda: score_chunk(key_chunk(jq), slot_diag, lambda s: s + diag_ref[...]))

    @pl.when(jq >= 1)
    def _():
        for mp in range(2):
            fixed = s_sc[mp, jq - 1, QB - BLK:QB, 0:BLK] + near
            s_sc[mp, jq - 1, QB - BLK:QB, 0:BLK] = fixed
            m_sc[mp, :, 0:BLK] = jnp.maximum(m_sc[mp, :, 0:BLK], _fold8(fixed, jnp.max))

    m8 = [jnp.broadcast_to(jnp.max(m_sc[mp], axis=0, keepdims=True), (8, QB)) for mp in range(2)]
    l_sc[...] = jnp.zeros_like(l_sc)
    acc_sc[...] = jnp.zeros_like(acc_sc)

    walk(lambda c: value_chunk(vt_sc[c], c, m8),
         lambda: value_chunk(vtm_sc[...], slot_meta, m8),
         lambda: value_chunk(vt_sc[jq], slot_diag, m8))

    lam = _attn_lambda(lamv_ref, lambda_init)
    l0 = jnp.sum(l_sc[0], axis=0, keepdims=True)
    l1 = jnp.sum(l_sc[1], axis=0, keepdims=True)
    ot = acc_sc[0] / l0 - lam * (acc_sc[1] / l1)
    ot = ot * lax.rsqrt(jnp.mean(ot * ot, axis=0, keepdims=True) + EPS)
    y_ref[...] = (ot.T * sw_ref[...] * (1.0 - lambda_init)).astype(y_ref.dtype)


def _attn_meta_kernel(q_ref, k_ref, v_ref, bias_ref, lamv_ref, sw_ref, ymain_ref, y_ref, *, lambda_init):
    del ymain_ref
    qs = _scaled_q(q_ref)
    kb = k_ref[...]
    colmask = jnp.where(lax.broadcasted_iota(jnp.int32, (1, BLK), 1) >= PAD_ROWS, 0.0, NEG)
    outs = []
    bias = bias_ref[0].T + colmask
    for mp in range(2):
        s = lax.dot_general(qs[mp], kb[:, mp * DIFF_HEAD_DIM:(mp + 1) * DIFF_HEAD_DIM],
                            (((1,), (1,)), ((), ())), preferred_element_type=F32) + bias
        p = jnp.exp(s - jnp.max(s, axis=-1, keepdims=True))
        acc = jnp.dot(p.astype(BF16), v_ref[...], preferred_element_type=F32)
        outs.append(acc / jnp.sum(p, axis=-1, keepdims=True))
    o = outs[0] - _attn_lambda(lamv_ref, lambda_init) * outs[1]
    o = _rms(o, sw_ref[...]) * (1.0 - lambda_init)
    row = lax.broadcasted_iota(jnp.int32, (BLK, 1), 0)
    y_ref[...] = jnp.where(row >= PAD_ROWS, o, 0.0).astype(y_ref.dtype)


def _attn(zd, rel_bias, bias_tiles, bias_diag, lam_vec, subln_w, nreal, nblk, lambda_init):
    rows = zd.shape[0]
    nqb = nblk * BLK // QB
    nq_total = nreal * BLK // QB
    seq = nblk * BLK
    kcol, vcol = HEADS, 2 * HEADS
    y_main = pl.pallas_call(
        functools.partial(_attn_kernel, nqb=nqb, nblk=nblk, lambda_init=lambda_init),
        out_shape=jax.ShapeDtypeStruct((rows, WIDTH), BF16),
        grid=(HEADS, nq_total),
        in_specs=[pl.BlockSpec(memory_space=pltpu.SMEM),
                  pl.BlockSpec((QB, HEAD_W), lambda h, i: (i, h)),
                  pl.BlockSpec((seq, HEAD_W), lambda h, i: (i // nqb, kcol + h)),
                  pl.BlockSpec((seq, HEAD_W), lambda h, i: (i // nqb, vcol + h)),
                  pl.BlockSpec((BLK, HEAD_W), lambda h, i: (nreal, kcol + h)),
                  pl.BlockSpec((BLK, HEAD_W), lambda h, i: (nreal, vcol + h)),
                  pl.BlockSpec((None, 2, BLK, BLK), lambda h, i: (h, 0, 0, 0)),
                  pl.BlockSpec((None, QB, QB), lambda h, i: (h, 0, 0)),
                  pl.BlockSpec((4, DIFF_HEAD_DIM), lambda h, i: (0, 0)),
                  pl.BlockSpec((1, HEAD_W), lambda h, i: (0, 0))],
        out_specs=pl.BlockSpec((QB, HEAD_W), lambda h, i: (i, h)),
        scratch_shapes=[pltpu.VMEM((2, nqb + 1, QB, QB), F32),
                        pltpu.VMEM((nqb, HEAD_W, QB), BF16),
                        pltpu.VMEM((HEAD_W, BLK), BF16),
                        pltpu.VMEM((2, 8, QB), F32),
                        pltpu.VMEM((2, 8, QB), F32),
                        pltpu.VMEM((2, HEAD_W, QB), F32)],
        compiler_params=_cparams(("arbitrary", "arbitrary")),
        name="diff_attn",
    )(rel_bias, zd, zd, zd, zd, zd, bias_tiles, bias_diag, lam_vec, subln_w)
    return pl.pallas_call(
        functools.partial(_attn_meta_kernel, lambda_init=lambda_init),
        out_shape=jax.ShapeDtypeStruct((rows, WIDTH), BF16),
        grid=(HEADS,),
        in_specs=[pl.BlockSpec((BLK, HEAD_W), lambda h: (nreal, h)),
                  pl.BlockSpec((BLK, HEAD_W), lambda h: (nreal, kcol + h)),
                  pl.BlockSpec((BLK, HEAD_W), lambda h: (nreal, vcol + h)),
                  pl.BlockSpec((None, 2, BLK, BLK), lambda h: (h, 0, 0, 0)),
                  pl.BlockSpec((4, DIFF_HEAD_DIM), lambda h: (0, 0)),
                  pl.BlockSpec((1, HEAD_W), lambda h: (0, 0)),
                  pl.BlockSpec(memory_space=pl.ANY)],
        out_specs=pl.BlockSpec((BLK, HEAD_W), lambda h: (nreal, h)),
        input_output_aliases={6: 0},
        compiler_params=_cparams(("arbitrary",)),
        name="diff_attn_meta",
    )(zd, zd, zd, bias_tiles, lam_vec, subln_w, y_main)


def _largest_tile(rows, cap, align=16):
    best = align
    for t in range(align, cap + 1, align):
        if rows % t == 0:
            best = t
    return best


def _forward(x, meta_tokens, rel_bias, hgrn_lower_bounds, norm_mix_pre, norm_mix_post, norm_ffn_pre,
             norm_ffn_post, w_in, lru_conv_w, lru_conv_b, lru_w_a, lru_b_a, lru_w_x, lru_b_x, lru_lambda,
             pool_w, pool_scale, hgrn_norm, diff_lambda, diff_subln, w_branch, w_out, ffn_w_gu, ffn_w_down):
    bsz, seq, _ = x.shape
    nblk = seq // BLK
    nreal = bsz * nblk
    rows = (nreal + 1) * BLK
    rows_real = nreal * BLK
    tm_big = _largest_tile(rows, 640)
    tm_epi = _largest_tile(rows, 640)
    tm_down = _largest_tile(rows, 320)
    tm_last = _largest_tile(rows_real, 256)

    def vec(a):
        return a.reshape(1, -1)

    lbs, bias_tiles, bias_diag = _prologue(hgrn_lower_bounds, rel_bias)
    h, hn = _embed(x.reshape(rows_real, D_MODEL), meta_tokens, vec(norm_mix_pre[0]), nreal)

    for layer in range(DEPTH):
        lambda_init = 0.8 - 0.6 * math.exp(-0.3 * layer)
        z = _mixer_in_proj(hn, w_in, layer, 0, 4 * WIDTH, F32, tm_big, 1024, 1024)
        z_b = _mixer_in_proj(hn, w_in, layer, 4 * WIDTH, 3 * WIDTH, F32, tm_big, 3 * WIDTH, WIDTH)
        zd = _mixer_in_proj(hn, w_in, layer, 7 * WIDTH, 3 * WIDTH, BF16, tm_big, 3 * WIDTH, WIDTH)
        y_a = _lru(z, lru_conv_w[layer], vec(lru_conv_b[layer]), lru_w_a[layer], vec(lru_b_a[layer]),
                   lru_w_x[layer], vec(lru_b_x[layer]), vec(lru_lambda[layer]), nreal, nblk)
        y_b = _pool(z, pool_w[layer], vec(pool_scale[layer]), nreal, nblk)
        y_c = _hgrn(z, z_b, lbs[layer:layer + 1], vec(hgrn_norm[layer]), nreal, nblk)
        y_d = _attn(zd, rel_bias, bias_tiles, bias_diag, diff_lambda[layer], vec(diff_subln[layer]),
                    nreal, nblk, lambda_init)
        merged = _gate_merge(hn, w_in, (y_a, y_b, y_c, y_d), w_branch, layer, tm_big, 256)
        h, hn = _out_proj(merged, _cast_layer_bf16(w_out, layer, 512), h, vec(norm_mix_post[layer]),
                          vec(norm_ffn_pre[layer]), tm_epi)
        a = _swiglu_up(hn, ffn_w_gu, layer, tm_big, 512)
        last = layer == DEPTH - 1
        w_next = vec(norm_mix_pre[layer + 1]) if not last else vec(norm_mix_pre[layer])
        h, hn = _down_proj(a, _cast_layer_bf16(ffn_w_down, layer, 512), h, vec(norm_ffn_post[layer]), w_next,
                           rows_real if last else rows, tm_last if last else tm_down, not last)
    return h.reshape(bsz, seq, D_MODEL)


def kernel(x, meta_tokens, rel_bias, hgrn_lower_bounds, norm_mix_pre, norm_mix_post, norm_ffn_pre, norm_ffn_post, w_in, lru_conv_w, lru_conv_b, lru_w_a, lru_b_a, lru_w_x, lru_b_x, lru_lambda, pool_w, pool_scale, hgrn_norm, diff_lambda, diff_subln, w_branch, w_out, ffn_w_gu, ffn_w_down):
    return _forward(x, meta_tokens, rel_bias, hgrn_lower_bounds, norm_mix_pre, norm_mix_post, norm_ffn_pre,
                    norm_ffn_post, w_in, lru_conv_w, lru_conv_b, lru_w_a, lru_b_a, lru_w_x, lru_b_x, lru_lambda,
                    pool_w, pool_scale, hgrn_norm, diff_lambda, diff_subln, w_branch, w_out, ffn_w_gu, ffn_w_down)
```

```python
import functools
import math

import numpy as np
import jax
import jax.numpy as jnp
from jax import lax
from jax.experimental import pallas as pl
from jax.experimental.pallas import tpu as pltpu

F32 = jnp.float32
BF16 = jnp.bfloat16

D_MODEL = 2048
SEQ = 2048
DEPTH = 2
N_META = 16
BLK = 128
PAD_ROWS = BLK - N_META
QB = 512
NSUB = QB // BLK
WIDTH = 512
HEADS = 4
HEAD_W = 128
CHUNK = 64
SUB = 16
LRU_C = 8.0
POOL_WINDOWS = (2, 4, 8, 16)
DIFF_HEAD_DIM = 64
REL_BUCKETS = 32
REL_MAX_DIST = 128
FFN_HIDDEN = 5632
MIX_COLS = 10 * WIDTH
NEG = -1e30
EPS = 1e-6
LOG2E = math.log2(math.e)
VMEM_LIMIT = 56 * 1024 * 1024


def _cparams(sem):
    return pltpu.CompilerParams(dimension_semantics=sem, vmem_limit_bytes=VMEM_LIMIT)


def _rms(x, w):
    return x * lax.rsqrt(jnp.mean(x * x, axis=-1, keepdims=True) + EPS) * w


def _log_sigmoid(z):
    return -(jnp.maximum(-z, 0.0) + jnp.log(1.0 + jnp.exp(-jnp.abs(z))))


def _sigmoid(z):
    return 1.0 / (1.0 + jnp.exp(-z))


def _gelu_tanh(x):
    c = math.sqrt(2.0 / math.pi)
    return 0.5 * x * (1.0 + jnp.tanh(c * (x + 0.044715 * (x * x * x))))


def _shift_rows(x, s, fill, row):
    return jnp.where(row >= s, pltpu.roll(x, s, axis=0), fill)


def _bucket_tiles():
    r = np.arange(BLK)[None, :]
    c = np.arange(BLK)[:, None]
    max_exact = REL_BUCKETS // 2

    def bucket(n):
        nf = np.maximum(n, 1).astype(np.float32)
        large = max_exact + (np.log(nf / np.float32(max_exact)) / np.float32(math.log(REL_MAX_DIST / max_exact))
                             * np.float32(REL_BUCKETS - max_exact)).astype(np.int32)
        large = np.minimum(large, REL_BUCKETS - 1)
        return np.where(n < max_exact, n, large).astype(np.int32)

    d0 = r - c
    t0 = np.where(d0 >= 0, bucket(np.maximum(d0, 0)), -1)
    t1 = bucket(BLK + r - c)
    return np.stack([t0, t1]).astype(np.int32)


def _prologue_kernel(lbraw_ref, relb_ref, idx_ref, lb_ref, bias_ref, diag_ref):
    raw = lbraw_ref[...]
    mx = jnp.max(raw, axis=0, keepdims=True)
    e = jnp.exp(raw - mx)
    sm = e / jnp.sum(e, axis=0, keepdims=True)
    cum = sm[0:1]
    lb_ref[0:1, :] = cum - sm[0:1]
    for l in range(1, DEPTH):
        cum = cum + sm[l:l + 1]
        lb_ref[l:l + 1, :] = cum - sm[0:1]
    for t in range(2):
        idx = idx_ref[t]
        for hd in range(HEADS):
            acc = jnp.zeros((BLK, BLK), F32)
            for bk in range(REL_BUCKETS):
                acc = jnp.where(idx == bk, relb_ref[bk, hd], acc)
            bias_ref[hd, t] = jnp.where(idx < 0, NEG, acc)
    for hd in range(HEADS):
        far = relb_ref[REL_BUCKETS - 1, hd]
        for kb in range(NSUB):
            for qb in range(NSUB):
                delta = qb - kb
                if delta == 0:
                    blk = bias_ref[hd, 0]
                elif delta == 1:
                    blk = bias_ref[hd, 1]
                else:
                    blk = jnp.full((BLK, BLK), far if delta > 1 else NEG, F32)
                diag_ref[hd, kb * BLK:(kb + 1) * BLK, qb * BLK:(qb + 1) * BLK] = blk * LOG2E


def _prologue(hgrn_lower_bounds, rel_bias):
    idx = jnp.asarray(_bucket_tiles())
    vmem = pl.BlockSpec(memory_space=pltpu.VMEM)
    return pl.pallas_call(
        _prologue_kernel,
        out_shape=(jax.ShapeDtypeStruct((DEPTH, WIDTH), F32),
                   jax.ShapeDtypeStruct((HEADS, 2, BLK, BLK), F32),
                   jax.ShapeDtypeStruct((HEADS, QB, QB), F32)),
        in_specs=[vmem, pl.BlockSpec(memory_space=pltpu.SMEM), vmem],
        out_specs=(vmem, vmem, vmem),
        name="prologue",
    )(hgrn_lower_bounds, rel_bias, idx)


def _embed_kernel(x_ref, meta_ref, w_ref, h_ref, hn_ref, *, nreal):
    i = pl.program_id(0)

    @pl.when(i < nreal)
    def _():
        h_ref[...] = x_ref[...]

    @pl.when(i == nreal)
    def _():
        h_ref[0:PAD_ROWS, :] = jnp.zeros((PAD_ROWS, D_MODEL), F32)
        h_ref[PAD_ROWS:BLK, :] = meta_ref[...]

    hn_ref[...] = _rms(h_ref[...], w_ref[...]).astype(BF16)


def _embed(x2d, meta, w_pre, nreal):
    rows = (nreal + 1) * BLK
    return pl.pallas_call(
        functools.partial(_embed_kernel, nreal=nreal),
        out_shape=(jax.ShapeDtypeStruct((rows, D_MODEL), F32),
                   jax.ShapeDtypeStruct((rows, D_MODEL), BF16)),
        grid=(nreal + 1,),
        in_specs=[pl.BlockSpec((BLK, D_MODEL), lambda i: (jnp.minimum(i, nreal - 1), 0)),
                  pl.BlockSpec((N_META, D_MODEL), lambda i: (0, 0)),
                  pl.BlockSpec((1, D_MODEL), lambda i: (0, 0))],
        out_specs=(pl.BlockSpec((BLK, D_MODEL), lambda i: (i, 0)),
                   pl.BlockSpec((BLK, D_MODEL), lambda i: (i, 0))),
        compiler_params=_cparams(("arbitrary",)),
        name="embed",
    )(x2d, meta, w_pre)


def _cast_tiles_once(pairs):
    @pl.when(pl.program_id(1) == 0)
    def _():
        for src, dst in pairs:
            dst[...] = src[...].astype(BF16)


def _matmul_kernel(x_ref, *rest):
    *w_refs, o_ref, wbf = rest
    wblk = w_refs[0].shape[-1]
    _cast_tiles_once([(w, wbf.at[:, j * wblk:(j + 1) * wblk]) for j, w in enumerate(w_refs)])
    o_ref[...] = jnp.dot(x_ref[...], wbf[...], preferred_element_type=F32).astype(o_ref.dtype)


def _mixer_in_proj(hn, w_in, layer, col0, ncols, out_dtype, tm, tn, wblk):
    rows, k = hn.shape
    nw = tn // wblk
    w_specs = [pl.BlockSpec((None, k, wblk), functools.partial(
        lambda n, m, j: (layer, 0, col0 // wblk + n * nw + j), j=j)) for j in range(nw)]
    return pl.pallas_call(
        _matmul_kernel,
        out_shape=jax.ShapeDtypeStruct((rows, ncols), out_dtype),
        grid=(ncols // tn, rows // tm),
        in_specs=[pl.BlockSpec((tm, k), lambda n, m: (m, 0))] + w_specs,
        out_specs=pl.BlockSpec((tm, tn), lambda n, m: (m, n)),
        scratch_shapes=[pltpu.VMEM((k, tn), BF16)],
        compiler_params=_cparams(("arbitrary", "arbitrary")),
        name="mixer_in_proj",
    )(hn, *([w_in] * nw))


def _cast_kernel(w_ref, o_ref):
    o_ref[...] = w_ref[...].astype(BF16)


def _cast_layer_bf16(w, layer, tr, col0=0, ncols=None, tc=None):
    _, r, c = w.shape
    ncols = c if ncols is None else ncols
    tc = ncols if tc is None else tc
    return pl.pallas_call(
        _cast_kernel,
        out_shape=jax.ShapeDtypeStruct((r, ncols), BF16),
        grid=(r // tr, ncols // tc),
        in_specs=[pl.BlockSpec((None, tr, tc), lambda i, j: (layer, i, col0 // tc + j))],
        out_specs=pl.BlockSpec((tr, tc), lambda i, j: (i, j)),
        compiler_params=_cparams(("arbitrary", "arbitrary")),
        name="cast_bf16",
    )(w)


def _residual_epilogue(acc, h_ref, wpost_ref, wnext_ref, hnew_ref, hn_ref):
    h_new = h_ref[...] + _rms(acc, wpost_ref[...])
    hnew_ref[...] = h_new
    if hn_ref is not None:
        hn_ref[...] = _rms(h_new, wnext_ref[...]).astype(BF16)


EPI_ROWS = 160


def _row_subtiles(tm):
    sub = next(s for s in (EPI_ROWS, 128, 64, 32, 16) if tm % s == 0)
    return [slice(r, r + sub) for r in range(0, tm, sub)]


def _gate_merge_kernel(hn_ref, g0, g1, g2, g3, y0, y1, y2, y3, wb_ref, o_ref, gbf, wbbf):
    _cast_tiles_once([(g, gbf.at[k]) for k, g in enumerate((g0, g1, g2, g3))] + [(wb_ref, wbbf)])
    hn = hn_ref[...]
    acc = None
    for k, y_ref in enumerate((y0, y1, y2, y3)):
        gate = _sigmoid(jnp.dot(hn, gbf[k], preferred_element_type=F32))
        proj = jnp.dot(y_ref[...], wbbf[k], preferred_element_type=F32)
        acc = gate * proj if acc is None else acc + gate * proj
    o_ref[...] = acc.astype(o_ref.dtype)


def _gate_merge(hn, w_in, ys, w_branch, layer, tm, tn):
    rows, k = hn.shape
    gate_specs = [
        pl.BlockSpec((None, k, tn), functools.partial(
            lambda n, m, base: (layer, 0, base + n), base=(MIX_COLS + br * D_MODEL) // tn))
        for br in range(4)]
    y_specs = [pl.BlockSpec((tm, WIDTH), lambda n, m: (m, 0)) for _ in range(4)]
    return pl.pallas_call(
        _gate_merge_kernel,
        out_shape=jax.ShapeDtypeStruct((rows, D_MODEL), BF16),
        grid=(D_MODEL // tn, rows // tm),
        in_specs=[pl.BlockSpec((tm, k), lambda n, m: (m, 0))] + gate_specs + y_specs
                 + [pl.BlockSpec((None, 4, WIDTH, tn), lambda n, m: (layer, 0, 0, n))],
        out_specs=pl.BlockSpec((tm, tn), lambda n, m: (m, n)),
        scratch_shapes=[pltpu.VMEM((4, k, tn), BF16), pltpu.VMEM((4, WIDTH, tn), BF16)],
        compiler_params=_cparams(("arbitrary", "arbitrary")),
        name="gate_merge",
    )(hn, w_in, w_in, w_in, w_in, *ys, w_branch)


def _out_proj_kernel(x_ref, w_ref, h_ref, wpost_ref, wnext_ref, hnew_ref, hn_ref):
    for rs in _row_subtiles(x_ref.shape[0]):
        acc = jnp.dot(x_ref[rs, :], w_ref[...], preferred_element_type=F32)
        _residual_epilogue(acc, h_ref.at[rs, :], wpost_ref, wnext_ref, hnew_ref.at[rs, :], hn_ref.at[rs, :])


def _out_proj(merged, w_out, h, w_post, w_next, tm):
    rows = h.shape[0]
    row_spec = pl.BlockSpec((tm, D_MODEL), lambda m: (m, 0))
    return pl.pallas_call(
        _out_proj_kernel,
        out_shape=(jax.ShapeDtypeStruct((rows, D_MODEL), F32),
                   jax.ShapeDtypeStruct((rows, D_MODEL), BF16)),
        grid=(rows // tm,),
        in_specs=[row_spec, _resident((D_MODEL, D_MODEL), (0, 0)), row_spec,
                  _pspec((1, D_MODEL)), _pspec((1, D_MODEL))],
        out_specs=(row_spec, row_spec),
        compiler_params=_cparams(("arbitrary",)),
        name="out_proj",
    )(merged, w_out, h, w_post, w_next)


def _swiglu_up_kernel(x_ref, wg_ref, wu_ref, o_ref, wgbf, wubf):
    _cast_tiles_once([(wg_ref, wgbf), (wu_ref, wubf)])
    x = x_ref[...]
    g = jnp.dot(x, wgbf[...], preferred_element_type=F32)
    u = jnp.dot(x, wubf[...], preferred_element_type=F32)
    o_ref[...] = (g * _sigmoid(g) * u).astype(o_ref.dtype)


def _swiglu_up(hn, w_gu, layer, tm, tn):
    rows, k = hn.shape
    nt = FFN_HIDDEN // tn
    return pl.pallas_call(
        _swiglu_up_kernel,
        out_shape=jax.ShapeDtypeStruct((rows, FFN_HIDDEN), BF16),
        grid=(nt, rows // tm),
        in_specs=[pl.BlockSpec((tm, k), lambda n, m: (m, 0)),
                  pl.BlockSpec((None, k, tn), lambda n, m: (layer, 0, n)),
                  pl.BlockSpec((None, k, tn), lambda n, m: (layer, 0, nt + n))],
        out_specs=pl.BlockSpec((tm, tn), lambda n, m: (m, n)),
        scratch_shapes=[pltpu.VMEM((k, tn), BF16), pltpu.VMEM((k, tn), BF16)],
        compiler_params=_cparams(("arbitrary", "arbitrary")),
        name="swiglu_up",
    )(hn, w_gu, w_gu)


def _down_proj_kernel(a_ref, w_ref, h_ref, wpost_ref, wnext_ref, hnew_ref, hn_ref=None):
    for rs in _row_subtiles(a_ref.shape[0]):
        acc = jnp.dot(a_ref[rs, :], w_ref[...], preferred_element_type=F32)
        _residual_epilogue(acc, h_ref.at[rs, :], wpost_ref, wnext_ref, hnew_ref.at[rs, :],
                           None if hn_ref is None else hn_ref.at[rs, :])


def _down_proj(a, w_down, h, w_post, w_next, rows_out, tm, emit_hn):
    row_spec = lambda w: pl.BlockSpec((tm, w), lambda m: (m, 0))
    out_shape = [jax.ShapeDtypeStruct((rows_out, D_MODEL), F32)]
    out_specs = [row_spec(D_MODEL)]
    if emit_hn:
        out_shape.append(jax.ShapeDtypeStruct((rows_out, D_MODEL), BF16))
        out_specs.append(row_spec(D_MODEL))
    res = pl.pallas_call(
        _down_proj_kernel,
        out_shape=tuple(out_shape),
        grid=(rows_out // tm,),
        in_specs=[row_spec(FFN_HIDDEN), _resident((FFN_HIDDEN, D_MODEL), (0, 0)), row_spec(D_MODEL),
                  _pspec((1, D_MODEL)), _pspec((1, D_MODEL))],
        out_specs=tuple(out_specs),
        compiler_params=_cparams(("arbitrary",)),
        name="down_proj",
    )(a, w_down, h, w_post, w_next)
    return res if emit_hn else (res[0], None)


def _row_block(i, nreal):
    return (i + nreal) % (nreal + 1)


def _zspec(col_block, nreal):
    return pl.BlockSpec((BLK, WIDTH), lambda i: (_row_block(i, nreal), col_block))


def _pspec(shape):
    nd = len(shape)
    return pl.BlockSpec(shape, lambda *_: (0,) * nd)


def _resident(shape, index):
    return pl.BlockSpec(shape, lambda *_: index, pipeline_mode=pl.Buffered(1))


def _lru_kernel(u_ref, gate_ref, cw_ref, cb_ref, wa_ref, ba_ref, wx_ref, bx_ref, lam_ref, y_ref,
                ubuf, hst, hist_meta, h_meta, *, nblk):
    i = pl.program_id(0)
    is_meta = i == 0

    @pl.when(is_meta)
    def _():
        ubuf[0:8, :] = jnp.zeros((8, WIDTH), F32)
        hst[...] = jnp.zeros_like(hst)

    @pl.when(jnp.logical_and(i >= 1, (i - 1) % nblk == 0))
    def _():
        ubuf[0:8, :] = hist_meta[...]
        hst[...] = h_meta[...]

    u = u_ref[...]
    ubuf[8:8 + BLK, :] = u
    cw = cw_ref[...]
    xc = (cb_ref[...] + cw[3:4] * u + cw[2:3] * ubuf[7:7 + BLK, :]
          + cw[1:2] * ubuf[6:6 + BLK, :] + cw[0:1] * ubuf[5:5 + BLK, :])
    xb = xc.astype(BF16)
    ra, ia = [], []
    for hd in range(HEADS):
        sl = slice(hd * HEAD_W, (hd + 1) * HEAD_W)
        ra.append(jnp.dot(xb[:, sl], wa_ref[hd].astype(BF16), preferred_element_type=F32))
        ia.append(jnp.dot(xb[:, sl], wx_ref[hd].astype(BF16), preferred_element_type=F32))
    r = _sigmoid(jnp.concatenate(ra, axis=1) + ba_ref[...])
    ig = _sigmoid(jnp.concatenate(ia, axis=1) + bx_ref[...])
    lam = lam_ref[...]
    softplus_neg_lam = jnp.maximum(-lam, 0.0) + jnp.log1p(jnp.exp(-jnp.abs(lam)))
    log_a = -LRU_C * r * softplus_neg_lam
    a = jnp.exp(log_a)
    bb = jnp.sqrt(-jnp.tanh(log_a) * (a * a + 1.0)) * (ig * xc)
    row = lax.broadcasted_iota(jnp.int32, (BLK, 1), 0)
    bb = jnp.where(row >= PAD_ROWS * is_meta.astype(jnp.int32), bb, 0.0)

    acum, bcum = a, bb
    s = 1
    while s < BLK:
        a_sh = _shift_rows(acum, s, 1.0, row)
        b_sh = _shift_rows(bcum, s, 0.0, row)
        bcum = acum * b_sh + bcum
        acum = acum * a_sh
        s *= 2
    h = acum * hst[0:1, :] + bcum
    y_ref[...] = (h * _gelu_tanh(gate_ref[...])).astype(y_ref.dtype)

    hist = u[BLK - 8:BLK, :]
    hlast = jnp.broadcast_to(h[BLK - 1:BLK, :], (8, WIDTH))
    ubuf[0:8, :] = hist
    hst[...] = hlast

    @pl.when(is_meta)
    def _():
        hist_meta[...] = hist
        h_meta[...] = hlast


def _lru(z, cw, cb, wa, ba, wx, bx, lam, nreal, nblk):
    rows = z.shape[0]
    return pl.pallas_call(
        functools.partial(_lru_kernel, nblk=nblk),
        out_shape=jax.ShapeDtypeStruct((rows, WIDTH), BF16),
        grid=(nreal + 1,),
        in_specs=[_zspec(0, nreal), _zspec(1, nreal),
                  _pspec((4, WIDTH)), _pspec((1, WIDTH)),
                  _pspec((HEADS, HEAD_W, HEAD_W)), _pspec((1, WIDTH)),
                  _pspec((HEADS, HEAD_W, HEAD_W)), _pspec((1, WIDTH)), _pspec((1, WIDTH))],
        out_specs=pl.BlockSpec((BLK, WIDTH), lambda i: (_row_block(i, nreal), 0)),
        scratch_shapes=[pltpu.VMEM((8 + BLK, WIDTH), F32), pltpu.VMEM((8, WIDTH), F32),
                        pltpu.VMEM((8, WIDTH), F32), pltpu.VMEM((8, WIDTH), F32)],
        compiler_params=_cparams(("arbitrary",)),
        name="rg_lru",
    )(z, z, cw, cb, wa, ba, wx, bx, lam)


POOL_HIST = 16


def _pool_kernel(u_ref, pw_ref, ps_ref, y_ref, ubuf, hist_meta, *, nblk):
    i = pl.program_id(0)
    is_meta = i == 0

    @pl.when(is_meta)
    def _():
        ubuf[0:POOL_HIST, :] = jnp.zeros((POOL_HIST, WIDTH), F32)

    @pl.when(jnp.logical_and(i >= 1, (i - 1) % nblk == 0))
    def _():
        ubuf[0:POOL_HIST, :] = hist_meta[...]

    u = u_ref[...]
    ubuf[POOL_HIST:POOL_HIST + BLK, :] = u
    row = lax.broadcasted_iota(jnp.int32, (BLK, 1), 0)
    meta_i = is_meta.astype(jnp.int32)
    pos1 = row + 1 - PAD_ROWS * meta_i + 2 * POOL_HIST * (1 - meta_i)
    outs = []
    for g, win in enumerate(POOL_WINDOWS):
        sl = slice(g * HEAD_W, (g + 1) * HEAD_W)
        acc = u[:, sl]
        for d in range(1, win):
            acc = acc + ubuf[POOL_HIST - d:POOL_HIST - d + BLK, sl]
        count = jnp.clip(pos1, 1, win).astype(F32)
        pooled = acc / count - u[:, sl]
        outs.append(jnp.dot(pooled.astype(BF16), pw_ref[g].astype(BF16), preferred_element_type=F32))
    y_ref[...] = (jnp.concatenate(outs, axis=1) * ps_ref[...]).astype(y_ref.dtype)

    hist = u[BLK - POOL_HIST:BLK, :]
    ubuf[0:POOL_HIST, :] = hist

    @pl.when(is_meta)
    def _():
        hist_meta[...] = hist


def _pool(z, pw, ps, nreal, nblk):
    rows = z.shape[0]
    return pl.pallas_call(
        functools.partial(_pool_kernel, nblk=nblk),
        out_shape=jax.ShapeDtypeStruct((rows, WIDTH), BF16),
        grid=(nreal + 1,),
        in_specs=[_zspec(2, nreal), _pspec((4, HEAD_W, HEAD_W)), _pspec((1, WIDTH))],
        out_specs=pl.BlockSpec((BLK, WIDTH), lambda i: (_row_block(i, nreal), 0)),
        scratch_shapes=[pltpu.VMEM((POOL_HIST + BLK, WIDTH), F32), pltpu.VMEM((POOL_HIST, WIDTH), F32)],
        compiler_params=_cparams(("arbitrary",)),
        name="ms_pool",
    )(z, pw, ps)


def _hgrn_chunk(q, z, v, lbh, state_t, valid, ones_bf):
    ls = _log_sigmoid(z)
    x1 = jnp.log(lbh)
    x2 = jnp.log1p(-lbh) + ls
    mx = jnp.maximum(x1, x2)
    g = mx + jnp.log(1.0 + jnp.exp(-jnp.abs(x1 - x2)))
    k = (1.0 - lbh) * _sigmoid(-z)
    if valid is not None:
        g = jnp.where(valid, g, 0.0)
    row = lax.broadcasted_iota(jnp.int32, (CHUNK, 1), 0)
    b = g * LOG2E
    s = 1
    while s < CHUNK:
        b = b + _shift_rows(b, s, 0.0, row)
        s *= 2
    b_last = b[CHUNK - 1:CHUNK, :]

    qe = (q * jnp.exp2(b)).astype(BF16)
    o = lax.dot_general(qe, state_t.astype(BF16), (((1,), (1,)), ((), ())), preferred_element_type=F32)

    col = lax.broadcasted_iota(jnp.int32, (SUB, CHUNK), 1)
    rsub = lax.broadcasted_iota(jnp.int32, (SUB, CHUNK), 0)
    lane = lax.broadcasted_iota(jnp.int32, (SUB, HEAD_W), 1)
    s_rows = []
    for blk in range(CHUNK // SUB):
        lo = blk * SUB
        bi = b[lo:lo + SUB, :]
        qi = q[lo:lo + SUB, :]
        ki = k[lo:lo + SUB, :]
        parts = []
        for sr in range(SUB):
            e = jnp.exp2(jnp.minimum(bi - bi[sr:sr + 1, :], 0.0))
            parts.append(qi * e * ki[sr:sr + 1, :])
        m3 = jnp.concatenate(parts, axis=0).astype(BF16)
        red = jnp.dot(m3, ones_bf, preferred_element_type=F32)
        diag = jnp.zeros((SUB, HEAD_W), F32)
        for sr in range(SUB):
            diag = jnp.where(lane == lo + sr, red[sr * SUB:(sr + 1) * SUB, :], diag)
        diag = diag[:, :CHUNK]
        s_blk = jnp.where(jnp.logical_and(col >= lo, col - lo <= rsub), diag, 0.0)
        if blk > 0:
            b0 = b[lo - 1:lo, :]
            kt = (k * jnp.exp2(jnp.minimum(b0 - b, 0.0))).astype(BF16)
            qd = (qi * jnp.exp2(bi - b0)).astype(BF16)
            off = lax.dot_general(qd, kt, (((1,), (1,)), ((), ())), preferred_element_type=F32)
            s_blk = jnp.where(col < lo, off, s_blk)
        s_rows.append(s_blk)
    scores = jnp.concatenate(s_rows, axis=0).astype(BF16)
    vb = v.astype(BF16)
    o = o + jnp.dot(scores, vb, preferred_element_type=F32)

    kd = (k * jnp.exp2(b_last - b)).astype(BF16)
    upd = lax.dot_general(vb, kd, (((0,), (0,)), ((), ())), preferred_element_type=F32)
    new_state_t = state_t * jnp.exp2(b_last) + upd
    return o, new_state_t


def _hgrn_kernel(q_ref, f_ref, v_ref, og_ref, lb_ref, nw_ref, y_ref, state, state_meta, *, nblk):
    i = pl.program_id(0)
    is_meta = i == 0

    @pl.when(is_meta)
    def _():
        state[...] = jnp.zeros_like(state)

    @pl.when(jnp.logical_and(i >= 1, (i - 1) % nblk == 0))
    def _():
        state[...] = state_meta[...]

    ones_bf = jnp.ones((HEAD_W, HEAD_W), BF16)
    nw = nw_ref[...]
    for hd in range(HEADS):
        sl = slice(hd * HEAD_W, (hd + 1) * HEAD_W)
        lbh = lb_ref[:, sl]
        st = state[hd]
        for c in range(BLK // CHUNK):
            rs = slice(c * CHUNK, (c + 1) * CHUNK)
            rowg = lax.broadcasted_iota(jnp.int32, (CHUNK, 1), 0) + c * CHUNK
            valid = rowg >= PAD_ROWS * is_meta.astype(jnp.int32)
            o, st = _hgrn_chunk(q_ref[rs, sl], f_ref[rs, sl], v_ref[rs, sl], lbh, st, valid, ones_bf)
            og = og_ref[rs, sl]
            y_ref[rs, sl] = (_rms(o, nw) * (og * _sigmoid(og))).astype(y_ref.dtype)
        state[hd] = st

    @pl.when(is_meta)
    def _():
        state_meta[...] = state[...]


def _hgrn(z_a, z_b, lb, nw, nreal, nblk):
    rows = z_a.shape[0]
    return pl.pallas_call(
        functools.partial(_hgrn_kernel, nblk=nblk),
        out_shape=jax.ShapeDtypeStruct((rows, WIDTH), BF16),
        grid=(nreal + 1,),
        in_specs=[_zspec(3, nreal), _zspec(0, nreal), _zspec(1, nreal), _zspec(2, nreal),
                  _pspec((1, WIDTH)), _pspec((1, HEAD_W))],
        out_specs=pl.BlockSpec((BLK, WIDTH), lambda i: (_row_block(i, nreal), 0)),
        scratch_shapes=[pltpu.VMEM((HEADS, HEAD_W, HEAD_W), F32), pltpu.VMEM((HEADS, HEAD_W, HEAD_W), F32)],
        compiler_params=_cparams(("arbitrary",)),
        name="hgrn2",
    )(z_a, z_b, z_b, z_b, lb, nw)


def _attn_lambda(lamv_ref, lambda_init):
    lv = lamv_ref[...]
    return (jnp.exp(jnp.sum(lv[0:1] * lv[1:2], axis=-1, keepdims=True))
            - jnp.exp(jnp.sum(lv[2:3] * lv[3:4], axis=-1, keepdims=True)) + lambda_init)


def _scaled_q(q_ref):
    qf = (q_ref[...].astype(F32) * (DIFF_HEAD_DIM ** -0.5)).astype(BF16)
    return qf[:, :DIFF_HEAD_DIM], qf[:, DIFF_HEAD_DIM:]


def _fold8(x, op):
    r, c = x.shape
    return op(x.reshape(r // 8, 8, c), axis=0)


def _attn_kernel(relb_ref, q_ref, k_ref, v_ref, km_ref, vm_ref, bias_ref, diag_ref, lamv_ref, sw_ref, y_ref,
                 s_sc, vt_sc, vtm_sc, m_sc, l_sc, acc_sc, *, nqb, nblk, lambda_init):
    hd = pl.program_id(0)
    jq = pl.program_id(1) % nqb
    far = relb_ref[REL_BUCKETS - 1, hd] * LOG2E
    t1 = bias_ref[1] * LOG2E
    slot_diag, slot_meta = nqb - 1, nqb

    @pl.when(jq == 0)
    def _():
        for t in range(nblk):
            vt_sc[t // NSUB, :, (t % NSUB) * BLK:(t % NSUB + 1) * BLK] = (
                v_ref[t * BLK:(t + 1) * BLK, :].astype(F32).T.astype(BF16))
        vtm_sc[...] = vm_ref[...].astype(F32).T.astype(BF16)

    qt = (q_ref[...].astype(F32) * (DIFF_HEAD_DIM ** -0.5)).T
    zero = jnp.zeros((DIFF_HEAD_DIM, QB), F32)
    qtp = (jnp.concatenate([qt[:DIFF_HEAD_DIM], zero], axis=0).astype(BF16),
           jnp.concatenate([zero, qt[DIFF_HEAD_DIM:]], axis=0).astype(BF16))

    def key_chunk(c):
        return k_ref[pl.ds(pl.multiple_of(c * QB, QB), QB), :]

    def score_chunk(k_rows, slot, add_bias):
        r = k_rows.shape[0]
        for mp in range(2):
            s = add_bias(jnp.dot(k_rows, qtp[mp], preferred_element_type=F32) * LOG2E)
            s_sc[mp, slot, 0:r, :] = s
            m_sc[mp] = jnp.maximum(m_sc[mp], _fold8(s, jnp.max))

    def value_chunk(vt_cols, slot, m8):
        r = vt_cols.shape[1]
        for mp in range(2):
            p = jnp.exp2(s_sc[mp, slot, 0:r, :].reshape(r // 8, 8, QB) - m8[mp][None])
            l_sc[mp] += jnp.sum(p, axis=0)
            acc_sc[mp] += jnp.dot(vt_cols, p.reshape(r, QB).astype(BF16), preferred_element_type=F32)

    def walk(fn_far, fn_meta, fn_diag):
        def body(c, carry):
            fn_far(c)
            return carry

        lax.fori_loop(0, jq, body, 0)
        fn_meta()
        fn_diag()

    near = t1 - far
    krow = lax.broadcasted_iota(jnp.int32, (BLK, 1), 0)
    meta_rows = jnp.where(krow >= PAD_ROWS, far, NEG)
    first = jnp.full((1, BLK), jq, jnp.int32) == 0

    def meta_bias(s):
        head = s[:, 0:BLK] + (meta_rows + jnp.where(first, near, 0.0))
        return jnp.concatenate([head, s[:, BLK:] + meta_rows], axis=1)

    m_sc[...] = jnp.full_like(m_sc, NEG)
    walk(lambda c: score_chunk(key_chunk(c), c, lambda s: s + far),
         lambda: score_chunk(km_ref[...], slot_meta, meta_bias),
         lambda: score_chunk(key_chunk(jq), slot_diag, lambda s: s + diag_ref[...]))

    @pl.when(jq >= 1)
    def _():
        for mp in range(2):
            fixed = s_sc[mp, jq - 1, QB - BLK:QB, 0:BLK] + near
            s_sc[mp, jq - 1, QB - BLK:QB, 0:BLK] = fixed
            m_sc[mp, :, 0:BLK] = jnp.maximum(m_sc[mp, :, 0:BLK], _fold8(fixed, jnp.max))

    m8 = [jnp.broadcast_to(jnp.max(m_sc[mp], axis=0, keepdims=True), (8, QB)) for mp in range(2)]
    l_sc[...] = jnp.zeros_like(l_sc)
    acc_sc[...] = jnp.zeros_like(acc_sc)

    walk(lambda c: value_chunk(vt_sc[c], c, m8),
         lambda: value_chunk(vtm_sc[...], slot_meta, m8),
         lambda: value_chunk(vt_sc[jq], slot_diag, m8))

    lam = _attn_lambda(lamv_ref, lambda_init)
    l0 = jnp.sum(l_sc[0], axis=0, keepdims=True)
    l1 = jnp.sum(l_sc[1], axis=0, keepdims=True)
    ot = acc_sc[0] / l0 - lam * (acc_sc[1] / l1)
    ot = ot * lax.rsqrt(jnp.mean(ot * ot, axis=0, keepdims=True) + EPS)
    y_ref[...] = (ot.T * sw_ref[...] * (1.0 - lambda_init)).astype(y_ref.dtype)


def _attn_meta_kernel(q_ref, k_ref, v_ref, bias_ref, lamv_ref, sw_ref, y_ref, *, lambda_init):
    qs = _scaled_q(q_ref)
    kb = k_ref[...]
    colmask = jnp.where(lax.broadcasted_iota(jnp.int32, (1, BLK), 1) >= PAD_ROWS, 0.0, NEG)
    outs = []
    bias = bias_ref[0].T + colmask
    for mp in range(2):
        s = lax.dot_general(qs[mp], kb[:, mp * DIFF_HEAD_DIM:(mp + 1) * DIFF_HEAD_DIM],
                            (((1,), (1,)), ((), ())), preferred_element_type=F32) + bias
        p = jnp.exp(s - jnp.max(s, axis=-1, keepdims=True))
        acc = jnp.dot(p.astype(BF16), v_ref[...], preferred_element_type=F32)
        outs.append(acc / jnp.sum(p, axis=-1, keepdims=True))
    o = outs[0] - _attn_lambda(lamv_ref, lambda_init) * outs[1]
    o = _rms(o, sw_ref[...]) * (1.0 - lambda_init)
    row = lax.broadcasted_iota(jnp.int32, (BLK, 1), 0)
    y_ref[...] = jnp.where(row >= PAD_ROWS, o, 0.0).astype(y_ref.dtype)


def _attn(zd, rel_bias, bias_tiles, bias_diag, lam_vec, subln_w, nreal, nblk, lambda_init):
    nqb = nblk * BLK // QB
    nq_total = nreal * BLK // QB
    seq = nblk * BLK
    kcol, vcol = HEADS, 2 * HEADS
    y_main = pl.pallas_call(
        functools.partial(_attn_kernel, nqb=nqb, nblk=nblk, lambda_init=lambda_init),
        out_shape=jax.ShapeDtypeStruct((nreal * BLK, WIDTH), BF16),
        grid=(HEADS, nq_total),
        in_specs=[pl.BlockSpec(memory_space=pltpu.SMEM),
                  pl.BlockSpec((QB, HEAD_W), lambda h, i: (i, h)),
                  pl.BlockSpec((seq, HEAD_W), lambda h, i: (i // nqb, kcol + h)),
                  pl.BlockSpec((seq, HEAD_W), lambda h, i: (i // nqb, vcol + h)),
                  pl.BlockSpec((BLK, HEAD_W), lambda h, i: (nreal, kcol + h)),
                  pl.BlockSpec((BLK, HEAD_W), lambda h, i: (nreal, vcol + h)),
                  pl.BlockSpec((None, 2, BLK, BLK), lambda h, i: (h, 0, 0, 0)),
                  pl.BlockSpec((None, QB, QB), lambda h, i: (h, 0, 0)),
                  pl.BlockSpec((4, DIFF_HEAD_DIM), lambda h, i: (0, 0)),
                  pl.BlockSpec((1, HEAD_W), lambda h, i: (0, 0))],
        out_specs=pl.BlockSpec((QB, HEAD_W), lambda h, i: (i, h)),
        scratch_shapes=[pltpu.VMEM((2, nqb + 1, QB, QB), F32),
                        pltpu.VMEM((nqb, HEAD_W, QB), BF16),
                        pltpu.VMEM((HEAD_W, BLK), BF16),
                        pltpu.VMEM((2, 8, QB), F32),
                        pltpu.VMEM((2, 8, QB), F32),
                        pltpu.VMEM((2, HEAD_W, QB), F32)],
        compiler_params=_cparams(("arbitrary", "arbitrary")),
        name="diff_attn",
    )(rel_bias, zd, zd, zd, zd, zd, bias_tiles, bias_diag, lam_vec, subln_w)
    y_meta = pl.pallas_call(
        functools.partial(_attn_meta_kernel, lambda_init=lambda_init),
        out_shape=jax.ShapeDtypeStruct((BLK, WIDTH), BF16),
        grid=(HEADS,),
        in_specs=[pl.BlockSpec((BLK, HEAD_W), lambda h: (nreal, h)),
                  pl.BlockSpec((BLK, HEAD_W), lambda h: (nreal, kcol + h)),
                  pl.BlockSpec((BLK, HEAD_W), lambda h: (nreal, vcol + h)),
                  pl.BlockSpec((None, 2, BLK, BLK), lambda h: (h, 0, 0, 0)),
                  pl.BlockSpec((4, DIFF_HEAD_DIM), lambda h: (0, 0)),
                  pl.BlockSpec((1, HEAD_W), lambda h: (0, 0))],
        out_specs=pl.BlockSpec((BLK, HEAD_W), lambda h: (0, h)),
        compiler_params=_cparams(("arbitrary",)),
        name="diff_attn_meta",
    )(zd, zd, zd, bias_tiles, lam_vec, subln_w)
    return jnp.concatenate([y_main, y_meta], axis=0)


def _largest_tile(rows, cap, align=16):
    best = align
    for t in range(align, cap + 1, align):
        if rows % t == 0:
            best = t
    return best


def _forward(x, meta_tokens, rel_bias, hgrn_lower_bounds, norm_mix_pre, norm_mix_post, norm_ffn_pre,
             norm_ffn_post, w_in, lru_conv_w, lru_conv_b, lru_w_a, lru_b_a, lru_w_x, lru_b_x, lru_lambda,
             pool_w, pool_scale, hgrn_norm, diff_lambda, diff_subln, w_branch, w_out, ffn_w_gu, ffn_w_down):
    bsz, seq, _ = x.shape
    nblk = seq // BLK
    nreal = bsz * nblk
    rows = (nreal + 1) * BLK
    rows_real = nreal * BLK
    tm_big = _largest_tile(rows, 640)
    tm_epi = _largest_tile(rows, 640)
    tm_down = _largest_tile(rows, 320)
    tm_last = _largest_tile(rows_real, 256)

    def vec(a):
        return a.reshape(1, -1)

    lbs, bias_tiles, bias_diag = _prologue(hgrn_lower_bounds, rel_bias)
    h, hn = _embed(x.reshape(rows_real, D_MODEL), meta_tokens, vec(norm_mix_pre[0]), nreal)

    for layer in range(DEPTH):
        lambda_init = 0.8 - 0.6 * math.exp(-0.3 * layer)
        z = _mixer_in_proj(hn, w_in, layer, 0, 4 * WIDTH, F32, tm_big, 1024, 1024)
        z_b = _mixer_in_proj(hn, w_in, layer, 4 * WIDTH, 3 * WIDTH, F32, tm_big, 3 * WIDTH, WIDTH)
        zd = _mixer_in_proj(hn, w_in, layer, 7 * WIDTH, 3 * WIDTH, BF16, tm_big, 3 * WIDTH, WIDTH)
        y_a = _lru(z, lru_conv_w[layer], vec(lru_conv_b[layer]), lru_w_a[layer], vec(lru_b_a[layer]),
                   lru_w_x[layer], vec(lru_b_x[layer]), vec(lru_lambda[layer]), nreal, nblk)
        y_b = _pool(z, pool_w[layer], vec(pool_scale[layer]), nreal, nblk)
        y_c = _hgrn(z, z_b, lbs[layer:layer + 1], vec(hgrn_norm[layer]), nreal, nblk)
        y_d = _attn(zd, rel_bias, bias_tiles, bias_diag, diff_lambda[layer], vec(diff_subln[layer]),
                    nreal, nblk, lambda_init)
        merged = _gate_merge(hn, w_in, (y_a, y_b, y_c, y_d), w_branch, layer, tm_big, 256)
        h, hn = _out_proj(merged, _cast_layer_bf16(w_out, layer, 512), h, vec(norm_mix_post[layer]),
                          vec(norm_ffn_pre[layer]), tm_epi)
        a = _swiglu_up(hn, ffn_w_gu, layer, tm_big, 512)
        last = layer == DEPTH - 1
        w_next = vec(norm_mix_pre[layer + 1]) if not last else vec(norm_mix_pre[layer])
        h, hn = _down_proj(a, _cast_layer_bf16(ffn_w_down, layer, 512), h, vec(norm_ffn_post[layer]), w_next,
                           rows_real if last else rows, tm_last if last else tm_down, not last)
    return h.reshape(bsz, seq, D_MODEL)


def kernel(x, meta_tokens, rel_bias, hgrn_lower_bounds, norm_mix_pre, norm_mix_post, norm_ffn_pre, norm_ffn_post, w_in, lru_conv_w, lru_conv_b, lru_w_a, lru_b_a, lru_w_x, lru_b_x, lru_lambda, pool_w, pool_scale, hgrn_norm, diff_lambda, diff_subln, w_branch, w_out, ffn_w_gu, ffn_w_down):
    return _forward(x, meta_tokens, rel_bias, hgrn_lower_bounds, norm_mix_pre, norm_mix_post, norm_ffn_pre,
                    norm_ffn_post, w_in, lru_conv_w, lru_conv_b, lru_w_a, lru_b_a, lru_w_x, lru_b_x, lru_lambda,
                    pool_w, pool_scale, hgrn_norm, diff_lambda, diff_subln, w_branch, w_out, ffn_w_gu, ffn_w_down)
```

```python
import functools
import math

import numpy as np
import jax
import jax.numpy as jnp
from jax import lax
from jax.experimental import pallas as pl
from jax.experimental.pallas import tpu as pltpu

F32 = jnp.float32
BF16 = jnp.bfloat16

D_MODEL = 2048
SEQ = 2048
DEPTH = 2
N_META = 16
BLK = 128
PAD_ROWS = BLK - N_META
QB = 512
NSUB = QB // BLK
WIDTH = 512
HEADS = 4
HEAD_W = 128
CHUNK = 64
SUB = 16
LRU_C = 8.0
POOL_WINDOWS = (2, 4, 8, 16)
DIFF_HEAD_DIM = 64
REL_BUCKETS = 32
REL_MAX_DIST = 128
FFN_HIDDEN = 5632
MIX_COLS = 10 * WIDTH
NEG = -1e30
EPS = 1e-6
LOG2E = math.log2(math.e)
VMEM_LIMIT = 56 * 1024 * 1024


def _cparams(sem):
    return pltpu.CompilerParams(dimension_semantics=sem, vmem_limit_bytes=VMEM_LIMIT)


def _rms(x, w):
    return x * lax.rsqrt(jnp.mean(x * x, axis=-1, keepdims=True) + EPS) * w


def _log_sigmoid(z):
    return -(jnp.maximum(-z, 0.0) + jnp.log(1.0 + jnp.exp(-jnp.abs(z))))


def _sigmoid(z):
    return 1.0 / (1.0 + jnp.exp(-z))


def _gelu_tanh(x):
    c = math.sqrt(2.0 / math.pi)
    return 0.5 * x * (1.0 + jnp.tanh(c * (x + 0.044715 * (x * x * x))))


def _shift_rows(x, s, fill, row):
    return jnp.where(row >= s, pltpu.roll(x, s, axis=0), fill)


def _bucket_tiles():
    r = np.arange(BLK)[None, :]
    c = np.arange(BLK)[:, None]
    max_exact = REL_BUCKETS // 2

    def bucket(n):
        nf = np.maximum(n, 1).astype(np.float32)
        large = max_exact + (np.log(nf / np.float32(max_exact)) / np.float32(math.log(REL_MAX_DIST / max_exact))
                             * np.float32(REL_BUCKETS - max_exact)).astype(np.int32)
        large = np.minimum(large, REL_BUCKETS - 1)
        return np.where(n < max_exact, n, large).astype(np.int32)

    d0 = r - c
    t0 = np.where(d0 >= 0, bucket(np.maximum(d0, 0)), -1)
    t1 = bucket(BLK + r - c)
    return np.stack([t0, t1]).astype(np.int32)


def _prologue_kernel(lbraw_ref, relb_ref, idx_ref, lb_ref, bias_ref, diag_ref):
    raw = lbraw_ref[...]
    mx = jnp.max(raw, axis=0, keepdims=True)
    e = jnp.exp(raw - mx)
    sm = e / jnp.sum(e, axis=0, keepdims=True)
    cum = sm[0:1]
    lb_ref[0:1, :] = cum - sm[0:1]
    for l in range(1, DEPTH):
        cum = cum + sm[l:l + 1]
        lb_ref[l:l + 1, :] = cum - sm[0:1]
    for t in range(2):
        idx = idx_ref[t]
        for hd in range(HEADS):
            acc = jnp.zeros((BLK, BLK), F32)
            for bk in range(REL_BUCKETS):
                acc = jnp.where(idx == bk, relb_ref[bk, hd], acc)
            bias_ref[hd, t] = jnp.where(idx < 0, NEG, acc)
    for hd in range(HEADS):
        far = relb_ref[REL_BUCKETS - 1, hd]
        for kb in range(NSUB):
            for qb in range(NSUB):
                delta = qb - kb
                if delta == 0:
                    blk = bias_ref[hd, 0]
                elif delta == 1:
                    blk = bias_ref[hd, 1]
                else:
                    blk = jnp.full((BLK, BLK), far if delta > 1 else NEG, F32)
                diag_ref[hd, kb * BLK:(kb + 1) * BLK, qb * BLK:(qb + 1) * BLK] = blk * LOG2E


def _prologue(hgrn_lower_bounds, rel_bias):
    idx = jnp.asarray(_bucket_tiles())
    vmem = pl.BlockSpec(memory_space=pltpu.VMEM)
    return pl.pallas_call(
        _prologue_kernel,
        out_shape=(jax.ShapeDtypeStruct((DEPTH, WIDTH), F32),
                   jax.ShapeDtypeStruct((HEADS, 2, BLK, BLK), F32),
                   jax.ShapeDtypeStruct((HEADS, QB, QB), F32)),
        in_specs=[vmem, pl.BlockSpec(memory_space=pltpu.SMEM), vmem],
        out_specs=(vmem, vmem, vmem),
        name="prologue",
    )(hgrn_lower_bounds, rel_bias, idx)


def _embed_kernel(x_ref, meta_ref, w_ref, h_ref, hn_ref, *, nreal):
    i = pl.program_id(0)

    @pl.when(i < nreal)
    def _():
        h_ref[...] = x_ref[...]

    @pl.when(i == nreal)
    def _():
        h_ref[0:PAD_ROWS, :] = jnp.zeros((PAD_ROWS, D_MODEL), F32)
        h_ref[PAD_ROWS:BLK, :] = meta_ref[...]

    hn_ref[...] = _rms(h_ref[...], w_ref[...]).astype(BF16)


def _embed(x2d, meta, w_pre, nreal):
    rows = (nreal + 1) * BLK
    return pl.pallas_call(
        functools.partial(_embed_kernel, nreal=nreal),
        out_shape=(jax.ShapeDtypeStruct((rows, D_MODEL), F32),
                   jax.ShapeDtypeStruct((rows, D_MODEL), BF16)),
        grid=(nreal + 1,),
        in_specs=[pl.BlockSpec((BLK, D_MODEL), lambda i: (jnp.minimum(i, nreal - 1), 0)),
                  pl.BlockSpec((N_META, D_MODEL), lambda i: (0, 0)),
                  pl.BlockSpec((1, D_MODEL), lambda i: (0, 0))],
        out_specs=(pl.BlockSpec((BLK, D_MODEL), lambda i: (i, 0)),
                   pl.BlockSpec((BLK, D_MODEL), lambda i: (i, 0))),
        compiler_params=_cparams(("arbitrary",)),
        name="embed",
    )(x2d, meta, w_pre)


def _cast_tiles_once(pairs):
    @pl.when(pl.program_id(1) == 0)
    def _():
        for src, dst in pairs:
            dst[...] = src[...].astype(BF16)


def _matmul_kernel(x_ref, *rest):
    *w_refs, o_ref, wbf = rest
    wblk = w_refs[0].shape[-1]
    _cast_tiles_once([(w, wbf.at[:, j * wblk:(j + 1) * wblk]) for j, w in enumerate(w_refs)])
    o_ref[...] = jnp.dot(x_ref[...], wbf[...], preferred_element_type=F32).astype(o_ref.dtype)


def _mixer_in_proj(hn, w_in, layer, col0, ncols, out_dtype, tm, tn, wblk):
    rows, k = hn.shape
    nw = tn // wblk
    w_specs = [pl.BlockSpec((None, k, wblk), functools.partial(
        lambda n, m, j: (layer, 0, col0 // wblk + n * nw + j), j=j)) for j in range(nw)]
    return pl.pallas_call(
        _matmul_kernel,
        out_shape=jax.ShapeDtypeStruct((rows, ncols), out_dtype),
        grid=(ncols // tn, rows // tm),
        in_specs=[pl.BlockSpec((tm, k), lambda n, m: (m, 0))] + w_specs,
        out_specs=pl.BlockSpec((tm, tn), lambda n, m: (m, n)),
        scratch_shapes=[pltpu.VMEM((k, tn), BF16)],
        compiler_params=_cparams(("arbitrary", "arbitrary")),
        name="mixer_in_proj",
    )(hn, *([w_in] * nw))


def _cast_kernel(w_ref, o_ref):
    o_ref[...] = w_ref[...].astype(BF16)


def _cast_layer_bf16(w, layer, tr, col0=0, ncols=None, tc=None):
    _, r, c = w.shape
    ncols = c if ncols is None else ncols
    tc = ncols if tc is None else tc
    return pl.pallas_call(
        _cast_kernel,
        out_shape=jax.ShapeDtypeStruct((r, ncols), BF16),
        grid=(r // tr, ncols // tc),
        in_specs=[pl.BlockSpec((None, tr, tc), lambda i, j: (layer, i, col0 // tc + j))],
        out_specs=pl.BlockSpec((tr, tc), lambda i, j: (i, j)),
        compiler_params=_cparams(("arbitrary", "arbitrary")),
        name="cast_bf16",
    )(w)


def _residual_epilogue(acc, h_ref, wpost_ref, wnext_ref, hnew_ref, hn_ref):
    h_new = h_ref[...] + _rms(acc, wpost_ref[...])
    hnew_ref[...] = h_new
    if hn_ref is not None:
        hn_ref[...] = _rms(h_new, wnext_ref[...]).astype(BF16)


EPI_ROWS = 160


def _row_subtiles(tm):
    sub = next(s for s in (EPI_ROWS, 128, 64, 32, 16) if tm % s == 0)
    return [slice(r, r + sub) for r in range(0, tm, sub)]


def _gate_merge_kernel(hn_ref, g0, g1, g2, g3, y0, y1, y2, y3, wb_ref, o_ref, gbf, wbbf):
    _cast_tiles_once([(g, gbf.at[k]) for k, g in enumerate((g0, g1, g2, g3))] + [(wb_ref, wbbf)])
    hn = hn_ref[...]
    acc = None
    for k, y_ref in enumerate((y0, y1, y2, y3)):
        gate = _sigmoid(jnp.dot(hn, gbf[k], preferred_element_type=F32))
        proj = jnp.dot(y_ref[...], wbbf[k], preferred_element_type=F32)
        acc = gate * proj if acc is None else acc + gate * proj
    o_ref[...] = acc.astype(o_ref.dtype)


def _gate_merge(hn, w_in, ys, w_branch, layer, tm, tn):
    rows, k = hn.shape
    gate_specs = [
        pl.BlockSpec((None, k, tn), functools.partial(
            lambda n, m, base: (layer, 0, base + n), base=(MIX_COLS + br * D_MODEL) // tn))
        for br in range(4)]
    y_specs = [pl.BlockSpec((tm, WIDTH), lambda n, m: (m, 0)) for _ in range(4)]
    return pl.pallas_call(
        _gate_merge_kernel,
        out_shape=jax.ShapeDtypeStruct((rows, D_MODEL), BF16),
        grid=(D_MODEL // tn, rows // tm),
        in_specs=[pl.BlockSpec((tm, k), lambda n, m: (m, 0))] + gate_specs + y_specs
                 + [pl.BlockSpec((None, 4, WIDTH, tn), lambda n, m: (layer, 0, 0, n))],
        out_specs=pl.BlockSpec((tm, tn), lambda n, m: (m, n)),
        scratch_shapes=[pltpu.VMEM((4, k, tn), BF16), pltpu.VMEM((4, WIDTH, tn), BF16)],
        compiler_params=_cparams(("arbitrary", "arbitrary")),
        name="gate_merge",
    )(hn, w_in, w_in, w_in, w_in, *ys, w_branch)


def _out_proj_kernel(x_ref, w_ref, h_ref, wpost_ref, wnext_ref, hnew_ref, hn_ref):
    for rs in _row_subtiles(x_ref.shape[0]):
        acc = jnp.dot(x_ref[rs, :], w_ref[...], preferred_element_type=F32)
        _residual_epilogue(acc, h_ref.at[rs, :], wpost_ref, wnext_ref, hnew_ref.at[rs, :], hn_ref.at[rs, :])


def _out_proj(merged, w_out, h, w_post, w_next, tm):
    rows = h.shape[0]
    row_spec = pl.BlockSpec((tm, D_MODEL), lambda m: (m, 0))
    return pl.pallas_call(
        _out_proj_kernel,
        out_shape=(jax.ShapeDtypeStruct((rows, D_MODEL), F32),
                   jax.ShapeDtypeStruct((rows, D_MODEL), BF16)),
        grid=(rows // tm,),
        in_specs=[row_spec, _resident((D_MODEL, D_MODEL), (0, 0)), row_spec,
                  _pspec((1, D_MODEL)), _pspec((1, D_MODEL))],
        out_specs=(row_spec, row_spec),
        compiler_params=_cparams(("arbitrary",)),
        name="out_proj",
    )(merged, w_out, h, w_post, w_next)


def _swiglu_up_kernel(x_ref, wg_ref, wu_ref, o_ref, wgbf, wubf):
    _cast_tiles_once([(wg_ref, wgbf), (wu_ref, wubf)])
    x = x_ref[...]
    g = jnp.dot(x, wgbf[...], preferred_element_type=F32)
    u = jnp.dot(x, wubf[...], preferred_element_type=F32)
    o_ref[...] = (g * _sigmoid(g) * u).astype(o_ref.dtype)


def _swiglu_up(hn, w_gu, layer, tm, tn):
    rows, k = hn.shape
    nt = FFN_HIDDEN // tn
    return pl.pallas_call(
        _swiglu_up_kernel,
        out_shape=jax.ShapeDtypeStruct((rows, FFN_HIDDEN), BF16),
        grid=(nt, rows // tm),
        in_specs=[pl.BlockSpec((tm, k), lambda n, m: (m, 0)),
                  pl.BlockSpec((None, k, tn), lambda n, m: (layer, 0, n)),
                  pl.BlockSpec((None, k, tn), lambda n, m: (layer, 0, nt + n))],
        out_specs=pl.BlockSpec((tm, tn), lambda n, m: (m, n)),
        scratch_shapes=[pltpu.VMEM((k, tn), BF16), pltpu.VMEM((k, tn), BF16)],
        compiler_params=_cparams(("arbitrary", "arbitrary")),
        name="swiglu_up",
    )(hn, w_gu, w_gu)


def _down_proj_kernel(a_ref, w_ref, h_ref, wpost_ref, wnext_ref, hnew_ref, hn_ref=None):
    for rs in _row_subtiles(a_ref.shape[0]):
        acc = jnp.dot(a_ref[rs, :], w_ref[...], preferred_element_type=F32)
        _residual_epilogue(acc, h_ref.at[rs, :], wpost_ref, wnext_ref, hnew_ref.at[rs, :],
                           None if hn_ref is None else hn_ref.at[rs, :])


def _down_proj(a, w_down, h, w_post, w_next, rows_out, tm, emit_hn):
    row_spec = lambda w: pl.BlockSpec((tm, w), lambda m: (m, 0))
    out_shape = [jax.ShapeDtypeStruct((rows_out, D_MODEL), F32)]
    out_specs = [row_spec(D_MODEL)]
    if emit_hn:
        out_shape.append(jax.ShapeDtypeStruct((rows_out, D_MODEL), BF16))
        out_specs.append(row_spec(D_MODEL))
    res = pl.pallas_call(
        _down_proj_kernel,
        out_shape=tuple(out_shape),
        grid=(rows_out // tm,),
        in_specs=[row_spec(FFN_HIDDEN), _resident((FFN_HIDDEN, D_MODEL), (0, 0)), row_spec(D_MODEL),
                  _pspec((1, D_MODEL)), _pspec((1, D_MODEL))],
        out_specs=tuple(out_specs),
        compiler_params=_cparams(("arbitrary",)),
        name="down_proj",
    )(a, w_down, h, w_post, w_next)
    return res if emit_hn else (res[0], None)


def _row_block(i, nreal):
    return (i + nreal) % (nreal + 1)


def _zspec(col_block, nreal):
    return pl.BlockSpec((BLK, WIDTH), lambda i: (_row_block(i, nreal), col_block))


def _pspec(shape):
    nd = len(shape)
    return pl.BlockSpec(shape, lambda *_: (0,) * nd)


def _resident(shape, index):
    return pl.BlockSpec(shape, lambda *_: index, pipeline_mode=pl.Buffered(1))


def _lru_kernel(u_ref, gate_ref, cw_ref, cb_ref, wa_ref, ba_ref, wx_ref, bx_ref, lam_ref, y_ref,
                ubuf, hst, hist_meta, h_meta, *, nblk):
    i = pl.program_id(0)
    is_meta = i == 0

    @pl.when(is_meta)
    def _():
        ubuf[0:8, :] = jnp.zeros((8, WIDTH), F32)
        hst[...] = jnp.zeros_like(hst)

    @pl.when(jnp.logical_and(i >= 1, (i - 1) % nblk == 0))
    def _():
        ubuf[0:8, :] = hist_meta[...]
        hst[...] = h_meta[...]

    u = u_ref[...]
    ubuf[8:8 + BLK, :] = u
    cw = cw_ref[...]
    xc = (cb_ref[...] + cw[3:4] * u + cw[2:3] * ubuf[7:7 + BLK, :]
          + cw[1:2] * ubuf[6:6 + BLK, :] + cw[0:1] * ubuf[5:5 + BLK, :])
    xb = xc.astype(BF16)
    ra, ia = [], []
    for hd in range(HEADS):
        sl = slice(hd * HEAD_W, (hd + 1) * HEAD_W)
        ra.append(jnp.dot(xb[:, sl], wa_ref[hd].astype(BF16), preferred_element_type=F32))
        ia.append(jnp.dot(xb[:, sl], wx_ref[hd].astype(BF16), preferred_element_type=F32))
    r = _sigmoid(jnp.concatenate(ra, axis=1) + ba_ref[...])
    ig = _sigmoid(jnp.concatenate(ia, axis=1) + bx_ref[...])
    lam = lam_ref[...]
    softplus_neg_lam = jnp.maximum(-lam, 0.0) + jnp.log1p(jnp.exp(-jnp.abs(lam)))
    log_a = -LRU_C * r * softplus_neg_lam
    a = jnp.exp(log_a)
    bb = jnp.sqrt(-jnp.tanh(log_a) * (a * a + 1.0)) * (ig * xc)
    row = lax.broadcasted_iota(jnp.int32, (BLK, 1), 0)
    bb = jnp.where(row >= PAD_ROWS * is_meta.astype(jnp.int32), bb, 0.0)

    acum, bcum = a, bb
    s = 1
    while s < BLK:
        a_sh = _shift_rows(acum, s, 1.0, row)
        b_sh = _shift_rows(bcum, s, 0.0, row)
        bcum = acum * b_sh + bcum
        acum = acum * a_sh
        s *= 2
    h = acum * hst[0:1, :] + bcum
    y_ref[...] = (h * _gelu_tanh(gate_ref[...])).astype(y_ref.dtype)

    hist = u[BLK - 8:BLK, :]
    hlast = jnp.broadcast_to(h[BLK - 1:BLK, :], (8, WIDTH))
    ubuf[0:8, :] = hist
    hst[...] = hlast

    @pl.when(is_meta)
    def _():
        hist_meta[...] = hist
        h_meta[...] = hlast


def _lru(z, cw, cb, wa, ba, wx, bx, lam, nreal, nblk):
    rows = z.shape[0]
    return pl.pallas_call(
        functools.partial(_lru_kernel, nblk=nblk),
        out_shape=jax.ShapeDtypeStruct((rows, WIDTH), BF16),
        grid=(nreal + 1,),
        in_specs=[_zspec(0, nreal), _zspec(1, nreal),
                  _pspec((4, WIDTH)), _pspec((1, WIDTH)),
                  _pspec((HEADS, HEAD_W, HEAD_W)), _pspec((1, WIDTH)),
                  _pspec((HEADS, HEAD_W, HEAD_W)), _pspec((1, WIDTH)), _pspec((1, WIDTH))],
        out_specs=pl.BlockSpec((BLK, WIDTH), lambda i: (_row_block(i, nreal), 0)),
        scratch_shapes=[pltpu.VMEM((8 + BLK, WIDTH), F32), pltpu.VMEM((8, WIDTH), F32),
                        pltpu.VMEM((8, WIDTH), F32), pltpu.VMEM((8, WIDTH), F32)],
        compiler_params=_cparams(("arbitrary",)),
        name="rg_lru",
    )(z, z, cw, cb, wa, ba, wx, bx, lam)


POOL_HIST = 16


def _pool_kernel(u_ref, pw_ref, ps_ref, y_ref, ubuf, hist_meta, *, nblk):
    i = pl.program_id(0)
    is_meta = i == 0

    @pl.when(is_meta)
    def _():
        ubuf[0:POOL_HIST, :] = jnp.zeros((POOL_HIST, WIDTH), F32)

    @pl.when(jnp.logical_and(i >= 1, (i - 1) % nblk == 0))
    def _():
        ubuf[0:POOL_HIST, :] = hist_meta[...]

    u = u_ref[...]
    ubuf[POOL_HIST:POOL_HIST + BLK, :] = u
    row = lax.broadcasted_iota(jnp.int32, (BLK, 1), 0)
    meta_i = is_meta.astype(jnp.int32)
    pos1 = row + 1 - PAD_ROWS * meta_i + 2 * POOL_HIST * (1 - meta_i)
    outs = []
    for g, win in enumerate(POOL_WINDOWS):
        sl = slice(g * HEAD_W, (g + 1) * HEAD_W)
        acc = u[:, sl]
        for d in range(1, win):
            acc = acc + ubuf[POOL_HIST - d:POOL_HIST - d + BLK, sl]
        count = jnp.clip(pos1, 1, win).astype(F32)
        pooled = acc / count - u[:, sl]
        outs.append(jnp.dot(pooled.astype(BF16), pw_ref[g].astype(BF16), preferred_element_type=F32))
    y_ref[...] = (jnp.concatenate(outs, axis=1) * ps_ref[...]).astype(y_ref.dtype)

    hist = u[BLK - POOL_HIST:BLK, :]
    ubuf[0:POOL_HIST, :] = hist

    @pl.when(is_meta)
    def _():
        hist_meta[...] = hist


def _pool(z, pw, ps, nreal, nblk):
    rows = z.shape[0]
    return pl.pallas_call(
        functools.partial(_pool_kernel, nblk=nblk),
        out_shape=jax.ShapeDtypeStruct((rows, WIDTH), BF16),
        grid=(nreal + 1,),
        in_specs=[_zspec(2, nreal), _pspec((4, HEAD_W, HEAD_W)), _pspec((1, WIDTH))],
        out_specs=pl.BlockSpec((BLK, WIDTH), lambda i: (_row_block(i, nreal), 0)),
        scratch_shapes=[pltpu.VMEM((POOL_HIST + BLK, WIDTH), F32), pltpu.VMEM((POOL_HIST, WIDTH), F32)],
        compiler_params=_cparams(("arbitrary",)),
        name="ms_pool",
    )(z, pw, ps)


def _hgrn_chunk(q, z, v, lbh, state_t, valid, ones_bf):
    ls = _log_sigmoid(z)
    x1 = jnp.log(lbh)
    x2 = jnp.log1p(-lbh) + ls
    mx = jnp.maximum(x1, x2)
    g = mx + jnp.log(1.0 + jnp.exp(-jnp.abs(x1 - x2)))
    k = (1.0 - lbh) * _sigmoid(-z)
    if valid is not None:
        g = jnp.where(valid, g, 0.0)
    row = lax.broadcasted_iota(jnp.int32, (CHUNK, 1), 0)
    b = g * LOG2E
    s = 1
    while s < CHUNK:
        b = b + _shift_rows(b, s, 0.0, row)
        s *= 2
    b_last = b[CHUNK - 1:CHUNK, :]

    qe = (q * jnp.exp2(b)).astype(BF16)
    o = lax.dot_general(qe, state_t.astype(BF16), (((1,), (1,)), ((), ())), preferred_element_type=F32)

    col = lax.broadcasted_iota(jnp.int32, (SUB, CHUNK), 1)
    rsub = lax.broadcasted_iota(jnp.int32, (SUB, CHUNK), 0)
    lane = lax.broadcasted_iota(jnp.int32, (SUB, HEAD_W), 1)
    s_rows = []
    for blk in range(CHUNK // SUB):
        lo = blk * SUB
        bi = b[lo:lo + SUB, :]
        qi = q[lo:lo + SUB, :]
        ki = k[lo:lo + SUB, :]
        parts = []
        for sr in range(SUB):
            e = jnp.exp2(jnp.minimum(bi - bi[sr:sr + 1, :], 0.0))
            parts.append(qi * e * ki[sr:sr + 1, :])
        m3 = jnp.concatenate(parts, axis=0).astype(BF16)
        red = jnp.dot(m3, ones_bf, preferred_element_type=F32)
        diag = jnp.zeros((SUB, HEAD_W), F32)
        for sr in range(SUB):
            diag = jnp.where(lane == lo + sr, red[sr * SUB:(sr + 1) * SUB, :], diag)
        diag = diag[:, :CHUNK]
        s_blk = jnp.where(jnp.logical_and(col >= lo, col - lo <= rsub), diag, 0.0)
        if blk > 0:
            b0 = b[lo - 1:lo, :]
            kt = (k * jnp.exp2(jnp.minimum(b0 - b, 0.0))).astype(BF16)
            qd = (qi * jnp.exp2(bi - b0)).astype(BF16)
            off = lax.dot_general(qd, kt, (((1,), (1,)), ((), ())), preferred_element_type=F32)
            s_blk = jnp.where(col < lo, off, s_blk)
        s_rows.append(s_blk)
    scores = jnp.concatenate(s_rows, axis=0).astype(BF16)
    vb = v.astype(BF16)
    o = o + jnp.dot(scores, vb, preferred_element_type=F32)

    kd = (k * jnp.exp2(b_last - b)).astype(BF16)
    upd = lax.dot_general(vb, kd, (((0,), (0,)), ((), ())), preferred_element_type=F32)
    new_state_t = state_t * jnp.exp2(b_last) + upd
    return o, new_state_t


def _hgrn_kernel(q_ref, f_ref, v_ref, og_ref, lb_ref, nw_ref, y_ref, state, state_meta, *, nblk):
    i = pl.program_id(0)
    is_meta = i == 0

    @pl.when(is_meta)
    def _():
        state[...] = jnp.zeros_like(state)

    @pl.when(jnp.logical_and(i >= 1, (i - 1) % nblk == 0))
    def _():
        state[...] = state_meta[...]

    ones_bf = jnp.ones((HEAD_W, HEAD_W), BF16)
    nw = nw_ref[...]
    for hd in range(HEADS):
        sl = slice(hd * HEAD_W, (hd + 1) * HEAD_W)
        lbh = lb_ref[:, sl]
        st = state[hd]
        for c in range(BLK // CHUNK):
            rs = slice(c * CHUNK, (c + 1) * CHUNK)
            rowg = lax.broadcasted_iota(jnp.int32, (CHUNK, 1), 0) + c * CHUNK
            valid = rowg >= PAD_ROWS * is_meta.astype(jnp.int32)
            o, st = _hgrn_chunk(q_ref[rs, sl], f_ref[rs, sl], v_ref[rs, sl], lbh, st, valid, ones_bf)
            og = og_ref[rs, sl]
            y_ref[rs, sl] = (_rms(o, nw) * (og * _sigmoid(og))).astype(y_ref.dtype)
        state[hd] = st

    @pl.when(is_meta)
    def _():
        state_meta[...] = state[...]


def _hgrn(z_a, z_b, lb, nw, nreal, nblk):
    rows = z_a.shape[0]
    return pl.pallas_call(
        functools.partial(_hgrn_kernel, nblk=nblk),
        out_shape=jax.ShapeDtypeStruct((rows, WIDTH), BF16),
        grid=(nreal + 1,),
        in_specs=[_zspec(3, nreal), _zspec(0, nreal), _zspec(1, nreal), _zspec(2, nreal),
                  _pspec((1, WIDTH)), _pspec((1, HEAD_W))],
        out_specs=pl.BlockSpec((BLK, WIDTH), lambda i: (_row_block(i, nreal), 0)),
        scratch_shapes=[pltpu.VMEM((HEADS, HEAD_W, HEAD_W), F32), pltpu.VMEM((HEADS, HEAD_W, HEAD_W), F32)],
        compiler_params=_cparams(("arbitrary",)),
        name="hgrn2",
    )(z_a, z_b, z_b, z_b, lb, nw)


def _attn_lambda(lamv_ref, lambda_init):
    lv = lamv_ref[...]
    return (jnp.exp(jnp.sum(lv[0:1] * lv[1:2], axis=-1, keepdims=True))
            - jnp.exp(jnp.sum(lv[2:3] * lv[3:4], axis=-1, keepdims=True)) + lambda_init)


def _scaled_q(q_ref):
    qf = (q_ref[...].astype(F32) * (DIFF_HEAD_DIM ** -0.5)).astype(BF16)
    return qf[:, :DIFF_HEAD_DIM], qf[:, DIFF_HEAD_DIM:]


def _fold8(x, op):
    r, c = x.shape
    return op(x.reshape(r // 8, 8, c), axis=0)


def _attn_kernel(relb_ref, q_ref, k_ref, v_ref, km_ref, vm_ref, bias_ref, diag_ref, lamv_ref, sw_ref, y_ref,
                 s_sc, vt_sc, vtm_sc, m_sc, l_sc, acc_sc, *, nqb, nblk, lambda_init):
    hd = pl.program_id(0)
    jq = pl.program_id(1) % nqb
    far = relb_ref[REL_BUCKETS - 1, hd] * LOG2E
    t1 = bias_ref[1] * LOG2E
    slot_diag, slot_meta = nqb - 1, nqb

    @pl.when(jq == 0)
    def _():
        for t in range(nblk):
            vt_sc[t // NSUB, :, (t % NSUB) * BLK:(t % NSUB + 1) * BLK] = (
                v_ref[t * BLK:(t + 1) * BLK, :].astype(F32).T.astype(BF16))
        vtm_sc[...] = vm_ref[...].astype(F32).T.astype(BF16)

    qt = (q_ref[...].astype(F32) * (DIFF_HEAD_DIM ** -0.5)).T
    zero = jnp.zeros((DIFF_HEAD_DIM, QB), F32)
    qtp = (jnp.concatenate([qt[:DIFF_HEAD_DIM], zero], axis=0).astype(BF16),
           jnp.concatenate([zero, qt[DIFF_HEAD_DIM:]], axis=0).astype(BF16))

    def key_chunk(c):
        return k_ref[pl.ds(pl.multiple_of(c * QB, QB), QB), :]

    def score_chunk(k_rows, slot, add_bias, r0=0, c0=0):
        r = k_rows.shape[0]
        for mp in range(2):
            s = add_bias(jnp.dot(k_rows, qtp[mp][:, c0:], preferred_element_type=F32) * LOG2E)
            s_sc[mp, slot, r0:r0 + r, c0:] = s
            m_sc[mp, :, c0:] = jnp.maximum(m_sc[mp, :, c0:], _fold8(s, jnp.max))

    def value_chunk(vt_cols, slot, m8, r0=0, c0=0):
        r = vt_cols.shape[1]
        for mp in range(2):
            p = jnp.exp2(s_sc[mp, slot, r0:r0 + r, c0:].reshape(r // 8, 8, QB - c0) - m8[mp][None, :, c0:])
            l_sc[mp, :, c0:] += jnp.sum(p, axis=0)
            acc_sc[mp, :, c0:] += jnp.dot(vt_cols, p.reshape(r, QB - c0).astype(BF16),
                                         preferred_element_type=F32)

    def diag_tiles(fn):
        for kb in range(NSUB):
            fn(kb, kb * BLK)

    def walk(fn_far, fn_meta, fn_diag):
        def body(c, carry):
            fn_far(c)
            return carry

        lax.fori_loop(0, jq, body, 0)
        fn_meta()
        fn_diag()

    near = t1 - far
    krow = lax.broadcasted_iota(jnp.int32, (BLK, 1), 0)
    meta_rows = jnp.where(krow >= PAD_ROWS, far, NEG)
    first = jnp.full((1, BLK), jq, jnp.int32) == 0

    def meta_bias(s):
        head = s[:, 0:BLK] + (meta_rows + jnp.where(first, near, 0.0))
        return jnp.concatenate([head, s[:, BLK:] + meta_rows], axis=1)

    m_sc[...] = jnp.full_like(m_sc, NEG)
    walk(lambda c: score_chunk(key_chunk(c), c, lambda s: s + far),
         lambda: score_chunk(km_ref[...], slot_meta, meta_bias),
         lambda: diag_tiles(lambda kb, c0: score_chunk(
             k_ref[pl.ds(pl.multiple_of(jq * QB + c0, BLK), BLK), :], slot_diag,
             lambda s: s + diag_ref[c0:c0 + BLK, c0:], r0=c0, c0=c0)))

    @pl.when(jq >= 1)
    def _():
        for mp in range(2):
            fixed = s_sc[mp, jq - 1, QB - BLK:QB, 0:BLK] + near
            s_sc[mp, jq - 1, QB - BLK:QB, 0:BLK] = fixed
            m_sc[mp, :, 0:BLK] = jnp.maximum(m_sc[mp, :, 0:BLK], _fold8(fixed, jnp.max))

    m8 = [jnp.broadcast_to(jnp.max(m_sc[mp], axis=0, keepdims=True), (8, QB)) for mp in range(2)]
    l_sc[...] = jnp.zeros_like(l_sc)
    acc_sc[...] = jnp.zeros_like(acc_sc)

    walk(lambda c: value_chunk(vt_sc[c], c, m8),
         lambda: value_chunk(vtm_sc[...], slot_meta, m8),
         lambda: diag_tiles(lambda kb, c0: value_chunk(
             vt_sc[jq, :, c0:c0 + BLK], slot_diag, m8, r0=c0, c0=c0)))

    lam = _attn_lambda(lamv_ref, lambda_init)
    l0 = jnp.sum(l_sc[0], axis=0, keepdims=True)
    l1 = jnp.sum(l_sc[1], axis=0, keepdims=True)
    ot = acc_sc[0] / l0 - lam * (acc_sc[1] / l1)
    ot = ot * lax.rsqrt(jnp.mean(ot * ot, axis=0, keepdims=True) + EPS)
    y_ref[...] = (ot.T * sw_ref[...] * (1.0 - lambda_init)).astype(y_ref.dtype)


def _attn_meta_kernel(q_ref, k_ref, v_ref, bias_ref, lamv_ref, sw_ref, y_ref, *, lambda_init):
    qs = _scaled_q(q_ref)
    kb = k_ref[...]
    colmask = jnp.where(lax.broadcasted_iota(jnp.int32, (1, BLK), 1) >= PAD_ROWS, 0.0, NEG)
    outs = []
    bias = bias_ref[0].T + colmask
    for mp in range(2):
        s = lax.dot_general(qs[mp], kb[:, mp * DIFF_HEAD_DIM:(mp + 1) * DIFF_HEAD_DIM],
                            (((1,), (1,)), ((), ())), preferred_element_type=F32) + bias
        p = jnp.exp(s - jnp.max(s, axis=-1, keepdims=True))
        acc = jnp.dot(p.astype(BF16), v_ref[...], preferred_element_type=F32)
        outs.append(acc / jnp.sum(p, axis=-1, keepdims=True))
    o = outs[0] - _attn_lambda(lamv_ref, lambda_init) * outs[1]
    o = _rms(o, sw_ref[...]) * (1.0 - lambda_init)
    row = lax.broadcasted_iota(jnp.int32, (BLK, 1), 0)
    y_ref[...] = jnp.where(row >= PAD_ROWS, o, 0.0).astype(y_ref.dtype)


def _attn(zd, rel_bias, bias_tiles, bias_diag, lam_vec, subln_w, nreal, nblk, lambda_init):
    nqb = nblk * BLK // QB
    nq_total = nreal * BLK // QB
    seq = nblk * BLK
    kcol, vcol = HEADS, 2 * HEADS
    y_main = pl.pallas_call(
        functools.partial(_attn_kernel, nqb=nqb, nblk=nblk, lambda_init=lambda_init),
        out_shape=jax.ShapeDtypeStruct((nreal * BLK, WIDTH), BF16),
        grid=(HEADS, nq_total),
        in_specs=[pl.BlockSpec(memory_space=pltpu.SMEM),
                  pl.BlockSpec((QB, HEAD_W), lambda h, i: (i, h)),
                  pl.BlockSpec((seq, HEAD_W), lambda h, i: (i // nqb, kcol + h)),
                  pl.BlockSpec((seq, HEAD_W), lambda h, i: (i // nqb, vcol + h)),
                  pl.BlockSpec((BLK, HEAD_W), lambda h, i: (nreal, kcol + h)),
                  pl.BlockSpec((BLK, HEAD_W), lambda h, i: (nreal, vcol + h)),
                  pl.BlockSpec((None, 2, BLK, BLK), lambda h, i: (h, 0, 0, 0)),
                  pl.BlockSpec((None, QB, QB), lambda h, i: (h, 0, 0)),
                  pl.BlockSpec((4, DIFF_HEAD_DIM), lambda h, i: (0, 0)),
                  pl.BlockSpec((1, HEAD_W), lambda h, i: (0, 0))],
        out_specs=pl.BlockSpec((QB, HEAD_W), lambda h, i: (i, h)),
        scratch_shapes=[pltpu.VMEM((2, nqb + 1, QB, QB), F32),
                        pltpu.VMEM((nqb, HEAD_W, QB), BF16),
                        pltpu.VMEM((HEAD_W, BLK), BF16),
                        pltpu.VMEM((2, 8, QB), F32),
                        pltpu.VMEM((2, 8, QB), F32),
                        pltpu.VMEM((2, HEAD_W, QB), F32)],
        compiler_params=_cparams(("arbitrary", "arbitrary")),
        name="diff_attn",
    )(rel_bias, zd, zd, zd, zd, zd, bias_tiles, bias_diag, lam_vec, subln_w)
    y_meta = pl.pallas_call(
        functools.partial(_attn_meta_kernel, lambda_init=lambda_init),
        out_shape=jax.ShapeDtypeStruct((BLK, WIDTH), BF16),
        grid=(HEADS,),
        in_specs=[pl.BlockSpec((BLK, HEAD_W), lambda h: (nreal, h)),
                  pl.BlockSpec((BLK, HEAD_W), lambda h: (nreal, kcol + h)),
                  pl.BlockSpec((BLK, HEAD_W), lambda h: (nreal, vcol + h)),
                  pl.BlockSpec((None, 2, BLK, BLK), lambda h: (h, 0, 0, 0)),
                  pl.BlockSpec((4, DIFF_HEAD_DIM), lambda h: (0, 0)),
                  pl.BlockSpec((1, HEAD_W), lambda h: (0, 0))],
        out_specs=pl.BlockSpec((BLK, HEAD_W), lambda h: (0, h)),
        compiler_params=_cparams(("arbitrary",)),
        name="diff_attn_meta",
    )(zd, zd, zd, bias_tiles, lam_vec, subln_w)
    return jnp.concatenate([y_main, y_meta], axis=0)


def _largest_tile(rows, cap, align=16):
    best = align
    for t in range(align, cap + 1, align):
        if rows % t == 0:
            best = t
    return best


def _forward(x, meta_tokens, rel_bias, hgrn_lower_bounds, norm_mix_pre, norm_mix_post, norm_ffn_pre,
             norm_ffn_post, w_in, lru_conv_w, lru_conv_b, lru_w_a, lru_b_a, lru_w_x, lru_b_x, lru_lambda,
             pool_w, pool_scale, hgrn_norm, diff_lambda, diff_subln, w_branch, w_out, ffn_w_gu, ffn_w_down):
    bsz, seq, _ = x.shape
    nblk = seq // BLK
    nreal = bsz * nblk
    rows = (nreal + 1) * BLK
    rows_real = nreal * BLK
    tm_big = _largest_tile(rows, 832)
    tm_epi = _largest_tile(rows, 640)
    tm_down = _largest_tile(rows, 320)
    tm_last = _largest_tile(rows_real, 256)

    def vec(a):
        return a.reshape(1, -1)

    lbs, bias_tiles, bias_diag = _prologue(hgrn_lower_bounds, rel_bias)
    h, hn = _embed(x.reshape(rows_real, D_MODEL), meta_tokens, vec(norm_mix_pre[0]), nreal)

    for layer in range(DEPTH):
        lambda_init = 0.8 - 0.6 * math.exp(-0.3 * layer)
        z = _mixer_in_proj(hn, w_in, layer, 0, 4 * WIDTH, F32, tm_big, 1024, 1024)
        z_b = _mixer_in_proj(hn, w_in, layer, 4 * WIDTH, 3 * WIDTH, F32, tm_big, 3 * WIDTH, WIDTH)
        zd = _mixer_in_proj(hn, w_in, layer, 7 * WIDTH, 3 * WIDTH, BF16, tm_big, 3 * WIDTH, WIDTH)
        y_a = _lru(z, lru_conv_w[layer], vec(lru_conv_b[layer]), lru_w_a[layer], vec(lru_b_a[layer]),
                   lru_w_x[layer], vec(lru_b_x[layer]), vec(lru_lambda[layer]), nreal, nblk)
        y_b = _pool(z, pool_w[layer], vec(pool_scale[layer]), nreal, nblk)
        y_c = _hgrn(z, z_b, lbs[layer:layer + 1], vec(hgrn_norm[layer]), nreal, nblk)
        y_d = _attn(zd, rel_bias, bias_tiles, bias_diag, diff_lambda[layer], vec(diff_subln[layer]),
                    nreal, nblk, lambda_init)
        merged = _gate_merge(hn, w_in, (y_a, y_b, y_c, y_d), w_branch, layer, tm_big, 256)
        h, hn = _out_proj(merged, _cast_layer_bf16(w_out, layer, 512), h, vec(norm_mix_post[layer]),
                          vec(norm_ffn_pre[layer]), tm_epi)
        a = _swiglu_up(hn, ffn_w_gu, layer, tm_big, 512)
        last = layer == DEPTH - 1
        w_next = vec(norm_mix_pre[layer + 1]) if not last else vec(norm_mix_pre[layer])
        h, hn = _down_proj(a, _cast_layer_bf16(ffn_w_down, layer, 512), h, vec(norm_ffn_post[layer]), w_next,
                           rows_real if last else rows, tm_last if last else tm_down, not last)
    return h.reshape(bsz, seq, D_MODEL)


def kernel(x, meta_tokens, rel_bias, hgrn_lower_bounds, norm_mix_pre, norm_mix_post, norm_ffn_pre, norm_ffn_post, w_in, lru_conv_w, lru_conv_b, lru_w_a, lru_b_a, lru_w_x, lru_b_x, lru_lambda, pool_w, pool_scale, hgrn_norm, diff_lambda, diff_subln, w_branch, w_out, ffn_w_gu, ffn_w_down):
    return _forward(x, meta_tokens, rel_bias, hgrn_lower_bounds, norm_mix_pre, norm_mix_post, norm_ffn_pre,
                    norm_ffn_post, w_in, lru_conv_w, lru_conv_b, lru_w_a, lru_b_a, lru_w_x, lru_b_x, lru_lambda,
                    pool_w, pool_scale, hgrn_norm, diff_lambda, diff_subln, w_branch, w_out, ffn_w_gu, ffn_w_down)
```

```python
import functools
import math

import numpy as np
import jax
import jax.numpy as jnp
from jax import lax
from jax.experimental import pallas as pl
from jax.experimental.pallas import tpu as pltpu

F32 = jnp.float32
BF16 = jnp.bfloat16

D_MODEL = 2048
SEQ = 2048
DEPTH = 2
N_META = 16
BLK = 128
PAD_ROWS = BLK - N_META
QB = 512
NSUB = QB // BLK
WIDTH = 512
HEADS = 4
HEAD_W = 128
CHUNK = 64
SUB = 16
LRU_C = 8.0
POOL_WINDOWS = (2, 4, 8, 16)
DIFF_HEAD_DIM = 64
REL_BUCKETS = 32
REL_MAX_DIST = 128
FFN_HIDDEN = 5632
MIX_COLS = 10 * WIDTH
NEG = -1e30
EPS = 1e-6
LOG2E = math.log2(math.e)
VMEM_LIMIT = 56 * 1024 * 1024


def _cparams(sem):
    return pltpu.CompilerParams(dimension_semantics=sem, vmem_limit_bytes=VMEM_LIMIT)


def _rms(x, w):
    return x * lax.rsqrt(jnp.mean(x * x, axis=-1, keepdims=True) + EPS) * w


def _log_sigmoid(z):
    return -(jnp.maximum(-z, 0.0) + jnp.log(1.0 + jnp.exp(-jnp.abs(z))))


def _sigmoid(z):
    return 1.0 / (1.0 + jnp.exp(-z))


def _gelu_tanh(x):
    c = math.sqrt(2.0 / math.pi)
    return 0.5 * x * (1.0 + jnp.tanh(c * (x + 0.044715 * (x * x * x))))


def _shift_rows(x, s, fill, row):
    return jnp.where(row >= s, pltpu.roll(x, s, axis=0), fill)


def _bucket_tiles():
    r = np.arange(BLK)[None, :]
    c = np.arange(BLK)[:, None]
    max_exact = REL_BUCKETS // 2

    def bucket(n):
        nf = np.maximum(n, 1).astype(np.float32)
        large = max_exact + (np.log(nf / np.float32(max_exact)) / np.float32(math.log(REL_MAX_DIST / max_exact))
                             * np.float32(REL_BUCKETS - max_exact)).astype(np.int32)
        large = np.minimum(large, REL_BUCKETS - 1)
        return np.where(n < max_exact, n, large).astype(np.int32)

    d0 = r - c
    t0 = np.where(d0 >= 0, bucket(np.maximum(d0, 0)), -1)
    t1 = bucket(BLK + r - c)
    return np.stack([t0, t1]).astype(np.int32)


def _prologue_kernel(lbraw_ref, relb_ref, idx_ref, lb_ref, bias_ref, diag_ref):
    raw = lbraw_ref[...]
    mx = jnp.max(raw, axis=0, keepdims=True)
    e = jnp.exp(raw - mx)
    sm = e / jnp.sum(e, axis=0, keepdims=True)
    cum = sm[0:1]
    lb_ref[0:1, :] = cum - sm[0:1]
    for l in range(1, DEPTH):
        cum = cum + sm[l:l + 1]
        lb_ref[l:l + 1, :] = cum - sm[0:1]
    for t in range(2):
        idx = idx_ref[t]
        for hd in range(HEADS):
            acc = jnp.zeros((BLK, BLK), F32)
            for bk in range(REL_BUCKETS):
                acc = jnp.where(idx == bk, relb_ref[bk, hd], acc)
            bias_ref[hd, t] = jnp.where(idx < 0, NEG, acc)
    for hd in range(HEADS):
        far = relb_ref[REL_BUCKETS - 1, hd]
        for kb in range(NSUB):
            for qb in range(NSUB):
                delta = qb - kb
                if delta == 0:
                    blk = bias_ref[hd, 0]
                elif delta == 1:
                    blk = bias_ref[hd, 1]
                else:
                    blk = jnp.full((BLK, BLK), far if delta > 1 else NEG, F32)
                diag_ref[hd, kb * BLK:(kb + 1) * BLK, qb * BLK:(qb + 1) * BLK] = blk * LOG2E


def _prologue(hgrn_lower_bounds, rel_bias):
    idx = jnp.asarray(_bucket_tiles())
    vmem = pl.BlockSpec(memory_space=pltpu.VMEM)
    return pl.pallas_call(
        _prologue_kernel,
        out_shape=(jax.ShapeDtypeStruct((DEPTH, WIDTH), F32),
                   jax.ShapeDtypeStruct((HEADS, 2, BLK, BLK), F32),
                   jax.ShapeDtypeStruct((HEADS, QB, QB), F32)),
        in_specs=[vmem, pl.BlockSpec(memory_space=pltpu.SMEM), vmem],
        out_specs=(vmem, vmem, vmem),
        name="prologue",
    )(hgrn_lower_bounds, rel_bias, idx)


def _embed_kernel(x_ref, meta_ref, w_ref, h_ref, hn_ref, *, nreal):
    i = pl.program_id(0)

    @pl.when(i < nreal)
    def _():
        h_ref[...] = x_ref[...]

    @pl.when(i == nreal)
    def _():
        h_ref[0:PAD_ROWS, :] = jnp.zeros((PAD_ROWS, D_MODEL), F32)
        h_ref[PAD_ROWS:BLK, :] = meta_ref[...]

    hn_ref[...] = _rms(h_ref[...], w_ref[...]).astype(BF16)


def _embed(x2d, meta, w_pre, nreal):
    rows = (nreal + 1) * BLK
    return pl.pallas_call(
        functools.partial(_embed_kernel, nreal=nreal),
        out_shape=(jax.ShapeDtypeStruct((rows, D_MODEL), F32),
                   jax.ShapeDtypeStruct((rows, D_MODEL), BF16)),
        grid=(nreal + 1,),
        in_specs=[pl.BlockSpec((BLK, D_MODEL), lambda i: (jnp.minimum(i, nreal - 1), 0)),
                  pl.BlockSpec((N_META, D_MODEL), lambda i: (0, 0)),
                  pl.BlockSpec((1, D_MODEL), lambda i: (0, 0))],
        out_specs=(pl.BlockSpec((BLK, D_MODEL), lambda i: (i, 0)),
                   pl.BlockSpec((BLK, D_MODEL), lambda i: (i, 0))),
        compiler_params=_cparams(("arbitrary",)),
        name="embed",
    )(x2d, meta, w_pre)


def _cast_tiles_once(pairs):
    @pl.when(pl.program_id(1) == 0)
    def _():
        for src, dst in pairs:
            dst[...] = src[...].astype(BF16)


def _matmul_kernel(x_ref, *rest):
    *w_refs, o_ref, wbf = rest
    wblk = w_refs[0].shape[-1]
    _cast_tiles_once([(w, wbf.at[:, j * wblk:(j + 1) * wblk]) for j, w in enumerate(w_refs)])
    o_ref[...] = jnp.dot(x_ref[...], wbf[...], preferred_element_type=F32).astype(o_ref.dtype)


def _mixer_in_proj(hn, w_in, layer, col0, ncols, out_dtype, tm, tn, wblk):
    rows, k = hn.shape
    nw = tn // wblk
    w_specs = [pl.BlockSpec((None, k, wblk), functools.partial(
        lambda n, m, j: (layer, 0, col0 // wblk + n * nw + j), j=j)) for j in range(nw)]
    return pl.pallas_call(
        _matmul_kernel,
        out_shape=jax.ShapeDtypeStruct((rows, ncols), out_dtype),
        grid=(ncols // tn, rows // tm),
        in_specs=[pl.BlockSpec((tm, k), lambda n, m: (m, 0))] + w_specs,
        out_specs=pl.BlockSpec((tm, tn), lambda n, m: (m, n)),
        scratch_shapes=[pltpu.VMEM((k, tn), BF16)],
        compiler_params=_cparams(("arbitrary", "arbitrary")),
        name="mixer_in_proj",
    )(hn, *([w_in] * nw))


def _cast_kernel(w_ref, o_ref):
    o_ref[...] = w_ref[...].astype(BF16)


def _cast_layer_bf16(w, layer, tr, col0=0, ncols=None, tc=None):
    _, r, c = w.shape
    ncols = c if ncols is None else ncols
    tc = ncols if tc is None else tc
    return pl.pallas_call(
        _cast_kernel,
        out_shape=jax.ShapeDtypeStruct((r, ncols), BF16),
        grid=(r // tr, ncols // tc),
        in_specs=[pl.BlockSpec((None, tr, tc), lambda i, j: (layer, i, col0 // tc + j))],
        out_specs=pl.BlockSpec((tr, tc), lambda i, j: (i, j)),
        compiler_params=_cparams(("arbitrary", "arbitrary")),
        name="cast_bf16",
    )(w)


def _residual_epilogue(acc, h_ref, wpost_ref, wnext_ref, hnew_ref, hn_ref):
    h_new = h_ref[...] + _rms(acc, wpost_ref[...])
    hnew_ref[...] = h_new
    if hn_ref is not None:
        hn_ref[...] = _rms(h_new, wnext_ref[...]).astype(BF16)


EPI_ROWS = 160


def _row_subtiles(tm):
    sub = next(s for s in (EPI_ROWS, 128, 64, 32, 16) if tm % s == 0)
    return [slice(r, r + sub) for r in range(0, tm, sub)]


def _gate_merge_kernel(hn_ref, g0, g1, g2, g3, y0, y1, y2, y3, wb_ref, o_ref, gbf, wbbf):
    _cast_tiles_once([(g, gbf.at[k]) for k, g in enumerate((g0, g1, g2, g3))] + [(wb_ref, wbbf)])
    hn = hn_ref[...]
    acc = None
    for k, y_ref in enumerate((y0, y1, y2, y3)):
        gate = _sigmoid(jnp.dot(hn, gbf[k], preferred_element_type=F32))
        proj = jnp.dot(y_ref[...], wbbf[k], preferred_element_type=F32)
        acc = gate * proj if acc is None else acc + gate * proj
    o_ref[...] = acc.astype(o_ref.dtype)


def _gate_merge(hn, w_in, ys, w_branch, layer, tm, tn):
    rows, k = hn.shape
    gate_specs = [
        pl.BlockSpec((None, k, tn), functools.partial(
            lambda n, m, base: (layer, 0, base + n), base=(MIX_COLS + br * D_MODEL) // tn))
        for br in range(4)]
    y_specs = [pl.BlockSpec((tm, WIDTH), lambda n, m: (m, 0)) for _ in range(4)]
    return pl.pallas_call(
        _gate_merge_kernel,
        out_shape=jax.ShapeDtypeStruct((rows, D_MODEL), BF16),
        grid=(D_MODEL // tn, rows // tm),
        in_specs=[pl.BlockSpec((tm, k), lambda n, m: (m, 0))] + gate_specs + y_specs
                 + [pl.BlockSpec((None, 4, WIDTH, tn), lambda n, m: (layer, 0, 0, n))],
        out_specs=pl.BlockSpec((tm, tn), lambda n, m: (m, n)),
        scratch_shapes=[pltpu.VMEM((4, k, tn), BF16), pltpu.VMEM((4, WIDTH, tn), BF16)],
        compiler_params=_cparams(("arbitrary", "arbitrary")),
        name="gate_merge",
    )(hn, w_in, w_in, w_in, w_in, *ys, w_branch)


def _out_proj_kernel(x_ref, w_ref, h_ref, wpost_ref, wnext_ref, hnew_ref, hn_ref):
    for rs in _row_subtiles(x_ref.shape[0]):
        acc = jnp.dot(x_ref[rs, :], w_ref[...], preferred_element_type=F32)
        _residual_epilogue(acc, h_ref.at[rs, :], wpost_ref, wnext_ref, hnew_ref.at[rs, :], hn_ref.at[rs, :])


def _out_proj(merged, w_out, h, w_post, w_next, tm):
    rows = h.shape[0]
    row_spec = pl.BlockSpec((tm, D_MODEL), lambda m: (m, 0))
    return pl.pallas_call(
        _out_proj_kernel,
        out_shape=(jax.ShapeDtypeStruct((rows, D_MODEL), F32),
                   jax.ShapeDtypeStruct((rows, D_MODEL), BF16)),
        grid=(rows // tm,),
        in_specs=[row_spec, _resident((D_MODEL, D_MODEL), (0, 0)), row_spec,
                  _pspec((1, D_MODEL)), _pspec((1, D_MODEL))],
        out_specs=(row_spec, row_spec),
        compiler_params=_cparams(("arbitrary",)),
        name="out_proj",
    )(merged, w_out, h, w_post, w_next)


def _swiglu_up_kernel(x_ref, wg_ref, wu_ref, o_ref, wgbf, wubf):
    _cast_tiles_once([(wg_ref, wgbf), (wu_ref, wubf)])
    x = x_ref[...]
    g = jnp.dot(x, wgbf[...], preferred_element_type=F32)
    u = jnp.dot(x, wubf[...], preferred_element_type=F32)
    o_ref[...] = (g * _sigmoid(g) * u).astype(o_ref.dtype)


def _swiglu_up(hn, w_gu, layer, tm, tn):
    rows, k = hn.shape
    nt = FFN_HIDDEN // tn
    return pl.pallas_call(
        _swiglu_up_kernel,
        out_shape=jax.ShapeDtypeStruct((rows, FFN_HIDDEN), BF16),
        grid=(nt, rows // tm),
        in_specs=[pl.BlockSpec((tm, k), lambda n, m: (m, 0)),
                  pl.BlockSpec((None, k, tn), lambda n, m: (layer, 0, n)),
                  pl.BlockSpec((None, k, tn), lambda n, m: (layer, 0, nt + n))],
        out_specs=pl.BlockSpec((tm, tn), lambda n, m: (m, n)),
        scratch_shapes=[pltpu.VMEM((k, tn), BF16), pltpu.VMEM((k, tn), BF16)],
        compiler_params=_cparams(("arbitrary", "arbitrary")),
        name="swiglu_up",
    )(hn, w_gu, w_gu)


def _down_proj_kernel(a_ref, w_ref, h_ref, wpost_ref, wnext_ref, hnew_ref, hn_ref=None):
    for rs in _row_subtiles(a_ref.shape[0]):
        acc = jnp.dot(a_ref[rs, :], w_ref[...], preferred_element_type=F32)
        _residual_epilogue(acc, h_ref.at[rs, :], wpost_ref, wnext_ref, hnew_ref.at[rs, :],
                           None if hn_ref is None else hn_ref.at[rs, :])


def _down_proj(a, w_down, h, w_post, w_next, rows_out, tm, emit_hn):
    row_spec = lambda w: pl.BlockSpec((tm, w), lambda m: (m, 0))
    out_shape = [jax.ShapeDtypeStruct((rows_out, D_MODEL), F32)]
    out_specs = [row_spec(D_MODEL)]
    if emit_hn:
        out_shape.append(jax.ShapeDtypeStruct((rows_out, D_MODEL), BF16))
        out_specs.append(row_spec(D_MODEL))
    res = pl.pallas_call(
        _down_proj_kernel,
        out_shape=tuple(out_shape),
        grid=(rows_out // tm,),
        in_specs=[row_spec(FFN_HIDDEN), _resident((FFN_HIDDEN, D_MODEL), (0, 0)), row_spec(D_MODEL),
                  _pspec((1, D_MODEL)), _pspec((1, D_MODEL))],
        out_specs=tuple(out_specs),
        compiler_params=_cparams(("arbitrary",)),
        name="down_proj",
    )(a, w_down, h, w_post, w_next)
    return res if emit_hn else (res[0], None)


def _row_block(i, nreal):
    return (i + nreal) % (nreal + 1)


def _zspec(col_block, nreal):
    return pl.BlockSpec((BLK, WIDTH), lambda i: (_row_block(i, nreal), col_block))


def _pspec(shape):
    nd = len(shape)
    return pl.BlockSpec(shape, lambda *_: (0,) * nd)


def _resident(shape, index):
    return pl.BlockSpec(shape, lambda *_: index, pipeline_mode=pl.Buffered(1))


def _lru_kernel(u_ref, gate_ref, cw_ref, cb_ref, wa_ref, ba_ref, wx_ref, bx_ref, lam_ref, y_ref,
                ubuf, hst, hist_meta, h_meta, *, nblk):
    i = pl.program_id(0)
    is_meta = i == 0

    @pl.when(is_meta)
    def _():
        ubuf[0:8, :] = jnp.zeros((8, WIDTH), F32)
        hst[...] = jnp.zeros_like(hst)

    @pl.when(jnp.logical_and(i >= 1, (i - 1) % nblk == 0))
    def _():
        ubuf[0:8, :] = hist_meta[...]
        hst[...] = h_meta[...]

    u = u_ref[...]
    ubuf[8:8 + BLK, :] = u
    cw = cw_ref[...]
    xc = (cb_ref[...] + cw[3:4] * u + cw[2:3] * ubuf[7:7 + BLK, :]
          + cw[1:2] * ubuf[6:6 + BLK, :] + cw[0:1] * ubuf[5:5 + BLK, :])
    xb = xc.astype(BF16)
    ra, ia = [], []
    for hd in range(HEADS):
        sl = slice(hd * HEAD_W, (hd + 1) * HEAD_W)
        ra.append(jnp.dot(xb[:, sl], wa_ref[hd].astype(BF16), preferred_element_type=F32))
        ia.append(jnp.dot(xb[:, sl], wx_ref[hd].astype(BF16), preferred_element_type=F32))
    r = _sigmoid(jnp.concatenate(ra, axis=1) + ba_ref[...])
    ig = _sigmoid(jnp.concatenate(ia, axis=1) + bx_ref[...])
    lam = lam_ref[...]
    softplus_neg_lam = jnp.maximum(-lam, 0.0) + jnp.log1p(jnp.exp(-jnp.abs(lam)))
    log_a = -LRU_C * r * softplus_neg_lam
    a = jnp.exp(log_a)
    bb = jnp.sqrt(-jnp.tanh(log_a) * (a * a + 1.0)) * (ig * xc)
    row = lax.broadcasted_iota(jnp.int32, (BLK, 1), 0)
    bb = jnp.where(row >= PAD_ROWS * is_meta.astype(jnp.int32), bb, 0.0)

    acum, bcum = a, bb
    s = 1
    while s < BLK:
        a_sh = _shift_rows(acum, s, 1.0, row)
        b_sh = _shift_rows(bcum, s, 0.0, row)
        bcum = acum * b_sh + bcum
        acum = acum * a_sh
        s *= 2
    h = acum * hst[0:1, :] + bcum
    y_ref[...] = (h * _gelu_tanh(gate_ref[...])).astype(y_ref.dtype)

    hist = u[BLK - 8:BLK, :]
    hlast = jnp.broadcast_to(h[BLK - 1:BLK, :], (8, WIDTH))
    ubuf[0:8, :] = hist
    hst[...] = hlast

    @pl.when(is_meta)
    def _():
        hist_meta[...] = hist
        h_meta[...] = hlast


POOL_HIST = 16


def _pool_kernel(u_ref, pw_ref, ps_ref, y_ref, ubuf, hist_meta, *, nblk):
    i = pl.program_id(0)
    is_meta = i == 0

    @pl.when(is_meta)
    def _():
        ubuf[0:POOL_HIST, :] = jnp.zeros((POOL_HIST, WIDTH), F32)

    @pl.when(jnp.logical_and(i >= 1, (i - 1) % nblk == 0))
    def _():
        ubuf[0:POOL_HIST, :] = hist_meta[...]

    u = u_ref[...]
    ubuf[POOL_HIST:POOL_HIST + BLK, :] = u
    row = lax.broadcasted_iota(jnp.int32, (BLK, 1), 0)
    meta_i = is_meta.astype(jnp.int32)
    pos1 = row + 1 - PAD_ROWS * meta_i + 2 * POOL_HIST * (1 - meta_i)
    outs = []
    for g, win in enumerate(POOL_WINDOWS):
        sl = slice(g * HEAD_W, (g + 1) * HEAD_W)
        acc = u[:, sl]
        for d in range(1, win):
            acc = acc + ubuf[POOL_HIST - d:POOL_HIST - d + BLK, sl]
        count = jnp.clip(pos1, 1, win).astype(F32)
        pooled = acc / count - u[:, sl]
        outs.append(jnp.dot(pooled.astype(BF16), pw_ref[g].astype(BF16), preferred_element_type=F32))
    y_ref[...] = (jnp.concatenate(outs, axis=1) * ps_ref[...]).astype(y_ref.dtype)

    hist = u[BLK - POOL_HIST:BLK, :]
    ubuf[0:POOL_HIST, :] = hist

    @pl.when(is_meta)
    def _():
        hist_meta[...] = hist


def _lru_pool_kernel(u_ref, gate_ref, cw_ref, cb_ref, wa_ref, ba_ref, wx_ref, bx_ref, lam_ref,
                     pu_ref, pw_ref, ps_ref, ya_ref, yb_ref,
                     ubuf, hst, hist_meta, h_meta, pbuf, phist_meta, *, nblk):
    _lru_kernel(u_ref, gate_ref, cw_ref, cb_ref, wa_ref, ba_ref, wx_ref, bx_ref, lam_ref, ya_ref,
                ubuf, hst, hist_meta, h_meta, nblk=nblk)
    _pool_kernel(pu_ref, pw_ref, ps_ref, yb_ref, pbuf, phist_meta, nblk=nblk)


def _lru_pool(z, cw, cb, wa, ba, wx, bx, lam, pw, ps, nreal, nblk):
    rows = z.shape[0]
    out_spec = pl.BlockSpec((BLK, WIDTH), lambda i: (_row_block(i, nreal), 0))
    return pl.pallas_call(
        functools.partial(_lru_pool_kernel, nblk=nblk),
        out_shape=(jax.ShapeDtypeStruct((rows, WIDTH), BF16), jax.ShapeDtypeStruct((rows, WIDTH), BF16)),
        grid=(nreal + 1,),
        in_specs=[_zspec(0, nreal), _zspec(1, nreal),
                  _pspec((4, WIDTH)), _pspec((1, WIDTH)),
                  _pspec((HEADS, HEAD_W, HEAD_W)), _pspec((1, WIDTH)),
                  _pspec((HEADS, HEAD_W, HEAD_W)), _pspec((1, WIDTH)), _pspec((1, WIDTH)),
                  _zspec(2, nreal), _pspec((4, HEAD_W, HEAD_W)), _pspec((1, WIDTH))],
        out_specs=(out_spec, out_spec),
        scratch_shapes=[pltpu.VMEM((8 + BLK, WIDTH), F32), pltpu.VMEM((8, WIDTH), F32),
                        pltpu.VMEM((8, WIDTH), F32), pltpu.VMEM((8, WIDTH), F32),
                        pltpu.VMEM((POOL_HIST + BLK, WIDTH), F32), pltpu.VMEM((POOL_HIST, WIDTH), F32)],
        compiler_params=_cparams(("arbitrary",)),
        name="lru_pool",
    )(z, z, cw, cb, wa, ba, wx, bx, lam, z, pw, ps)


def _hgrn_chunk(q, z, v, lbh, state_t, valid, ones_bf):
    ls = _log_sigmoid(z)
    x1 = jnp.log(lbh)
    x2 = jnp.log1p(-lbh) + ls
    mx = jnp.maximum(x1, x2)
    g = mx + jnp.log(1.0 + jnp.exp(-jnp.abs(x1 - x2)))
    k = (1.0 - lbh) * _sigmoid(-z)
    if valid is not None:
        g = jnp.where(valid, g, 0.0)
    row = lax.broadcasted_iota(jnp.int32, (CHUNK, 1), 0)
    b = g * LOG2E
    s = 1
    while s < CHUNK:
        b = b + _shift_rows(b, s, 0.0, row)
        s *= 2
    b_last = b[CHUNK - 1:CHUNK, :]

    qe = (q * jnp.exp2(b)).astype(BF16)
    o = lax.dot_general(qe, state_t.astype(BF16), (((1,), (1,)), ((), ())), preferred_element_type=F32)

    col = lax.broadcasted_iota(jnp.int32, (SUB, CHUNK), 1)
    rsub = lax.broadcasted_iota(jnp.int32, (SUB, CHUNK), 0)
    lane = lax.broadcasted_iota(jnp.int32, (SUB, HEAD_W), 1)
    s_rows = []
    for blk in range(CHUNK // SUB):
        lo = blk * SUB
        bi = b[lo:lo + SUB, :]
        qi = q[lo:lo + SUB, :]
        ki = k[lo:lo + SUB, :]
        parts = []
        for sr in range(SUB):
            t0 = 0 if sr < SUB // 2 else SUB // 2
            e = jnp.exp2(jnp.minimum(bi[t0:, :] - bi[sr:sr + 1, :], 0.0))
            parts.append(qi[t0:, :] * e * ki[sr:sr + 1, :])
        m3 = jnp.concatenate(parts, axis=0).astype(BF16)
        red = jnp.dot(m3, ones_bf, preferred_element_type=F32)
        halves = [jnp.zeros((SUB // 2, HEAD_W), F32), jnp.zeros((SUB // 2, HEAD_W), F32)]
        r0 = 0
        for sr in range(SUB):
            for hf in range(0 if sr < SUB // 2 else 1, 2):
                halves[hf] = jnp.where(lane[:SUB // 2] == lo + sr, red[r0:r0 + SUB // 2, :], halves[hf])
                r0 += SUB // 2
        diag = jnp.concatenate(halves, axis=0)[:, :CHUNK]
        s_blk = jnp.where(jnp.logical_and(col >= lo, col - lo <= rsub), diag, 0.0)
        if blk > 0:
            b0 = b[lo - 1:lo, :]
            kt = (k * jnp.exp2(jnp.minimum(b0 - b, 0.0))).astype(BF16)
            qd = (qi * jnp.exp2(bi - b0)).astype(BF16)
            off = lax.dot_general(qd, kt, (((1,), (1,)), ((), ())), preferred_element_type=F32)
            s_blk = jnp.where(col < lo, off, s_blk)
        s_rows.append(s_blk)
    scores = jnp.concatenate(s_rows, axis=0).astype(BF16)
    vb = v.astype(BF16)
    o = o + jnp.dot(scores, vb, preferred_element_type=F32)

    kd = (k * jnp.exp2(b_last - b)).astype(BF16)
    upd = lax.dot_general(vb, kd, (((0,), (0,)), ((), ())), preferred_element_type=F32)
    new_state_t = state_t * jnp.exp2(b_last) + upd
    return o, new_state_t


def _hgrn_kernel(q_ref, f_ref, v_ref, og_ref, lb_ref, nw_ref, y_ref, state, state_meta, *, nblk):
    i = pl.program_id(0)
    is_meta = i == 0

    @pl.when(is_meta)
    def _():
        state[...] = jnp.zeros_like(state)

    @pl.when(jnp.logical_and(i >= 1, (i - 1) % nblk == 0))
    def _():
        state[...] = state_meta[...]

    ones_bf = jnp.ones((HEAD_W, HEAD_W), BF16)
    nw = nw_ref[...]
    for hd in range(HEADS):
        sl = slice(hd * HEAD_W, (hd + 1) * HEAD_W)
        lbh = lb_ref[:, sl]
        st = state[hd]
        for c in range(BLK // CHUNK):
            rs = slice(c * CHUNK, (c + 1) * CHUNK)
            rowg = lax.broadcasted_iota(jnp.int32, (CHUNK, 1), 0) + c * CHUNK
            valid = rowg >= PAD_ROWS * is_meta.astype(jnp.int32)
            o, st = _hgrn_chunk(q_ref[rs, sl], f_ref[rs, sl], v_ref[rs, sl], lbh, st, valid, ones_bf)
            og = og_ref[rs, sl]
            y_ref[rs, sl] = (_rms(o, nw) * (og * _sigmoid(og))).astype(y_ref.dtype)
        state[hd] = st

    @pl.when(is_meta)
    def _():
        state_meta[...] = state[...]


def _hgrn(z_a, z_b, lb, nw, nreal, nblk):
    rows = z_a.shape[0]
    return pl.pallas_call(
        functools.partial(_hgrn_kernel, nblk=nblk),
        out_shape=jax.ShapeDtypeStruct((rows, WIDTH), BF16),
        grid=(nreal + 1,),
        in_specs=[_zspec(3, nreal), _zspec(0, nreal), _zspec(1, nreal), _zspec(2, nreal),
                  _pspec((1, WIDTH)), _pspec((1, HEAD_W))],
        out_specs=pl.BlockSpec((BLK, WIDTH), lambda i: (_row_block(i, nreal), 0)),
        scratch_shapes=[pltpu.VMEM((HEADS, HEAD_W, HEAD_W), F32), pltpu.VMEM((HEADS, HEAD_W, HEAD_W), F32)],
        compiler_params=_cparams(("arbitrary",)),
        name="hgrn2",
    )(z_a, z_b, z_b, z_b, lb, nw)


def _attn_lambda(lamv_ref, lambda_init):
    lv = lamv_ref[...]
    return (jnp.exp(jnp.sum(lv[0:1] * lv[1:2], axis=-1, keepdims=True))
            - jnp.exp(jnp.sum(lv[2:3] * lv[3:4], axis=-1, keepdims=True)) + lambda_init)


def _scaled_q(q_ref):
    qf = (q_ref[...].astype(F32) * (DIFF_HEAD_DIM ** -0.5)).astype(BF16)
    return qf[:, :DIFF_HEAD_DIM], qf[:, DIFF_HEAD_DIM:]


def _fold8(x, op):
    r, c = x.shape
    return op(x.reshape(r // 8, 8, c), axis=0)


def _attn_kernel(relb_ref, q_ref, k_ref, v_ref, km_ref, vm_ref, bias_ref, diag_ref, lamv_ref, sw_ref, y_ref,
                 s_sc, vt_sc, vtm_sc, m_sc, l_sc, acc_sc, *, nqb, nblk, lambda_init):
    hd = pl.program_id(0)
    jq = pl.program_id(1) % nqb
    far = relb_ref[REL_BUCKETS - 1, hd] * LOG2E
    t1 = bias_ref[1] * LOG2E
    slot_diag, slot_meta = nqb - 1, nqb

    @pl.when(jq == 0)
    def _():
        for t in range(nblk):
            vt_sc[t // NSUB, :, (t % NSUB) * BLK:(t % NSUB + 1) * BLK] = (
                v_ref[t * BLK:(t + 1) * BLK, :].astype(F32).T.astype(BF16))
        vtm_sc[...] = vm_ref[...].astype(F32).T.astype(BF16)

    qt = (q_ref[...].astype(F32) * (DIFF_HEAD_DIM ** -0.5)).T
    zero = jnp.zeros((DIFF_HEAD_DIM, QB), F32)
    qtp = (jnp.concatenate([qt[:DIFF_HEAD_DIM], zero], axis=0).astype(BF16),
           jnp.concatenate([zero, qt[DIFF_HEAD_DIM:]], axis=0).astype(BF16))

    def key_chunk(c):
        return k_ref[pl.ds(pl.multiple_of(c * QB, QB), QB), :]

    def score_chunk(k_rows, slot, add_bias, r0=0, c0=0):
        r = k_rows.shape[0]
        for mp in range(2):
            s = add_bias(jnp.dot(k_rows, qtp[mp][:, c0:], preferred_element_type=F32) * LOG2E)
            s_sc[mp, slot, r0:r0 + r, c0:] = s
            m_sc[mp, :, c0:] = jnp.maximum(m_sc[mp, :, c0:], _fold8(s, jnp.max))

    def value_chunk(vt_cols, slot, m8, r0=0, c0=0):
        r = vt_cols.shape[1]
        for mp in range(2):
            p = jnp.exp2(s_sc[mp, slot, r0:r0 + r, c0:].reshape(r // 8, 8, QB - c0) - m8[mp][None, :, c0:])
            l_sc[mp, :, c0:] += jnp.sum(p, axis=0)
            acc_sc[mp, :, c0:] += jnp.dot(vt_cols, p.reshape(r, QB - c0).astype(BF16),
                                         preferred_element_type=F32)

    def diag_tiles(fn):
        for kb in range(NSUB):
            fn(kb, kb * BLK)

    def walk(fn_far, fn_meta, fn_diag):
        def body(c, carry):
            fn_far(c)
            return carry

        lax.fori_loop(0, jq, body, 0)
        fn_meta()
        fn_diag()

    near = t1 - far
    krow = lax.broadcasted_iota(jnp.int32, (BLK, 1), 0)
    meta_rows = jnp.where(krow >= PAD_ROWS, far, NEG)
    first = jnp.full((1, BLK), jq, jnp.int32) == 0

    def meta_bias(s):
        head = s[:, 0:BLK] + (meta_rows + jnp.where(first, near, 0.0))
        return jnp.concatenate([head, s[:, BLK:] + meta_rows], axis=1)

    m_sc[...] = jnp.full_like(m_sc, NEG)
    walk(lambda c: score_chunk(key_chunk(c), c, lambda s: s + far),
         lambda: score_chunk(km_ref[...], slot_meta, meta_bias),
         lambda: diag_tiles(lambda kb, c0: score_chunk(
             k_ref[pl.ds(pl.multiple_of(jq * QB + c0, BLK), BLK), :], slot_diag,
             lambda s: s + diag_ref[c0:c0 + BLK, c0:], r0=c0, c0=c0)))

    @pl.when(jq >= 1)
    def _():
        for mp in range(2):
            fixed = s_sc[mp, jq - 1, QB - BLK:QB, 0:BLK] + near
            s_sc[mp, jq - 1, QB - BLK:QB, 0:BLK] = fixed
            m_sc[mp, :, 0:BLK] = jnp.maximum(m_sc[mp, :, 0:BLK], _fold8(fixed, jnp.max))

    m8 = [jnp.broadcast_to(jnp.max(m_sc[mp], axis=0, keepdims=True), (8, QB)) for mp in range(2)]
    l_sc[...] = jnp.zeros_like(l_sc)
    acc_sc[...] = jnp.zeros_like(acc_sc)

    walk(lambda c: value_chunk(vt_sc[c], c, m8),
         lambda: value_chunk(vtm_sc[...], slot_meta, m8),
         lambda: diag_tiles(lambda kb, c0: value_chunk(
             vt_sc[jq, :, c0:c0 + BLK], slot_diag, m8, r0=c0, c0=c0)))

    lam = _attn_lambda(lamv_ref, lambda_init)
    l0 = jnp.sum(l_sc[0], axis=0, keepdims=True)
    l1 = jnp.sum(l_sc[1], axis=0, keepdims=True)
    ot = acc_sc[0] / l0 - lam * (acc_sc[1] / l1)
    ot = ot * lax.rsqrt(jnp.mean(ot * ot, axis=0, keepdims=True) + EPS)
    y_ref[...] = (ot.T * sw_ref[...] * (1.0 - lambda_init)).astype(y_ref.dtype)


def _attn_meta_kernel(q_ref, k_ref, v_ref, bias_ref, lamv_ref, sw_ref, y_ref, *, lambda_init):
    qs = _scaled_q(q_ref)
    kb = k_ref[...]
    colmask = jnp.where(lax.broadcasted_iota(jnp.int32, (1, BLK), 1) >= PAD_ROWS, 0.0, NEG)
    outs = []
    bias = bias_ref[0].T + colmask
    for mp in range(2):
        s = lax.dot_general(qs[mp], kb[:, mp * DIFF_HEAD_DIM:(mp + 1) * DIFF_HEAD_DIM],
                            (((1,), (1,)), ((), ())), preferred_element_type=F32) + bias
        p = jnp.exp(s - jnp.max(s, axis=-1, keepdims=True))
        acc = jnp.dot(p.astype(BF16), v_ref[...], preferred_element_type=F32)
        outs.append(acc / jnp.sum(p, axis=-1, keepdims=True))
    o = outs[0] - _attn_lambda(lamv_ref, lambda_init) * outs[1]
    o = _rms(o, sw_ref[...]) * (1.0 - lambda_init)
    row = lax.broadcasted_iota(jnp.int32, (BLK, 1), 0)
    y_ref[...] = jnp.where(row >= PAD_ROWS, o, 0.0).astype(y_ref.dtype)


def _attn(zd, rel_bias, bias_tiles, bias_diag, lam_vec, subln_w, nreal, nblk, lambda_init):
    nqb = nblk * BLK // QB
    nq_total = nreal * BLK // QB
    seq = nblk * BLK
    kcol, vcol = HEADS, 2 * HEADS
    y_main = pl.pallas_call(
        functools.partial(_attn_kernel, nqb=nqb, nblk=nblk, lambda_init=lambda_init),
        out_shape=jax.ShapeDtypeStruct((nreal * BLK, WIDTH), BF16),
        grid=(HEADS, nq_total),
        in_specs=[pl.BlockSpec(memory_space=pltpu.SMEM),
                  pl.BlockSpec((QB, HEAD_W), lambda h, i: (i, h)),
                  pl.BlockSpec((seq, HEAD_W), lambda h, i: (i // nqb, kcol + h)),
                  pl.BlockSpec((seq, HEAD_W), lambda h, i: (i // nqb, vcol + h)),
                  pl.BlockSpec((BLK, HEAD_W), lambda h, i: (nreal, kcol + h)),
                  pl.BlockSpec((BLK, HEAD_W), lambda h, i: (nreal, vcol + h)),
                  pl.BlockSpec((None, 2, BLK, BLK), lambda h, i: (h, 0, 0, 0)),
                  pl.BlockSpec((None, QB, QB), lambda h, i: (h, 0, 0)),
                  pl.BlockSpec((4, DIFF_HEAD_DIM), lambda h, i: (0, 0)),
                  pl.BlockSpec((1, HEAD_W), lambda h, i: (0, 0))],
        out_specs=pl.BlockSpec((QB, HEAD_W), lambda h, i: (i, h)),
        scratch_shapes=[pltpu.VMEM((2, nqb + 1, QB, QB), F32),
                        pltpu.VMEM((nqb, HEAD_W, QB), BF16),
                        pltpu.VMEM((HEAD_W, BLK), BF16),
                        pltpu.VMEM((2, 8, QB), F32),
                        pltpu.VMEM((2, 8, QB), F32),
                        pltpu.VMEM((2, HEAD_W, QB), F32)],
        compiler_params=_cparams(("arbitrary", "arbitrary")),
        name="diff_attn",
    )(rel_bias, zd, zd, zd, zd, zd, bias_tiles, bias_diag, lam_vec, subln_w)
    y_meta = pl.pallas_call(
        functools.partial(_attn_meta_kernel, lambda_init=lambda_init),
        out_shape=jax.ShapeDtypeStruct((BLK, WIDTH), BF16),
        grid=(HEADS,),
        in_specs=[pl.BlockSpec((BLK, HEAD_W), lambda h: (nreal, h)),
                  pl.BlockSpec((BLK, HEAD_W), lambda h: (nreal, kcol + h)),
                  pl.BlockSpec((BLK, HEAD_W), lambda h: (nreal, vcol + h)),
                  pl.BlockSpec((None, 2, BLK, BLK), lambda h: (h, 0, 0, 0)),
                  pl.BlockSpec((4, DIFF_HEAD_DIM), lambda h: (0, 0)),
                  pl.BlockSpec((1, HEAD_W), lambda h: (0, 0))],
        out_specs=pl.BlockSpec((BLK, HEAD_W), lambda h: (0, h)),
        compiler_params=_cparams(("arbitrary",)),
        name="diff_attn_meta",
    )(zd, zd, zd, bias_tiles, lam_vec, subln_w)
    return jnp.concatenate([y_main, y_meta], axis=0)


def _largest_tile(rows, cap, align=16):
    best = align
    for t in range(align, cap + 1, align):
        if rows % t == 0:
            best = t
    return best


def _forward(x, meta_tokens, rel_bias, hgrn_lower_bounds, norm_mix_pre, norm_mix_post, norm_ffn_pre,
             norm_ffn_post, w_in, lru_conv_w, lru_conv_b, lru_w_a, lru_b_a, lru_w_x, lru_b_x, lru_lambda,
             pool_w, pool_scale, hgrn_norm, diff_lambda, diff_subln, w_branch, w_out, ffn_w_gu, ffn_w_down):
    bsz, seq, _ = x.shape
    nblk = seq // BLK
    nreal = bsz * nblk
    rows = (nreal + 1) * BLK
    rows_real = nreal * BLK
    tm_big = _largest_tile(rows, 832)
    tm_epi = _largest_tile(rows, 640)
    tm_down = _largest_tile(rows, 320)
    tm_last = _largest_tile(rows_real, 256)

    def vec(a):
        return a.reshape(1, -1)

    lbs, bias_tiles, bias_diag = _prologue(hgrn_lower_bounds, rel_bias)
    h, hn = _embed(x.reshape(rows_real, D_MODEL), meta_tokens, vec(norm_mix_pre[0]), nreal)

    for layer in range(DEPTH):
        lambda_init = 0.8 - 0.6 * math.exp(-0.3 * layer)
        z = _mixer_in_proj(hn, w_in, layer, 0, 4 * WIDTH, F32, tm_big, 1024, 1024)
        z_b = _mixer_in_proj(hn, w_in, layer, 4 * WIDTH, 3 * WIDTH, F32, tm_big, 3 * WIDTH, WIDTH)
        zd = _mixer_in_proj(hn, w_in, layer, 7 * WIDTH, 3 * WIDTH, BF16, tm_big, 3 * WIDTH, WIDTH)
        y_a, y_b = _lru_pool(z, lru_conv_w[layer], vec(lru_conv_b[layer]), lru_w_a[layer], vec(lru_b_a[layer]),
                             lru_w_x[layer], vec(lru_b_x[layer]), vec(lru_lambda[layer]),
                             pool_w[layer], vec(pool_scale[layer]), nreal, nblk)
        y_c = _hgrn(z, z_b, lbs[layer:layer + 1], vec(hgrn_norm[layer]), nreal, nblk)
        y_d = _attn(zd, rel_bias, bias_tiles, bias_diag, diff_lambda[layer], vec(diff_subln[layer]),
                    nreal, nblk, lambda_init)
        merged = _gate_merge(hn, w_in, (y_a, y_b, y_c, y_d), w_branch, layer, tm_big, 256)
        h, hn = _out_proj(merged, _cast_layer_bf16(w_out, layer, 512), h, vec(norm_mix_post[layer]),
                          vec(norm_ffn_pre[layer]), tm_epi)
        a = _swiglu_up(hn, ffn_w_gu, layer, tm_big, 512)
        last = layer == DEPTH - 1
        w_next = vec(norm_mix_pre[layer + 1]) if not last else vec(norm_mix_pre[layer])
        h, hn = _down_proj(a, _cast_layer_bf16(ffn_w_down, layer, 512), h, vec(norm_ffn_post[layer]), w_next,
                           rows_real if last else rows, tm_last if last else tm_down, not last)
    return h.reshape(bsz, seq, D_MODEL)


def kernel(x, meta_tokens, rel_bias, hgrn_lower_bounds, norm_mix_pre, norm_mix_post, norm_ffn_pre, norm_ffn_post, w_in, lru_conv_w, lru_conv_b, lru_w_a, lru_b_a, lru_w_x, lru_b_x, lru_lambda, pool_w, pool_scale, hgrn_norm, diff_lambda, diff_subln, w_branch, w_out, ffn_w_gu, ffn_w_down):
    return _forward(x, meta_tokens, rel_bias, hgrn_lower_bounds, norm_mix_pre, norm_mix_post, norm_ffn_pre,
                    norm_ffn_post, w_in, lru_conv_w, lru_conv_b, lru_w_a, lru_b_a, lru_w_x, lru_b_x, lru_lambda,
                    pool_w, pool_scale, hgrn_norm, diff_lambda, diff_subln, w_branch, w_out, ffn_w_gu, ffn_w_down)
```

```python
import functools
import math

import numpy as np
import jax
import jax.numpy as jnp
from jax import lax
from jax.experimental import pallas as pl
from jax.experimental.pallas import tpu as pltpu

F32 = jnp.float32
BF16 = jnp.bfloat16

D_MODEL = 2048
SEQ = 2048
DEPTH = 2
N_META = 16
BLK = 128
PAD_ROWS = BLK - N_META
QB = 512
NSUB = QB // BLK
WIDTH = 512
HEADS = 4
HEAD_W = 128
CHUNK = 64
SUB = 16
LRU_C = 8.0
POOL_WINDOWS = (2, 4, 8, 16)
DIFF_HEAD_DIM = 64
REL_BUCKETS = 32
REL_MAX_DIST = 128
FFN_HIDDEN = 5632
MIX_COLS = 10 * WIDTH
NEG = -1e30
EPS = 1e-6
LOG2E = math.log2(math.e)
VMEM_LIMIT = 56 * 1024 * 1024


def _cparams(sem):
    return pltpu.CompilerParams(dimension_semantics=sem, vmem_limit_bytes=VMEM_LIMIT)


def _rms(x, w):
    return x * lax.rsqrt(jnp.mean(x * x, axis=-1, keepdims=True) + EPS) * w


def _log_sigmoid(z):
    return -(jnp.maximum(-z, 0.0) + jnp.log(1.0 + jnp.exp(-jnp.abs(z))))


def _sigmoid(z):
    return 1.0 / (1.0 + jnp.exp(-z))


def _gelu_tanh(x):
    c = math.sqrt(2.0 / math.pi)
    return 0.5 * x * (1.0 + jnp.tanh(c * (x + 0.044715 * (x * x * x))))


def _shift_rows(x, s, fill, row):
    return jnp.where(row >= s, pltpu.roll(x, s, axis=0), fill)


def _bucket_tiles():
    r = np.arange(BLK)[None, :]
    c = np.arange(BLK)[:, None]
    max_exact = REL_BUCKETS // 2

    def bucket(n):
        nf = np.maximum(n, 1).astype(np.float32)
        large = max_exact + (np.log(nf / np.float32(max_exact)) / np.float32(math.log(REL_MAX_DIST / max_exact))
                             * np.float32(REL_BUCKETS - max_exact)).astype(np.int32)
        large = np.minimum(large, REL_BUCKETS - 1)
        return np.where(n < max_exact, n, large).astype(np.int32)

    d0 = r - c
    t0 = np.where(d0 >= 0, bucket(np.maximum(d0, 0)), -1)
    t1 = bucket(BLK + r - c)
    return np.stack([t0, t1]).astype(np.int32)


def _prologue_kernel(lbraw_ref, relb_ref, idx_ref, lb_ref, bias_ref, diag_ref):
    raw = lbraw_ref[...]
    mx = jnp.max(raw, axis=0, keepdims=True)
    e = jnp.exp(raw - mx)
    sm = e / jnp.sum(e, axis=0, keepdims=True)
    cum = sm[0:1]
    lb_ref[0:1, :] = cum - sm[0:1]
    for l in range(1, DEPTH):
        cum = cum + sm[l:l + 1]
        lb_ref[l:l + 1, :] = cum - sm[0:1]
    for t in range(2):
        idx = idx_ref[t]
        for hd in range(HEADS):
            acc = jnp.zeros((BLK, BLK), F32)
            for bk in range(REL_BUCKETS):
                acc = jnp.where(idx == bk, relb_ref[bk, hd], acc)
            bias_ref[hd, t] = jnp.where(idx < 0, NEG, acc)
    for hd in range(HEADS):
        far = relb_ref[REL_BUCKETS - 1, hd]
        for kb in range(NSUB):
            for qb in range(NSUB):
                delta = qb - kb
                if delta == 0:
                    blk = bias_ref[hd, 0]
                elif delta == 1:
                    blk = bias_ref[hd, 1]
                else:
                    blk = jnp.full((BLK, BLK), far if delta > 1 else NEG, F32)
                diag_ref[hd, kb * BLK:(kb + 1) * BLK, qb * BLK:(qb + 1) * BLK] = blk * LOG2E


def _prologue(hgrn_lower_bounds, rel_bias):
    idx = jnp.asarray(_bucket_tiles())
    vmem = pl.BlockSpec(memory_space=pltpu.VMEM)
    return pl.pallas_call(
        _prologue_kernel,
        out_shape=(jax.ShapeDtypeStruct((DEPTH, WIDTH), F32),
                   jax.ShapeDtypeStruct((HEADS, 2, BLK, BLK), F32),
                   jax.ShapeDtypeStruct((HEADS, QB, QB), F32)),
        in_specs=[vmem, pl.BlockSpec(memory_space=pltpu.SMEM), vmem],
        out_specs=(vmem, vmem, vmem),
        name="prologue",
    )(hgrn_lower_bounds, rel_bias, idx)


def _embed_kernel(x_ref, meta_ref, w_ref, h_ref, hn_ref, *, nreal):
    i = pl.program_id(0)

    @pl.when(i < nreal)
    def _():
        h_ref[...] = x_ref[...]

    @pl.when(i == nreal)
    def _():
        h_ref[0:PAD_ROWS, :] = jnp.zeros((PAD_ROWS, D_MODEL), F32)
        h_ref[PAD_ROWS:BLK, :] = meta_ref[...]

    hn_ref[...] = _rms(h_ref[...], w_ref[...]).astype(BF16)


def _embed(x2d, meta, w_pre, nreal):
    rows = (nreal + 1) * BLK
    return pl.pallas_call(
        functools.partial(_embed_kernel, nreal=nreal),
        out_shape=(jax.ShapeDtypeStruct((rows, D_MODEL), F32),
                   jax.ShapeDtypeStruct((rows, D_MODEL), BF16)),
        grid=(nreal + 1,),
        in_specs=[pl.BlockSpec((BLK, D_MODEL), lambda i: (jnp.minimum(i, nreal - 1), 0)),
                  pl.BlockSpec((N_META, D_MODEL), lambda i: (0, 0)),
                  pl.BlockSpec((1, D_MODEL), lambda i: (0, 0))],
        out_specs=(pl.BlockSpec((BLK, D_MODEL), lambda i: (i, 0)),
                   pl.BlockSpec((BLK, D_MODEL), lambda i: (i, 0))),
        compiler_params=_cparams(("arbitrary",)),
        name="embed",
    )(x2d, meta, w_pre)


def _cast_tiles_once(pairs):
    @pl.when(pl.program_id(1) == 0)
    def _():
        for src, dst in pairs:
            dst[...] = src[...].astype(BF16)


def _matmul_kernel(x_ref, *rest):
    *w_refs, o_ref, wbf = rest
    wblk = w_refs[0].shape[-1]
    _cast_tiles_once([(w, wbf.at[:, j * wblk:(j + 1) * wblk]) for j, w in enumerate(w_refs)])
    o_ref[...] = jnp.dot(x_ref[...], wbf[...], preferred_element_type=F32).astype(o_ref.dtype)


def _mixer_in_proj(hn, w_in, layer, col0, ncols, out_dtype, tm, tn, wblk):
    rows, k = hn.shape
    nw = tn // wblk
    w_specs = [pl.BlockSpec((None, k, wblk), functools.partial(
        lambda n, m, j: (layer, 0, col0 // wblk + n * nw + j), j=j)) for j in range(nw)]
    return pl.pallas_call(
        _matmul_kernel,
        out_shape=jax.ShapeDtypeStruct((rows, ncols), out_dtype),
        grid=(ncols // tn, rows // tm),
        in_specs=[pl.BlockSpec((tm, k), lambda n, m: (m, 0))] + w_specs,
        out_specs=pl.BlockSpec((tm, tn), lambda n, m: (m, n)),
        scratch_shapes=[pltpu.VMEM((k, tn), BF16)],
        compiler_params=_cparams(("arbitrary", "arbitrary")),
        name="mixer_in_proj",
    )(hn, *([w_in] * nw))


def _cast_kernel(w_ref, o_ref):
    o_ref[...] = w_ref[...].astype(BF16)


def _cast_layer_bf16(w, layer, tr, col0=0, ncols=None, tc=None):
    _, r, c = w.shape
    ncols = c if ncols is None else ncols
    tc = ncols if tc is None else tc
    return pl.pallas_call(
        _cast_kernel,
        out_shape=jax.ShapeDtypeStruct((r, ncols), BF16),
        grid=(r // tr, ncols // tc),
        in_specs=[pl.BlockSpec((None, tr, tc), lambda i, j: (layer, i, col0 // tc + j))],
        out_specs=pl.BlockSpec((tr, tc), lambda i, j: (i, j)),
        compiler_params=_cparams(("arbitrary", "arbitrary")),
        name="cast_bf16",
    )(w)


def _residual_epilogue(acc, h_ref, wpost_ref, wnext_ref, hnew_ref, hn_ref):
    h_new = h_ref[...] + _rms(acc, wpost_ref[...])
    hnew_ref[...] = h_new
    if hn_ref is not None:
        hn_ref[...] = _rms(h_new, wnext_ref[...]).astype(BF16)


EPI_ROWS = 160


def _row_subtiles(tm):
    sub = next(s for s in (EPI_ROWS, 128, 64, 32, 16) if tm % s == 0)
    return [slice(r, r + sub) for r in range(0, tm, sub)]


def _gate_merge_kernel(hn_ref, g0, g1, g2, g3, y0, y1, y2, y3, wb_ref, o_ref, gbf, wbbf):
    _cast_tiles_once([(g, gbf.at[k]) for k, g in enumerate((g0, g1, g2, g3))] + [(wb_ref, wbbf)])
    hn = hn_ref[...]
    acc = None
    for k, y_ref in enumerate((y0, y1, y2, y3)):
        gate = _sigmoid(jnp.dot(hn, gbf[k], preferred_element_type=F32))
        proj = jnp.dot(y_ref[...], wbbf[k], preferred_element_type=F32)
        acc = gate * proj if acc is None else acc + gate * proj
    o_ref[...] = acc.astype(o_ref.dtype)


def _gate_merge(hn, w_in, ys, w_branch, layer, tm, tn):
    rows, k = hn.shape
    gate_specs = [
        pl.BlockSpec((None, k, tn), functools.partial(
            lambda n, m, base: (layer, 0, base + n), base=(MIX_COLS + br * D_MODEL) // tn))
        for br in range(4)]
    y_specs = [pl.BlockSpec((tm, WIDTH), lambda n, m: (m, 0)) for _ in range(4)]
    return pl.pallas_call(
        _gate_merge_kernel,
        out_shape=jax.ShapeDtypeStruct((rows, D_MODEL), BF16),
        grid=(D_MODEL // tn, rows // tm),
        in_specs=[pl.BlockSpec((tm, k), lambda n, m: (m, 0))] + gate_specs + y_specs
                 + [pl.BlockSpec((None, 4, WIDTH, tn), lambda n, m: (layer, 0, 0, n))],
        out_specs=pl.BlockSpec((tm, tn), lambda n, m: (m, n)),
        scratch_shapes=[pltpu.VMEM((4, k, tn), BF16), pltpu.VMEM((4, WIDTH, tn), BF16)],
        compiler_params=_cparams(("arbitrary", "arbitrary")),
        name="gate_merge",
    )(hn, w_in, w_in, w_in, w_in, *ys, w_branch)


def _out_proj_kernel(x_ref, w_ref, h_ref, wpost_ref, wnext_ref, hnew_ref, hn_ref):
    for rs in _row_subtiles(x_ref.shape[0]):
        acc = jnp.dot(x_ref[rs, :], w_ref[...], preferred_element_type=F32)
        _residual_epilogue(acc, h_ref.at[rs, :], wpost_ref, wnext_ref, hnew_ref.at[rs, :], hn_ref.at[rs, :])


def _out_proj(merged, w_out, h, w_post, w_next, tm):
    rows = h.shape[0]
    row_spec = pl.BlockSpec((tm, D_MODEL), lambda m: (m, 0))
    return pl.pallas_call(
        _out_proj_kernel,
        out_shape=(jax.ShapeDtypeStruct((rows, D_MODEL), F32),
                   jax.ShapeDtypeStruct((rows, D_MODEL), BF16)),
        grid=(rows // tm,),
        in_specs=[row_spec, _resident((D_MODEL, D_MODEL), (0, 0)), row_spec,
                  _pspec((1, D_MODEL)), _pspec((1, D_MODEL))],
        out_specs=(row_spec, row_spec),
        compiler_params=_cparams(("arbitrary",)),
        name="out_proj",
    )(merged, w_out, h, w_post, w_next)


def _swiglu_up_kernel(x_ref, wg_ref, wu_ref, o_ref, wgbf, wubf):
    _cast_tiles_once([(wg_ref, wgbf), (wu_ref, wubf)])
    x = x_ref[...]
    g = jnp.dot(x, wgbf[...], preferred_element_type=F32)
    u = jnp.dot(x, wubf[...], preferred_element_type=F32)
    o_ref[...] = (g * _sigmoid(g) * u).astype(o_ref.dtype)


def _swiglu_up(hn, w_gu, layer, tm, tn):
    rows, k = hn.shape
    nt = FFN_HIDDEN // tn
    return pl.pallas_call(
        _swiglu_up_kernel,
        out_shape=jax.ShapeDtypeStruct((rows, FFN_HIDDEN), BF16),
        grid=(nt, rows // tm),
        in_specs=[pl.BlockSpec((tm, k), lambda n, m: (m, 0)),
                  pl.BlockSpec((None, k, tn), lambda n, m: (layer, 0, n)),
                  pl.BlockSpec((None, k, tn), lambda n, m: (layer, 0, nt + n))],
        out_specs=pl.BlockSpec((tm, tn), lambda n, m: (m, n)),
        scratch_shapes=[pltpu.VMEM((k, tn), BF16), pltpu.VMEM((k, tn), BF16)],
        compiler_params=_cparams(("arbitrary", "arbitrary")),
        name="swiglu_up",
    )(hn, w_gu, w_gu)


def _down_proj_kernel(a_ref, w_ref, h_ref, wpost_ref, wnext_ref, hnew_ref, hn_ref=None):
    for rs in _row_subtiles(a_ref.shape[0]):
        acc = jnp.dot(a_ref[rs, :], w_ref[...], preferred_element_type=F32)
        _residual_epilogue(acc, h_ref.at[rs, :], wpost_ref, wnext_ref, hnew_ref.at[rs, :],
                           None if hn_ref is None else hn_ref.at[rs, :])


def _down_proj(a, w_down, h, w_post, w_next, rows_out, tm, emit_hn):
    row_spec = lambda w: pl.BlockSpec((tm, w), lambda m: (m, 0))
    out_shape = [jax.ShapeDtypeStruct((rows_out, D_MODEL), F32)]
    out_specs = [row_spec(D_MODEL)]
    if emit_hn:
        out_shape.append(jax.ShapeDtypeStruct((rows_out, D_MODEL), BF16))
        out_specs.append(row_spec(D_MODEL))
    res = pl.pallas_call(
        _down_proj_kernel,
        out_shape=tuple(out_shape),
        grid=(rows_out // tm,),
        in_specs=[row_spec(FFN_HIDDEN), _resident((FFN_HIDDEN, D_MODEL), (0, 0)), row_spec(D_MODEL),
                  _pspec((1, D_MODEL)), _pspec((1, D_MODEL))],
        out_specs=tuple(out_specs),
        compiler_params=_cparams(("arbitrary",)),
        name="down_proj",
    )(a, w_down, h, w_post, w_next)
    return res if emit_hn else (res[0], None)


def _row_block(i, nreal):
    return (i + nreal) % (nreal + 1)


def _zspec(col_block, nreal):
    return pl.BlockSpec((BLK, WIDTH), lambda i: (_row_block(i, nreal), col_block))


def _pspec(shape):
    nd = len(shape)
    return pl.BlockSpec(shape, lambda *_: (0,) * nd)


def _resident(shape, index):
    return pl.BlockSpec(shape, lambda *_: index, pipeline_mode=pl.Buffered(1))


def _lru_kernel(u_ref, gate_ref, cw_ref, cb_ref, wa_ref, ba_ref, wx_ref, bx_ref, lam_ref, y_ref,
                ubuf, hst, hist_meta, h_meta, *, nblk):
    i = pl.program_id(0)
    is_meta = i == 0

    @pl.when(is_meta)
    def _():
        ubuf[0:8, :] = jnp.zeros((8, WIDTH), F32)
        hst[...] = jnp.zeros_like(hst)

    @pl.when(jnp.logical_and(i >= 1, (i - 1) % nblk == 0))
    def _():
        ubuf[0:8, :] = hist_meta[...]
        hst[...] = h_meta[...]

    u = u_ref[...]
    ubuf[8:8 + BLK, :] = u
    cw = cw_ref[...]
    xc = (cb_ref[...] + cw[3:4] * u + cw[2:3] * ubuf[7:7 + BLK, :]
          + cw[1:2] * ubuf[6:6 + BLK, :] + cw[0:1] * ubuf[5:5 + BLK, :])
    xb = xc.astype(BF16)
    ra, ia = [], []
    for hd in range(HEADS):
        sl = slice(hd * HEAD_W, (hd + 1) * HEAD_W)
        ra.append(jnp.dot(xb[:, sl], wa_ref[hd].astype(BF16), preferred_element_type=F32))
        ia.append(jnp.dot(xb[:, sl], wx_ref[hd].astype(BF16), preferred_element_type=F32))
    r = _sigmoid(jnp.concatenate(ra, axis=1) + ba_ref[...])
    ig = _sigmoid(jnp.concatenate(ia, axis=1) + bx_ref[...])
    lam = lam_ref[...]
    softplus_neg_lam = jnp.maximum(-lam, 0.0) + jnp.log1p(jnp.exp(-jnp.abs(lam)))
    log_a = -LRU_C * r * softplus_neg_lam
    a = jnp.exp(log_a)
    bb = jnp.sqrt(-jnp.tanh(log_a) * (a * a + 1.0)) * (ig * xc)
    row = lax.broadcasted_iota(jnp.int32, (BLK, 1), 0)
    bb = jnp.where(row >= PAD_ROWS * is_meta.astype(jnp.int32), bb, 0.0)

    acum, bcum = a, bb
    s = 1
    while s < BLK:
        a_sh = _shift_rows(acum, s, 1.0, row)
        b_sh = _shift_rows(bcum, s, 0.0, row)
        bcum = acum * b_sh + bcum
        acum = acum * a_sh
        s *= 2
    h = acum * hst[0:1, :] + bcum
    y_ref[...] = (h * _gelu_tanh(gate_ref[...])).astype(y_ref.dtype)

    hist = u[BLK - 8:BLK, :]
    hlast = jnp.broadcast_to(h[BLK - 1:BLK, :], (8, WIDTH))
    ubuf[0:8, :] = hist
    hst[...] = hlast

    @pl.when(is_meta)
    def _():
        hist_meta[...] = hist
        h_meta[...] = hlast


POOL_HIST = 16


def _pool_kernel(u_ref, pw_ref, ps_ref, y_ref, ubuf, hist_meta, *, nblk):
    i = pl.program_id(0)
    is_meta = i == 0

    @pl.when(is_meta)
    def _():
        ubuf[0:POOL_HIST, :] = jnp.zeros((POOL_HIST, WIDTH), F32)

    @pl.when(jnp.logical_and(i >= 1, (i - 1) % nblk == 0))
    def _():
        ubuf[0:POOL_HIST, :] = hist_meta[...]

    u = u_ref[...]
    ubuf[POOL_HIST:POOL_HIST + BLK, :] = u
    row = lax.broadcasted_iota(jnp.int32, (BLK, 1), 0)
    meta_i = is_meta.astype(jnp.int32)
    pos1 = row + 1 - PAD_ROWS * meta_i + 2 * POOL_HIST * (1 - meta_i)
    outs = []
    for g, win in enumerate(POOL_WINDOWS):
        sl = slice(g * HEAD_W, (g + 1) * HEAD_W)
        acc = u[:, sl]
        for d in range(1, win):
            acc = acc + ubuf[POOL_HIST - d:POOL_HIST - d + BLK, sl]
        count = jnp.clip(pos1, 1, win).astype(F32)
        pooled = acc / count - u[:, sl]
        outs.append(jnp.dot(pooled.astype(BF16), pw_ref[g].astype(BF16), preferred_element_type=F32))
    y_ref[...] = (jnp.concatenate(outs, axis=1) * ps_ref[...]).astype(y_ref.dtype)

    hist = u[BLK - POOL_HIST:BLK, :]
    ubuf[0:POOL_HIST, :] = hist

    @pl.when(is_meta)
    def _():
        hist_meta[...] = hist


def _lru_pool_kernel(u_ref, gate_ref, cw_ref, cb_ref, wa_ref, ba_ref, wx_ref, bx_ref, lam_ref,
                     pu_ref, pw_ref, ps_ref, ya_ref, yb_ref,
                     ubuf, hst, hist_meta, h_meta, pbuf, phist_meta, *, nblk):
    _lru_kernel(u_ref, gate_ref, cw_ref, cb_ref, wa_ref, ba_ref, wx_ref, bx_ref, lam_ref, ya_ref,
                ubuf, hst, hist_meta, h_meta, nblk=nblk)
    _pool_kernel(pu_ref, pw_ref, ps_ref, yb_ref, pbuf, phist_meta, nblk=nblk)


def _lru_pool(z, cw, cb, wa, ba, wx, bx, lam, pw, ps, nreal, nblk):
    rows = z.shape[0]
    out_spec = pl.BlockSpec((BLK, WIDTH), lambda i: (_row_block(i, nreal), 0))
    return pl.pallas_call(
        functools.partial(_lru_pool_kernel, nblk=nblk),
        out_shape=(jax.ShapeDtypeStruct((rows, WIDTH), BF16), jax.ShapeDtypeStruct((rows, WIDTH), BF16)),
        grid=(nreal + 1,),
        in_specs=[_zspec(0, nreal), _zspec(1, nreal),
                  _pspec((4, WIDTH)), _pspec((1, WIDTH)),
                  _pspec((HEADS, HEAD_W, HEAD_W)), _pspec((1, WIDTH)),
                  _pspec((HEADS, HEAD_W, HEAD_W)), _pspec((1, WIDTH)), _pspec((1, WIDTH)),
                  _zspec(2, nreal), _pspec((4, HEAD_W, HEAD_W)), _pspec((1, WIDTH))],
        out_specs=(out_spec, out_spec),
        scratch_shapes=[pltpu.VMEM((8 + BLK, WIDTH), F32), pltpu.VMEM((8, WIDTH), F32),
                        pltpu.VMEM((8, WIDTH), F32), pltpu.VMEM((8, WIDTH), F32),
                        pltpu.VMEM((POOL_HIST + BLK, WIDTH), F32), pltpu.VMEM((POOL_HIST, WIDTH), F32)],
        compiler_params=_cparams(("arbitrary",)),
        name="lru_pool",
    )(z, z, cw, cb, wa, ba, wx, bx, lam, z, pw, ps)


def _hgrn_chunk(q, z, v, lbh, state_t, valid, ones_bf):
    ls = _log_sigmoid(z)
    x1 = jnp.log(lbh)
    x2 = jnp.log1p(-lbh) + ls
    mx = jnp.maximum(x1, x2)
    g = mx + jnp.log(1.0 + jnp.exp(-jnp.abs(x1 - x2)))
    k = (1.0 - lbh) * _sigmoid(-z)
    if valid is not None:
        g = jnp.where(valid, g, 0.0)
    row = lax.broadcasted_iota(jnp.int32, (CHUNK, 1), 0)
    b = g * LOG2E
    s = 1
    while s < CHUNK:
        b = b + _shift_rows(b, s, 0.0, row)
        s *= 2
    b_last = b[CHUNK - 1:CHUNK, :]

    qe = (q * jnp.exp2(b)).astype(BF16)
    o = lax.dot_general(qe, state_t.astype(BF16), (((1,), (1,)), ((), ())), preferred_element_type=F32)

    col = lax.broadcasted_iota(jnp.int32, (SUB, CHUNK), 1)
    rsub = lax.broadcasted_iota(jnp.int32, (SUB, CHUNK), 0)
    lane = lax.broadcasted_iota(jnp.int32, (SUB, HEAD_W), 1)
    s_rows = []
    for blk in range(CHUNK // SUB):
        lo = blk * SUB
        bi = b[lo:lo + SUB, :]
        qi = q[lo:lo + SUB, :]
        ki = k[lo:lo + SUB, :]
        parts = []
        for sr in range(SUB):
            t0 = 0 if sr < SUB // 2 else SUB // 2
            e = jnp.exp2(jnp.minimum(bi[t0:, :] - bi[sr:sr + 1, :], 0.0))
            parts.append(qi[t0:, :] * e * ki[sr:sr + 1, :])
        m3 = jnp.concatenate(parts, axis=0).astype(BF16)
        red = jnp.dot(m3, ones_bf, preferred_element_type=F32)
        halves = [jnp.zeros((SUB // 2, HEAD_W), F32), jnp.zeros((SUB // 2, HEAD_W), F32)]
        r0 = 0
        for sr in range(SUB):
            for hf in range(0 if sr < SUB // 2 else 1, 2):
                halves[hf] = jnp.where(lane[:SUB // 2] == lo + sr, red[r0:r0 + SUB // 2, :], halves[hf])
                r0 += SUB // 2
        diag = jnp.concatenate(halves, axis=0)[:, :CHUNK]
        s_blk = jnp.where(jnp.logical_and(col >= lo, col - lo <= rsub), diag, 0.0)
        if blk > 0:
            b0 = b[lo - 1:lo, :]
            kt = (k * jnp.exp2(jnp.minimum(b0 - b, 0.0))).astype(BF16)
            qd = (qi * jnp.exp2(bi - b0)).astype(BF16)
            off = lax.dot_general(qd, kt, (((1,), (1,)), ((), ())), preferred_element_type=F32)
            s_blk = jnp.where(col < lo, off, s_blk)
        s_rows.append(s_blk)
    scores = jnp.concatenate(s_rows, axis=0).astype(BF16)
    vb = v.astype(BF16)
    o = o + jnp.dot(scores, vb, preferred_element_type=F32)

    kd = (k * jnp.exp2(b_last - b)).astype(BF16)
    upd = lax.dot_general(vb, kd, (((0,), (0,)), ((), ())), preferred_element_type=F32)
    new_state_t = state_t * jnp.exp2(b_last) + upd
    return o, new_state_t


def _hgrn_kernel(q_ref, f_ref, v_ref, og_ref, lb_ref, nw_ref, y_ref, state, state_meta, *, nblk):
    i = pl.program_id(0)
    is_meta = i == 0

    @pl.when(is_meta)
    def _():
        state[...] = jnp.zeros_like(state)

    @pl.when(jnp.logical_and(i >= 1, (i - 1) % nblk == 0))
    def _():
        state[...] = state_meta[...]

    ones_bf = jnp.ones((HEAD_W, HEAD_W), BF16)
    nw = nw_ref[...]
    for hd in range(HEADS):
        sl = slice(hd * HEAD_W, (hd + 1) * HEAD_W)
        lbh = lb_ref[:, sl]
        st = state[hd]
        for c in range(BLK // CHUNK):
            rs = slice(c * CHUNK, (c + 1) * CHUNK)
            rowg = lax.broadcasted_iota(jnp.int32, (CHUNK, 1), 0) + c * CHUNK
            valid = rowg >= PAD_ROWS * is_meta.astype(jnp.int32)
            o, st = _hgrn_chunk(q_ref[rs, sl], f_ref[rs, sl], v_ref[rs, sl], lbh, st, valid, ones_bf)
            og = og_ref[rs, sl]
            y_ref[rs, sl] = (_rms(o, nw) * (og * _sigmoid(og))).astype(y_ref.dtype)
        state[hd] = st

    @pl.when(is_meta)
    def _():
        state_meta[...] = state[...]


def _hgrn(z_a, z_b, lb, nw, nreal, nblk):
    rows = z_a.shape[0]
    return pl.pallas_call(
        functools.partial(_hgrn_kernel, nblk=nblk),
        out_shape=jax.ShapeDtypeStruct((rows, WIDTH), BF16),
        grid=(nreal + 1,),
        in_specs=[_zspec(3, nreal), _zspec(0, nreal), _zspec(1, nreal), _zspec(2, nreal),
                  _pspec((1, WIDTH)), _pspec((1, HEAD_W))],
        out_specs=pl.BlockSpec((BLK, WIDTH), lambda i: (_row_block(i, nreal), 0)),
        scratch_shapes=[pltpu.VMEM((HEADS, HEAD_W, HEAD_W), F32), pltpu.VMEM((HEADS, HEAD_W, HEAD_W), F32)],
        compiler_params=_cparams(("arbitrary",)),
        name="hgrn2",
    )(z_a, z_b, z_b, z_b, lb, nw)


def _attn_lambda(lamv_ref, lambda_init):
    lv = lamv_ref[...]
    return (jnp.exp(jnp.sum(lv[0:1] * lv[1:2], axis=-1, keepdims=True))
            - jnp.exp(jnp.sum(lv[2:3] * lv[3:4], axis=-1, keepdims=True)) + lambda_init)


def _scaled_q(q_ref):
    qf = (q_ref[...].astype(F32) * (DIFF_HEAD_DIM ** -0.5)).astype(BF16)
    return qf[:, :DIFF_HEAD_DIM], qf[:, DIFF_HEAD_DIM:]


def _fold8(x, op):
    r, c = x.shape
    return op(x.reshape(r // 8, 8, c), axis=0)


def _attn_kernel(relb_ref, q_ref, k_ref, v_ref, km_ref, vm_ref, bias_ref, diag_ref, lamv_ref, sw_ref, y_ref,
                 s_sc, vt_sc, m_sc, l_sc, acc_sc, *, nqb, nblk, lambda_init):
    hd = pl.program_id(0)
    jq = pl.program_id(1) % nqb
    far = relb_ref[REL_BUCKETS - 1, hd] * LOG2E
    t1 = bias_ref[1] * LOG2E
    slot_diag, slot_meta = nqb - 1, nqb

    @pl.when(jq == 0)
    def _():
        for t in range(nblk):
            vt_sc[t // NSUB, :, (t % NSUB) * BLK:(t % NSUB + 1) * BLK] = (
                v_ref[t * BLK:(t + 1) * BLK, :].astype(F32).T.astype(BF16))

    qt = (q_ref[...].astype(F32) * (DIFF_HEAD_DIM ** -0.5)).T
    zero = jnp.zeros((DIFF_HEAD_DIM, QB), F32)
    qtp = (jnp.concatenate([qt[:DIFF_HEAD_DIM], zero], axis=0).astype(BF16),
           jnp.concatenate([zero, qt[DIFF_HEAD_DIM:]], axis=0).astype(BF16))

    def key_chunk(c):
        return k_ref[pl.ds(pl.multiple_of(c * QB, QB), QB), :]

    def score_chunk(k_rows, slot, add_bias, r0=0, c0=0):
        r = k_rows.shape[0]
        for mp in range(2):
            s = add_bias(jnp.dot(k_rows, qtp[mp][:, c0:], preferred_element_type=F32) * LOG2E)
            s_sc[mp, slot, r0:r0 + r, c0:] = s
            m_sc[mp, :, c0:] = jnp.maximum(m_sc[mp, :, c0:], _fold8(s, jnp.max))

    def value_chunk(vt_cols, slot, m8, r0=0, c0=0):
        r = vt_cols.shape[1]
        for mp in range(2):
            p = jnp.exp2(s_sc[mp, slot, r0:r0 + r, c0:].reshape(r // 8, 8, QB - c0) - m8[mp][None, :, c0:])
            l_sc[mp, :, c0:] += jnp.sum(p, axis=0)
            acc_sc[mp, :, c0:] += jnp.dot(vt_cols, p.reshape(r, QB - c0).astype(BF16),
                                         preferred_element_type=F32)

    def diag_tiles(fn):
        for kb in range(NSUB):
            fn(kb, kb * BLK)

    def walk(fn_far, fn_meta, fn_diag):
        def body(c, carry):
            fn_far(c)
            return carry

        lax.fori_loop(0, jq, body, 0)
        fn_meta()
        fn_diag()

    near = t1 - far
    first = jnp.full((1, BLK), jq, jnp.int32) == 0

    def meta_bias(s):
        head = s[:, 0:BLK] + (far + jnp.where(first, near[PAD_ROWS:, :], 0.0))
        return jnp.concatenate([head, s[:, BLK:] + far], axis=1)

    def meta_values(m8):
        v16 = vm_ref[PAD_ROWS:BLK, :]
        for mp in range(2):
            p = jnp.exp2(s_sc[mp, slot_meta, 0:N_META, :].reshape(N_META // 8, 8, QB) - m8[mp][None])
            l_sc[mp] += jnp.sum(p, axis=0)
            acc_sc[mp] += lax.dot_general(v16, p.reshape(N_META, QB).astype(BF16), (((0,), (0,)), ((), ())),
                                          preferred_element_type=F32)

    m_sc[...] = jnp.full_like(m_sc, NEG)
    walk(lambda c: score_chunk(key_chunk(c), c, lambda s: s + far),
         lambda: score_chunk(km_ref[PAD_ROWS:BLK, :], slot_meta, meta_bias),
         lambda: diag_tiles(lambda kb, c0: score_chunk(
             k_ref[pl.ds(pl.multiple_of(jq * QB + c0, BLK), BLK), :], slot_diag,
             lambda s: s + diag_ref[c0:c0 + BLK, c0:], r0=c0, c0=c0)))

    @pl.when(jq >= 1)
    def _():
        for mp in range(2):
            fixed = s_sc[mp, jq - 1, QB - BLK:QB, 0:BLK] + near
            s_sc[mp, jq - 1, QB - BLK:QB, 0:BLK] = fixed
            m_sc[mp, :, 0:BLK] = jnp.maximum(m_sc[mp, :, 0:BLK], _fold8(fixed, jnp.max))

    m8 = [jnp.broadcast_to(jnp.max(m_sc[mp], axis=0, keepdims=True), (8, QB)) for mp in range(2)]
    l_sc[...] = jnp.zeros_like(l_sc)
    acc_sc[...] = jnp.zeros_like(acc_sc)

    walk(lambda c: value_chunk(vt_sc[c], c, m8),
         lambda: meta_values(m8),
         lambda: diag_tiles(lambda kb, c0: value_chunk(
             vt_sc[jq, :, c0:c0 + BLK], slot_diag, m8, r0=c0, c0=c0)))

    lam = _attn_lambda(lamv_ref, lambda_init)
    l0 = jnp.sum(l_sc[0], axis=0, keepdims=True)
    l1 = jnp.sum(l_sc[1], axis=0, keepdims=True)
    ot = acc_sc[0] / l0 - lam * (acc_sc[1] / l1)
    ot = ot * lax.rsqrt(jnp.mean(ot * ot, axis=0, keepdims=True) + EPS)
    y_ref[...] = (ot.T * sw_ref[...] * (1.0 - lambda_init)).astype(y_ref.dtype)


def _attn_meta_kernel(q_ref, k_ref, v_ref, bias_ref, lamv_ref, sw_ref, y_ref, *, lambda_init):
    qs = _scaled_q(q_ref)
    kb = k_ref[...]
    colmask = jnp.where(lax.broadcasted_iota(jnp.int32, (1, BLK), 1) >= PAD_ROWS, 0.0, NEG)
    outs = []
    bias = bias_ref[0].T + colmask
    for mp in range(2):
        s = lax.dot_general(qs[mp], kb[:, mp * DIFF_HEAD_DIM:(mp + 1) * DIFF_HEAD_DIM],
                            (((1,), (1,)), ((), ())), preferred_element_type=F32) + bias
        p = jnp.exp(s - jnp.max(s, axis=-1, keepdims=True))
        acc = jnp.dot(p.astype(BF16), v_ref[...], preferred_element_type=F32)
        outs.append(acc / jnp.sum(p, axis=-1, keepdims=True))
    o = outs[0] - _attn_lambda(lamv_ref, lambda_init) * outs[1]
    o = _rms(o, sw_ref[...]) * (1.0 - lambda_init)
    row = lax.broadcasted_iota(jnp.int32, (BLK, 1), 0)
    y_ref[...] = jnp.where(row >= PAD_ROWS, o, 0.0).astype(y_ref.dtype)


def _attn(zd, rel_bias, bias_tiles, bias_diag, lam_vec, subln_w, nreal, nblk, lambda_init):
    nqb = nblk * BLK // QB
    nq_total = nreal * BLK // QB
    seq = nblk * BLK
    kcol, vcol = HEADS, 2 * HEADS
    y_main = pl.pallas_call(
        functools.partial(_attn_kernel, nqb=nqb, nblk=nblk, lambda_init=lambda_init),
        out_shape=jax.ShapeDtypeStruct((nreal * BLK, WIDTH), BF16),
        grid=(HEADS, nq_total),
        in_specs=[pl.BlockSpec(memory_space=pltpu.SMEM),
                  pl.BlockSpec((QB, HEAD_W), lambda h, i: (i, h)),
                  pl.BlockSpec((seq, HEAD_W), lambda h, i: (i // nqb, kcol + h)),
                  pl.BlockSpec((seq, HEAD_W), lambda h, i: (i // nqb, vcol + h)),
                  pl.BlockSpec((BLK, HEAD_W), lambda h, i: (nreal, kcol + h)),
                  pl.BlockSpec((BLK, HEAD_W), lambda h, i: (nreal, vcol + h)),
                  pl.BlockSpec((None, 2, BLK, BLK), lambda h, i: (h, 0, 0, 0)),
                  pl.BlockSpec((None, QB, QB), lambda h, i: (h, 0, 0)),
                  pl.BlockSpec((4, DIFF_HEAD_DIM), lambda h, i: (0, 0)),
                  pl.BlockSpec((1, HEAD_W), lambda h, i: (0, 0))],
        out_specs=pl.BlockSpec((QB, HEAD_W), lambda h, i: (i, h)),
        scratch_shapes=[pltpu.VMEM((2, nqb + 1, QB, QB), F32),
                        pltpu.VMEM((nqb, HEAD_W, QB), BF16),
                        pltpu.VMEM((2, 8, QB), F32),
                        pltpu.VMEM((2, 8, QB), F32),
                        pltpu.VMEM((2, HEAD_W, QB), F32)],
        compiler_params=_cparams(("arbitrary", "arbitrary")),
        name="diff_attn",
    )(rel_bias, zd, zd, zd, zd, zd, bias_tiles, bias_diag, lam_vec, subln_w)
    y_meta = pl.pallas_call(
        functools.partial(_attn_meta_kernel, lambda_init=lambda_init),
        out_shape=jax.ShapeDtypeStruct((BLK, WIDTH), BF16),
        grid=(HEADS,),
        in_specs=[pl.BlockSpec((BLK, HEAD_W), lambda h: (nreal, h)),
                  pl.BlockSpec((BLK, HEAD_W), lambda h: (nreal, kcol + h)),
                  pl.BlockSpec((BLK, HEAD_W), lambda h: (nreal, vcol + h)),
                  pl.BlockSpec((None, 2, BLK, BLK), lambda h: (h, 0, 0, 0)),
                  pl.BlockSpec((4, DIFF_HEAD_DIM), lambda h: (0, 0)),
                  pl.BlockSpec((1, HEAD_W), lambda h: (0, 0))],
        out_specs=pl.BlockSpec((BLK, HEAD_W), lambda h: (0, h)),
        compiler_params=_cparams(("arbitrary",)),
        name="diff_attn_meta",
    )(zd, zd, zd, bias_tiles, lam_vec, subln_w)
    return jnp.concatenate([y_main, y_meta], axis=0)


def _largest_tile(rows, cap, align=16):
    best = align
    for t in range(align, cap + 1, align):
        if rows % t == 0:
            best = t
    return best


def _forward(x, meta_tokens, rel_bias, hgrn_lower_bounds, norm_mix_pre, norm_mix_post, norm_ffn_pre,
             norm_ffn_post, w_in, lru_conv_w, lru_conv_b, lru_w_a, lru_b_a, lru_w_x, lru_b_x, lru_lambda,
             pool_w, pool_scale, hgrn_norm, diff_lambda, diff_subln, w_branch, w_out, ffn_w_gu, ffn_w_down):
    bsz, seq, _ = x.shape
    nblk = seq // BLK
    nreal = bsz * nblk
    rows = (nreal + 1) * BLK
    rows_real = nreal * BLK
    tm_big = _largest_tile(rows, 832)
    tm_epi = _largest_tile(rows, 640)
    tm_down = _largest_tile(rows, 320)
    tm_last = _largest_tile(rows_real, 256)

    def vec(a):
        return a.reshape(1, -1)

    lbs, bias_tiles, bias_diag = _prologue(hgrn_lower_bounds, rel_bias)
    h, hn = _embed(x.reshape(rows_real, D_MODEL), meta_tokens, vec(norm_mix_pre[0]), nreal)

    for layer in range(DEPTH):
        lambda_init = 0.8 - 0.6 * math.exp(-0.3 * layer)
        z = _mixer_in_proj(hn, w_in, layer, 0, 4 * WIDTH, F32, tm_big, 1024, 1024)
        z_b = _mixer_in_proj(hn, w_in, layer, 4 * WIDTH, 3 * WIDTH, F32, tm_big, 3 * WIDTH, WIDTH)
        zd = _mixer_in_proj(hn, w_in, layer, 7 * WIDTH, 3 * WIDTH, BF16, tm_big, 3 * WIDTH, WIDTH)
        y_a, y_b = _lru_pool(z, lru_conv_w[layer], vec(lru_conv_b[layer]), lru_w_a[layer], vec(lru_b_a[layer]),
                             lru_w_x[layer], vec(lru_b_x[layer]), vec(lru_lambda[layer]),
                             pool_w[layer], vec(pool_scale[layer]), nreal, nblk)
        y_c = _hgrn(z, z_b, lbs[layer:layer + 1], vec(hgrn_norm[layer]), nreal, nblk)
        y_d = _attn(zd, rel_bias, bias_tiles, bias_diag, diff_lambda[layer], vec(diff_subln[layer]),
                    nreal, nblk, lambda_init)
        merged = _gate_merge(hn, w_in, (y_a, y_b, y_c, y_d), w_branch, layer, tm_big, 256)
        h, hn = _out_proj(merged, _cast_layer_bf16(w_out, layer, 512), h, vec(norm_mix_post[layer]),
                          vec(norm_ffn_pre[layer]), tm_epi)
        a = _swiglu_up(hn, ffn_w_gu, layer, tm_big, 512)
        last = layer == DEPTH - 1
        w_next = vec(norm_mix_pre[layer + 1]) if not last else vec(norm_mix_pre[layer])
        h, hn = _down_proj(a, _cast_layer_bf16(ffn_w_down, layer, 512), h, vec(norm_ffn_post[layer]), w_next,
                           rows_real if last else rows, tm_last if last else tm_down, not last)
    return h.reshape(bsz, seq, D_MODEL)


def kernel(x, meta_tokens, rel_bias, hgrn_lower_bounds, norm_mix_pre, norm_mix_post, norm_ffn_pre, norm_ffn_post, w_in, lru_conv_w, lru_conv_b, lru_w_a, lru_b_a, lru_w_x, lru_b_x, lru_lambda, pool_w, pool_scale, hgrn_norm, diff_lambda, diff_subln, w_branch, w_out, ffn_w_gu, ffn_w_down):
    return _forward(x, meta_tokens, rel_bias, hgrn_lower_bounds, norm_mix_pre, norm_mix_post, norm_ffn_pre,
                    norm_ffn_post, w_in, lru_conv_w, lru_conv_b, lru_w_a, lru_b_a, lru_w_x, lru_b_x, lru_lambda,
                    pool_w, pool_scale, hgrn_norm, diff_lambda, diff_subln, w_branch, w_out, ffn_w_gu, ffn_w_down)
```

```python
import functools
import math

import numpy as np
import jax
import jax.numpy as jnp
from jax import lax
from jax.experimental import pallas as pl
from jax.experimental.pallas import tpu as pltpu

F32 = jnp.float32
BF16 = jnp.bfloat16

D_MODEL = 2048
SEQ = 2048
DEPTH = 2
N_META = 16
BLK = 128
PAD_ROWS = BLK - N_META
QB = 512
NSUB = QB // BLK
WIDTH = 512
HEADS = 4
HEAD_W = 128
CHUNK = 64
SUB = 16
LRU_C = 8.0
POOL_WINDOWS = (2, 4, 8, 16)
DIFF_HEAD_DIM = 64
REL_BUCKETS = 32
REL_MAX_DIST = 128
FFN_HIDDEN = 5632
MIX_COLS = 10 * WIDTH
NEG = -1e30
EPS = 1e-6
LOG2E = math.log2(math.e)
VMEM_LIMIT = 56 * 1024 * 1024


def _cparams(sem):
    return pltpu.CompilerParams(dimension_semantics=sem, vmem_limit_bytes=VMEM_LIMIT)


def _rms(x, w):
    return x * lax.rsqrt(jnp.mean(x * x, axis=-1, keepdims=True) + EPS) * w


def _log_sigmoid(z):
    return -(jnp.maximum(-z, 0.0) + jnp.log(1.0 + jnp.exp(-jnp.abs(z))))


def _sigmoid(z):
    return 1.0 / (1.0 + jnp.exp(-z))


def _gelu_tanh(x):
    c = math.sqrt(2.0 / math.pi)
    return 0.5 * x * (1.0 + jnp.tanh(c * (x + 0.044715 * (x * x * x))))


def _shift_rows(x, s, fill, row):
    return jnp.where(row >= s, pltpu.roll(x, s, axis=0), fill)


def _bucket_tiles():
    r = np.arange(BLK)[None, :]
    c = np.arange(BLK)[:, None]
    max_exact = REL_BUCKETS // 2

    def bucket(n):
        nf = np.maximum(n, 1).astype(np.float32)
        large = max_exact + (np.log(nf / np.float32(max_exact)) / np.float32(math.log(REL_MAX_DIST / max_exact))
                             * np.float32(REL_BUCKETS - max_exact)).astype(np.int32)
        large = np.minimum(large, REL_BUCKETS - 1)
        return np.where(n < max_exact, n, large).astype(np.int32)

    d0 = r - c
    t0 = np.where(d0 >= 0, bucket(np.maximum(d0, 0)), -1)
    t1 = bucket(BLK + r - c)
    return np.stack([t0, t1]).astype(np.int32)


def _prologue_kernel(lbraw_ref, relb_ref, idx_ref, lb_ref, bias_ref, diag_ref):
    raw = lbraw_ref[...]
    mx = jnp.max(raw, axis=0, keepdims=True)
    e = jnp.exp(raw - mx)
    sm = e / jnp.sum(e, axis=0, keepdims=True)
    cum = sm[0:1]
    lb_ref[0:1, :] = cum - sm[0:1]
    for l in range(1, DEPTH):
        cum = cum + sm[l:l + 1]
        lb_ref[l:l + 1, :] = cum - sm[0:1]
    for t in range(2):
        idx = idx_ref[t]
        for hd in range(HEADS):
            acc = jnp.zeros((BLK, BLK), F32)
            for bk in range(REL_BUCKETS):
                acc = jnp.where(idx == bk, relb_ref[bk, hd], acc)
            bias_ref[hd, t] = jnp.where(idx < 0, NEG, acc)
    for hd in range(HEADS):
        far = relb_ref[REL_BUCKETS - 1, hd]
        for kb in range(NSUB):
            for qb in range(NSUB):
                delta = qb - kb
                if delta == 0:
                    blk = bias_ref[hd, 0]
                elif delta == 1:
                    blk = bias_ref[hd, 1]
                else:
                    blk = jnp.full((BLK, BLK), far if delta > 1 else NEG, F32)
                diag_ref[hd, kb * BLK:(kb + 1) * BLK, qb * BLK:(qb + 1) * BLK] = blk * LOG2E


def _prologue(hgrn_lower_bounds, rel_bias):
    idx = jnp.asarray(_bucket_tiles())
    vmem = pl.BlockSpec(memory_space=pltpu.VMEM)
    return pl.pallas_call(
        _prologue_kernel,
        out_shape=(jax.ShapeDtypeStruct((DEPTH, WIDTH), F32),
                   jax.ShapeDtypeStruct((HEADS, 2, BLK, BLK), F32),
                   jax.ShapeDtypeStruct((HEADS, QB, QB), F32)),
        in_specs=[vmem, pl.BlockSpec(memory_space=pltpu.SMEM), vmem],
        out_specs=(vmem, vmem, vmem),
        name="prologue",
    )(hgrn_lower_bounds, rel_bias, idx)


EMBED_ROWS = 512


def _embed_kernel(x_ref, meta_ref, w_ref, h_ref, hn_ref, *, nfull):
    i = pl.program_id(0)

    @pl.when(i < nfull)
    def _():
        h_ref[...] = x_ref[...]

    @pl.when(i == nfull)
    def _():
        h_ref[...] = jnp.zeros_like(h_ref)
        h_ref[PAD_ROWS:BLK, :] = meta_ref[...]

    hn_ref[...] = _rms(h_ref[...], w_ref[...]).astype(BF16)


def _embed(x2d, meta, w_pre, nreal):
    rows = (nreal + 1) * BLK
    nfull = nreal * BLK // EMBED_ROWS
    return pl.pallas_call(
        functools.partial(_embed_kernel, nfull=nfull),
        out_shape=(jax.ShapeDtypeStruct((rows, D_MODEL), F32),
                   jax.ShapeDtypeStruct((rows, D_MODEL), BF16)),
        grid=(nfull + 1,),
        in_specs=[pl.BlockSpec((EMBED_ROWS, D_MODEL), lambda i: (jnp.minimum(i, nfull - 1), 0)),
                  pl.BlockSpec((N_META, D_MODEL), lambda i: (0, 0)),
                  pl.BlockSpec((1, D_MODEL), lambda i: (0, 0))],
        out_specs=(pl.BlockSpec((EMBED_ROWS, D_MODEL), lambda i: (i, 0)),
                   pl.BlockSpec((EMBED_ROWS, D_MODEL), lambda i: (i, 0))),
        compiler_params=_cparams(("arbitrary",)),
        name="embed",
    )(x2d, meta, w_pre)


def _cast_tiles_once(pairs):
    @pl.when(pl.program_id(1) == 0)
    def _():
        for src, dst in pairs:
            dst[...] = src[...].astype(BF16)


def _matmul_kernel(x_ref, *rest):
    *w_refs, o_ref, wbf = rest
    wblk = w_refs[0].shape[-1]
    _cast_tiles_once([(w, wbf.at[:, j * wblk:(j + 1) * wblk]) for j, w in enumerate(w_refs)])
    o_ref[...] = jnp.dot(x_ref[...], wbf[...], preferred_element_type=F32).astype(o_ref.dtype)


def _mixer_in_proj(hn, w_in, layer, col0, ncols, out_dtype, tm, tn, wblk):
    rows, k = hn.shape
    nw = tn // wblk
    w_specs = [pl.BlockSpec((None, k, wblk), functools.partial(
        lambda n, m, j: (layer, 0, col0 // wblk + n * nw + j), j=j)) for j in range(nw)]
    return pl.pallas_call(
        _matmul_kernel,
        out_shape=jax.ShapeDtypeStruct((rows, ncols), out_dtype),
        grid=(ncols // tn, rows // tm),
        in_specs=[pl.BlockSpec((tm, k), lambda n, m: (m, 0))] + w_specs,
        out_specs=pl.BlockSpec((tm, tn), lambda n, m: (m, n)),
        scratch_shapes=[pltpu.VMEM((k, tn), BF16)],
        compiler_params=_cparams(("arbitrary", "arbitrary")),
        name="mixer_in_proj",
    )(hn, *([w_in] * nw))


def _cast_kernel(w_ref, o_ref):
    o_ref[...] = w_ref[...].astype(BF16)


def _cast_layer_bf16(w, layer, tr, col0=0, ncols=None, tc=None):
    _, r, c = w.shape
    ncols = c if ncols is None else ncols
    tc = ncols if tc is None else tc
    return pl.pallas_call(
        _cast_kernel,
        out_shape=jax.ShapeDtypeStruct((r, ncols), BF16),
        grid=(r // tr, ncols // tc),
        in_specs=[pl.BlockSpec((None, tr, tc), lambda i, j: (layer, i, col0 // tc + j))],
        out_specs=pl.BlockSpec((tr, tc), lambda i, j: (i, j)),
        compiler_params=_cparams(("arbitrary", "arbitrary")),
        name="cast_bf16",
    )(w)


def _residual_epilogue(acc, h_ref, wpost_ref, wnext_ref, hnew_ref, hn_ref):
    h_new = h_ref[...] + _rms(acc, wpost_ref[...])
    hnew_ref[...] = h_new
    if hn_ref is not None:
        hn_ref[...] = _rms(h_new, wnext_ref[...]).astype(BF16)


EPI_ROWS = 160


def _row_subtiles(tm):
    sub = next(s for s in (EPI_ROWS, 128, 64, 32, 16) if tm % s == 0)
    return [slice(r, r + sub) for r in range(0, tm, sub)]


def _gate_merge_kernel(hn_ref, g0, g1, g2, g3, y0, y1, y2, y3, wb_ref, o_ref, gbf, wbbf):
    _cast_tiles_once([(g, gbf.at[k]) for k, g in enumerate((g0, g1, g2, g3))] + [(wb_ref, wbbf)])
    hn = hn_ref[...]
    acc = None
    for k, y_ref in enumerate((y0, y1, y2, y3)):
        gate = _sigmoid(jnp.dot(hn, gbf[k], preferred_element_type=F32))
        proj = jnp.dot(y_ref[...], wbbf[k], preferred_element_type=F32)
        acc = gate * proj if acc is None else acc + gate * proj
    o_ref[...] = acc.astype(o_ref.dtype)


def _gate_merge(hn, w_in, ys, w_branch, layer, tm, tn):
    rows, k = hn.shape
    gate_specs = [
        pl.BlockSpec((None, k, tn), functools.partial(
            lambda n, m, base: (layer, 0, base + n), base=(MIX_COLS + br * D_MODEL) // tn))
        for br in range(4)]
    y_specs = [pl.BlockSpec((tm, WIDTH), lambda n, m: (m, 0)) for _ in range(4)]
    return pl.pallas_call(
        _gate_merge_kernel,
        out_shape=jax.ShapeDtypeStruct((rows, D_MODEL), BF16),
        grid=(D_MODEL // tn, rows // tm),
        in_specs=[pl.BlockSpec((tm, k), lambda n, m: (m, 0))] + gate_specs + y_specs
                 + [pl.BlockSpec((None, 4, WIDTH, tn), lambda n, m: (layer, 0, 0, n))],
        out_specs=pl.BlockSpec((tm, tn), lambda n, m: (m, n)),
        scratch_shapes=[pltpu.VMEM((4, k, tn), BF16), pltpu.VMEM((4, WIDTH, tn), BF16)],
        compiler_params=_cparams(("arbitrary", "arbitrary")),
        name="gate_merge",
    )(hn, w_in, w_in, w_in, w_in, *ys, w_branch)


def _out_proj_kernel(x_ref, w_ref, h_ref, wpost_ref, wnext_ref, hnew_ref, hn_ref):
    for rs in _row_subtiles(x_ref.shape[0]):
        acc = jnp.dot(x_ref[rs, :], w_ref[...], preferred_element_type=F32)
        _residual_epilogue(acc, h_ref.at[rs, :], wpost_ref, wnext_ref, hnew_ref.at[rs, :], hn_ref.at[rs, :])


def _out_proj(merged, w_out, h, w_post, w_next, tm):
    rows = h.shape[0]
    row_spec = pl.BlockSpec((tm, D_MODEL), lambda m: (m, 0))
    return pl.pallas_call(
        _out_proj_kernel,
        out_shape=(jax.ShapeDtypeStruct((rows, D_MODEL), F32),
                   jax.ShapeDtypeStruct((rows, D_MODEL), BF16)),
        grid=(rows // tm,),
        in_specs=[row_spec, _resident((D_MODEL, D_MODEL), (0, 0)), row_spec,
                  _pspec((1, D_MODEL)), _pspec((1, D_MODEL))],
        out_specs=(row_spec, row_spec),
        compiler_params=_cparams(("arbitrary",)),
        name="out_proj",
    )(merged, w_out, h, w_post, w_next)


def _swiglu_up_kernel(x_ref, wg_ref, wu_ref, o_ref, wgbf, wubf):
    _cast_tiles_once([(wg_ref, wgbf), (wu_ref, wubf)])
    x = x_ref[...]
    g = jnp.dot(x, wgbf[...], preferred_element_type=F32)
    u = jnp.dot(x, wubf[...], preferred_element_type=F32)
    o_ref[...] = (g * _sigmoid(g) * u).astype(o_ref.dtype)


def _swiglu_up(hn, w_gu, layer, tm, tn):
    rows, k = hn.shape
    nt = FFN_HIDDEN // tn
    return pl.pallas_call(
        _swiglu_up_kernel,
        out_shape=jax.ShapeDtypeStruct((rows, FFN_HIDDEN), BF16),
        grid=(nt, rows // tm),
        in_specs=[pl.BlockSpec((tm, k), lambda n, m: (m, 0)),
                  pl.BlockSpec((None, k, tn), lambda n, m: (layer, 0, n)),
                  pl.BlockSpec((None, k, tn), lambda n, m: (layer, 0, nt + n))],
        out_specs=pl.BlockSpec((tm, tn), lambda n, m: (m, n)),
        scratch_shapes=[pltpu.VMEM((k, tn), BF16), pltpu.VMEM((k, tn), BF16)],
        compiler_params=_cparams(("arbitrary", "arbitrary")),
        name="swiglu_up",
    )(hn, w_gu, w_gu)


def _down_proj_kernel(a_ref, w_ref, h_ref, wpost_ref, wnext_ref, hnew_ref, hn_ref=None):
    for rs in _row_subtiles(a_ref.shape[0]):
        acc = jnp.dot(a_ref[rs, :], w_ref[...], preferred_element_type=F32)
        _residual_epilogue(acc, h_ref.at[rs, :], wpost_ref, wnext_ref, hnew_ref.at[rs, :],
                           None if hn_ref is None else hn_ref.at[rs, :])


def _down_proj(a, w_down, h, w_post, w_next, rows_out, tm, emit_hn):
    row_spec = lambda w: pl.BlockSpec((tm, w), lambda m: (m, 0))
    out_shape = [jax.ShapeDtypeStruct((rows_out, D_MODEL), F32)]
    out_specs = [row_spec(D_MODEL)]
    if emit_hn:
        out_shape.append(jax.ShapeDtypeStruct((rows_out, D_MODEL), BF16))
        out_specs.append(row_spec(D_MODEL))
    res = pl.pallas_call(
        _down_proj_kernel,
        out_shape=tuple(out_shape),
        grid=(rows_out // tm,),
        in_specs=[row_spec(FFN_HIDDEN), _resident((FFN_HIDDEN, D_MODEL), (0, 0)), row_spec(D_MODEL),
                  _pspec((1, D_MODEL)), _pspec((1, D_MODEL))],
        out_specs=tuple(out_specs),
        compiler_params=_cparams(("arbitrary",)),
        name="down_proj",
    )(a, w_down, h, w_post, w_next)
    return res if emit_hn else (res[0], None)


def _row_block(i, nreal):
    return (i + nreal) % (nreal + 1)


def _zspec(col_block, nreal):
    return pl.BlockSpec((BLK, WIDTH), lambda i: (_row_block(i, nreal), col_block))


def _pspec(shape):
    nd = len(shape)
    return pl.BlockSpec(shape, lambda *_: (0,) * nd)


def _resident(shape, index):
    return pl.BlockSpec(shape, lambda *_: index, pipeline_mode=pl.Buffered(1))


def _lru_kernel(u_ref, gate_ref, cw_ref, cb_ref, wa_ref, ba_ref, wx_ref, bx_ref, lam_ref, y_ref,
                ubuf, hst, hist_meta, h_meta, *, nblk):
    i = pl.program_id(0)
    is_meta = i == 0

    @pl.when(is_meta)
    def _():
        ubuf[0:8, :] = jnp.zeros((8, WIDTH), F32)
        hst[...] = jnp.zeros_like(hst)

    @pl.when(jnp.logical_and(i >= 1, (i - 1) % nblk == 0))
    def _():
        ubuf[0:8, :] = hist_meta[...]
        hst[...] = h_meta[...]

    u = u_ref[...]
    ubuf[8:8 + BLK, :] = u
    cw = cw_ref[...]
    xc = (cb_ref[...] + cw[3:4] * u + cw[2:3] * ubuf[7:7 + BLK, :]
          + cw[1:2] * ubuf[6:6 + BLK, :] + cw[0:1] * ubuf[5:5 + BLK, :])
    xb = xc.astype(BF16)
    ra, ia = [], []
    for hd in range(HEADS):
        sl = slice(hd * HEAD_W, (hd + 1) * HEAD_W)
        ra.append(jnp.dot(xb[:, sl], wa_ref[hd].astype(BF16), preferred_element_type=F32))
        ia.append(jnp.dot(xb[:, sl], wx_ref[hd].astype(BF16), preferred_element_type=F32))
    r = _sigmoid(jnp.concatenate(ra, axis=1) + ba_ref[...])
    ig = _sigmoid(jnp.concatenate(ia, axis=1) + bx_ref[...])
    lam = lam_ref[...]
    softplus_neg_lam = jnp.maximum(-lam, 0.0) + jnp.log1p(jnp.exp(-jnp.abs(lam)))
    log_a = -LRU_C * r * softplus_neg_lam
    a = jnp.exp(log_a)
    bb = jnp.sqrt(-jnp.tanh(log_a) * (a * a + 1.0)) * (ig * xc)
    row = lax.broadcasted_iota(jnp.int32, (BLK, 1), 0)
    bb = jnp.where(row >= PAD_ROWS * is_meta.astype(jnp.int32), bb, 0.0)

    acum, bcum = a, bb
    s = 1
    while s < BLK:
        a_sh = _shift_rows(acum, s, 1.0, row)
        b_sh = _shift_rows(bcum, s, 0.0, row)
        bcum = acum * b_sh + bcum
        acum = acum * a_sh
        s *= 2
    h = acum * hst[0:1, :] + bcum
    y_ref[...] = (h * _gelu_tanh(gate_ref[...])).astype(y_ref.dtype)

    hist = u[BLK - 8:BLK, :]
    hlast = jnp.broadcast_to(h[BLK - 1:BLK, :], (8, WIDTH))
    ubuf[0:8, :] = hist
    hst[...] = hlast

    @pl.when(is_meta)
    def _():
        hist_meta[...] = hist
        h_meta[...] = hlast


POOL_HIST = 16


def _pool_kernel(u_ref, pw_ref, ps_ref, y_ref, ubuf, hist_meta, *, nblk):
    i = pl.program_id(0)
    is_meta = i == 0

    @pl.when(is_meta)
    def _():
        ubuf[0:POOL_HIST, :] = jnp.zeros((POOL_HIST, WIDTH), F32)

    @pl.when(jnp.logical_and(i >= 1, (i - 1) % nblk == 0))
    def _():
        ubuf[0:POOL_HIST, :] = hist_meta[...]

    u = u_ref[...]
    ubuf[POOL_HIST:POOL_HIST + BLK, :] = u
    row = lax.broadcasted_iota(jnp.int32, (BLK, 1), 0)
    meta_i = is_meta.astype(jnp.int32)
    pos1 = row + 1 - PAD_ROWS * meta_i + 2 * POOL_HIST * (1 - meta_i)
    outs = []
    for g, win in enumerate(POOL_WINDOWS):
        sl = slice(g * HEAD_W, (g + 1) * HEAD_W)
        acc = u[:, sl]
        for d in range(1, win):
            acc = acc + ubuf[POOL_HIST - d:POOL_HIST - d + BLK, sl]
        count = jnp.clip(pos1, 1, win).astype(F32)
        pooled = acc / count - u[:, sl]
        outs.append(jnp.dot(pooled.astype(BF16), pw_ref[g].astype(BF16), preferred_element_type=F32))
    y_ref[...] = (jnp.concatenate(outs, axis=1) * ps_ref[...]).astype(y_ref.dtype)

    hist = u[BLK - POOL_HIST:BLK, :]
    ubuf[0:POOL_HIST, :] = hist

    @pl.when(is_meta)
    def _():
        hist_meta[...] = hist


def _lru_pool_kernel(u_ref, gate_ref, cw_ref, cb_ref, wa_ref, ba_ref, wx_ref, bx_ref, lam_ref,
                     pu_ref, pw_ref, ps_ref, ya_ref, yb_ref,
                     ubuf, hst, hist_meta, h_meta, pbuf, phist_meta, *, nblk):
    _lru_kernel(u_ref, gate_ref, cw_ref, cb_ref, wa_ref, ba_ref, wx_ref, bx_ref, lam_ref, ya_ref,
                ubuf, hst, hist_meta, h_meta, nblk=nblk)
    _pool_kernel(pu_ref, pw_ref, ps_ref, yb_ref, pbuf, phist_meta, nblk=nblk)


def _lru_pool(z, cw, cb, wa, ba, wx, bx, lam, pw, ps, nreal, nblk):
    rows = z.shape[0]
    out_spec = pl.BlockSpec((BLK, WIDTH), lambda i: (_row_block(i, nreal), 0))
    return pl.pallas_call(
        functools.partial(_lru_pool_kernel, nblk=nblk),
        out_shape=(jax.ShapeDtypeStruct((rows, WIDTH), BF16), jax.ShapeDtypeStruct((rows, WIDTH), BF16)),
        grid=(nreal + 1,),
        in_specs=[_zspec(0, nreal), _zspec(1, nreal),
                  _pspec((4, WIDTH)), _pspec((1, WIDTH)),
                  _pspec((HEADS, HEAD_W, HEAD_W)), _pspec((1, WIDTH)),
                  _pspec((HEADS, HEAD_W, HEAD_W)), _pspec((1, WIDTH)), _pspec((1, WIDTH)),
                  _zspec(2, nreal), _pspec((4, HEAD_W, HEAD_W)), _pspec((1, WIDTH))],
        out_specs=(out_spec, out_spec),
        scratch_shapes=[pltpu.VMEM((8 + BLK, WIDTH), F32), pltpu.VMEM((8, WIDTH), F32),
                        pltpu.VMEM((8, WIDTH), F32), pltpu.VMEM((8, WIDTH), F32),
                        pltpu.VMEM((POOL_HIST + BLK, WIDTH), F32), pltpu.VMEM((POOL_HIST, WIDTH), F32)],
        compiler_params=_cparams(("arbitrary",)),
        name="lru_pool",
    )(z, z, cw, cb, wa, ba, wx, bx, lam, z, pw, ps)


def _hgrn_chunk(q, z, v, lbh, state_t, valid, ones_bf):
    ls = _log_sigmoid(z)
    x1 = jnp.log(lbh)
    x2 = jnp.log1p(-lbh) + ls
    mx = jnp.maximum(x1, x2)
    g = mx + jnp.log(1.0 + jnp.exp(-jnp.abs(x1 - x2)))
    k = (1.0 - lbh) * _sigmoid(-z)
    if valid is not None:
        g = jnp.where(valid, g, 0.0)
    row = lax.broadcasted_iota(jnp.int32, (CHUNK, 1), 0)
    b = g * LOG2E
    s = 1
    while s < CHUNK:
        b = b + _shift_rows(b, s, 0.0, row)
        s *= 2
    b_last = b[CHUNK - 1:CHUNK, :]

    qe = (q * jnp.exp2(b)).astype(BF16)
    o = lax.dot_general(qe, state_t.astype(BF16), (((1,), (1,)), ((), ())), preferred_element_type=F32)

    col = lax.broadcasted_iota(jnp.int32, (SUB, CHUNK), 1)
    rsub = lax.broadcasted_iota(jnp.int32, (SUB, CHUNK), 0)
    lane = lax.broadcasted_iota(jnp.int32, (SUB, HEAD_W), 1)
    s_rows = []
    for blk in range(CHUNK // SUB):
        lo = blk * SUB
        bi = b[lo:lo + SUB, :]
        qi = q[lo:lo + SUB, :]
        ki = k[lo:lo + SUB, :]
        parts = []
        for sr in range(SUB):
            t0 = 0 if sr < SUB // 2 else SUB // 2
            e = jnp.exp2(jnp.minimum(bi[t0:, :] - bi[sr:sr + 1, :], 0.0))
            parts.append(qi[t0:, :] * e * ki[sr:sr + 1, :])
        m3 = jnp.concatenate(parts, axis=0).astype(BF16)
        red = jnp.dot(m3, ones_bf, preferred_element_type=F32)
        halves = [jnp.zeros((SUB // 2, HEAD_W), F32), jnp.zeros((SUB // 2, HEAD_W), F32)]
        r0 = 0
        for sr in range(SUB):
            for hf in range(0 if sr < SUB // 2 else 1, 2):
                halves[hf] = jnp.where(lane[:SUB // 2] == lo + sr, red[r0:r0 + SUB // 2, :], halves[hf])
                r0 += SUB // 2
        diag = jnp.concatenate(halves, axis=0)[:, :CHUNK]
        s_blk = jnp.where(jnp.logical_and(col >= lo, col - lo <= rsub), diag, 0.0)
        if blk > 0:
            b0 = b[lo - 1:lo, :]
            kt = (k * jnp.exp2(jnp.minimum(b0 - b, 0.0))).astype(BF16)
            qd = (qi * jnp.exp2(bi - b0)).astype(BF16)
            off = lax.dot_general(qd, kt, (((1,), (1,)), ((), ())), preferred_element_type=F32)
            s_blk = jnp.where(col < lo, off, s_blk)
        s_rows.append(s_blk)
    scores = jnp.concatenate(s_rows, axis=0).astype(BF16)
    vb = v.astype(BF16)
    o = o + jnp.dot(scores, vb, preferred_element_type=F32)

    kd = (k * jnp.exp2(b_last - b)).astype(BF16)
    upd = lax.dot_general(vb, kd, (((0,), (0,)), ((), ())), preferred_element_type=F32)
    new_state_t = state_t * jnp.exp2(b_last) + upd
    return o, new_state_t


def _hgrn_kernel(q_ref, f_ref, v_ref, og_ref, lb_ref, nw_ref, y_ref, state, state_meta, *, nblk):
    i = pl.program_id(0)
    is_meta = i == 0

    @pl.when(is_meta)
    def _():
        state[...] = jnp.zeros_like(state)

    @pl.when(jnp.logical_and(i >= 1, (i - 1) % nblk == 0))
    def _():
        state[...] = state_meta[...]

    ones_bf = jnp.ones((HEAD_W, HEAD_W), BF16)
    nw = nw_ref[...]
    for hd in range(HEADS):
        sl = slice(hd * HEAD_W, (hd + 1) * HEAD_W)
        lbh = lb_ref[:, sl]
        st = state[hd]
        for c in range(BLK // CHUNK):
            rs = slice(c * CHUNK, (c + 1) * CHUNK)
            rowg = lax.broadcasted_iota(jnp.int32, (CHUNK, 1), 0) + c * CHUNK
            valid = rowg >= PAD_ROWS * is_meta.astype(jnp.int32)
            o, st = _hgrn_chunk(q_ref[rs, sl], f_ref[rs, sl], v_ref[rs, sl], lbh, st, valid, ones_bf)
            og = og_ref[rs, sl]
            y_ref[rs, sl] = (_rms(o, nw) * (og * _sigmoid(og))).astype(y_ref.dtype)
        state[hd] = st

    @pl.when(is_meta)
    def _():
        state_meta[...] = state[...]


def _hgrn(z_a, z_b, lb, nw, nreal, nblk):
    rows = z_a.shape[0]
    return pl.pallas_call(
        functools.partial(_hgrn_kernel, nblk=nblk),
        out_shape=jax.ShapeDtypeStruct((rows, WIDTH), BF16),
        grid=(nreal + 1,),
        in_specs=[_zspec(3, nreal), _zspec(0, nreal), _zspec(1, nreal), _zspec(2, nreal),
                  _pspec((1, WIDTH)), _pspec((1, HEAD_W))],
        out_specs=pl.BlockSpec((BLK, WIDTH), lambda i: (_row_block(i, nreal), 0)),
        scratch_shapes=[pltpu.VMEM((HEADS, HEAD_W, HEAD_W), F32), pltpu.VMEM((HEADS, HEAD_W, HEAD_W), F32)],
        compiler_params=_cparams(("arbitrary",)),
        name="hgrn2",
    )(z_a, z_b, z_b, z_b, lb, nw)


def _attn_lambda(lamv_ref, lambda_init):
    lv = lamv_ref[...]
    return (jnp.exp(jnp.sum(lv[0:1] * lv[1:2], axis=-1, keepdims=True))
            - jnp.exp(jnp.sum(lv[2:3] * lv[3:4], axis=-1, keepdims=True)) + lambda_init)


def _scaled_q(q_ref):
    qf = (q_ref[...].astype(F32) * (DIFF_HEAD_DIM ** -0.5)).astype(BF16)
    return qf[:, :DIFF_HEAD_DIM], qf[:, DIFF_HEAD_DIM:]


def _fold8(x, op):
    r, c = x.shape
    return op(x.reshape(r // 8, 8, c), axis=0)


def _attn_kernel(relb_ref, q_ref, k_ref, v_ref, km_ref, vm_ref, bias_ref, diag_ref, lamv_ref, sw_ref, y_ref,
                 s_sc, vt_sc, m_sc, l_sc, acc_sc, *, nqb, nblk, lambda_init):
    hd = pl.program_id(0)
    jq = pl.program_id(1) % nqb
    far = relb_ref[REL_BUCKETS - 1, hd] * LOG2E
    t1 = bias_ref[1] * LOG2E
    slot_diag, slot_meta = nqb - 1, nqb

    @pl.when(jq == 0)
    def _():
        for t in range(nblk):
            vt_sc[t // NSUB, :, (t % NSUB) * BLK:(t % NSUB + 1) * BLK] = (
                v_ref[t * BLK:(t + 1) * BLK, :].astype(F32).T.astype(BF16))

    qt = (q_ref[...].astype(F32) * (DIFF_HEAD_DIM ** -0.5)).T
    zero = jnp.zeros((DIFF_HEAD_DIM, QB), F32)
    qtp = (jnp.concatenate([qt[:DIFF_HEAD_DIM], zero], axis=0).astype(BF16),
           jnp.concatenate([zero, qt[DIFF_HEAD_DIM:]], axis=0).astype(BF16))

    def key_chunk(c):
        return k_ref[pl.ds(pl.multiple_of(c * QB, QB), QB), :]

    def score_chunk(k_rows, slot, add_bias, r0=0, c0=0):
        r = k_rows.shape[0]
        for mp in range(2):
            s = add_bias(jnp.dot(k_rows, qtp[mp][:, c0:], preferred_element_type=F32) * LOG2E)
            s_sc[mp, slot, r0:r0 + r, c0:] = s
            m_sc[mp, :, c0:] = jnp.maximum(m_sc[mp, :, c0:], _fold8(s, jnp.max))

    def value_chunk(vt_cols, slot, m8, r0=0, c0=0):
        r = vt_cols.shape[1]
        for mp in range(2):
            p = jnp.exp2(s_sc[mp, slot, r0:r0 + r, c0:].reshape(r // 8, 8, QB - c0) - m8[mp][None, :, c0:])
            l_sc[mp, :, c0:] += jnp.sum(p, axis=0)
            acc_sc[mp, :, c0:] += jnp.dot(vt_cols, p.reshape(r, QB - c0).astype(BF16),
                                         preferred_element_type=F32)

    def diag_tiles(fn):
        for kb in range(NSUB):
            fn(kb, kb * BLK)

    def walk(fn_far, fn_meta, fn_diag):
        def body(c, carry):
            fn_far(c)
            return carry

        lax.fori_loop(0, jq, body, 0)
        fn_meta()
        fn_diag()

    near = t1 - far
    first = jnp.full((1, BLK), jq, jnp.int32) == 0

    def meta_bias(s):
        head = s[:, 0:BLK] + (far + jnp.where(first, near[PAD_ROWS:, :], 0.0))
        return jnp.concatenate([head, s[:, BLK:] + far], axis=1)

    def meta_values(m8):
        v16 = vm_ref[PAD_ROWS:BLK, :]
        for mp in range(2):
            p = jnp.exp2(s_sc[mp, slot_meta, 0:N_META, :].reshape(N_META // 8, 8, QB) - m8[mp][None])
            l_sc[mp] += jnp.sum(p, axis=0)
            acc_sc[mp] += lax.dot_general(v16, p.reshape(N_META, QB).astype(BF16), (((0,), (0,)), ((), ())),
                                          preferred_element_type=F32)

    m_sc[...] = jnp.full_like(m_sc, NEG)
    walk(lambda c: score_chunk(key_chunk(c), c, lambda s: s + far),
         lambda: score_chunk(km_ref[PAD_ROWS:BLK, :], slot_meta, meta_bias),
         lambda: diag_tiles(lambda kb, c0: score_chunk(
             k_ref[pl.ds(pl.multiple_of(jq * QB + c0, BLK), BLK), :], slot_diag,
             lambda s: s + diag_ref[c0:c0 + BLK, c0:], r0=c0, c0=c0)))

    @pl.when(jq >= 1)
    def _():
        for mp in range(2):
            fixed = s_sc[mp, jq - 1, QB - BLK:QB, 0:BLK] + near
            s_sc[mp, jq - 1, QB - BLK:QB, 0:BLK] = fixed
            m_sc[mp, :, 0:BLK] = jnp.maximum(m_sc[mp, :, 0:BLK], _fold8(fixed, jnp.max))

    m8 = [jnp.broadcast_to(jnp.max(m_sc[mp], axis=0, keepdims=True), (8, QB)) for mp in range(2)]
    l_sc[...] = jnp.zeros_like(l_sc)
    acc_sc[...] = jnp.zeros_like(acc_sc)

    walk(lambda c: value_chunk(vt_sc[c], c, m8),
         lambda: meta_values(m8),
         lambda: diag_tiles(lambda kb, c0: value_chunk(
             vt_sc[jq, :, c0:c0 + BLK], slot_diag, m8, r0=c0, c0=c0)))

    lam = _attn_lambda(lamv_ref, lambda_init)
    l0 = jnp.sum(l_sc[0], axis=0, keepdims=True)
    l1 = jnp.sum(l_sc[1], axis=0, keepdims=True)
    ot = acc_sc[0] / l0 - lam * (acc_sc[1] / l1)
    ot = ot * lax.rsqrt(jnp.mean(ot * ot, axis=0, keepdims=True) + EPS)
    y_ref[...] = (ot.T * sw_ref[...] * (1.0 - lambda_init)).astype(y_ref.dtype)


def _attn_meta_kernel(q_ref, k_ref, v_ref, bias_ref, lamv_ref, sw_ref, y_ref, *, lambda_init):
    qs = _scaled_q(q_ref)
    kb = k_ref[...]
    colmask = jnp.where(lax.broadcasted_iota(jnp.int32, (1, BLK), 1) >= PAD_ROWS, 0.0, NEG)
    outs = []
    bias = bias_ref[0].T + colmask
    for mp in range(2):
        s = lax.dot_general(qs[mp], kb[:, mp * DIFF_HEAD_DIM:(mp + 1) * DIFF_HEAD_DIM],
                            (((1,), (1,)), ((), ())), preferred_element_type=F32) + bias
        p = jnp.exp(s - jnp.max(s, axis=-1, keepdims=True))
        acc = jnp.dot(p.astype(BF16), v_ref[...], preferred_element_type=F32)
        outs.append(acc / jnp.sum(p, axis=-1, keepdims=True))
    o = outs[0] - _attn_lambda(lamv_ref, lambda_init) * outs[1]
    o = _rms(o, sw_ref[...]) * (1.0 - lambda_init)
    row = lax.broadcasted_iota(jnp.int32, (BLK, 1), 0)
    y_ref[...] = jnp.where(row >= PAD_ROWS, o, 0.0).astype(y_ref.dtype)


def _attn(zd, rel_bias, bias_tiles, bias_diag, lam_vec, subln_w, nreal, nblk, lambda_init):
    nqb = nblk * BLK // QB
    nq_total = nreal * BLK // QB
    seq = nblk * BLK
    kcol, vcol = HEADS, 2 * HEADS
    y_main = pl.pallas_call(
        functools.partial(_attn_kernel, nqb=nqb, nblk=nblk, lambda_init=lambda_init),
        out_shape=jax.ShapeDtypeStruct((nreal * BLK, WIDTH), BF16),
        grid=(HEADS, nq_total),
        in_specs=[pl.BlockSpec(memory_space=pltpu.SMEM),
                  pl.BlockSpec((QB, HEAD_W), lambda h, i: (i, h)),
                  pl.BlockSpec((seq, HEAD_W), lambda h, i: (i // nqb, kcol + h)),
                  pl.BlockSpec((seq, HEAD_W), lambda h, i: (i // nqb, vcol + h)),
                  pl.BlockSpec((BLK, HEAD_W), lambda h, i: (nreal, kcol + h)),
                  pl.BlockSpec((BLK, HEAD_W), lambda h, i: (nreal, vcol + h)),
                  pl.BlockSpec((None, 2, BLK, BLK), lambda h, i: (h, 0, 0, 0)),
                  pl.BlockSpec((None, QB, QB), lambda h, i: (h, 0, 0)),
                  pl.BlockSpec((4, DIFF_HEAD_DIM), lambda h, i: (0, 0)),
                  pl.BlockSpec((1, HEAD_W), lambda h, i: (0, 0))],
        out_specs=pl.BlockSpec((QB, HEAD_W), lambda h, i: (i, h)),
        scratch_shapes=[pltpu.VMEM((2, nqb + 1, QB, QB), F32),
                        pltpu.VMEM((nqb, HEAD_W, QB), BF16),
                        pltpu.VMEM((2, 8, QB), F32),
                        pltpu.VMEM((2, 8, QB), F32),
                        pltpu.VMEM((2, HEAD_W, QB), F32)],
        compiler_params=_cparams(("arbitrary", "arbitrary")),
        name="diff_attn",
    )(rel_bias, zd, zd, zd, zd, zd, bias_tiles, bias_diag, lam_vec, subln_w)
    y_meta = pl.pallas_call(
        functools.partial(_attn_meta_kernel, lambda_init=lambda_init),
        out_shape=jax.ShapeDtypeStruct((BLK, WIDTH), BF16),
        grid=(HEADS,),
        in_specs=[pl.BlockSpec((BLK, HEAD_W), lambda h: (nreal, h)),
                  pl.BlockSpec((BLK, HEAD_W), lambda h: (nreal, kcol + h)),
                  pl.BlockSpec((BLK, HEAD_W), lambda h: (nreal, vcol + h)),
                  pl.BlockSpec((None, 2, BLK, BLK), lambda h: (h, 0, 0, 0)),
                  pl.BlockSpec((4, DIFF_HEAD_DIM), lambda h: (0, 0)),
                  pl.BlockSpec((1, HEAD_W), lambda h: (0, 0))],
        out_specs=pl.BlockSpec((BLK, HEAD_W), lambda h: (0, h)),
        compiler_params=_cparams(("arbitrary",)),
        name="diff_attn_meta",
    )(zd, zd, zd, bias_tiles, lam_vec, subln_w)
    return jnp.concatenate([y_main, y_meta], axis=0)


def _largest_tile(rows, cap, align=16):
    best = align
    for t in range(align, cap + 1, align):
        if rows % t == 0:
            best = t
    return best


def _forward(x, meta_tokens, rel_bias, hgrn_lower_bounds, norm_mix_pre, norm_mix_post, norm_ffn_pre,
             norm_ffn_post, w_in, lru_conv_w, lru_conv_b, lru_w_a, lru_b_a, lru_w_x, lru_b_x, lru_lambda,
             pool_w, pool_scale, hgrn_norm, diff_lambda, diff_subln, w_branch, w_out, ffn_w_gu, ffn_w_down):
    bsz, seq, _ = x.shape
    nblk = seq // BLK
    nreal = bsz * nblk
    rows = (nreal + 1) * BLK
    rows_real = nreal * BLK
    tm_big = _largest_tile(rows, 832)
    tm_epi = _largest_tile(rows, 640)
    tm_down = _largest_tile(rows, 320)
    tm_last = _largest_tile(rows_real, 256)

    def vec(a):
        return a.reshape(1, -1)

    lbs, bias_tiles, bias_diag = _prologue(hgrn_lower_bounds, rel_bias)
    h, hn = _embed(x.reshape(rows_real, D_MODEL), meta_tokens, vec(norm_mix_pre[0]), nreal)

    for layer in range(DEPTH):
        lambda_init = 0.8 - 0.6 * math.exp(-0.3 * layer)
        z = _mixer_in_proj(hn, w_in, layer, 0, 4 * WIDTH, F32, tm_big, 1024, 1024)
        z_b = _mixer_in_proj(hn, w_in, layer, 4 * WIDTH, 3 * WIDTH, F32, tm_big, 3 * WIDTH, WIDTH)
        zd = _mixer_in_proj(hn, w_in, layer, 7 * WIDTH, 3 * WIDTH, BF16, tm_big, 3 * WIDTH, WIDTH)
        y_a, y_b = _lru_pool(z, lru_conv_w[layer], vec(lru_conv_b[layer]), lru_w_a[layer], vec(lru_b_a[layer]),
                             lru_w_x[layer], vec(lru_b_x[layer]), vec(lru_lambda[layer]),
                             pool_w[layer], vec(pool_scale[layer]), nreal, nblk)
        y_c = _hgrn(z, z_b, lbs[layer:layer + 1], vec(hgrn_norm[layer]), nreal, nblk)
        y_d = _attn(zd, rel_bias, bias_tiles, bias_diag, diff_lambda[layer], vec(diff_subln[layer]),
                    nreal, nblk, lambda_init)
        merged = _gate_merge(hn, w_in, (y_a, y_b, y_c, y_d), w_branch, layer, tm_big, 256)
        h, hn = _out_proj(merged, _cast_layer_bf16(w_out, layer, 512), h, vec(norm_mix_post[layer]),
                          vec(norm_ffn_pre[layer]), tm_epi)
        a = _swiglu_up(hn, ffn_w_gu, layer, tm_big, 512)
        last = layer == DEPTH - 1
        w_next = vec(norm_mix_pre[layer + 1]) if not last else vec(norm_mix_pre[layer])
        h, hn = _down_proj(a, _cast_layer_bf16(ffn_w_down, layer, 512), h, vec(norm_ffn_post[layer]), w_next,
                           rows_real if last else rows, tm_last if last else tm_down, not last)
    return h.reshape(bsz, seq, D_MODEL)


def kernel(x, meta_tokens, rel_bias, hgrn_lower_bounds, norm_mix_pre, norm_mix_post, norm_ffn_pre, norm_ffn_post, w_in, lru_conv_w, lru_conv_b, lru_w_a, lru_b_a, lru_w_x, lru_b_x, lru_lambda, pool_w, pool_scale, hgrn_norm, diff_lambda, diff_subln, w_branch, w_out, ffn_w_gu, ffn_w_down):
    return _forward(x, meta_tokens, rel_bias, hgrn_lower_bounds, norm_mix_pre, norm_mix_post, norm_ffn_pre,
                    norm_ffn_post, w_in, lru_conv_w, lru_conv_b, lru_w_a, lru_b_a, lru_w_x, lru_b_x, lru_lambda,
                    pool_w, pool_scale, hgrn_norm, diff_lambda, diff_subln, w_branch, w_out, ffn_w_gu, ffn_w_down)
```

```python
import functools
import math

import numpy as np
import jax
import jax.numpy as jnp
from jax import lax
from jax.experimental import pallas as pl
from jax.experimental.pallas import tpu as pltpu

F32 = jnp.float32
BF16 = jnp.bfloat16

D_MODEL = 2048
SEQ = 2048
DEPTH = 2
N_META = 16
BLK = 128
PAD_ROWS = BLK - N_META
QB = 512
NSUB = QB // BLK
WIDTH = 512
HEADS = 4
HEAD_W = 128
CHUNK = 64
SUB = 16
LRU_C = 8.0
POOL_WINDOWS = (2, 4, 8, 16)
DIFF_HEAD_DIM = 64
REL_BUCKETS = 32
REL_MAX_DIST = 128
FFN_HIDDEN = 5632
MIX_COLS = 10 * WIDTH
NEG = -1e30
EPS = 1e-6
LOG2E = math.log2(math.e)
VMEM_LIMIT = 56 * 1024 * 1024


def _cparams(sem):
    return pltpu.CompilerParams(dimension_semantics=sem, vmem_limit_bytes=VMEM_LIMIT)


def _rms(x, w):
    return x * lax.rsqrt(jnp.mean(x * x, axis=-1, keepdims=True) + EPS) * w


def _log_sigmoid(z):
    return -(jnp.maximum(-z, 0.0) + jnp.log(1.0 + jnp.exp(-jnp.abs(z))))


def _sigmoid(z):
    return 1.0 / (1.0 + jnp.exp(-z))


def _gelu_tanh(x):
    c = math.sqrt(2.0 / math.pi)
    return 0.5 * x * (1.0 + jnp.tanh(c * (x + 0.044715 * (x * x * x))))


def _shift_rows(x, s, fill, row):
    return jnp.where(row >= s, pltpu.roll(x, s, axis=0), fill)


def _bucket_tiles():
    r = np.arange(BLK)[None, :]
    c = np.arange(BLK)[:, None]
    max_exact = REL_BUCKETS // 2

    def bucket(n):
        nf = np.maximum(n, 1).astype(np.float32)
        large = max_exact + (np.log(nf / np.float32(max_exact)) / np.float32(math.log(REL_MAX_DIST / max_exact))
                             * np.float32(REL_BUCKETS - max_exact)).astype(np.int32)
        large = np.minimum(large, REL_BUCKETS - 1)
        return np.where(n < max_exact, n, large).astype(np.int32)

    d0 = r - c
    t0 = np.where(d0 >= 0, bucket(np.maximum(d0, 0)), -1)
    t1 = bucket(BLK + r - c)
    return np.stack([t0, t1]).astype(np.int32)


def _prologue_kernel(lbraw_ref, relb_ref, idx_ref, lb_ref, bias_ref, diag_ref):
    raw = lbraw_ref[...]
    mx = jnp.max(raw, axis=0, keepdims=True)
    e = jnp.exp(raw - mx)
    sm = e / jnp.sum(e, axis=0, keepdims=True)
    cum = sm[0:1]
    lb_ref[0:1, :] = cum - sm[0:1]
    for l in range(1, DEPTH):
        cum = cum + sm[l:l + 1]
        lb_ref[l:l + 1, :] = cum - sm[0:1]
    for t in range(2):
        idx = idx_ref[t]
        for hd in range(HEADS):
            acc = jnp.zeros((BLK, BLK), F32)
            for bk in range(REL_BUCKETS):
                acc = jnp.where(idx == bk, relb_ref[bk, hd], acc)
            bias_ref[hd, t] = jnp.where(idx < 0, NEG, acc)
    for hd in range(HEADS):
        far = relb_ref[REL_BUCKETS - 1, hd]
        for kb in range(NSUB):
            for qb in range(NSUB):
                delta = qb - kb
                if delta == 0:
                    blk = bias_ref[hd, 0]
                elif delta == 1:
                    blk = bias_ref[hd, 1]
                else:
                    blk = jnp.full((BLK, BLK), far if delta > 1 else NEG, F32)
                diag_ref[hd, kb * BLK:(kb + 1) * BLK, qb * BLK:(qb + 1) * BLK] = blk * LOG2E


def _prologue(hgrn_lower_bounds, rel_bias):
    idx = jnp.asarray(_bucket_tiles())
    vmem = pl.BlockSpec(memory_space=pltpu.VMEM)
    return pl.pallas_call(
        _prologue_kernel,
        out_shape=(jax.ShapeDtypeStruct((DEPTH, WIDTH), F32),
                   jax.ShapeDtypeStruct((HEADS, 2, BLK, BLK), F32),
                   jax.ShapeDtypeStruct((HEADS, QB, QB), F32)),
        in_specs=[vmem, pl.BlockSpec(memory_space=pltpu.SMEM), vmem],
        out_specs=(vmem, vmem, vmem),
        name="prologue",
    )(hgrn_lower_bounds, rel_bias, idx)


EMBED_ROWS = 512


def _embed_kernel(x_ref, meta_ref, w_ref, h_ref, hn_ref, *, nfull):
    i = pl.program_id(0)

    @pl.when(i < nfull)
    def _():
        h_ref[...] = x_ref[...]

    @pl.when(i == nfull)
    def _():
        h_ref[...] = jnp.zeros_like(h_ref)
        h_ref[PAD_ROWS:BLK, :] = meta_ref[...]

    hn_ref[...] = _rms(h_ref[...], w_ref[...]).astype(BF16)


def _embed(x2d, meta, w_pre, nreal):
    rows = (nreal + 1) * BLK
    nfull = nreal * BLK // EMBED_ROWS
    return pl.pallas_call(
        functools.partial(_embed_kernel, nfull=nfull),
        out_shape=(jax.ShapeDtypeStruct((rows, D_MODEL), F32),
                   jax.ShapeDtypeStruct((rows, D_MODEL), BF16)),
        grid=(nfull + 1,),
        in_specs=[pl.BlockSpec((EMBED_ROWS, D_MODEL), lambda i: (jnp.minimum(i, nfull - 1), 0)),
                  pl.BlockSpec((N_META, D_MODEL), lambda i: (0, 0)),
                  pl.BlockSpec((1, D_MODEL), lambda i: (0, 0))],
        out_specs=(pl.BlockSpec((EMBED_ROWS, D_MODEL), lambda i: (i, 0)),
                   pl.BlockSpec((EMBED_ROWS, D_MODEL), lambda i: (i, 0))),
        compiler_params=_cparams(("arbitrary",)),
        name="embed",
    )(x2d, meta, w_pre)


def _cast_tiles_once(pairs):
    @pl.when(pl.program_id(1) == 0)
    def _():
        for src, dst in pairs:
            dst[...] = src[...].astype(BF16)


def _matmul_kernel(x_ref, *rest):
    *w_refs, o_ref, wbf = rest
    wblk = w_refs[0].shape[-1]
    _cast_tiles_once([(w, wbf.at[:, j * wblk:(j + 1) * wblk]) for j, w in enumerate(w_refs)])
    o_ref[...] = jnp.dot(x_ref[...], wbf[...], preferred_element_type=F32).astype(o_ref.dtype)


def _mixer_in_proj(hn, w_in, layer, col0, ncols, out_dtype, tm, tn, wblk):
    rows, k = hn.shape
    nw = tn // wblk
    w_specs = [pl.BlockSpec((None, k, wblk), functools.partial(
        lambda n, m, j: (layer, 0, col0 // wblk + n * nw + j), j=j)) for j in range(nw)]
    return pl.pallas_call(
        _matmul_kernel,
        out_shape=jax.ShapeDtypeStruct((rows, ncols), out_dtype),
        grid=(ncols // tn, rows // tm),
        in_specs=[pl.BlockSpec((tm, k), lambda n, m: (m, 0))] + w_specs,
        out_specs=pl.BlockSpec((tm, tn), lambda n, m: (m, n)),
        scratch_shapes=[pltpu.VMEM((k, tn), BF16)],
        compiler_params=_cparams(("arbitrary", "arbitrary")),
        name="mixer_in_proj",
    )(hn, *([w_in] * nw))


def _residual_epilogue(acc, h_ref, wpost_ref, wnext_ref, hnew_ref, hn_ref):
    h_new = h_ref[...] + _rms(acc, wpost_ref[...])
    hnew_ref[...] = h_new
    if hn_ref is not None:
        hn_ref[...] = _rms(h_new, wnext_ref[...]).astype(BF16)


EPI_ROWS = 160


def _row_subtiles(tm):
    sub = next(s for s in (EPI_ROWS, 128, 64, 32, 16) if tm % s == 0)
    return [slice(r, r + sub) for r in range(0, tm, sub)]


def _gate_merge_kernel(hn_ref, g0, g1, g2, g3, y0, y1, y2, y3, wb_ref, wnext_ref, o_ref, wnext_bf_ref, gbf, wbbf):
    _cast_tiles_once([(g, gbf.at[k]) for k, g in enumerate((g0, g1, g2, g3))] + [(wb_ref, wbbf)]
                     + [(wnext_ref, wnext_bf_ref)])
    hn = hn_ref[...]
    acc = None
    for k, y_ref in enumerate((y0, y1, y2, y3)):
        gate = _sigmoid(jnp.dot(hn, gbf[k], preferred_element_type=F32))
        proj = jnp.dot(y_ref[...], wbbf[k], preferred_element_type=F32)
        acc = gate * proj if acc is None else acc + gate * proj
    o_ref[...] = acc.astype(o_ref.dtype)


def _next_weight_specs(w_next, layer, n_tiles):
    _, r, c = w_next.shape
    slab = r // n_tiles
    return (pl.BlockSpec((None, slab, c), lambda n, m: (layer, n, 0)),
            jax.ShapeDtypeStruct((r, c), BF16),
            pl.BlockSpec((slab, c), lambda n, m: (n, 0)))


def _gate_merge(hn, w_in, ys, w_branch, w_out, layer, tm, tn):
    rows, k = hn.shape
    gate_specs = [
        pl.BlockSpec((None, k, tn), functools.partial(
            lambda n, m, base: (layer, 0, base + n), base=(MIX_COLS + br * D_MODEL) // tn))
        for br in range(4)]
    y_specs = [pl.BlockSpec((tm, WIDTH), lambda n, m: (m, 0)) for _ in range(4)]
    wn_in, wn_shape, wn_out = _next_weight_specs(w_out, layer, D_MODEL // tn)
    return pl.pallas_call(
        _gate_merge_kernel,
        out_shape=(jax.ShapeDtypeStruct((rows, D_MODEL), BF16), wn_shape),
        grid=(D_MODEL // tn, rows // tm),
        in_specs=[pl.BlockSpec((tm, k), lambda n, m: (m, 0))] + gate_specs + y_specs
                 + [pl.BlockSpec((None, 4, WIDTH, tn), lambda n, m: (layer, 0, 0, n)), wn_in],
        out_specs=(pl.BlockSpec((tm, tn), lambda n, m: (m, n)), wn_out),
        scratch_shapes=[pltpu.VMEM((4, k, tn), BF16), pltpu.VMEM((4, WIDTH, tn), BF16)],
        compiler_params=_cparams(("arbitrary", "arbitrary")),
        name="gate_merge",
    )(hn, w_in, w_in, w_in, w_in, *ys, w_branch, w_out)


def _out_proj_kernel(x_ref, w_ref, h_ref, wpost_ref, wnext_ref, hnew_ref, hn_ref):
    for rs in _row_subtiles(x_ref.shape[0]):
        acc = jnp.dot(x_ref[rs, :], w_ref[...], preferred_element_type=F32)
        _residual_epilogue(acc, h_ref.at[rs, :], wpost_ref, wnext_ref, hnew_ref.at[rs, :], hn_ref.at[rs, :])


def _out_proj(merged, w_out, h, w_post, w_next, tm):
    rows = h.shape[0]
    row_spec = pl.BlockSpec((tm, D_MODEL), lambda m: (m, 0))
    return pl.pallas_call(
        _out_proj_kernel,
        out_shape=(jax.ShapeDtypeStruct((rows, D_MODEL), F32),
                   jax.ShapeDtypeStruct((rows, D_MODEL), BF16)),
        grid=(rows // tm,),
        in_specs=[row_spec, _resident((D_MODEL, D_MODEL), (0, 0)), row_spec,
                  _pspec((1, D_MODEL)), _pspec((1, D_MODEL))],
        out_specs=(row_spec, row_spec),
        compiler_params=_cparams(("arbitrary",)),
        name="out_proj",
    )(merged, w_out, h, w_post, w_next)


def _swiglu_up_kernel(x_ref, wg_ref, wu_ref, wnext_ref, o_ref, wnext_bf_ref, wgbf, wubf):
    _cast_tiles_once([(wg_ref, wgbf), (wu_ref, wubf), (wnext_ref, wnext_bf_ref)])
    x = x_ref[...]
    g = jnp.dot(x, wgbf[...], preferred_element_type=F32)
    u = jnp.dot(x, wubf[...], preferred_element_type=F32)
    o_ref[...] = (g * _sigmoid(g) * u).astype(o_ref.dtype)


def _swiglu_up(hn, w_gu, w_down, layer, tm, tn):
    rows, k = hn.shape
    nt = FFN_HIDDEN // tn
    wn_in, wn_shape, wn_out = _next_weight_specs(w_down, layer, nt)
    return pl.pallas_call(
        _swiglu_up_kernel,
        out_shape=(jax.ShapeDtypeStruct((rows, FFN_HIDDEN), BF16), wn_shape),
        grid=(nt, rows // tm),
        in_specs=[pl.BlockSpec((tm, k), lambda n, m: (m, 0)),
                  pl.BlockSpec((None, k, tn), lambda n, m: (layer, 0, n)),
                  pl.BlockSpec((None, k, tn), lambda n, m: (layer, 0, nt + n)), wn_in],
        out_specs=(pl.BlockSpec((tm, tn), lambda n, m: (m, n)), wn_out),
        scratch_shapes=[pltpu.VMEM((k, tn), BF16), pltpu.VMEM((k, tn), BF16)],
        compiler_params=_cparams(("arbitrary", "arbitrary")),
        name="swiglu_up",
    )(hn, w_gu, w_gu, w_down)


def _down_proj_kernel(a_ref, w_ref, h_ref, wpost_ref, wnext_ref, hnew_ref, hn_ref=None):
    for rs in _row_subtiles(a_ref.shape[0]):
        acc = jnp.dot(a_ref[rs, :], w_ref[...], preferred_element_type=F32)
        _residual_epilogue(acc, h_ref.at[rs, :], wpost_ref, wnext_ref, hnew_ref.at[rs, :],
                           None if hn_ref is None else hn_ref.at[rs, :])


def _down_proj(a, w_down, h, w_post, w_next, rows_out, tm, emit_hn):
    row_spec = lambda w: pl.BlockSpec((tm, w), lambda m: (m, 0))
    out_shape = [jax.ShapeDtypeStruct((rows_out, D_MODEL), F32)]
    out_specs = [row_spec(D_MODEL)]
    if emit_hn:
        out_shape.append(jax.ShapeDtypeStruct((rows_out, D_MODEL), BF16))
        out_specs.append(row_spec(D_MODEL))
    res = pl.pallas_call(
        _down_proj_kernel,
        out_shape=tuple(out_shape),
        grid=(rows_out // tm,),
        in_specs=[row_spec(FFN_HIDDEN), _resident((FFN_HIDDEN, D_MODEL), (0, 0)), row_spec(D_MODEL),
                  _pspec((1, D_MODEL)), _pspec((1, D_MODEL))],
        out_specs=tuple(out_specs),
        compiler_params=_cparams(("arbitrary",)),
        name="down_proj",
    )(a, w_down, h, w_post, w_next)
    return res if emit_hn else (res[0], None)


def _row_block(i, nreal):
    return (i + nreal) % (nreal + 1)


def _zspec(col_block, nreal):
    return pl.BlockSpec((BLK, WIDTH), lambda i: (_row_block(i, nreal), col_block))


def _pspec(shape):
    nd = len(shape)
    return pl.BlockSpec(shape, lambda *_: (0,) * nd)


def _resident(shape, index):
    return pl.BlockSpec(shape, lambda *_: index, pipeline_mode=pl.Buffered(1))


def _lru_kernel(u_ref, gate_ref, cw_ref, cb_ref, wa_ref, ba_ref, wx_ref, bx_ref, lam_ref, y_ref,
                ubuf, hst, hist_meta, h_meta, *, nblk):
    i = pl.program_id(0)
    is_meta = i == 0

    @pl.when(is_meta)
    def _():
        ubuf[0:8, :] = jnp.zeros((8, WIDTH), F32)
        hst[...] = jnp.zeros_like(hst)

    @pl.when(jnp.logical_and(i >= 1, (i - 1) % nblk == 0))
    def _():
        ubuf[0:8, :] = hist_meta[...]
        hst[...] = h_meta[...]

    u = u_ref[...]
    ubuf[8:8 + BLK, :] = u
    cw = cw_ref[...]
    xc = (cb_ref[...] + cw[3:4] * u + cw[2:3] * ubuf[7:7 + BLK, :]
          + cw[1:2] * ubuf[6:6 + BLK, :] + cw[0:1] * ubuf[5:5 + BLK, :])
    xb = xc.astype(BF16)
    ra, ia = [], []
    for hd in range(HEADS):
        sl = slice(hd * HEAD_W, (hd + 1) * HEAD_W)
        ra.append(jnp.dot(xb[:, sl], wa_ref[hd].astype(BF16), preferred_element_type=F32))
        ia.append(jnp.dot(xb[:, sl], wx_ref[hd].astype(BF16), preferred_element_type=F32))
    r = _sigmoid(jnp.concatenate(ra, axis=1) + ba_ref[...])
    ig = _sigmoid(jnp.concatenate(ia, axis=1) + bx_ref[...])
    lam = lam_ref[...]
    softplus_neg_lam = jnp.maximum(-lam, 0.0) + jnp.log1p(jnp.exp(-jnp.abs(lam)))
    log_a = -LRU_C * r * softplus_neg_lam
    a = jnp.exp(log_a)
    bb = jnp.sqrt(-jnp.tanh(log_a) * (a * a + 1.0)) * (ig * xc)
    row = lax.broadcasted_iota(jnp.int32, (BLK, 1), 0)
    bb = jnp.where(row >= PAD_ROWS * is_meta.astype(jnp.int32), bb, 0.0)

    acum, bcum = a, bb
    s = 1
    while s < BLK:
        a_sh = _shift_rows(acum, s, 1.0, row)
        b_sh = _shift_rows(bcum, s, 0.0, row)
        bcum = acum * b_sh + bcum
        acum = acum * a_sh
        s *= 2
    h = acum * hst[0:1, :] + bcum
    y_ref[...] = (h * _gelu_tanh(gate_ref[...])).astype(y_ref.dtype)

    hist = u[BLK - 8:BLK, :]
    hlast = jnp.broadcast_to(h[BLK - 1:BLK, :], (8, WIDTH))
    ubuf[0:8, :] = hist
    hst[...] = hlast

    @pl.when(is_meta)
    def _():
        hist_meta[...] = hist
        h_meta[...] = hlast


POOL_HIST = 16


def _pool_kernel(u_ref, pw_ref, ps_ref, y_ref, ubuf, hist_meta, *, nblk):
    i = pl.program_id(0)
    is_meta = i == 0

    @pl.when(is_meta)
    def _():
        ubuf[0:POOL_HIST, :] = jnp.zeros((POOL_HIST, WIDTH), F32)

    @pl.when(jnp.logical_and(i >= 1, (i - 1) % nblk == 0))
    def _():
        ubuf[0:POOL_HIST, :] = hist_meta[...]

    u = u_ref[...]
    ubuf[POOL_HIST:POOL_HIST + BLK, :] = u
    row = lax.broadcasted_iota(jnp.int32, (BLK, 1), 0)
    meta_i = is_meta.astype(jnp.int32)
    pos1 = row + 1 - PAD_ROWS * meta_i + 2 * POOL_HIST * (1 - meta_i)
    outs = []
    for g, win in enumerate(POOL_WINDOWS):
        sl = slice(g * HEAD_W, (g + 1) * HEAD_W)
        acc = u[:, sl]
        for d in range(1, win):
            acc = acc + ubuf[POOL_HIST - d:POOL_HIST - d + BLK, sl]
        count = jnp.clip(pos1, 1, win).astype(F32)
        pooled = acc / count - u[:, sl]
        outs.append(jnp.dot(pooled.astype(BF16), pw_ref[g].astype(BF16), preferred_element_type=F32))
    y_ref[...] = (jnp.concatenate(outs, axis=1) * ps_ref[...]).astype(y_ref.dtype)

    hist = u[BLK - POOL_HIST:BLK, :]
    ubuf[0:POOL_HIST, :] = hist

    @pl.when(is_meta)
    def _():
        hist_meta[...] = hist


def _lru_pool_kernel(u_ref, gate_ref, cw_ref, cb_ref, wa_ref, ba_ref, wx_ref, bx_ref, lam_ref,
                     pu_ref, pw_ref, ps_ref, ya_ref, yb_ref,
                     ubuf, hst, hist_meta, h_meta, pbuf, phist_meta, *, nblk):
    _lru_kernel(u_ref, gate_ref, cw_ref, cb_ref, wa_ref, ba_ref, wx_ref, bx_ref, lam_ref, ya_ref,
                ubuf, hst, hist_meta, h_meta, nblk=nblk)
    _pool_kernel(pu_ref, pw_ref, ps_ref, yb_ref, pbuf, phist_meta, nblk=nblk)


def _lru_pool(z, cw, cb, wa, ba, wx, bx, lam, pw, ps, nreal, nblk):
    rows = z.shape[0]
    out_spec = pl.BlockSpec((BLK, WIDTH), lambda i: (_row_block(i, nreal), 0))
    return pl.pallas_call(
        functools.partial(_lru_pool_kernel, nblk=nblk),
        out_shape=(jax.ShapeDtypeStruct((rows, WIDTH), BF16), jax.ShapeDtypeStruct((rows, WIDTH), BF16)),
        grid=(nreal + 1,),
        in_specs=[_zspec(0, nreal), _zspec(1, nreal),
                  _pspec((4, WIDTH)), _pspec((1, WIDTH)),
                  _pspec((HEADS, HEAD_W, HEAD_W)), _pspec((1, WIDTH)),
                  _pspec((HEADS, HEAD_W, HEAD_W)), _pspec((1, WIDTH)), _pspec((1, WIDTH)),
                  _zspec(2, nreal), _pspec((4, HEAD_W, HEAD_W)), _pspec((1, WIDTH))],
        out_specs=(out_spec, out_spec),
        scratch_shapes=[pltpu.VMEM((8 + BLK, WIDTH), F32), pltpu.VMEM((8, WIDTH), F32),
                        pltpu.VMEM((8, WIDTH), F32), pltpu.VMEM((8, WIDTH), F32),
                        pltpu.VMEM((POOL_HIST + BLK, WIDTH), F32), pltpu.VMEM((POOL_HIST, WIDTH), F32)],
        compiler_params=_cparams(("arbitrary",)),
        name="lru_pool",
    )(z, z, cw, cb, wa, ba, wx, bx, lam, z, pw, ps)


def _hgrn_chunk(q, z, v, lbh, state_t, valid, ones_bf):
    ls = _log_sigmoid(z)
    x1 = jnp.log(lbh)
    x2 = jnp.log1p(-lbh) + ls
    mx = jnp.maximum(x1, x2)
    g = mx + jnp.log(1.0 + jnp.exp(-jnp.abs(x1 - x2)))
    k = (1.0 - lbh) * _sigmoid(-z)
    if valid is not None:
        g = jnp.where(valid, g, 0.0)
    row = lax.broadcasted_iota(jnp.int32, (CHUNK, 1), 0)
    b = g * LOG2E
    s = 1
    while s < CHUNK:
        b = b + _shift_rows(b, s, 0.0, row)
        s *= 2
    b_last = b[CHUNK - 1:CHUNK, :]

    qe = (q * jnp.exp2(b)).astype(BF16)
    o = lax.dot_general(qe, state_t.astype(BF16), (((1,), (1,)), ((), ())), preferred_element_type=F32)

    col = lax.broadcasted_iota(jnp.int32, (SUB, CHUNK), 1)
    rsub = lax.broadcasted_iota(jnp.int32, (SUB, CHUNK), 0)
    lane = lax.broadcasted_iota(jnp.int32, (SUB, HEAD_W), 1)
    s_rows = []
    for blk in range(CHUNK // SUB):
        lo = blk * SUB
        bi = b[lo:lo + SUB, :]
        qi = q[lo:lo + SUB, :]
        ki = k[lo:lo + SUB, :]
        parts = []
        for sr in range(SUB):
            t0 = 0 if sr < SUB // 2 else SUB // 2
            e = jnp.exp2(jnp.minimum(bi[t0:, :] - bi[sr:sr + 1, :], 0.0))
            parts.append(qi[t0:, :] * e * ki[sr:sr + 1, :])
        m3 = jnp.concatenate(parts, axis=0).astype(BF16)
        red = jnp.dot(m3, ones_bf, preferred_element_type=F32)
        halves = [jnp.zeros((SUB // 2, HEAD_W), F32), jnp.zeros((SUB // 2, HEAD_W), F32)]
        r0 = 0
        for sr in range(SUB):
            for hf in range(0 if sr < SUB // 2 else 1, 2):
                halves[hf] = jnp.where(lane[:SUB // 2] == lo + sr, red[r0:r0 + SUB // 2, :], halves[hf])
                r0 += SUB // 2
        diag = jnp.concatenate(halves, axis=0)[:, :CHUNK]
        s_blk = jnp.where(jnp.logical_and(col >= lo, col - lo <= rsub), diag, 0.0)
        if blk > 0:
            b0 = b[lo - 1:lo, :]
            kt = (k * jnp.exp2(jnp.minimum(b0 - b, 0.0))).astype(BF16)
            qd = (qi * jnp.exp2(bi - b0)).astype(BF16)
            off = lax.dot_general(qd, kt, (((1,), (1,)), ((), ())), preferred_element_type=F32)
            s_blk = jnp.where(col < lo, off, s_blk)
        s_rows.append(s_blk)
    scores = jnp.concatenate(s_rows, axis=0).astype(BF16)
    vb = v.astype(BF16)
    o = o + jnp.dot(scores, vb, preferred_element_type=F32)

    kd = (k * jnp.exp2(b_last - b)).astype(BF16)
    upd = lax.dot_general(vb, kd, (((0,), (0,)), ((), ())), preferred_element_type=F32)
    new_state_t = state_t * jnp.exp2(b_last) + upd
    return o, new_state_t


def _hgrn_kernel(q_ref, f_ref, v_ref, og_ref, lb_ref, nw_ref, y_ref, state, state_meta, *, nblk):
    i = pl.program_id(0)
    is_meta = i == 0

    @pl.when(is_meta)
    def _():
        state[...] = jnp.zeros_like(state)

    @pl.when(jnp.logical_and(i >= 1, (i - 1) % nblk == 0))
    def _():
        state[...] = state_meta[...]

    ones_bf = jnp.ones((HEAD_W, HEAD_W), BF16)
    nw = nw_ref[...]
    for hd in range(HEADS):
        sl = slice(hd * HEAD_W, (hd + 1) * HEAD_W)
        lbh = lb_ref[:, sl]
        st = state[hd]
        for c in range(BLK // CHUNK):
            rs = slice(c * CHUNK, (c + 1) * CHUNK)
            rowg = lax.broadcasted_iota(jnp.int32, (CHUNK, 1), 0) + c * CHUNK
            valid = rowg >= PAD_ROWS * is_meta.astype(jnp.int32)
            o, st = _hgrn_chunk(q_ref[rs, sl], f_ref[rs, sl], v_ref[rs, sl], lbh, st, valid, ones_bf)
            og = og_ref[rs, sl]
            y_ref[rs, sl] = (_rms(o, nw) * (og * _sigmoid(og))).astype(y_ref.dtype)
        state[hd] = st

    @pl.when(is_meta)
    def _():
        state_meta[...] = state[...]


def _hgrn(z_a, z_b, lb, nw, nreal, nblk):
    rows = z_a.shape[0]
    return pl.pallas_call(
        functools.partial(_hgrn_kernel, nblk=nblk),
        out_shape=jax.ShapeDtypeStruct((rows, WIDTH), BF16),
        grid=(nreal + 1,),
        in_specs=[_zspec(3, nreal), _zspec(0, nreal), _zspec(1, nreal), _zspec(2, nreal),
                  _pspec((1, WIDTH)), _pspec((1, HEAD_W))],
        out_specs=pl.BlockSpec((BLK, WIDTH), lambda i: (_row_block(i, nreal), 0)),
        scratch_shapes=[pltpu.VMEM((HEADS, HEAD_W, HEAD_W), F32), pltpu.VMEM((HEADS, HEAD_W, HEAD_W), F32)],
        compiler_params=_cparams(("arbitrary",)),
        name="hgrn2",
    )(z_a, z_b, z_b, z_b, lb, nw)


def _attn_lambda(lamv_ref, lambda_init):
    lv = lamv_ref[...]
    return (jnp.exp(jnp.sum(lv[0:1] * lv[1:2], axis=-1, keepdims=True))
            - jnp.exp(jnp.sum(lv[2:3] * lv[3:4], axis=-1, keepdims=True)) + lambda_init)


def _scaled_q(q_ref):
    qf = (q_ref[...].astype(F32) * (DIFF_HEAD_DIM ** -0.5)).astype(BF16)
    return qf[:, :DIFF_HEAD_DIM], qf[:, DIFF_HEAD_DIM:]


def _fold8(x, op):
    r, c = x.shape
    return op(x.reshape(r // 8, 8, c), axis=0)


def _attn_kernel(relb_ref, q_ref, k_ref, v_ref, km_ref, vm_ref, bias_ref, diag_ref, lamv_ref, sw_ref, y_ref,
                 s_sc, vt_sc, m_sc, l_sc, acc_sc, *, nqb, nblk, lambda_init):
    hd = pl.program_id(0)
    jq = pl.program_id(1) % nqb
    far = relb_ref[REL_BUCKETS - 1, hd] * LOG2E
    t1 = bias_ref[1] * LOG2E
    slot_diag, slot_meta = nqb - 1, nqb

    @pl.when(jq == 0)
    def _():
        for t in range(nblk):
            vt_sc[t // NSUB, :, (t % NSUB) * BLK:(t % NSUB + 1) * BLK] = (
                v_ref[t * BLK:(t + 1) * BLK, :].astype(F32).T.astype(BF16))

    qt = (q_ref[...].astype(F32) * (DIFF_HEAD_DIM ** -0.5)).T
    zero = jnp.zeros((DIFF_HEAD_DIM, QB), F32)
    qtp = (jnp.concatenate([qt[:DIFF_HEAD_DIM], zero], axis=0).astype(BF16),
           jnp.concatenate([zero, qt[DIFF_HEAD_DIM:]], axis=0).astype(BF16))

    def key_chunk(c):
        return k_ref[pl.ds(pl.multiple_of(c * QB, QB), QB), :]

    def score_chunk(k_rows, slot, add_bias, r0=0, c0=0):
        r = k_rows.shape[0]
        for mp in range(2):
            s = add_bias(jnp.dot(k_rows, qtp[mp][:, c0:], preferred_element_type=F32) * LOG2E)
            s_sc[mp, slot, r0:r0 + r, c0:] = s
            m_sc[mp, :, c0:] = jnp.maximum(m_sc[mp, :, c0:], _fold8(s, jnp.max))

    def value_chunk(vt_cols, slot, m8, r0=0, c0=0):
        r = vt_cols.shape[1]
        for mp in range(2):
            p = jnp.exp2(s_sc[mp, slot, r0:r0 + r, c0:].reshape(r // 8, 8, QB - c0) - m8[mp][None, :, c0:])
            l_sc[mp, :, c0:] += jnp.sum(p, axis=0)
            acc_sc[mp, :, c0:] += jnp.dot(vt_cols, p.reshape(r, QB - c0).astype(BF16),
                                         preferred_element_type=F32)

    def diag_tiles(fn):
        for kb in range(NSUB):
            fn(kb, kb * BLK)

    def walk(fn_far, fn_meta, fn_diag):
        def body(c, carry):
            fn_far(c)
            return carry

        lax.fori_loop(0, jq, body, 0)
        fn_meta()
        fn_diag()

    near = t1 - far
    first = jnp.full((1, BLK), jq, jnp.int32) == 0

    def meta_bias(s):
        head = s[:, 0:BLK] + (far + jnp.where(first, near[PAD_ROWS:, :], 0.0))
        return jnp.concatenate([head, s[:, BLK:] + far], axis=1)

    def meta_values(m8):
        v16 = vm_ref[PAD_ROWS:BLK, :]
        for mp in range(2):
            p = jnp.exp2(s_sc[mp, slot_meta, 0:N_META, :].reshape(N_META // 8, 8, QB) - m8[mp][None])
            l_sc[mp] += jnp.sum(p, axis=0)
            acc_sc[mp] += lax.dot_general(v16, p.reshape(N_META, QB).astype(BF16), (((0,), (0,)), ((), ())),
                                          preferred_element_type=F32)

    m_sc[...] = jnp.full_like(m_sc, NEG)
    walk(lambda c: score_chunk(key_chunk(c), c, lambda s: s + far),
         lambda: score_chunk(km_ref[PAD_ROWS:BLK, :], slot_meta, meta_bias),
         lambda: diag_tiles(lambda kb, c0: score_chunk(
             k_ref[pl.ds(pl.multiple_of(jq * QB + c0, BLK), BLK), :], slot_diag,
             lambda s: s + diag_ref[c0:c0 + BLK, c0:], r0=c0, c0=c0)))

    @pl.when(jq >= 1)
    def _():
        for mp in range(2):
            fixed = s_sc[mp, jq - 1, QB - BLK:QB, 0:BLK] + near
            s_sc[mp, jq - 1, QB - BLK:QB, 0:BLK] = fixed
            m_sc[mp, :, 0:BLK] = jnp.maximum(m_sc[mp, :, 0:BLK], _fold8(fixed, jnp.max))

    m8 = [jnp.broadcast_to(jnp.max(m_sc[mp], axis=0, keepdims=True), (8, QB)) for mp in range(2)]
    l_sc[...] = jnp.zeros_like(l_sc)
    acc_sc[...] = jnp.zeros_like(acc_sc)

    walk(lambda c: value_chunk(vt_sc[c], c, m8),
         lambda: meta_values(m8),
         lambda: diag_tiles(lambda kb, c0: value_chunk(
             vt_sc[jq, :, c0:c0 + BLK], slot_diag, m8, r0=c0, c0=c0)))

    lam = _attn_lambda(lamv_ref, lambda_init)
    l0 = jnp.sum(l_sc[0], axis=0, keepdims=True)
    l1 = jnp.sum(l_sc[1], axis=0, keepdims=True)
    ot = acc_sc[0] / l0 - lam * (acc_sc[1] / l1)
    ot = ot * lax.rsqrt(jnp.mean(ot * ot, axis=0, keepdims=True) + EPS)
    y_ref[...] = (ot.T * sw_ref[...] * (1.0 - lambda_init)).astype(y_ref.dtype)


def _attn_meta_kernel(q_ref, k_ref, v_ref, bias_ref, lamv_ref, sw_ref, y_ref, *, lambda_init):
    qs = _scaled_q(q_ref)
    kb = k_ref[...]
    colmask = jnp.where(lax.broadcasted_iota(jnp.int32, (1, BLK), 1) >= PAD_ROWS, 0.0, NEG)
    outs = []
    bias = bias_ref[0].T + colmask
    for mp in range(2):
        s = lax.dot_general(qs[mp], kb[:, mp * DIFF_HEAD_DIM:(mp + 1) * DIFF_HEAD_DIM],
                            (((1,), (1,)), ((), ())), preferred_element_type=F32) + bias
        p = jnp.exp(s - jnp.max(s, axis=-1, keepdims=True))
        acc = jnp.dot(p.astype(BF16), v_ref[...], preferred_element_type=F32)
        outs.append(acc / jnp.sum(p, axis=-1, keepdims=True))
    o = outs[0] - _attn_lambda(lamv_ref, lambda_init) * outs[1]
    o = _rms(o, sw_ref[...]) * (1.0 - lambda_init)
    row = lax.broadcasted_iota(jnp.int32, (BLK, 1), 0)
    y_ref[...] = jnp.where(row >= PAD_ROWS, o, 0.0).astype(y_ref.dtype)


def _attn(zd, rel_bias, bias_tiles, bias_diag, lam_vec, subln_w, nreal, nblk, lambda_init):
    nqb = nblk * BLK // QB
    nq_total = nreal * BLK // QB
    seq = nblk * BLK
    kcol, vcol = HEADS, 2 * HEADS
    y_main = pl.pallas_call(
        functools.partial(_attn_kernel, nqb=nqb, nblk=nblk, lambda_init=lambda_init),
        out_shape=jax.ShapeDtypeStruct((nreal * BLK, WIDTH), BF16),
        grid=(HEADS, nq_total),
        in_specs=[pl.BlockSpec(memory_space=pltpu.SMEM),
                  pl.BlockSpec((QB, HEAD_W), lambda h, i: (i, h)),
                  pl.BlockSpec((seq, HEAD_W), lambda h, i: (i // nqb, kcol + h)),
                  pl.BlockSpec((seq, HEAD_W), lambda h, i: (i // nqb, vcol + h)),
                  pl.BlockSpec((BLK, HEAD_W), lambda h, i: (nreal, kcol + h)),
                  pl.BlockSpec((BLK, HEAD_W), lambda h, i: (nreal, vcol + h)),
                  pl.BlockSpec((None, 2, BLK, BLK), lambda h, i: (h, 0, 0, 0)),
                  pl.BlockSpec((None, QB, QB), lambda h, i: (h, 0, 0)),
                  pl.BlockSpec((4, DIFF_HEAD_DIM), lambda h, i: (0, 0)),
                  pl.BlockSpec((1, HEAD_W), lambda h, i: (0, 0))],
        out_specs=pl.BlockSpec((QB, HEAD_W), lambda h, i: (i, h)),
        scratch_shapes=[pltpu.VMEM((2, nqb + 1, QB, QB), F32),
                        pltpu.VMEM((nqb, HEAD_W, QB), BF16),
                        pltpu.VMEM((2, 8, QB), F32),
                        pltpu.VMEM((2, 8, QB), F32),
                        pltpu.VMEM((2, HEAD_W, QB), F32)],
        compiler_params=_cparams(("arbitrary", "arbitrary")),
        name="diff_attn",
    )(rel_bias, zd, zd, zd, zd, zd, bias_tiles, bias_diag, lam_vec, subln_w)
    y_meta = pl.pallas_call(
        functools.partial(_attn_meta_kernel, lambda_init=lambda_init),
        out_shape=jax.ShapeDtypeStruct((BLK, WIDTH), BF16),
        grid=(HEADS,),
        in_specs=[pl.BlockSpec((BLK, HEAD_W), lambda h: (nreal, h)),
                  pl.BlockSpec((BLK, HEAD_W), lambda h: (nreal, kcol + h)),
                  pl.BlockSpec((BLK, HEAD_W), lambda h: (nreal, vcol + h)),
                  pl.BlockSpec((None, 2, BLK, BLK), lambda h: (h, 0, 0, 0)),
                  pl.BlockSpec((4, DIFF_HEAD_DIM), lambda h: (0, 0)),
                  pl.BlockSpec((1, HEAD_W), lambda h: (0, 0))],
        out_specs=pl.BlockSpec((BLK, HEAD_W), lambda h: (0, h)),
        compiler_params=_cparams(("arbitrary",)),
        name="diff_attn_meta",
    )(zd, zd, zd, bias_tiles, lam_vec, subln_w)
    return jnp.concatenate([y_main, y_meta], axis=0)


def _largest_tile(rows, cap, align=16):
    best = align
    for t in range(align, cap + 1, align):
        if rows % t == 0:
            best = t
    return best


def _forward(x, meta_tokens, rel_bias, hgrn_lower_bounds, norm_mix_pre, norm_mix_post, norm_ffn_pre,
             norm_ffn_post, w_in, lru_conv_w, lru_conv_b, lru_w_a, lru_b_a, lru_w_x, lru_b_x, lru_lambda,
             pool_w, pool_scale, hgrn_norm, diff_lambda, diff_subln, w_branch, w_out, ffn_w_gu, ffn_w_down):
    bsz, seq, _ = x.shape
    nblk = seq // BLK
    nreal = bsz * nblk
    rows = (nreal + 1) * BLK
    rows_real = nreal * BLK
    tm_big = _largest_tile(rows, 832)
    tm_epi = _largest_tile(rows, 640)
    tm_down = _largest_tile(rows, 320)
    tm_last = _largest_tile(rows_real, 256)

    def vec(a):
        return a.reshape(1, -1)

    lbs, bias_tiles, bias_diag = _prologue(hgrn_lower_bounds, rel_bias)
    h, hn = _embed(x.reshape(rows_real, D_MODEL), meta_tokens, vec(norm_mix_pre[0]), nreal)

    for layer in range(DEPTH):
        lambda_init = 0.8 - 0.6 * math.exp(-0.3 * layer)
        z = _mixer_in_proj(hn, w_in, layer, 0, 4 * WIDTH, F32, tm_big, 1024, 1024)
        z_b = _mixer_in_proj(hn, w_in, layer, 4 * WIDTH, 3 * WIDTH, F32, tm_big, 3 * WIDTH, WIDTH)
        zd = _mixer_in_proj(hn, w_in, layer, 7 * WIDTH, 3 * WIDTH, BF16, tm_big, 3 * WIDTH, WIDTH)
        y_a, y_b = _lru_pool(z, lru_conv_w[layer], vec(lru_conv_b[layer]), lru_w_a[layer], vec(lru_b_a[layer]),
                             lru_w_x[layer], vec(lru_b_x[layer]), vec(lru_lambda[layer]),
                             pool_w[layer], vec(pool_scale[layer]), nreal, nblk)
        y_c = _hgrn(z, z_b, lbs[layer:layer + 1], vec(hgrn_norm[layer]), nreal, nblk)
        y_d = _attn(zd, rel_bias, bias_tiles, bias_diag, diff_lambda[layer], vec(diff_subln[layer]),
                    nreal, nblk, lambda_init)
        merged, w_out_bf = _gate_merge(hn, w_in, (y_a, y_b, y_c, y_d), w_branch, w_out, layer, tm_big, 256)
        h, hn = _out_proj(merged, w_out_bf, h, vec(norm_mix_post[layer]), vec(norm_ffn_pre[layer]), tm_epi)
        a, w_down_bf = _swiglu_up(hn, ffn_w_gu, ffn_w_down, layer, tm_big, 512)
        last = layer == DEPTH - 1
        w_next = vec(norm_mix_pre[layer + 1]) if not last else vec(norm_mix_pre[layer])
        h, hn = _down_proj(a, w_down_bf, h, vec(norm_ffn_post[layer]), w_next,
                           rows_real if last else rows, tm_last if last else tm_down, not last)
    return h.reshape(bsz, seq, D_MODEL)


def kernel(x, meta_tokens, rel_bias, hgrn_lower_bounds, norm_mix_pre, norm_mix_post, norm_ffn_pre, norm_ffn_post, w_in, lru_conv_w, lru_conv_b, lru_w_a, lru_b_a, lru_w_x, lru_b_x, lru_lambda, pool_w, pool_scale, hgrn_norm, diff_lambda, diff_subln, w_branch, w_out, ffn_w_gu, ffn_w_down):
    return _forward(x, meta_tokens, rel_bias, hgrn_lower_bounds, norm_mix_pre, norm_mix_post, norm_ffn_pre,
                    norm_ffn_post, w_in, lru_conv_w, lru_conv_b, lru_w_a, lru_b_a, lru_w_x, lru_b_x, lru_lambda,
                    pool_w, pool_scale, hgrn_norm, diff_lambda, diff_subln, w_branch, w_out, ffn_w_gu, ffn_w_down)
```

```python
import functools
import math

import numpy as np
import jax
import jax.numpy as jnp
from jax import lax
from jax.experimental import pallas as pl
from jax.experimental.pallas import tpu as pltpu

F32 = jnp.float32
BF16 = jnp.bfloat16

D_MODEL = 2048
SEQ = 2048
DEPTH = 2
N_META = 16
BLK = 128
PAD_ROWS = BLK - N_META
QB = 512
NSUB = QB // BLK
WIDTH = 512
HEADS = 4
HEAD_W = 128
CHUNK = 64
SUB = 16
LRU_C = 8.0
POOL_WINDOWS = (2, 4, 8, 16)
DIFF_HEAD_DIM = 64
REL_BUCKETS = 32
REL_MAX_DIST = 128
FFN_HIDDEN = 5632
MIX_COLS = 10 * WIDTH
NEG = -1e30
EPS = 1e-6
LOG2E = math.log2(math.e)
VMEM_LIMIT = 56 * 1024 * 1024


def _cparams(sem):
    return pltpu.CompilerParams(dimension_semantics=sem, vmem_limit_bytes=VMEM_LIMIT)


def _rms(x, w):
    return x * lax.rsqrt(jnp.mean(x * x, axis=-1, keepdims=True) + EPS) * w


def _log_sigmoid(z):
    return -(jnp.maximum(-z, 0.0) + jnp.log(1.0 + jnp.exp(-jnp.abs(z))))


def _sigmoid(z):
    return 1.0 / (1.0 + jnp.exp(-z))


def _gelu_tanh(x):
    c = math.sqrt(2.0 / math.pi)
    return 0.5 * x * (1.0 + jnp.tanh(c * (x + 0.044715 * (x * x * x))))


def _shift_rows(x, s, fill, row):
    return jnp.where(row >= s, pltpu.roll(x, s, axis=0), fill)


def _bucket_tiles():
    r = np.arange(BLK)[None, :]
    c = np.arange(BLK)[:, None]
    max_exact = REL_BUCKETS // 2

    def bucket(n):
        nf = np.maximum(n, 1).astype(np.float32)
        large = max_exact + (np.log(nf / np.float32(max_exact)) / np.float32(math.log(REL_MAX_DIST / max_exact))
                             * np.float32(REL_BUCKETS - max_exact)).astype(np.int32)
        large = np.minimum(large, REL_BUCKETS - 1)
        return np.where(n < max_exact, n, large).astype(np.int32)

    d0 = r - c
    t0 = np.where(d0 >= 0, bucket(np.maximum(d0, 0)), -1)
    t1 = bucket(BLK + r - c)
    return np.stack([t0, t1]).astype(np.int32)


def _prologue_kernel(lbraw_ref, relb_ref, idx_ref, lb_ref, bias_ref, diag_ref):
    raw = lbraw_ref[...]
    mx = jnp.max(raw, axis=0, keepdims=True)
    e = jnp.exp(raw - mx)
    sm = e / jnp.sum(e, axis=0, keepdims=True)
    cum = sm[0:1]
    lb_ref[0:1, :] = cum - sm[0:1]
    for l in range(1, DEPTH):
        cum = cum + sm[l:l + 1]
        lb_ref[l:l + 1, :] = cum - sm[0:1]
    for t in range(2):
        idx = idx_ref[t]
        for hd in range(HEADS):
            acc = jnp.zeros((BLK, BLK), F32)
            for bk in range(REL_BUCKETS):
                acc = jnp.where(idx == bk, relb_ref[bk, hd], acc)
            bias_ref[hd, t] = jnp.where(idx < 0, NEG, acc)
    for hd in range(HEADS):
        far = relb_ref[REL_BUCKETS - 1, hd]
        for kb in range(NSUB):
            for qb in range(NSUB):
                delta = qb - kb
                if delta == 0:
                    blk = bias_ref[hd, 0]
                elif delta == 1:
                    blk = bias_ref[hd, 1]
                else:
                    blk = jnp.full((BLK, BLK), far if delta > 1 else NEG, F32)
                diag_ref[hd, kb * BLK:(kb + 1) * BLK, qb * BLK:(qb + 1) * BLK] = blk * LOG2E


def _prologue(hgrn_lower_bounds, rel_bias):
    idx = jnp.asarray(_bucket_tiles())
    vmem = pl.BlockSpec(memory_space=pltpu.VMEM)
    return pl.pallas_call(
        _prologue_kernel,
        out_shape=(jax.ShapeDtypeStruct((DEPTH, WIDTH), F32),
                   jax.ShapeDtypeStruct((HEADS, 2, BLK, BLK), F32),
                   jax.ShapeDtypeStruct((HEADS, QB, QB), F32)),
        in_specs=[vmem, pl.BlockSpec(memory_space=pltpu.SMEM), vmem],
        out_specs=(vmem, vmem, vmem),
        name="prologue",
    )(hgrn_lower_bounds, rel_bias, idx)


EMBED_ROWS = 512


def _embed_kernel(x_ref, meta_ref, w_ref, h_ref, hn_ref, *, nfull):
    i = pl.program_id(0)

    @pl.when(i < nfull)
    def _():
        h_ref[...] = x_ref[...]

    @pl.when(i == nfull)
    def _():
        h_ref[...] = jnp.zeros_like(h_ref)
        h_ref[PAD_ROWS:BLK, :] = meta_ref[...]

    hn_ref[...] = _rms(h_ref[...], w_ref[...]).astype(BF16)


def _embed(x2d, meta, w_pre, nreal):
    rows = (nreal + 1) * BLK
    nfull = nreal * BLK // EMBED_ROWS
    return pl.pallas_call(
        functools.partial(_embed_kernel, nfull=nfull),
        out_shape=(jax.ShapeDtypeStruct((rows, D_MODEL), F32),
                   jax.ShapeDtypeStruct((rows, D_MODEL), BF16)),
        grid=(nfull + 1,),
        in_specs=[pl.BlockSpec((EMBED_ROWS, D_MODEL), lambda i: (jnp.minimum(i, nfull - 1), 0)),
                  pl.BlockSpec((N_META, D_MODEL), lambda i: (0, 0)),
                  pl.BlockSpec((1, D_MODEL), lambda i: (0, 0))],
        out_specs=(pl.BlockSpec((EMBED_ROWS, D_MODEL), lambda i: (i, 0)),
                   pl.BlockSpec((EMBED_ROWS, D_MODEL), lambda i: (i, 0))),
        compiler_params=_cparams(("arbitrary",)),
        name="embed",
    )(x2d, meta, w_pre)


def _cast_tiles_once(pairs):
    @pl.when(pl.program_id(1) == 0)
    def _():
        for src, dst in pairs:
            dst[...] = src[...].astype(BF16)


def _matmul_kernel(x_ref, *rest):
    *w_refs, o_ref, wbf = rest
    wblk = w_refs[0].shape[-1]
    _cast_tiles_once([(w, wbf.at[:, j * wblk:(j + 1) * wblk]) for j, w in enumerate(w_refs)])
    o_ref[...] = jnp.dot(x_ref[...], wbf[...], preferred_element_type=F32).astype(o_ref.dtype)


def _mixer_in_proj(hn, w_in, layer, col0, ncols, out_dtype, tm, tn, wblk):
    rows, k = hn.shape
    nw = tn // wblk
    w_specs = [pl.BlockSpec((None, k, wblk), functools.partial(
        lambda n, m, j: (layer, 0, col0 // wblk + n * nw + j), j=j)) for j in range(nw)]
    return pl.pallas_call(
        _matmul_kernel,
        out_shape=jax.ShapeDtypeStruct((rows, ncols), out_dtype),
        grid=(ncols // tn, rows // tm),
        in_specs=[pl.BlockSpec((tm, k), lambda n, m: (m, 0))] + w_specs,
        out_specs=pl.BlockSpec((tm, tn), lambda n, m: (m, n)),
        scratch_shapes=[pltpu.VMEM((k, tn), BF16)],
        compiler_params=_cparams(("arbitrary", "arbitrary")),
        name="mixer_in_proj",
    )(hn, *([w_in] * nw))


def _cast_kernel(w_ref, o_ref):
    o_ref[...] = w_ref[...].astype(BF16)


def _cast_layer_bf16(w, layer, tr, col0=0, ncols=None, tc=None):
    _, r, c = w.shape
    ncols = c if ncols is None else ncols
    tc = ncols if tc is None else tc
    return pl.pallas_call(
        _cast_kernel,
        out_shape=jax.ShapeDtypeStruct((r, ncols), BF16),
        grid=(r // tr, ncols // tc),
        in_specs=[pl.BlockSpec((None, tr, tc), lambda i, j: (layer, i, col0 // tc + j))],
        out_specs=pl.BlockSpec((tr, tc), lambda i, j: (i, j)),
        compiler_params=_cparams(("arbitrary", "arbitrary")),
        name="cast_bf16",
    )(w)


def _residual_epilogue(acc, h_ref, wpost_ref, wnext_ref, hnew_ref, hn_ref):
    h_new = h_ref[...] + _rms(acc, wpost_ref[...])
    hnew_ref[...] = h_new
    if hn_ref is not None:
        hn_ref[...] = _rms(h_new, wnext_ref[...]).astype(BF16)


EPI_ROWS = 160


def _row_subtiles(tm):
    sub = next(s for s in (EPI_ROWS, 128, 64, 32, 16) if tm % s == 0)
    return [slice(r, r + sub) for r in range(0, tm, sub)]


def _gate_merge_kernel(hn_ref, g0, g1, g2, g3, y0, y1, y2, y3, wb_ref, o_ref, gbf, wbbf):
    _cast_tiles_once([(g, gbf.at[k]) for k, g in enumerate((g0, g1, g2, g3))] + [(wb_ref, wbbf)])
    hn = hn_ref[...]
    acc = None
    for k, y_ref in enumerate((y0, y1, y2, y3)):
        gate = _sigmoid(jnp.dot(hn, gbf[k], preferred_element_type=F32))
        proj = jnp.dot(y_ref[...], wbbf[k], preferred_element_type=F32)
        acc = gate * proj if acc is None else acc + gate * proj
    o_ref[...] = acc.astype(o_ref.dtype)


def _gate_merge(hn, w_in, ys, w_branch, layer, tm, tn):
    rows, k = hn.shape
    gate_specs = [
        pl.BlockSpec((None, k, tn), functools.partial(
            lambda n, m, base: (layer, 0, base + n), base=(MIX_COLS + br * D_MODEL) // tn))
        for br in range(4)]
    y_specs = [pl.BlockSpec((tm, WIDTH), lambda n, m: (m, 0)) for _ in range(4)]
    return pl.pallas_call(
        _gate_merge_kernel,
        out_shape=jax.ShapeDtypeStruct((rows, D_MODEL), BF16),
        grid=(D_MODEL // tn, rows // tm),
        in_specs=[pl.BlockSpec((tm, k), lambda n, m: (m, 0))] + gate_specs + y_specs
                 + [pl.BlockSpec((None, 4, WIDTH, tn), lambda n, m: (layer, 0, 0, n))],
        out_specs=pl.BlockSpec((tm, tn), lambda n, m: (m, n)),
        scratch_shapes=[pltpu.VMEM((4, k, tn), BF16), pltpu.VMEM((4, WIDTH, tn), BF16)],
        compiler_params=_cparams(("arbitrary", "arbitrary")),
        name="gate_merge",
    )(hn, w_in, w_in, w_in, w_in, *ys, w_branch)


def _out_proj_kernel(x_ref, w_ref, h_ref, wpost_ref, wnext_ref, hnew_ref, hn_ref):
    for rs in _row_subtiles(x_ref.shape[0]):
        acc = jnp.dot(x_ref[rs, :], w_ref[...], preferred_element_type=F32)
        _residual_epilogue(acc, h_ref.at[rs, :], wpost_ref, wnext_ref, hnew_ref.at[rs, :], hn_ref.at[rs, :])


def _out_proj(merged, w_out, h, w_post, w_next, tm):
    rows = h.shape[0]
    row_spec = pl.BlockSpec((tm, D_MODEL), lambda m: (m, 0))
    return pl.pallas_call(
        _out_proj_kernel,
        out_shape=(jax.ShapeDtypeStruct((rows, D_MODEL), F32),
                   jax.ShapeDtypeStruct((rows, D_MODEL), BF16)),
        grid=(rows // tm,),
        in_specs=[row_spec, _resident((D_MODEL, D_MODEL), (0, 0)), row_spec,
                  _pspec((1, D_MODEL)), _pspec((1, D_MODEL))],
        out_specs=(row_spec, row_spec),
        compiler_params=_cparams(("arbitrary",)),
        name="out_proj",
    )(merged, w_out, h, w_post, w_next)


def _swiglu_up_kernel(x_ref, wg_ref, wu_ref, o_ref, wgbf, wubf):
    _cast_tiles_once([(wg_ref, wgbf), (wu_ref, wubf)])
    x = x_ref[...]
    g = jnp.dot(x, wgbf[...], preferred_element_type=F32)
    u = jnp.dot(x, wubf[...], preferred_element_type=F32)
    o_ref[...] = (g * _sigmoid(g) * u).astype(o_ref.dtype)


def _swiglu_up(hn, w_gu, layer, tm, tn):
    rows, k = hn.shape
    nt = FFN_HIDDEN // tn
    return pl.pallas_call(
        _swiglu_up_kernel,
        out_shape=jax.ShapeDtypeStruct((rows, FFN_HIDDEN), BF16),
        grid=(nt, rows // tm),
        in_specs=[pl.BlockSpec((tm, k), lambda n, m: (m, 0)),
                  pl.BlockSpec((None, k, tn), lambda n, m: (layer, 0, n)),
                  pl.BlockSpec((None, k, tn), lambda n, m: (layer, 0, nt + n))],
        out_specs=pl.BlockSpec((tm, tn), lambda n, m: (m, n)),
        scratch_shapes=[pltpu.VMEM((k, tn), BF16), pltpu.VMEM((k, tn), BF16)],
        compiler_params=_cparams(("arbitrary", "arbitrary")),
        name="swiglu_up",
    )(hn, w_gu, w_gu)


def _down_proj_kernel(a_ref, w_ref, h_ref, wpost_ref, wnext_ref, hnew_ref, hn_ref=None):
    for rs in _row_subtiles(a_ref.shape[0]):
        acc = jnp.dot(a_ref[rs, :], w_ref[...], preferred_element_type=F32)
        _residual_epilogue(acc, h_ref.at[rs, :], wpost_ref, wnext_ref, hnew_ref.at[rs, :],
                           None if hn_ref is None else hn_ref.at[rs, :])


def _down_proj(a, w_down, h, w_post, w_next, rows_out, tm, emit_hn):
    row_spec = lambda w: pl.BlockSpec((tm, w), lambda m: (m, 0))
    out_shape = [jax.ShapeDtypeStruct((rows_out, D_MODEL), F32)]
    out_specs = [row_spec(D_MODEL)]
    if emit_hn:
        out_shape.append(jax.ShapeDtypeStruct((rows_out, D_MODEL), BF16))
        out_specs.append(row_spec(D_MODEL))
    res = pl.pallas_call(
        _down_proj_kernel,
        out_shape=tuple(out_shape),
        grid=(rows_out // tm,),
        in_specs=[row_spec(FFN_HIDDEN), _resident((FFN_HIDDEN, D_MODEL), (0, 0)), row_spec(D_MODEL),
                  _pspec((1, D_MODEL)), _pspec((1, D_MODEL))],
        out_specs=tuple(out_specs),
        compiler_params=_cparams(("arbitrary",)),
        name="down_proj",
    )(a, w_down, h, w_post, w_next)
    return res if emit_hn else (res[0], None)


def _row_block(i, nreal):
    return (i + nreal) % (nreal + 1)


def _zspec(col_block, nreal):
    return pl.BlockSpec((BLK, WIDTH), lambda i: (_row_block(i, nreal), col_block))


def _pspec(shape):
    nd = len(shape)
    return pl.BlockSpec(shape, lambda *_: (0,) * nd)


def _resident(shape, index):
    return pl.BlockSpec(shape, lambda *_: index, pipeline_mode=pl.Buffered(1))


def _lru_kernel(u_ref, gate_ref, cw_ref, cb_ref, wa_ref, ba_ref, wx_ref, bx_ref, lam_ref, y_ref,
                ubuf, hst, hist_meta, h_meta, *, nblk):
    i = pl.program_id(0)
    is_meta = i == 0

    @pl.when(is_meta)
    def _():
        ubuf[0:8, :] = jnp.zeros((8, WIDTH), F32)
        hst[...] = jnp.zeros_like(hst)

    @pl.when(jnp.logical_and(i >= 1, (i - 1) % nblk == 0))
    def _():
        ubuf[0:8, :] = hist_meta[...]
        hst[...] = h_meta[...]

    u = u_ref[...]
    ubuf[8:8 + BLK, :] = u
    cw = cw_ref[...]
    xc = (cb_ref[...] + cw[3:4] * u + cw[2:3] * ubuf[7:7 + BLK, :]
          + cw[1:2] * ubuf[6:6 + BLK, :] + cw[0:1] * ubuf[5:5 + BLK, :])
    xb = xc.astype(BF16)
    ra, ia = [], []
    for hd in range(HEADS):
        sl = slice(hd * HEAD_W, (hd + 1) * HEAD_W)
        ra.append(jnp.dot(xb[:, sl], wa_ref[hd].astype(BF16), preferred_element_type=F32))
        ia.append(jnp.dot(xb[:, sl], wx_ref[hd].astype(BF16), preferred_element_type=F32))
    r = _sigmoid(jnp.concatenate(ra, axis=1) + ba_ref[...])
    ig = _sigmoid(jnp.concatenate(ia, axis=1) + bx_ref[...])
    lam = lam_ref[...]
    softplus_neg_lam = jnp.maximum(-lam, 0.0) + jnp.log1p(jnp.exp(-jnp.abs(lam)))
    log_a = -LRU_C * r * softplus_neg_lam
    a = jnp.exp(log_a)
    bb = jnp.sqrt(-jnp.tanh(log_a) * (a * a + 1.0)) * (ig * xc)
    row = lax.broadcasted_iota(jnp.int32, (BLK, 1), 0)
    bb = jnp.where(row >= PAD_ROWS * is_meta.astype(jnp.int32), bb, 0.0)

    acum, bcum = a, bb
    s = 1
    while s < BLK:
        a_sh = _shift_rows(acum, s, 1.0, row)
        b_sh = _shift_rows(bcum, s, 0.0, row)
        bcum = acum * b_sh + bcum
        acum = acum * a_sh
        s *= 2
    h = acum * hst[0:1, :] + bcum
    y_ref[...] = (h * _gelu_tanh(gate_ref[...])).astype(y_ref.dtype)

    hist = u[BLK - 8:BLK, :]
    hlast = jnp.broadcast_to(h[BLK - 1:BLK, :], (8, WIDTH))
    ubuf[0:8, :] = hist
    hst[...] = hlast

    @pl.when(is_meta)
    def _():
        hist_meta[...] = hist
        h_meta[...] = hlast


POOL_HIST = 16


def _pool_kernel(u_ref, pw_ref, ps_ref, y_ref, ubuf, hist_meta, *, nblk):
    i = pl.program_id(0)
    is_meta = i == 0

    @pl.when(is_meta)
    def _():
        ubuf[0:POOL_HIST, :] = jnp.zeros((POOL_HIST, WIDTH), F32)

    @pl.when(jnp.logical_and(i >= 1, (i - 1) % nblk == 0))
    def _():
        ubuf[0:POOL_HIST, :] = hist_meta[...]

    u = u_ref[...]
    ubuf[POOL_HIST:POOL_HIST + BLK, :] = u
    row = lax.broadcasted_iota(jnp.int32, (BLK, 1), 0)
    meta_i = is_meta.astype(jnp.int32)
    pos1 = row + 1 - PAD_ROWS * meta_i + 2 * POOL_HIST * (1 - meta_i)
    outs = []
    for g, win in enumerate(POOL_WINDOWS):
        sl = slice(g * HEAD_W, (g + 1) * HEAD_W)
        acc = u[:, sl]
        for d in range(1, win):
            acc = acc + ubuf[POOL_HIST - d:POOL_HIST - d + BLK, sl]
        count = jnp.clip(pos1, 1, win).astype(F32)
        pooled = acc / count - u[:, sl]
        outs.append(jnp.dot(pooled.astype(BF16), pw_ref[g].astype(BF16), preferred_element_type=F32))
    y_ref[...] = (jnp.concatenate(outs, axis=1) * ps_ref[...]).astype(y_ref.dtype)

    hist = u[BLK - POOL_HIST:BLK, :]
    ubuf[0:POOL_HIST, :] = hist

    @pl.when(is_meta)
    def _():
        hist_meta[...] = hist


def _lru_pool_kernel(u_ref, gate_ref, cw_ref, cb_ref, wa_ref, ba_ref, wx_ref, bx_ref, lam_ref,
                     pu_ref, pw_ref, ps_ref, ya_ref, yb_ref,
                     ubuf, hst, hist_meta, h_meta, pbuf, phist_meta, *, nblk):
    _lru_kernel(u_ref, gate_ref, cw_ref, cb_ref, wa_ref, ba_ref, wx_ref, bx_ref, lam_ref, ya_ref,
                ubuf, hst, hist_meta, h_meta, nblk=nblk)
    _pool_kernel(pu_ref, pw_ref, ps_ref, yb_ref, pbuf, phist_meta, nblk=nblk)


def _lru_pool(z, cw, cb, wa, ba, wx, bx, lam, pw, ps, nreal, nblk):
    rows = z.shape[0]
    out_spec = pl.BlockSpec((BLK, WIDTH), lambda i: (_row_block(i, nreal), 0))
    return pl.pallas_call(
        functools.partial(_lru_pool_kernel, nblk=nblk),
        out_shape=(jax.ShapeDtypeStruct((rows, WIDTH), BF16), jax.ShapeDtypeStruct((rows, WIDTH), BF16)),
        grid=(nreal + 1,),
        in_specs=[_zspec(0, nreal), _zspec(1, nreal),
                  _pspec((4, WIDTH)), _pspec((1, WIDTH)),
                  _pspec((HEADS, HEAD_W, HEAD_W)), _pspec((1, WIDTH)),
                  _pspec((HEADS, HEAD_W, HEAD_W)), _pspec((1, WIDTH)), _pspec((1, WIDTH)),
                  _zspec(2, nreal), _pspec((4, HEAD_W, HEAD_W)), _pspec((1, WIDTH))],
        out_specs=(out_spec, out_spec),
        scratch_shapes=[pltpu.VMEM((8 + BLK, WIDTH), F32), pltpu.VMEM((8, WIDTH), F32),
                        pltpu.VMEM((8, WIDTH), F32), pltpu.VMEM((8, WIDTH), F32),
                        pltpu.VMEM((POOL_HIST + BLK, WIDTH), F32), pltpu.VMEM((POOL_HIST, WIDTH), F32)],
        compiler_params=_cparams(("arbitrary",)),
        name="lru_pool",
    )(z, z, cw, cb, wa, ba, wx, bx, lam, z, pw, ps)


def _hgrn_chunk(q, z, v, lbh, state_t, valid, ones_bf):
    ls = _log_sigmoid(z)
    x1 = jnp.log(lbh)
    x2 = jnp.log1p(-lbh) + ls
    mx = jnp.maximum(x1, x2)
    g = mx + jnp.log(1.0 + jnp.exp(-jnp.abs(x1 - x2)))
    k = (1.0 - lbh) * _sigmoid(-z)
    if valid is not None:
        g = jnp.where(valid, g, 0.0)
    row = lax.broadcasted_iota(jnp.int32, (CHUNK, 1), 0)
    b = g * LOG2E
    s = 1
    while s < CHUNK:
        b = b + _shift_rows(b, s, 0.0, row)
        s *= 2
    b_last = b[CHUNK - 1:CHUNK, :]

    qe = (q * jnp.exp2(b)).astype(BF16)
    o = lax.dot_general(qe, state_t.astype(BF16), (((1,), (1,)), ((), ())), preferred_element_type=F32)

    col = lax.broadcasted_iota(jnp.int32, (SUB, CHUNK), 1)
    rsub = lax.broadcasted_iota(jnp.int32, (SUB, CHUNK), 0)
    lane = lax.broadcasted_iota(jnp.int32, (SUB, HEAD_W), 1)
    s_rows = []
    for blk in range(CHUNK // SUB):
        lo = blk * SUB
        bi = b[lo:lo + SUB, :]
        qi = q[lo:lo + SUB, :]
        ki = k[lo:lo + SUB, :]
        parts = []
        for sr in range(SUB):
            t0 = 0 if sr < SUB // 2 else SUB // 2
            e = jnp.exp2(jnp.minimum(bi[t0:, :] - bi[sr:sr + 1, :], 0.0))
            parts.append(qi[t0:, :] * e * ki[sr:sr + 1, :])
        m3 = jnp.concatenate(parts, axis=0).astype(BF16)
        red = jnp.dot(m3, ones_bf, preferred_element_type=F32)
        halves = [jnp.zeros((SUB // 2, HEAD_W), F32), jnp.zeros((SUB // 2, HEAD_W), F32)]
        r0 = 0
        for sr in range(SUB):
            for hf in range(0 if sr < SUB // 2 else 1, 2):
                halves[hf] = jnp.where(lane[:SUB // 2] == lo + sr, red[r0:r0 + SUB // 2, :], halves[hf])
                r0 += SUB // 2
        diag = jnp.concatenate(halves, axis=0)[:, :CHUNK]
        s_blk = jnp.where(jnp.logical_and(col >= lo, col - lo <= rsub), diag, 0.0)
        if blk > 0:
            b0 = b[lo - 1:lo, :]
            kt = (k * jnp.exp2(jnp.minimum(b0 - b, 0.0))).astype(BF16)
            qd = (qi * jnp.exp2(bi - b0)).astype(BF16)
            off = lax.dot_general(qd, kt, (((1,), (1,)), ((), ())), preferred_element_type=F32)
            s_blk = jnp.where(col < lo, off, s_blk)
        s_rows.append(s_blk)
    scores = jnp.concatenate(s_rows, axis=0).astype(BF16)
    vb = v.astype(BF16)
    o = o + jnp.dot(scores, vb, preferred_element_type=F32)

    kd = (k * jnp.exp2(b_last - b)).astype(BF16)
    upd = lax.dot_general(vb, kd, (((0,), (0,)), ((), ())), preferred_element_type=F32)
    new_state_t = state_t * jnp.exp2(b_last) + upd
    return o, new_state_t


def _hgrn_kernel(q_ref, f_ref, v_ref, og_ref, lb_ref, nw_ref, y_ref, state, state_meta, *, nblk):
    i = pl.program_id(0)
    is_meta = i == 0

    @pl.when(is_meta)
    def _():
        state[...] = jnp.zeros_like(state)

    @pl.when(jnp.logical_and(i >= 1, (i - 1) % nblk == 0))
    def _():
        state[...] = state_meta[...]

    ones_bf = jnp.ones((HEAD_W, HEAD_W), BF16)
    nw = nw_ref[...]
    for hd in range(HEADS):
        sl = slice(hd * HEAD_W, (hd + 1) * HEAD_W)
        lbh = lb_ref[:, sl]
        st = state[hd]
        for c in range(BLK // CHUNK):
            rs = slice(c * CHUNK, (c + 1) * CHUNK)
            rowg = lax.broadcasted_iota(jnp.int32, (CHUNK, 1), 0) + c * CHUNK
            valid = rowg >= PAD_ROWS * is_meta.astype(jnp.int32)
            o, st = _hgrn_chunk(q_ref[rs, sl], f_ref[rs, sl], v_ref[rs, sl], lbh, st, valid, ones_bf)
            og = og_ref[rs, sl]
            y_ref[rs, sl] = (_rms(o, nw) * (og * _sigmoid(og))).astype(y_ref.dtype)
        state[hd] = st

    @pl.when(is_meta)
    def _():
        state_meta[...] = state[...]


def _hgrn(z_a, z_b, lb, nw, nreal, nblk):
    rows = z_a.shape[0]
    return pl.pallas_call(
        functools.partial(_hgrn_kernel, nblk=nblk),
        out_shape=jax.ShapeDtypeStruct((rows, WIDTH), BF16),
        grid=(nreal + 1,),
        in_specs=[_zspec(3, nreal), _zspec(0, nreal), _zspec(1, nreal), _zspec(2, nreal),
                  _pspec((1, WIDTH)), _pspec((1, HEAD_W))],
        out_specs=pl.BlockSpec((BLK, WIDTH), lambda i: (_row_block(i, nreal), 0)),
        scratch_shapes=[pltpu.VMEM((HEADS, HEAD_W, HEAD_W), F32), pltpu.VMEM((HEADS, HEAD_W, HEAD_W), F32)],
        compiler_params=_cparams(("arbitrary",)),
        name="hgrn2",
    )(z_a, z_b, z_b, z_b, lb, nw)


def _attn_lambda(lamv_ref, lambda_init):
    lv = lamv_ref[...]
    return (jnp.exp(jnp.sum(lv[0:1] * lv[1:2], axis=-1, keepdims=True))
            - jnp.exp(jnp.sum(lv[2:3] * lv[3:4], axis=-1, keepdims=True)) + lambda_init)


def _scaled_q(q_ref):
    qf = (q_ref[...].astype(F32) * (DIFF_HEAD_DIM ** -0.5)).astype(BF16)
    return qf[:, :DIFF_HEAD_DIM], qf[:, DIFF_HEAD_DIM:]


def _fold8(x, op):
    r, c = x.shape
    return op(x.reshape(r // 8, 8, c), axis=0)


def _attn_kernel(relb_ref, q_ref, k_ref, v_ref, km_ref, vm_ref, bias_ref, diag_ref, lamv_ref, sw_ref, y_ref,
                 s_sc, vt_sc, m_sc, l_sc, acc_sc, *, nqb, nblk, lambda_init):
    hd = pl.program_id(0)
    jq = pl.program_id(1) % nqb
    far = relb_ref[REL_BUCKETS - 1, hd] * LOG2E
    t1 = bias_ref[1] * LOG2E
    slot_diag, slot_meta = nqb - 1, nqb

    @pl.when(jq == 0)
    def _():
        for t in range(nblk):
            vt_sc[t // NSUB, :, (t % NSUB) * BLK:(t % NSUB + 1) * BLK] = (
                v_ref[t * BLK:(t + 1) * BLK, :].astype(F32).T.astype(BF16))

    qt = (q_ref[...].astype(F32) * (DIFF_HEAD_DIM ** -0.5)).T
    zero = jnp.zeros((DIFF_HEAD_DIM, QB), F32)
    qtp = (jnp.concatenate([qt[:DIFF_HEAD_DIM], zero], axis=0).astype(BF16),
           jnp.concatenate([zero, qt[DIFF_HEAD_DIM:]], axis=0).astype(BF16))

    def key_chunk(c):
        return k_ref[pl.ds(pl.multiple_of(c * QB, QB), QB), :]

    def score_chunk(k_rows, slot, add_bias, r0=0, c0=0):
        r = k_rows.shape[0]
        for mp in range(2):
            s = add_bias(jnp.dot(k_rows, qtp[mp][:, c0:], preferred_element_type=F32) * LOG2E)
            s_sc[mp, slot, r0:r0 + r, c0:] = s
            m_sc[mp, :, c0:] = jnp.maximum(m_sc[mp, :, c0:], _fold8(s, jnp.max))

    def value_chunk(vt_cols, slot, m8, r0=0, c0=0):
        r = vt_cols.shape[1]
        for mp in range(2):
            p = jnp.exp2(s_sc[mp, slot, r0:r0 + r, c0:].reshape(r // 8, 8, QB - c0) - m8[mp][None, :, c0:])
            l_sc[mp, :, c0:] += jnp.sum(p, axis=0)
            acc_sc[mp, :, c0:] += jnp.dot(vt_cols, p.reshape(r, QB - c0).astype(BF16),
                                         preferred_element_type=F32)

    def diag_tiles(fn):
        for kb in range(NSUB):
            fn(kb, kb * BLK)

    def walk(fn_far, fn_meta, fn_diag):
        def body(c2, carry):
            fn_far(2 * c2)
            fn_far(2 * c2 + 1)
            return carry

        lax.fori_loop(0, jq // 2, body, 0)

        @pl.when(jq % 2 == 1)
        def _():
            fn_far(jq - 1)

        fn_meta()
        fn_diag()

    near = t1 - far
    first = jnp.full((1, BLK), jq, jnp.int32) == 0

    def meta_bias(s):
        head = s[:, 0:BLK] + (far + jnp.where(first, near[PAD_ROWS:, :], 0.0))
        return jnp.concatenate([head, s[:, BLK:] + far], axis=1)

    def meta_values(m8):
        v16 = vm_ref[PAD_ROWS:BLK, :]
        for mp in range(2):
            p = jnp.exp2(s_sc[mp, slot_meta, 0:N_META, :].reshape(N_META // 8, 8, QB) - m8[mp][None])
            l_sc[mp] += jnp.sum(p, axis=0)
            acc_sc[mp] += lax.dot_general(v16, p.reshape(N_META, QB).astype(BF16), (((0,), (0,)), ((), ())),
                                          preferred_element_type=F32)

    m_sc[...] = jnp.full_like(m_sc, NEG)
    walk(lambda c: score_chunk(key_chunk(c), c, lambda s: s + far),
         lambda: score_chunk(km_ref[PAD_ROWS:BLK, :], slot_meta, meta_bias),
         lambda: diag_tiles(lambda kb, c0: score_chunk(
             k_ref[pl.ds(pl.multiple_of(jq * QB + c0, BLK), BLK), :], slot_diag,
             lambda s: s + diag_ref[c0:c0 + BLK, c0:], r0=c0, c0=c0)))

    @pl.when(jq >= 1)
    def _():
        for mp in range(2):
            fixed = s_sc[mp, jq - 1, QB - BLK:QB, 0:BLK] + near
            s_sc[mp, jq - 1, QB - BLK:QB, 0:BLK] = fixed
            m_sc[mp, :, 0:BLK] = jnp.maximum(m_sc[mp, :, 0:BLK], _fold8(fixed, jnp.max))

    m8 = [jnp.broadcast_to(jnp.max(m_sc[mp], axis=0, keepdims=True), (8, QB)) for mp in range(2)]
    l_sc[...] = jnp.zeros_like(l_sc)
    acc_sc[...] = jnp.zeros_like(acc_sc)

    walk(lambda c: value_chunk(vt_sc[c], c, m8),
         lambda: meta_values(m8),
         lambda: diag_tiles(lambda kb, c0: value_chunk(
             vt_sc[jq, :, c0:c0 + BLK], slot_diag, m8, r0=c0, c0=c0)))

    lam = _attn_lambda(lamv_ref, lambda_init)
    l0 = jnp.sum(l_sc[0], axis=0, keepdims=True)
    l1 = jnp.sum(l_sc[1], axis=0, keepdims=True)
    ot = acc_sc[0] / l0 - lam * (acc_sc[1] / l1)
    ot = ot * lax.rsqrt(jnp.mean(ot * ot, axis=0, keepdims=True) + EPS)
    y_ref[...] = (ot.T * sw_ref[...] * (1.0 - lambda_init)).astype(y_ref.dtype)


def _attn_meta_kernel(q_ref, k_ref, v_ref, bias_ref, lamv_ref, sw_ref, y_ref, *, lambda_init):
    qs = _scaled_q(q_ref)
    kb = k_ref[...]
    colmask = jnp.where(lax.broadcasted_iota(jnp.int32, (1, BLK), 1) >= PAD_ROWS, 0.0, NEG)
    outs = []
    bias = bias_ref[0].T + colmask
    for mp in range(2):
        s = lax.dot_general(qs[mp], kb[:, mp * DIFF_HEAD_DIM:(mp + 1) * DIFF_HEAD_DIM],
                            (((1,), (1,)), ((), ())), preferred_element_type=F32) + bias
        p = jnp.exp(s - jnp.max(s, axis=-1, keepdims=True))
        acc = jnp.dot(p.astype(BF16), v_ref[...], preferred_element_type=F32)
        outs.append(acc / jnp.sum(p, axis=-1, keepdims=True))
    o = outs[0] - _attn_lambda(lamv_ref, lambda_init) * outs[1]
    o = _rms(o, sw_ref[...]) * (1.0 - lambda_init)
    row = lax.broadcasted_iota(jnp.int32, (BLK, 1), 0)
    y_ref[...] = jnp.where(row >= PAD_ROWS, o, 0.0).astype(y_ref.dtype)


def _attn(zd, rel_bias, bias_tiles, bias_diag, lam_vec, subln_w, nreal, nblk, lambda_init):
    nqb = nblk * BLK // QB
    nq_total = nreal * BLK // QB
    seq = nblk * BLK
    kcol, vcol = HEADS, 2 * HEADS
    y_main = pl.pallas_call(
        functools.partial(_attn_kernel, nqb=nqb, nblk=nblk, lambda_init=lambda_init),
        out_shape=jax.ShapeDtypeStruct((nreal * BLK, WIDTH), BF16),
        grid=(HEADS, nq_total),
        in_specs=[pl.BlockSpec(memory_space=pltpu.SMEM),
                  pl.BlockSpec((QB, HEAD_W), lambda h, i: (i, h)),
                  pl.BlockSpec((seq, HEAD_W), lambda h, i: (i // nqb, kcol + h)),
                  pl.BlockSpec((seq, HEAD_W), lambda h, i: (i // nqb, vcol + h)),
                  pl.BlockSpec((BLK, HEAD_W), lambda h, i: (nreal, kcol + h)),
                  pl.BlockSpec((BLK, HEAD_W), lambda h, i: (nreal, vcol + h)),
                  pl.BlockSpec((None, 2, BLK, BLK), lambda h, i: (h, 0, 0, 0)),
                  pl.BlockSpec((None, QB, QB), lambda h, i: (h, 0, 0)),
                  pl.BlockSpec((4, DIFF_HEAD_DIM), lambda h, i: (0, 0)),
                  pl.BlockSpec((1, HEAD_W), lambda h, i: (0, 0))],
        out_specs=pl.BlockSpec((QB, HEAD_W), lambda h, i: (i, h)),
        scratch_shapes=[pltpu.VMEM((2, nqb + 1, QB, QB), F32),
                        pltpu.VMEM((nqb, HEAD_W, QB), BF16),
                        pltpu.VMEM((2, 8, QB), F32),
                        pltpu.VMEM((2, 8, QB), F32),
                        pltpu.VMEM((2, HEAD_W, QB), F32)],
        compiler_params=_cparams(("arbitrary", "arbitrary")),
        name="diff_attn",
    )(rel_bias, zd, zd, zd, zd, zd, bias_tiles, bias_diag, lam_vec, subln_w)
    y_meta = pl.pallas_call(
        functools.partial(_attn_meta_kernel, lambda_init=lambda_init),
        out_shape=jax.ShapeDtypeStruct((BLK, WIDTH), BF16),
        grid=(HEADS,),
        in_specs=[pl.BlockSpec((BLK, HEAD_W), lambda h: (nreal, h)),
                  pl.BlockSpec((BLK, HEAD_W), lambda h: (nreal, kcol + h)),
                  pl.BlockSpec((BLK, HEAD_W), lambda h: (nreal, vcol + h)),
                  pl.BlockSpec((None, 2, BLK, BLK), lambda h: (h, 0, 0, 0)),
                  pl.BlockSpec((4, DIFF_HEAD_DIM), lambda h: (0, 0)),
                  pl.BlockSpec((1, HEAD_W), lambda h: (0, 0))],
        out_specs=pl.BlockSpec((BLK, HEAD_W), lambda h: (0, h)),
        compiler_params=_cparams(("arbitrary",)),
        name="diff_attn_meta",
    )(zd, zd, zd, bias_tiles, lam_vec, subln_w)
    return jnp.concatenate([y_main, y_meta], axis=0)


def _largest_tile(rows, cap, align=16):
    best = align
    for t in range(align, cap + 1, align):
        if rows % t == 0:
            best = t
    return best


def _forward(x, meta_tokens, rel_bias, hgrn_lower_bounds, norm_mix_pre, norm_mix_post, norm_ffn_pre,
             norm_ffn_post, w_in, lru_conv_w, lru_conv_b, lru_w_a, lru_b_a, lru_w_x, lru_b_x, lru_lambda,
             pool_w, pool_scale, hgrn_norm, diff_lambda, diff_subln, w_branch, w_out, ffn_w_gu, ffn_w_down):
    bsz, seq, _ = x.shape
    nblk = seq // BLK
    nreal = bsz * nblk
    rows = (nreal + 1) * BLK
    rows_real = nreal * BLK
    tm_big = _largest_tile(rows, 832)
    tm_epi = _largest_tile(rows, 640)
    tm_down = _largest_tile(rows, 320)
    tm_last = _largest_tile(rows_real, 256)

    def vec(a):
        return a.reshape(1, -1)

    lbs, bias_tiles, bias_diag = _prologue(hgrn_lower_bounds, rel_bias)
    h, hn = _embed(x.reshape(rows_real, D_MODEL), meta_tokens, vec(norm_mix_pre[0]), nreal)

    for layer in range(DEPTH):
        lambda_init = 0.8 - 0.6 * math.exp(-0.3 * layer)
        z = _mixer_in_proj(hn, w_in, layer, 0, 4 * WIDTH, F32, tm_big, 1024, 1024)
        z_b = _mixer_in_proj(hn, w_in, layer, 4 * WIDTH, 3 * WIDTH, F32, tm_big, 3 * WIDTH, WIDTH)
        zd = _mixer_in_proj(hn, w_in, layer, 7 * WIDTH, 3 * WIDTH, BF16, tm_big, 3 * WIDTH, WIDTH)
        y_a, y_b = _lru_pool(z, lru_conv_w[layer], vec(lru_conv_b[layer]), lru_w_a[layer], vec(lru_b_a[layer]),
                             lru_w_x[layer], vec(lru_b_x[layer]), vec(lru_lambda[layer]),
                             pool_w[layer], vec(pool_scale[layer]), nreal, nblk)
        y_c = _hgrn(z, z_b, lbs[layer:layer + 1], vec(hgrn_norm[layer]), nreal, nblk)
        y_d = _attn(zd, rel_bias, bias_tiles, bias_diag, diff_lambda[layer], vec(diff_subln[layer]),
                    nreal, nblk, lambda_init)
        merged = _gate_merge(hn, w_in, (y_a, y_b, y_c, y_d), w_branch, layer, tm_big, 256)
        h, hn = _out_proj(merged, _cast_layer_bf16(w_out, layer, 512), h, vec(norm_mix_post[layer]),
                          vec(norm_ffn_pre[layer]), tm_epi)
        a = _swiglu_up(hn, ffn_w_gu, layer, tm_big, 512)
        last = layer == DEPTH - 1
        w_next = vec(norm_mix_pre[layer + 1]) if not last else vec(norm_mix_pre[layer])
        h, hn = _down_proj(a, _cast_layer_bf16(ffn_w_down, layer, 512), h, vec(norm_ffn_post[layer]), w_next,
                           rows_real if last else rows, tm_last if last else tm_down, not last)
    return h.reshape(bsz, seq, D_MODEL)


def kernel(x, meta_tokens, rel_bias, hgrn_lower_bounds, norm_mix_pre, norm_mix_post, norm_ffn_pre, norm_ffn_post, w_in, lru_conv_w, lru_conv_b, lru_w_a, lru_b_a, lru_w_x, lru_b_x, lru_lambda, pool_w, pool_scale, hgrn_norm, diff_lambda, diff_subln, w_branch, w_out, ffn_w_gu, ffn_w_down):
    return _forward(x, meta_tokens, rel_bias, hgrn_lower_bounds, norm_mix_pre, norm_mix_post, norm_ffn_pre,
                    norm_ffn_post, w_in, lru_conv_w, lru_conv_b, lru_w_a, lru_b_a, lru_w_x, lru_b_x, lru_lambda,
                    pool_w, pool_scale, hgrn_norm, diff_lambda, diff_subln, w_branch, w_out, ffn_w_gu, ffn_w_down)
```

```python
import functools
import math

import numpy as np
import jax
import jax.numpy as jnp
from jax import lax
from jax.experimental import pallas as pl
from jax.experimental.pallas import tpu as pltpu

F32 = jnp.float32
BF16 = jnp.bfloat16

D_MODEL = 2048
SEQ = 2048
DEPTH = 2
N_META = 16
BLK = 128
PAD_ROWS = BLK - N_META
QB = 512
NSUB = QB // BLK
WIDTH = 512
HEADS = 4
HEAD_W = 128
CHUNK = 64
SUB = 16
LRU_C = 8.0
POOL_WINDOWS = (2, 4, 8, 16)
DIFF_HEAD_DIM = 64
REL_BUCKETS = 32
REL_MAX_DIST = 128
FFN_HIDDEN = 5632
MIX_COLS = 10 * WIDTH
NEG = -1e30
EPS = 1e-6
LOG2E = math.log2(math.e)
VMEM_LIMIT = 56 * 1024 * 1024


def _cparams(sem):
    return pltpu.CompilerParams(dimension_semantics=sem, vmem_limit_bytes=VMEM_LIMIT)


def _rms(x, w):
    return x * lax.rsqrt(jnp.mean(x * x, axis=-1, keepdims=True) + EPS) * w


def _log_sigmoid(z):
    return -(jnp.maximum(-z, 0.0) + jnp.log(1.0 + jnp.exp(-jnp.abs(z))))


def _sigmoid(z):
    return 1.0 / (1.0 + jnp.exp(-z))


def _gelu_tanh(x):
    c = math.sqrt(2.0 / math.pi)
    return 0.5 * x * (1.0 + jnp.tanh(c * (x + 0.044715 * (x * x * x))))


def _shift_rows(x, s, fill, row):
    return jnp.where(row >= s, pltpu.roll(x, s, axis=0), fill)


def _bucket_tiles():
    r = np.arange(BLK)[None, :]
    c = np.arange(BLK)[:, None]
    max_exact = REL_BUCKETS // 2

    def bucket(n):
        nf = np.maximum(n, 1).astype(np.float32)
        large = max_exact + (np.log(nf / np.float32(max_exact)) / np.float32(math.log(REL_MAX_DIST / max_exact))
                             * np.float32(REL_BUCKETS - max_exact)).astype(np.int32)
        large = np.minimum(large, REL_BUCKETS - 1)
        return np.where(n < max_exact, n, large).astype(np.int32)

    d0 = r - c
    t0 = np.where(d0 >= 0, bucket(np.maximum(d0, 0)), -1)
    t1 = bucket(BLK + r - c)
    return np.stack([t0, t1]).astype(np.int32)


def _prologue_kernel(lbraw_ref, relb_ref, idx_ref, lb_ref, bias_ref, diag_ref):
    raw = lbraw_ref[...]
    mx = jnp.max(raw, axis=0, keepdims=True)
    e = jnp.exp(raw - mx)
    sm = e / jnp.sum(e, axis=0, keepdims=True)
    cum = sm[0:1]
    lb_ref[0:1, :] = cum - sm[0:1]
    for l in range(1, DEPTH):
        cum = cum + sm[l:l + 1]
        lb_ref[l:l + 1, :] = cum - sm[0:1]
    for t in range(2):
        idx = idx_ref[t]
        for hd in range(HEADS):
            acc = jnp.zeros((BLK, BLK), F32)
            for bk in range(REL_BUCKETS):
                acc = jnp.where(idx == bk, relb_ref[bk, hd], acc)
            bias_ref[hd, t] = jnp.where(idx < 0, NEG, acc)
    for hd in range(HEADS):
        far = relb_ref[REL_BUCKETS - 1, hd]
        for kb in range(NSUB):
            for qb in range(NSUB):
                delta = qb - kb
                if delta == 0:
                    blk = bias_ref[hd, 0]
                elif delta == 1:
                    blk = bias_ref[hd, 1]
                else:
                    blk = jnp.full((BLK, BLK), far if delta > 1 else NEG, F32)
                diag_ref[hd, kb * BLK:(kb + 1) * BLK, qb * BLK:(qb + 1) * BLK] = blk * LOG2E


def _prologue(hgrn_lower_bounds, rel_bias):
    idx = jnp.asarray(_bucket_tiles())
    vmem = pl.BlockSpec(memory_space=pltpu.VMEM)
    return pl.pallas_call(
        _prologue_kernel,
        out_shape=(jax.ShapeDtypeStruct((DEPTH, WIDTH), F32),
                   jax.ShapeDtypeStruct((HEADS, 2, BLK, BLK), F32),
                   jax.ShapeDtypeStruct((HEADS, QB, QB), F32)),
        in_specs=[vmem, pl.BlockSpec(memory_space=pltpu.SMEM), vmem],
        out_specs=(vmem, vmem, vmem),
        name="prologue",
    )(hgrn_lower_bounds, rel_bias, idx)


EMBED_ROWS = 512


def _embed_kernel(x_ref, meta_ref, w_ref, h_ref, hn_ref, *, nfull):
    i = pl.program_id(0)

    @pl.when(i < nfull)
    def _():
        h_ref[...] = x_ref[...]

    @pl.when(i == nfull)
    def _():
        h_ref[...] = jnp.zeros_like(h_ref)
        h_ref[PAD_ROWS:BLK, :] = meta_ref[...]

    hn_ref[...] = _rms(h_ref[...], w_ref[...]).astype(BF16)


def _embed(x2d, meta, w_pre, nreal):
    rows = (nreal + 1) * BLK
    nfull = nreal * BLK // EMBED_ROWS
    return pl.pallas_call(
        functools.partial(_embed_kernel, nfull=nfull),
        out_shape=(jax.ShapeDtypeStruct((rows, D_MODEL), F32),
                   jax.ShapeDtypeStruct((rows, D_MODEL), BF16)),
        grid=(nfull + 1,),
        in_specs=[pl.BlockSpec((EMBED_ROWS, D_MODEL), lambda i: (jnp.minimum(i, nfull - 1), 0)),
                  pl.BlockSpec((N_META, D_MODEL), lambda i: (0, 0)),
                  pl.BlockSpec((1, D_MODEL), lambda i: (0, 0))],
        out_specs=(pl.BlockSpec((EMBED_ROWS, D_MODEL), lambda i: (i, 0)),
                   pl.BlockSpec((EMBED_ROWS, D_MODEL), lambda i: (i, 0))),
        compiler_params=_cparams(("arbitrary",)),
        name="embed",
    )(x2d, meta, w_pre)


def _cast_tiles_once(pairs):
    @pl.when(pl.program_id(1) == 0)
    def _():
        for src, dst in pairs:
            dst[...] = src[...].astype(BF16)


def _matmul_kernel(x_ref, *rest):
    *w_refs, o_ref, wbf = rest
    wblk = w_refs[0].shape[-1]
    _cast_tiles_once([(w, wbf.at[:, j * wblk:(j + 1) * wblk]) for j, w in enumerate(w_refs)])
    o_ref[...] = jnp.dot(x_ref[...], wbf[...], preferred_element_type=F32).astype(o_ref.dtype)


def _mixer_in_proj(hn, w_in, layer, col0, ncols, out_dtype, tm, tn, wblk):
    rows, k = hn.shape
    nw = tn // wblk
    w_specs = [pl.BlockSpec((None, k, wblk), functools.partial(
        lambda n, m, j: (layer, 0, col0 // wblk + n * nw + j), j=j)) for j in range(nw)]
    return pl.pallas_call(
        _matmul_kernel,
        out_shape=jax.ShapeDtypeStruct((rows, ncols), out_dtype),
        grid=(ncols // tn, rows // tm),
        in_specs=[pl.BlockSpec((tm, k), lambda n, m: (m, 0))] + w_specs,
        out_specs=pl.BlockSpec((tm, tn), lambda n, m: (m, n)),
        scratch_shapes=[pltpu.VMEM((k, tn), BF16)],
        compiler_params=_cparams(("arbitrary", "arbitrary")),
        name="mixer_in_proj",
    )(hn, *([w_in] * nw))


def _cast_kernel(w_ref, o_ref):
    o_ref[...] = w_ref[...].astype(BF16)


def _cast_layer_bf16(w, layer, tr):
    _, r, c = w.shape
    return pl.pallas_call(
        _cast_kernel,
        out_shape=jax.ShapeDtypeStruct((r, c), BF16),
        grid=(r // tr,),
        in_specs=[pl.BlockSpec((None, tr, c), lambda i: (layer, i, 0))],
        out_specs=pl.BlockSpec((tr, c), lambda i: (i, 0)),
        compiler_params=_cparams(("arbitrary",)),
        name="cast_bf16",
    )(w)


def _residual_epilogue(acc, h_ref, wpost_ref, wnext_ref, hnew_ref, hn_ref):
    h_new = h_ref[...] + _rms(acc, wpost_ref[...])
    hnew_ref[...] = h_new
    if hn_ref is not None:
        hn_ref[...] = _rms(h_new, wnext_ref[...]).astype(BF16)


EPI_ROWS = 160


def _row_subtiles(tm):
    sub = next(s for s in (EPI_ROWS, 128, 64, 32, 16) if tm % s == 0)
    return [slice(r, r + sub) for r in range(0, tm, sub)]


def _gate_merge_kernel(hn_ref, g0, g1, g2, g3, y0, y1, y2, y3, wb_ref, o_ref, gbf, wbbf):
    _cast_tiles_once([(g, gbf.at[k]) for k, g in enumerate((g0, g1, g2, g3))] + [(wb_ref, wbbf)])
    hn = hn_ref[...]
    acc = None
    for k, y_ref in enumerate((y0, y1, y2, y3)):
        gate = _sigmoid(jnp.dot(hn, gbf[k], preferred_element_type=F32))
        proj = jnp.dot(y_ref[...], wbbf[k], preferred_element_type=F32)
        acc = gate * proj if acc is None else acc + gate * proj
    o_ref[...] = acc.astype(o_ref.dtype)


def _gate_merge(hn, w_in, ys, w_branch, layer, tm, tn):
    rows, k = hn.shape
    gate_specs = [
        pl.BlockSpec((None, k, tn), functools.partial(
            lambda n, m, base: (layer, 0, base + n), base=(MIX_COLS + br * D_MODEL) // tn))
        for br in range(4)]
    y_specs = [pl.BlockSpec((tm, WIDTH), lambda n, m: (m, 0)) for _ in range(4)]
    return pl.pallas_call(
        _gate_merge_kernel,
        out_shape=jax.ShapeDtypeStruct((rows, D_MODEL), BF16),
        grid=(D_MODEL // tn, rows // tm),
        in_specs=[pl.BlockSpec((tm, k), lambda n, m: (m, 0))] + gate_specs + y_specs
                 + [pl.BlockSpec((None, 4, WIDTH, tn), lambda n, m: (layer, 0, 0, n))],
        out_specs=pl.BlockSpec((tm, tn), lambda n, m: (m, n)),
        scratch_shapes=[pltpu.VMEM((4, k, tn), BF16), pltpu.VMEM((4, WIDTH, tn), BF16)],
        compiler_params=_cparams(("arbitrary", "arbitrary")),
        name="gate_merge",
    )(hn, w_in, w_in, w_in, w_in, *ys, w_branch)


def _out_proj_kernel(x_ref, w_ref, h_ref, wpost_ref, wnext_ref, hnew_ref, hn_ref):
    for rs in _row_subtiles(x_ref.shape[0]):
        acc = jnp.dot(x_ref[rs, :], w_ref[...], preferred_element_type=F32)
        _residual_epilogue(acc, h_ref.at[rs, :], wpost_ref, wnext_ref, hnew_ref.at[rs, :], hn_ref.at[rs, :])


def _out_proj(merged, w_out, h, w_post, w_next, tm):
    rows = h.shape[0]
    row_spec = pl.BlockSpec((tm, D_MODEL), lambda m: (m, 0))
    return pl.pallas_call(
        _out_proj_kernel,
        out_shape=(jax.ShapeDtypeStruct((rows, D_MODEL), F32),
                   jax.ShapeDtypeStruct((rows, D_MODEL), BF16)),
        grid=(rows // tm,),
        in_specs=[row_spec, _resident((D_MODEL, D_MODEL), (0, 0)), row_spec,
                  _pspec((1, D_MODEL)), _pspec((1, D_MODEL))],
        out_specs=(row_spec, row_spec),
        compiler_params=_cparams(("arbitrary",)),
        name="out_proj",
    )(merged, w_out, h, w_post, w_next)


def _swiglu_up_kernel(x_ref, wg_ref, wu_ref, o_ref, wgbf, wubf):
    _cast_tiles_once([(wg_ref, wgbf), (wu_ref, wubf)])
    x = x_ref[...]
    g = jnp.dot(x, wgbf[...], preferred_element_type=F32)
    u = jnp.dot(x, wubf[...], preferred_element_type=F32)
    o_ref[...] = (g * _sigmoid(g) * u).astype(o_ref.dtype)


def _swiglu_up(hn, w_gu, layer, tm, tn):
    rows, k = hn.shape
    nt = FFN_HIDDEN // tn
    return pl.pallas_call(
        _swiglu_up_kernel,
        out_shape=jax.ShapeDtypeStruct((rows, FFN_HIDDEN), BF16),
        grid=(nt, rows // tm),
        in_specs=[pl.BlockSpec((tm, k), lambda n, m: (m, 0)),
                  pl.BlockSpec((None, k, tn), lambda n, m: (layer, 0, n)),
                  pl.BlockSpec((None, k, tn), lambda n, m: (layer, 0, nt + n))],
        out_specs=pl.BlockSpec((tm, tn), lambda n, m: (m, n)),
        scratch_shapes=[pltpu.VMEM((k, tn), BF16), pltpu.VMEM((k, tn), BF16)],
        compiler_params=_cparams(("arbitrary", "arbitrary")),
        name="swiglu_up",
    )(hn, w_gu, w_gu)


def _down_proj_kernel(a_ref, w_ref, h_ref, wpost_ref, wnext_ref, hnew_ref, hn_ref=None):
    for rs in _row_subtiles(a_ref.shape[0]):
        acc = jnp.dot(a_ref[rs, :], w_ref[...], preferred_element_type=F32)
        _residual_epilogue(acc, h_ref.at[rs, :], wpost_ref, wnext_ref, hnew_ref.at[rs, :],
                           None if hn_ref is None else hn_ref.at[rs, :])


def _down_proj(a, w_down, h, w_post, w_next, rows_out, tm, emit_hn):
    row_spec = lambda w: pl.BlockSpec((tm, w), lambda m: (m, 0))
    out_shape = [jax.ShapeDtypeStruct((rows_out, D_MODEL), F32)]
    out_specs = [row_spec(D_MODEL)]
    if emit_hn:
        out_shape.append(jax.ShapeDtypeStruct((rows_out, D_MODEL), BF16))
        out_specs.append(row_spec(D_MODEL))
    res = pl.pallas_call(
        _down_proj_kernel,
        out_shape=tuple(out_shape),
        grid=(rows_out // tm,),
        in_specs=[row_spec(FFN_HIDDEN), _resident((FFN_HIDDEN, D_MODEL), (0, 0)), row_spec(D_MODEL),
                  _pspec((1, D_MODEL)), _pspec((1, D_MODEL))],
        out_specs=tuple(out_specs),
        compiler_params=_cparams(("arbitrary",)),
        name="down_proj",
    )(a, w_down, h, w_post, w_next)
    return res if emit_hn else (res[0], None)


def _row_block(i, nreal):
    return (i + nreal) % (nreal + 1)


def _zspec(col_block, nreal):
    return pl.BlockSpec((BLK, WIDTH), lambda i: (_row_block(i, nreal), col_block))


def _pspec(shape):
    nd = len(shape)
    return pl.BlockSpec(shape, lambda *_: (0,) * nd)


def _resident(shape, index):
    return pl.BlockSpec(shape, lambda *_: index, pipeline_mode=pl.Buffered(1))


def _lru_kernel(u_ref, gate_ref, cw_ref, cb_ref, wa_ref, ba_ref, wx_ref, bx_ref, lam_ref, y_ref,
                ubuf, hst, hist_meta, h_meta, *, nblk):
    i = pl.program_id(0)
    is_meta = i == 0

    @pl.when(is_meta)
    def _():
        ubuf[0:8, :] = jnp.zeros((8, WIDTH), F32)
        hst[...] = jnp.zeros_like(hst)

    @pl.when(jnp.logical_and(i >= 1, (i - 1) % nblk == 0))
    def _():
        ubuf[0:8, :] = hist_meta[...]
        hst[...] = h_meta[...]

    u = u_ref[...]
    ubuf[8:8 + BLK, :] = u
    cw = cw_ref[...]
    xc = (cb_ref[...] + cw[3:4] * u + cw[2:3] * ubuf[7:7 + BLK, :]
          + cw[1:2] * ubuf[6:6 + BLK, :] + cw[0:1] * ubuf[5:5 + BLK, :])
    xb = xc.astype(BF16)
    ra, ia = [], []
    for hd in range(HEADS):
        sl = slice(hd * HEAD_W, (hd + 1) * HEAD_W)
        ra.append(jnp.dot(xb[:, sl], wa_ref[hd].astype(BF16), preferred_element_type=F32))
        ia.append(jnp.dot(xb[:, sl], wx_ref[hd].astype(BF16), preferred_element_type=F32))
    r = _sigmoid(jnp.concatenate(ra, axis=1) + ba_ref[...])
    ig = _sigmoid(jnp.concatenate(ia, axis=1) + bx_ref[...])
    lam = lam_ref[...]
    softplus_neg_lam = jnp.maximum(-lam, 0.0) + jnp.log1p(jnp.exp(-jnp.abs(lam)))
    log_a = -LRU_C * r * softplus_neg_lam
    a = jnp.exp(log_a)
    bb = jnp.sqrt(-jnp.tanh(log_a) * (a * a + 1.0)) * (ig * xc)
    row = lax.broadcasted_iota(jnp.int32, (BLK, 1), 0)
    bb = jnp.where(row >= PAD_ROWS * is_meta.astype(jnp.int32), bb, 0.0)

    acum, bcum = a, bb
    s = 1
    while s < BLK:
        a_sh = _shift_rows(acum, s, 1.0, row)
        b_sh = _shift_rows(bcum, s, 0.0, row)
        bcum = acum * b_sh + bcum
        acum = acum * a_sh
        s *= 2
    h = acum * hst[0:1, :] + bcum
    y_ref[...] = (h * _gelu_tanh(gate_ref[...])).astype(y_ref.dtype)

    hist = u[BLK - 8:BLK, :]
    hlast = jnp.broadcast_to(h[BLK - 1:BLK, :], (8, WIDTH))
    ubuf[0:8, :] = hist
    hst[...] = hlast

    @pl.when(is_meta)
    def _():
        hist_meta[...] = hist
        h_meta[...] = hlast


POOL_HIST = 16


def _pool_kernel(u_ref, pw_ref, ps_ref, y_ref, ubuf, hist_meta, *, nblk):
    i = pl.program_id(0)
    is_meta = i == 0

    @pl.when(is_meta)
    def _():
        ubuf[0:POOL_HIST, :] = jnp.zeros((POOL_HIST, WIDTH), F32)

    @pl.when(jnp.logical_and(i >= 1, (i - 1) % nblk == 0))
    def _():
        ubuf[0:POOL_HIST, :] = hist_meta[...]

    u = u_ref[...]
    ubuf[POOL_HIST:POOL_HIST + BLK, :] = u
    row = lax.broadcasted_iota(jnp.int32, (BLK, 1), 0)
    meta_i = is_meta.astype(jnp.int32)
    pos1 = row + 1 - PAD_ROWS * meta_i + 2 * POOL_HIST * (1 - meta_i)
    outs = []
    for g, win in enumerate(POOL_WINDOWS):
        sl = slice(g * HEAD_W, (g + 1) * HEAD_W)
        acc = u[:, sl]
        for d in range(1, win):
            acc = acc + ubuf[POOL_HIST - d:POOL_HIST - d + BLK, sl]
        count = jnp.clip(pos1, 1, win).astype(F32)
        pooled = acc / count - u[:, sl]
        outs.append(jnp.dot(pooled.astype(BF16), pw_ref[g].astype(BF16), preferred_element_type=F32))
    y_ref[...] = (jnp.concatenate(outs, axis=1) * ps_ref[...]).astype(y_ref.dtype)

    hist = u[BLK - POOL_HIST:BLK, :]
    ubuf[0:POOL_HIST, :] = hist

    @pl.when(is_meta)
    def _():
        hist_meta[...] = hist


def _lru_pool_kernel(u_ref, gate_ref, cw_ref, cb_ref, wa_ref, ba_ref, wx_ref, bx_ref, lam_ref,
                     pu_ref, pw_ref, ps_ref, ya_ref, yb_ref,
                     ubuf, hst, hist_meta, h_meta, pbuf, phist_meta, *, nblk):
    _lru_kernel(u_ref, gate_ref, cw_ref, cb_ref, wa_ref, ba_ref, wx_ref, bx_ref, lam_ref, ya_ref,
                ubuf, hst, hist_meta, h_meta, nblk=nblk)
    _pool_kernel(pu_ref, pw_ref, ps_ref, yb_ref, pbuf, phist_meta, nblk=nblk)


def _lru_pool(z, cw, cb, wa, ba, wx, bx, lam, pw, ps, nreal, nblk):
    rows = z.shape[0]
    out_spec = pl.BlockSpec((BLK, WIDTH), lambda i: (_row_block(i, nreal), 0))
    return pl.pallas_call(
        functools.partial(_lru_pool_kernel, nblk=nblk),
        out_shape=(jax.ShapeDtypeStruct((rows, WIDTH), BF16), jax.ShapeDtypeStruct((rows, WIDTH), BF16)),
        grid=(nreal + 1,),
        in_specs=[_zspec(0, nreal), _zspec(1, nreal),
                  _pspec((4, WIDTH)), _pspec((1, WIDTH)),
                  _pspec((HEADS, HEAD_W, HEAD_W)), _pspec((1, WIDTH)),
                  _pspec((HEADS, HEAD_W, HEAD_W)), _pspec((1, WIDTH)), _pspec((1, WIDTH)),
                  _zspec(2, nreal), _pspec((4, HEAD_W, HEAD_W)), _pspec((1, WIDTH))],
        out_specs=(out_spec, out_spec),
        scratch_shapes=[pltpu.VMEM((8 + BLK, WIDTH), F32), pltpu.VMEM((8, WIDTH), F32),
                        pltpu.VMEM((8, WIDTH), F32), pltpu.VMEM((8, WIDTH), F32),
                        pltpu.VMEM((POOL_HIST + BLK, WIDTH), F32), pltpu.VMEM((POOL_HIST, WIDTH), F32)],
        compiler_params=_cparams(("arbitrary",)),
        name="lru_pool",
    )(z, z, cw, cb, wa, ba, wx, bx, lam, z, pw, ps)


def _hgrn_chunk(q, z, v, lbh, state_t, valid, ones_bf):
    ls = _log_sigmoid(z)
    x1 = jnp.log(lbh)
    x2 = jnp.log1p(-lbh) + ls
    mx = jnp.maximum(x1, x2)
    g = mx + jnp.log(1.0 + jnp.exp(-jnp.abs(x1 - x2)))
    k = (1.0 - lbh) * _sigmoid(-z)
    if valid is not None:
        g = jnp.where(valid, g, 0.0)
    row = lax.broadcasted_iota(jnp.int32, (CHUNK, 1), 0)
    b = g * LOG2E
    s = 1
    while s < CHUNK:
        b = b + _shift_rows(b, s, 0.0, row)
        s *= 2
    b_last = b[CHUNK - 1:CHUNK, :]

    qe = (q * jnp.exp2(b)).astype(BF16)
    o = lax.dot_general(qe, state_t.astype(BF16), (((1,), (1,)), ((), ())), preferred_element_type=F32)

    col = lax.broadcasted_iota(jnp.int32, (SUB, CHUNK), 1)
    rsub = lax.broadcasted_iota(jnp.int32, (SUB, CHUNK), 0)
    lane = lax.broadcasted_iota(jnp.int32, (SUB, HEAD_W), 1)
    s_rows = []
    for blk in range(CHUNK // SUB):
        lo = blk * SUB
        bi = b[lo:lo + SUB, :]
        qi = q[lo:lo + SUB, :]
        ki = k[lo:lo + SUB, :]
        parts = []
        for sr in range(SUB):
            t0 = 0 if sr < SUB // 2 else SUB // 2
            e = jnp.exp2(jnp.minimum(bi[t0:, :] - bi[sr:sr + 1, :], 0.0))
            parts.append(qi[t0:, :] * e * ki[sr:sr + 1, :])
        m3 = jnp.concatenate(parts, axis=0).astype(BF16)
        red = jnp.dot(m3, ones_bf, preferred_element_type=F32)
        halves = [jnp.zeros((SUB // 2, HEAD_W), F32), jnp.zeros((SUB // 2, HEAD_W), F32)]
        r0 = 0
        for sr in range(SUB):
            for hf in range(0 if sr < SUB // 2 else 1, 2):
                halves[hf] = jnp.where(lane[:SUB // 2] == lo + sr, red[r0:r0 + SUB // 2, :], halves[hf])
                r0 += SUB // 2
        diag = jnp.concatenate(halves, axis=0)[:, :CHUNK]
        s_blk = jnp.where(jnp.logical_and(col >= lo, col - lo <= rsub), diag, 0.0)
        if blk > 0:
            b0 = b[lo - 1:lo, :]
            kt = (k * jnp.exp2(jnp.minimum(b0 - b, 0.0))).astype(BF16)
            qd = (qi * jnp.exp2(bi - b0)).astype(BF16)
            off = lax.dot_general(qd, kt, (((1,), (1,)), ((), ())), preferred_element_type=F32)
            s_blk = jnp.where(col < lo, off, s_blk)
        s_rows.append(s_blk)
    scores = jnp.concatenate(s_rows, axis=0).astype(BF16)
    vb = v.astype(BF16)
    o = o + jnp.dot(scores, vb, preferred_element_type=F32)

    kd = (k * jnp.exp2(b_last - b)).astype(BF16)
    upd = lax.dot_general(vb, kd, (((0,), (0,)), ((), ())), preferred_element_type=F32)
    new_state_t = state_t * jnp.exp2(b_last) + upd
    return o, new_state_t


def _hgrn_kernel(q_ref, f_ref, v_ref, og_ref, lb_ref, nw_ref, y_ref, state, state_meta, *, nblk):
    i = pl.program_id(0)
    is_meta = i == 0

    @pl.when(is_meta)
    def _():
        state[...] = jnp.zeros_like(state)

    @pl.when(jnp.logical_and(i >= 1, (i - 1) % nblk == 0))
    def _():
        state[...] = state_meta[...]

    ones_bf = jnp.ones((HEAD_W, HEAD_W), BF16)
    nw = nw_ref[...]
    for hd in range(HEADS):
        sl = slice(hd * HEAD_W, (hd + 1) * HEAD_W)
        lbh = lb_ref[:, sl]
        st = state[hd]
        for c in range(BLK // CHUNK):
            rs = slice(c * CHUNK, (c + 1) * CHUNK)
            rowg = lax.broadcasted_iota(jnp.int32, (CHUNK, 1), 0) + c * CHUNK
            valid = rowg >= PAD_ROWS * is_meta.astype(jnp.int32)
            o, st = _hgrn_chunk(q_ref[rs, sl], f_ref[rs, sl], v_ref[rs, sl], lbh, st, valid, ones_bf)
            og = og_ref[rs, sl]
            y_ref[rs, sl] = (_rms(o, nw) * (og * _sigmoid(og))).astype(y_ref.dtype)
        state[hd] = st

    @pl.when(is_meta)
    def _():
        state_meta[...] = state[...]


def _hgrn(z_a, z_b, lb, nw, nreal, nblk):
    rows = z_a.shape[0]
    return pl.pallas_call(
        functools.partial(_hgrn_kernel, nblk=nblk),
        out_shape=jax.ShapeDtypeStruct((rows, WIDTH), BF16),
        grid=(nreal + 1,),
        in_specs=[_zspec(3, nreal), _zspec(0, nreal), _zspec(1, nreal), _zspec(2, nreal),
                  _pspec((1, WIDTH)), _pspec((1, HEAD_W))],
        out_specs=pl.BlockSpec((BLK, WIDTH), lambda i: (_row_block(i, nreal), 0)),
        scratch_shapes=[pltpu.VMEM((HEADS, HEAD_W, HEAD_W), F32), pltpu.VMEM((HEADS, HEAD_W, HEAD_W), F32)],
        compiler_params=_cparams(("arbitrary",)),
        name="hgrn2",
    )(z_a, z_b, z_b, z_b, lb, nw)


def _attn_lambda(lamv_ref, lambda_init):
    lv = lamv_ref[...]
    return (jnp.exp(jnp.sum(lv[0:1] * lv[1:2], axis=-1, keepdims=True))
            - jnp.exp(jnp.sum(lv[2:3] * lv[3:4], axis=-1, keepdims=True)) + lambda_init)


def _scaled_q(q_ref):
    qf = (q_ref[...].astype(F32) * (DIFF_HEAD_DIM ** -0.5)).astype(BF16)
    return qf[:, :DIFF_HEAD_DIM], qf[:, DIFF_HEAD_DIM:]


def _fold8(x, op):
    r, c = x.shape
    return op(x.reshape(r // 8, 8, c), axis=0)


def _attn_kernel(relb_ref, q_ref, k_ref, v_ref, km_ref, vm_ref, bias_ref, diag_ref, lamv_ref, sw_ref, y_ref,
                 s_sc, vt_sc, m_sc, l_sc, acc_sc, *, nqb, nblk, lambda_init):
    hd = pl.program_id(0)
    jq = pl.program_id(1) % nqb
    far = relb_ref[REL_BUCKETS - 1, hd] * LOG2E
    t1 = bias_ref[1] * LOG2E
    slot_diag, slot_meta = nqb - 1, nqb

    @pl.when(jq == 0)
    def _():
        for t in range(nblk):
            vt_sc[t // NSUB, :, (t % NSUB) * BLK:(t % NSUB + 1) * BLK] = (
                v_ref[t * BLK:(t + 1) * BLK, :].astype(F32).T.astype(BF16))

    qt = (q_ref[...].astype(F32) * (DIFF_HEAD_DIM ** -0.5)).T
    zero = jnp.zeros((DIFF_HEAD_DIM, QB), F32)
    qtp = (jnp.concatenate([qt[:DIFF_HEAD_DIM], zero], axis=0).astype(BF16),
           jnp.concatenate([zero, qt[DIFF_HEAD_DIM:]], axis=0).astype(BF16))

    def key_chunk(c):
        return k_ref[pl.ds(pl.multiple_of(c * QB, QB), QB), :]

    def score_chunk(k_rows, slot, add_bias, r0=0, c0=0):
        r = k_rows.shape[0]
        for mp in range(2):
            s = add_bias(jnp.dot(k_rows, qtp[mp][:, c0:], preferred_element_type=F32) * LOG2E)
            s_sc[mp, slot, r0:r0 + r, c0:] = s
            m_sc[mp, :, c0:] = jnp.maximum(m_sc[mp, :, c0:], _fold8(s, jnp.max))

    def value_chunk(vt_cols, slot, m8, r0=0, c0=0):
        r = vt_cols.shape[1]
        for mp in range(2):
            p = jnp.exp2(s_sc[mp, slot, r0:r0 + r, c0:].reshape(r // 8, 8, QB - c0) - m8[mp][None, :, c0:])
            l_sc[mp, :, c0:] += jnp.sum(p, axis=0)
            acc_sc[mp, :, c0:] += jnp.dot(vt_cols, p.reshape(r, QB - c0).astype(BF16),
                                         preferred_element_type=F32)

    def diag_tiles(fn):
        for kb in range(NSUB):
            fn(kb, kb * BLK)

    def walk(fn_far, fn_meta, fn_diag):
        def body(c2, carry):
            fn_far(2 * c2)
            fn_far(2 * c2 + 1)
            return carry

        lax.fori_loop(0, jq // 2, body, 0)

        @pl.when(jq % 2 == 1)
        def _():
            fn_far(jq - 1)

        fn_meta()
        fn_diag()

    near = t1 - far
    first = jnp.full((1, BLK), jq, jnp.int32) == 0

    def meta_bias(s):
        head = s[:, 0:BLK] + (far + jnp.where(first, near[PAD_ROWS:, :], 0.0))
        return jnp.concatenate([head, s[:, BLK:] + far], axis=1)

    def meta_values(m8):
        v16 = vm_ref[PAD_ROWS:BLK, :]
        for mp in range(2):
            p = jnp.exp2(s_sc[mp, slot_meta, 0:N_META, :].reshape(N_META // 8, 8, QB) - m8[mp][None])
            l_sc[mp] += jnp.sum(p, axis=0)
            acc_sc[mp] += lax.dot_general(v16, p.reshape(N_META, QB).astype(BF16), (((0,), (0,)), ((), ())),
                                          preferred_element_type=F32)

    def far_bias(c):
        last = jnp.full((1, BLK), c, jnp.int32) == jq - 1

        def add(s):
            corner = s[QB - BLK:, 0:BLK] + (far + jnp.where(last, near, 0.0))
            bottom = jnp.concatenate([corner, s[QB - BLK:, BLK:] + far], axis=1)
            return jnp.concatenate([s[:QB - BLK, :] + far, bottom], axis=0)

        return add

    m_sc[...] = jnp.full_like(m_sc, NEG)
    walk(lambda c: score_chunk(key_chunk(c), c, far_bias(c)),
         lambda: score_chunk(km_ref[PAD_ROWS:BLK, :], slot_meta, meta_bias),
         lambda: diag_tiles(lambda kb, c0: score_chunk(
             k_ref[pl.ds(pl.multiple_of(jq * QB + c0, BLK), BLK), :], slot_diag,
             lambda s: s + diag_ref[c0:c0 + BLK, c0:], r0=c0, c0=c0)))

    m8 = [jnp.broadcast_to(jnp.max(m_sc[mp], axis=0, keepdims=True), (8, QB)) for mp in range(2)]
    l_sc[...] = jnp.zeros_like(l_sc)
    acc_sc[...] = jnp.zeros_like(acc_sc)

    walk(lambda c: value_chunk(vt_sc[c], c, m8),
         lambda: meta_values(m8),
         lambda: diag_tiles(lambda kb, c0: value_chunk(
             vt_sc[jq, :, c0:c0 + BLK], slot_diag, m8, r0=c0, c0=c0)))

    lam = _attn_lambda(lamv_ref, lambda_init)
    l0 = jnp.sum(l_sc[0], axis=0, keepdims=True)
    l1 = jnp.sum(l_sc[1], axis=0, keepdims=True)
    ot = acc_sc[0] / l0 - lam * (acc_sc[1] / l1)
    ot = ot * lax.rsqrt(jnp.mean(ot * ot, axis=0, keepdims=True) + EPS)
    y_ref[...] = (ot.T * sw_ref[...] * (1.0 - lambda_init)).astype(y_ref.dtype)


def _attn_meta_kernel(q_ref, k_ref, v_ref, bias_ref, lamv_ref, sw_ref, y_ref, *, lambda_init):
    qs = _scaled_q(q_ref)
    kb = k_ref[...]
    colmask = jnp.where(lax.broadcasted_iota(jnp.int32, (1, BLK), 1) >= PAD_ROWS, 0.0, NEG)
    outs = []
    bias = bias_ref[0].T + colmask
    for mp in range(2):
        s = lax.dot_general(qs[mp], kb[:, mp * DIFF_HEAD_DIM:(mp + 1) * DIFF_HEAD_DIM],
                            (((1,), (1,)), ((), ())), preferred_element_type=F32) + bias
        p = jnp.exp(s - jnp.max(s, axis=-1, keepdims=True))
        acc = jnp.dot(p.astype(BF16), v_ref[...], preferred_element_type=F32)
        outs.append(acc / jnp.sum(p, axis=-1, keepdims=True))
    o = outs[0] - _attn_lambda(lamv_ref, lambda_init) * outs[1]
    o = _rms(o, sw_ref[...]) * (1.0 - lambda_init)
    row = lax.broadcasted_iota(jnp.int32, (BLK, 1), 0)
    y_ref[...] = jnp.where(row >= PAD_ROWS, o, 0.0).astype(y_ref.dtype)


def _attn(zd, rel_bias, bias_tiles, bias_diag, lam_vec, subln_w, nreal, nblk, lambda_init):
    nqb = nblk * BLK // QB
    nq_total = nreal * BLK // QB
    seq = nblk * BLK
    kcol, vcol = HEADS, 2 * HEADS
    y_main = pl.pallas_call(
        functools.partial(_attn_kernel, nqb=nqb, nblk=nblk, lambda_init=lambda_init),
        out_shape=jax.ShapeDtypeStruct((nreal * BLK, WIDTH), BF16),
        grid=(HEADS, nq_total),
        in_specs=[pl.BlockSpec(memory_space=pltpu.SMEM),
                  pl.BlockSpec((QB, HEAD_W), lambda h, i: (i, h)),
                  pl.BlockSpec((seq, HEAD_W), lambda h, i: (i // nqb, kcol + h)),
                  pl.BlockSpec((seq, HEAD_W), lambda h, i: (i // nqb, vcol + h)),
                  pl.BlockSpec((BLK, HEAD_W), lambda h, i: (nreal, kcol + h)),
                  pl.BlockSpec((BLK, HEAD_W), lambda h, i: (nreal, vcol + h)),
                  pl.BlockSpec((None, 2, BLK, BLK), lambda h, i: (h, 0, 0, 0)),
                  pl.BlockSpec((None, QB, QB), lambda h, i: (h, 0, 0)),
                  pl.BlockSpec((4, DIFF_HEAD_DIM), lambda h, i: (0, 0)),
                  pl.BlockSpec((1, HEAD_W), lambda h, i: (0, 0))],
        out_specs=pl.BlockSpec((QB, HEAD_W), lambda h, i: (i, h)),
        scratch_shapes=[pltpu.VMEM((2, nqb + 1, QB, QB), F32),
                        pltpu.VMEM((nqb, HEAD_W, QB), BF16),
                        pltpu.VMEM((2, 8, QB), F32),
                        pltpu.VMEM((2, 8, QB), F32),
                        pltpu.VMEM((2, HEAD_W, QB), F32)],
        compiler_params=_cparams(("arbitrary", "arbitrary")),
        name="diff_attn",
    )(rel_bias, zd, zd, zd, zd, zd, bias_tiles, bias_diag, lam_vec, subln_w)
    y_meta = pl.pallas_call(
        functools.partial(_attn_meta_kernel, lambda_init=lambda_init),
        out_shape=jax.ShapeDtypeStruct((BLK, WIDTH), BF16),
        grid=(HEADS,),
        in_specs=[pl.BlockSpec((BLK, HEAD_W), lambda h: (nreal, h)),
                  pl.BlockSpec((BLK, HEAD_W), lambda h: (nreal, kcol + h)),
                  pl.BlockSpec((BLK, HEAD_W), lambda h: (nreal, vcol + h)),
                  pl.BlockSpec((None, 2, BLK, BLK), lambda h: (h, 0, 0, 0)),
                  pl.BlockSpec((4, DIFF_HEAD_DIM), lambda h: (0, 0)),
                  pl.BlockSpec((1, HEAD_W), lambda h: (0, 0))],
        out_specs=pl.BlockSpec((BLK, HEAD_W), lambda h: (0, h)),
        compiler_params=_cparams(("arbitrary",)),
        name="diff_attn_meta",
    )(zd, zd, zd, bias_tiles, lam_vec, subln_w)
    return jnp.concatenate([y_main, y_meta], axis=0)


def _largest_tile(rows, cap, align=16):
    best = align
    for t in range(align, cap + 1, align):
        if rows % t == 0:
            best = t
    return best


def _forward(x, meta_tokens, rel_bias, hgrn_lower_bounds, norm_mix_pre, norm_mix_post, norm_ffn_pre,
             norm_ffn_post, w_in, lru_conv_w, lru_conv_b, lru_w_a, lru_b_a, lru_w_x, lru_b_x, lru_lambda,
             pool_w, pool_scale, hgrn_norm, diff_lambda, diff_subln, w_branch, w_out, ffn_w_gu, ffn_w_down):
    bsz, seq, _ = x.shape
    nblk = seq // BLK
    nreal = bsz * nblk
    rows = (nreal + 1) * BLK
    rows_real = nreal * BLK
    tm_big = _largest_tile(rows, 832)
    tm_epi = _largest_tile(rows, 640)
    tm_down = _largest_tile(rows, 320)
    tm_last = _largest_tile(rows_real, 256)

    def vec(a):
        return a.reshape(1, -1)

    lbs, bias_tiles, bias_diag = _prologue(hgrn_lower_bounds, rel_bias)
    h, hn = _embed(x.reshape(rows_real, D_MODEL), meta_tokens, vec(norm_mix_pre[0]), nreal)

    for layer in range(DEPTH):
        lambda_init = 0.8 - 0.6 * math.exp(-0.3 * layer)
        z = _mixer_in_proj(hn, w_in, layer, 0, 4 * WIDTH, F32, tm_big, 1024, 1024)
        z_b = _mixer_in_proj(hn, w_in, layer, 4 * WIDTH, 3 * WIDTH, F32, tm_big, 3 * WIDTH, WIDTH)
        zd = _mixer_in_proj(hn, w_in, layer, 7 * WIDTH, 3 * WIDTH, BF16, tm_big, 3 * WIDTH, WIDTH)
        y_a, y_b = _lru_pool(z, lru_conv_w[layer], vec(lru_conv_b[layer]), lru_w_a[layer], vec(lru_b_a[layer]),
                             lru_w_x[layer], vec(lru_b_x[layer]), vec(lru_lambda[layer]),
                             pool_w[layer], vec(pool_scale[layer]), nreal, nblk)
        y_c = _hgrn(z, z_b, lbs[layer:layer + 1], vec(hgrn_norm[layer]), nreal, nblk)
        y_d = _attn(zd, rel_bias, bias_tiles, bias_diag, diff_lambda[layer], vec(diff_subln[layer]),
                    nreal, nblk, lambda_init)
        merged = _gate_merge(hn, w_in, (y_a, y_b, y_c, y_d), w_branch, layer, tm_big, 256)
        h, hn = _out_proj(merged, _cast_layer_bf16(w_out, layer, 512), h, vec(norm_mix_post[layer]),
                          vec(norm_ffn_pre[layer]), tm_epi)
        a = _swiglu_up(hn, ffn_w_gu, layer, tm_big, 512)
        last = layer == DEPTH - 1
        w_next = vec(norm_mix_pre[layer + 1]) if not last else vec(norm_mix_pre[layer])
        h, hn = _down_proj(a, _cast_layer_bf16(ffn_w_down, layer, 512), h, vec(norm_ffn_post[layer]), w_next,
                           rows_real if last else rows, tm_last if last else tm_down, not last)
    return h.reshape(bsz, seq, D_MODEL)


def kernel(x, meta_tokens, rel_bias, hgrn_lower_bounds, norm_mix_pre, norm_mix_post, norm_ffn_pre, norm_ffn_post, w_in, lru_conv_w, lru_conv_b, lru_w_a, lru_b_a, lru_w_x, lru_b_x, lru_lambda, pool_w, pool_scale, hgrn_norm, diff_lambda, diff_subln, w_branch, w_out, ffn_w_gu, ffn_w_down):
    return _forward(x, meta_tokens, rel_bias, hgrn_lower_bounds, norm_mix_pre, norm_mix_post, norm_ffn_pre,
                    norm_ffn_post, w_in, lru_conv_w, lru_conv_b, lru_w_a, lru_b_a, lru_w_x, lru_b_x, lru_lambda,
                    pool_w, pool_scale, hgrn_norm, diff_lambda, diff_subln, w_branch, w_out, ffn_w_gu, ffn_w_down)
```

```python
import functools
import math

import numpy as np
import jax
import jax.numpy as jnp
from jax import lax
from jax.experimental import pallas as pl
from jax.experimental.pallas import tpu as pltpu

F32 = jnp.float32
BF16 = jnp.bfloat16

D_MODEL = 2048
SEQ = 2048
DEPTH = 2
N_META = 16
BLK = 128
PAD_ROWS = BLK - N_META
QB = 512
NSUB = QB // BLK
WIDTH = 512
HEADS = 4
HEAD_W = 128
CHUNK = 64
SUB = 16
LRU_C = 8.0
POOL_WINDOWS = (2, 4, 8, 16)
DIFF_HEAD_DIM = 64
REL_BUCKETS = 32
REL_MAX_DIST = 128
FFN_HIDDEN = 5632
MIX_COLS = 10 * WIDTH
NEG = -1e30
EPS = 1e-6
LOG2E = math.log2(math.e)
VMEM_LIMIT = 56 * 1024 * 1024


def _cparams(sem):
    return pltpu.CompilerParams(dimension_semantics=sem, vmem_limit_bytes=VMEM_LIMIT)


def _rms(x, w):
    return x * lax.rsqrt(jnp.mean(x * x, axis=-1, keepdims=True) + EPS) * w


def _log_sigmoid(z):
    return -(jnp.maximum(-z, 0.0) + jnp.log(1.0 + jnp.exp(-jnp.abs(z))))


def _sigmoid(z):
    return 1.0 / (1.0 + jnp.exp(-z))


def _gelu_tanh(x):
    c = math.sqrt(2.0 / math.pi)
    return 0.5 * x * (1.0 + jnp.tanh(c * (x + 0.044715 * (x * x * x))))


def _shift_rows(x, s, fill, row):
    return jnp.where(row >= s, pltpu.roll(x, s, axis=0), fill)


def _bucket_tiles():
    r = np.arange(BLK)[None, :]
    c = np.arange(BLK)[:, None]
    max_exact = REL_BUCKETS // 2

    def bucket(n):
        nf = np.maximum(n, 1).astype(np.float32)
        large = max_exact + (np.log(nf / np.float32(max_exact)) / np.float32(math.log(REL_MAX_DIST / max_exact))
                             * np.float32(REL_BUCKETS - max_exact)).astype(np.int32)
        large = np.minimum(large, REL_BUCKETS - 1)
        return np.where(n < max_exact, n, large).astype(np.int32)

    d0 = r - c
    t0 = np.where(d0 >= 0, bucket(np.maximum(d0, 0)), -1)
    t1 = bucket(BLK + r - c)
    return np.stack([t0, t1]).astype(np.int32)


def _prologue_kernel(lbraw_ref, relb_ref, idx_ref, lb_ref, bias_ref, diag_ref):
    raw = lbraw_ref[...]
    mx = jnp.max(raw, axis=0, keepdims=True)
    e = jnp.exp(raw - mx)
    sm = e / jnp.sum(e, axis=0, keepdims=True)
    cum = sm[0:1]
    lb_ref[0:1, :] = cum - sm[0:1]
    for l in range(1, DEPTH):
        cum = cum + sm[l:l + 1]
        lb_ref[l:l + 1, :] = cum - sm[0:1]
    for t in range(2):
        idx = idx_ref[t]
        for hd in range(HEADS):
            acc = jnp.zeros((BLK, BLK), F32)
            for bk in range(REL_BUCKETS):
                acc = jnp.where(idx == bk, relb_ref[bk, hd], acc)
            bias_ref[hd, t] = jnp.where(idx < 0, NEG, acc)
    for hd in range(HEADS):
        far = relb_ref[REL_BUCKETS - 1, hd]
        for kb in range(NSUB):
            for qb in range(NSUB):
                delta = qb - kb
                if delta == 0:
                    blk = bias_ref[hd, 0]
                elif delta == 1:
                    blk = bias_ref[hd, 1]
                else:
                    blk = jnp.full((BLK, BLK), far if delta > 1 else NEG, F32)
                diag_ref[hd, kb * BLK:(kb + 1) * BLK, qb * BLK:(qb + 1) * BLK] = blk * LOG2E


def _prologue(hgrn_lower_bounds, rel_bias):
    idx = jnp.asarray(_bucket_tiles())
    vmem = pl.BlockSpec(memory_space=pltpu.VMEM)
    return pl.pallas_call(
        _prologue_kernel,
        out_shape=(jax.ShapeDtypeStruct((DEPTH, WIDTH), F32),
                   jax.ShapeDtypeStruct((HEADS, 2, BLK, BLK), F32),
                   jax.ShapeDtypeStruct((HEADS, QB, QB), F32)),
        in_specs=[vmem, pl.BlockSpec(memory_space=pltpu.SMEM), vmem],
        out_specs=(vmem, vmem, vmem),
        name="prologue",
    )(hgrn_lower_bounds, rel_bias, idx)


EMBED_ROWS = 512


def _embed_kernel(x_ref, meta_ref, w_ref, h_ref, hn_ref, *, nfull):
    i = pl.program_id(0)

    @pl.when(i < nfull)
    def _():
        h_ref[...] = x_ref[...]

    @pl.when(i == nfull)
    def _():
        h_ref[...] = jnp.zeros_like(h_ref)
        h_ref[PAD_ROWS:BLK, :] = meta_ref[...]

    hn_ref[...] = _rms(h_ref[...], w_ref[...]).astype(BF16)


def _embed(x2d, meta, w_pre, nreal):
    rows = (nreal + 1) * BLK
    nfull = nreal * BLK // EMBED_ROWS
    return pl.pallas_call(
        functools.partial(_embed_kernel, nfull=nfull),
        out_shape=(jax.ShapeDtypeStruct((rows, D_MODEL), F32),
                   jax.ShapeDtypeStruct((rows, D_MODEL), BF16)),
        grid=(nfull + 1,),
        in_specs=[pl.BlockSpec((EMBED_ROWS, D_MODEL), lambda i: (jnp.minimum(i, nfull - 1), 0)),
                  pl.BlockSpec((N_META, D_MODEL), lambda i: (0, 0)),
                  pl.BlockSpec((1, D_MODEL), lambda i: (0, 0))],
        out_specs=(pl.BlockSpec((EMBED_ROWS, D_MODEL), lambda i: (i, 0)),
                   pl.BlockSpec((EMBED_ROWS, D_MODEL), lambda i: (i, 0))),
        compiler_params=_cparams(("arbitrary",)),
        name="embed",
    )(x2d, meta, w_pre)


def _cast_tiles_once(pairs):
    @pl.when(pl.program_id(1) == 0)
    def _():
        for src, dst in pairs:
            dst[...] = src[...].astype(BF16)


def _matmul_kernel(x_ref, *rest):
    *w_refs, o_ref, wbf = rest
    wblk = w_refs[0].shape[-1]
    _cast_tiles_once([(w, wbf.at[:, j * wblk:(j + 1) * wblk]) for j, w in enumerate(w_refs)])
    o_ref[...] = jnp.dot(x_ref[...], wbf[...], preferred_element_type=F32).astype(o_ref.dtype)


def _mixer_in_proj(hn, w_in, layer, col0, ncols, out_dtype, tm, tn, wblk):
    rows, k = hn.shape
    nw = tn // wblk
    w_specs = [pl.BlockSpec((None, k, wblk), functools.partial(
        lambda n, m, j: (layer, 0, col0 // wblk + n * nw + j), j=j)) for j in range(nw)]
    return pl.pallas_call(
        _matmul_kernel,
        out_shape=jax.ShapeDtypeStruct((rows, ncols), out_dtype),
        grid=(ncols // tn, rows // tm),
        in_specs=[pl.BlockSpec((tm, k), lambda n, m: (m, 0))] + w_specs,
        out_specs=pl.BlockSpec((tm, tn), lambda n, m: (m, n)),
        scratch_shapes=[pltpu.VMEM((k, tn), BF16)],
        compiler_params=_cparams(("arbitrary", "arbitrary")),
        name="mixer_in_proj",
    )(hn, *([w_in] * nw))


def _cast_kernel(w_ref, o_ref):
    o_ref[...] = w_ref[...].astype(BF16)


def _cast_layer_bf16(w, layer, tr):
    _, r, c = w.shape
    return pl.pallas_call(
        _cast_kernel,
        out_shape=jax.ShapeDtypeStruct((r, c), BF16),
        grid=(r // tr,),
        in_specs=[pl.BlockSpec((None, tr, c), lambda i: (layer, i, 0))],
        out_specs=pl.BlockSpec((tr, c), lambda i: (i, 0)),
        compiler_params=_cparams(("arbitrary",)),
        name="cast_bf16",
    )(w)


def _residual_epilogue(acc, h_ref, wpost_ref, wnext_ref, hnew_ref, hn_ref):
    h_new = h_ref[...] + _rms(acc, wpost_ref[...])
    hnew_ref[...] = h_new
    if hn_ref is not None:
        hn_ref[...] = _rms(h_new, wnext_ref[...]).astype(BF16)


EPI_ROWS = 160


def _row_subtiles(tm):
    sub = next(s for s in (EPI_ROWS, 128, 64, 32, 16) if tm % s == 0)
    return [slice(r, r + sub) for r in range(0, tm, sub)]


def _gate_merge_kernel(hn_ref, g0, g1, g2, g3, y0, y1, y2, y3, wb_ref, o_ref, gbf, wbbf):
    _cast_tiles_once([(g, gbf.at[k]) for k, g in enumerate((g0, g1, g2, g3))] + [(wb_ref, wbbf)])
    hn = hn_ref[...]
    acc = None
    for k, y_ref in enumerate((y0, y1, y2, y3)):
        gate = _sigmoid(jnp.dot(hn, gbf[k], preferred_element_type=F32))
        proj = jnp.dot(y_ref[...], wbbf[k], preferred_element_type=F32)
        acc = gate * proj if acc is None else acc + gate * proj
    o_ref[...] = acc.astype(o_ref.dtype)


def _gate_merge(hn, w_in, ys, w_branch, layer, tm, tn):
    rows, k = hn.shape
    gate_specs = [
        pl.BlockSpec((None, k, tn), functools.partial(
            lambda n, m, base: (layer, 0, base + n), base=(MIX_COLS + br * D_MODEL) // tn))
        for br in range(4)]
    y_specs = [pl.BlockSpec((tm, WIDTH), lambda n, m: (m, 0)) for _ in range(4)]
    return pl.pallas_call(
        _gate_merge_kernel,
        out_shape=jax.ShapeDtypeStruct((rows, D_MODEL), BF16),
        grid=(D_MODEL // tn, rows // tm),
        in_specs=[pl.BlockSpec((tm, k), lambda n, m: (m, 0))] + gate_specs + y_specs
                 + [pl.BlockSpec((None, 4, WIDTH, tn), lambda n, m: (layer, 0, 0, n))],
        out_specs=pl.BlockSpec((tm, tn), lambda n, m: (m, n)),
        scratch_shapes=[pltpu.VMEM((4, k, tn), BF16), pltpu.VMEM((4, WIDTH, tn), BF16)],
        compiler_params=_cparams(("arbitrary", "arbitrary")),
        name="gate_merge",
    )(hn, w_in, w_in, w_in, w_in, *ys, w_branch)


def _out_proj_kernel(x_ref, w_ref, h_ref, wpost_ref, wnext_ref, hnew_ref, hn_ref):
    for rs in _row_subtiles(x_ref.shape[0]):
        acc = jnp.dot(x_ref[rs, :], w_ref[...], preferred_element_type=F32)
        _residual_epilogue(acc, h_ref.at[rs, :], wpost_ref, wnext_ref, hnew_ref.at[rs, :], hn_ref.at[rs, :])


def _out_proj(merged, w_out, h, w_post, w_next, tm):
    rows = h.shape[0]
    row_spec = pl.BlockSpec((tm, D_MODEL), lambda m: (m, 0))
    return pl.pallas_call(
        _out_proj_kernel,
        out_shape=(jax.ShapeDtypeStruct((rows, D_MODEL), F32),
                   jax.ShapeDtypeStruct((rows, D_MODEL), BF16)),
        grid=(rows // tm,),
        in_specs=[row_spec, _resident((D_MODEL, D_MODEL), (0, 0)), row_spec,
                  _pspec((1, D_MODEL)), _pspec((1, D_MODEL))],
        out_specs=(row_spec, row_spec),
        compiler_params=_cparams(("arbitrary",)),
        name="out_proj",
    )(merged, w_out, h, w_post, w_next)


def _swiglu_up_kernel(x_ref, wg_ref, wu_ref, o_ref, wgbf, wubf):
    _cast_tiles_once([(wg_ref, wgbf), (wu_ref, wubf)])
    x = x_ref[...]
    g = jnp.dot(x, wgbf[...], preferred_element_type=F32)
    u = jnp.dot(x, wubf[...], preferred_element_type=F32)
    o_ref[...] = (g * _sigmoid(g) * u).astype(o_ref.dtype)


def _swiglu_up(hn, w_gu, layer, tm, tn):
    rows, k = hn.shape
    nt = FFN_HIDDEN // tn
    return pl.pallas_call(
        _swiglu_up_kernel,
        out_shape=jax.ShapeDtypeStruct((rows, FFN_HIDDEN), BF16),
        grid=(nt, rows // tm),
        in_specs=[pl.BlockSpec((tm, k), lambda n, m: (m, 0)),
                  pl.BlockSpec((None, k, tn), lambda n, m: (layer, 0, n)),
                  pl.BlockSpec((None, k, tn), lambda n, m: (layer, 0, nt + n))],
        out_specs=pl.BlockSpec((tm, tn), lambda n, m: (m, n)),
        scratch_shapes=[pltpu.VMEM((k, tn), BF16), pltpu.VMEM((k, tn), BF16)],
        compiler_params=_cparams(("arbitrary", "arbitrary")),
        name="swiglu_up",
    )(hn, w_gu, w_gu)


def _down_proj_kernel(a_ref, w_ref, h_ref, wpost_ref, wnext_ref, hnew_ref, hn_ref=None):
    for rs in _row_subtiles(a_ref.shape[0]):
        acc = jnp.dot(a_ref[rs, :], w_ref[...], preferred_element_type=F32)
        _residual_epilogue(acc, h_ref.at[rs, :], wpost_ref, wnext_ref, hnew_ref.at[rs, :],
                           None if hn_ref is None else hn_ref.at[rs, :])


def _down_proj(a, w_down, h, w_post, w_next, rows_out, tm, emit_hn):
    row_spec = lambda w: pl.BlockSpec((tm, w), lambda m: (m, 0))
    out_shape = [jax.ShapeDtypeStruct((rows_out, D_MODEL), F32)]
    out_specs = [row_spec(D_MODEL)]
    if emit_hn:
        out_shape.append(jax.ShapeDtypeStruct((rows_out, D_MODEL), BF16))
        out_specs.append(row_spec(D_MODEL))
    res = pl.pallas_call(
        _down_proj_kernel,
        out_shape=tuple(out_shape),
        grid=(rows_out // tm,),
        in_specs=[row_spec(FFN_HIDDEN), _resident((FFN_HIDDEN, D_MODEL), (0, 0)), row_spec(D_MODEL),
                  _pspec((1, D_MODEL)), _pspec((1, D_MODEL))],
        out_specs=tuple(out_specs),
        compiler_params=_cparams(("arbitrary",)),
        name="down_proj",
    )(a, w_down, h, w_post, w_next)
    return res if emit_hn else (res[0], None)


def _row_block(i, nreal):
    return (i + nreal) % (nreal + 1)


def _zspec(col_block, nreal):
    return pl.BlockSpec((BLK, WIDTH), lambda i: (_row_block(i, nreal), col_block))


def _pspec(shape):
    nd = len(shape)
    return pl.BlockSpec(shape, lambda *_: (0,) * nd)


def _resident(shape, index):
    return pl.BlockSpec(shape, lambda *_: index, pipeline_mode=pl.Buffered(1))


def _lru_kernel(u_ref, gate_ref, cw_ref, cb_ref, wa_ref, ba_ref, wx_ref, bx_ref, lam_ref, y_ref,
                ubuf, hst, hist_meta, h_meta, *, nblk):
    i = pl.program_id(0)
    is_meta = i == 0

    @pl.when(is_meta)
    def _():
        ubuf[0:8, :] = jnp.zeros((8, WIDTH), F32)
        hst[...] = jnp.zeros_like(hst)

    @pl.when(jnp.logical_and(i >= 1, (i - 1) % nblk == 0))
    def _():
        ubuf[0:8, :] = hist_meta[...]
        hst[...] = h_meta[...]

    u = u_ref[...]
    ubuf[8:8 + BLK, :] = u
    cw = cw_ref[...]
    xc = (cb_ref[...] + cw[3:4] * u + cw[2:3] * ubuf[7:7 + BLK, :]
          + cw[1:2] * ubuf[6:6 + BLK, :] + cw[0:1] * ubuf[5:5 + BLK, :])
    xb = xc.astype(BF16)
    ra, ia = [], []
    for hd in range(HEADS):
        sl = slice(hd * HEAD_W, (hd + 1) * HEAD_W)
        ra.append(jnp.dot(xb[:, sl], wa_ref[hd].astype(BF16), preferred_element_type=F32))
        ia.append(jnp.dot(xb[:, sl], wx_ref[hd].astype(BF16), preferred_element_type=F32))
    r = _sigmoid(jnp.concatenate(ra, axis=1) + ba_ref[...])
    ig = _sigmoid(jnp.concatenate(ia, axis=1) + bx_ref[...])
    lam = lam_ref[...]
    softplus_neg_lam = jnp.maximum(-lam, 0.0) + jnp.log1p(jnp.exp(-jnp.abs(lam)))
    log_a = -LRU_C * r * softplus_neg_lam
    a = jnp.exp(log_a)
    bb = jnp.sqrt(-jnp.tanh(log_a) * (a * a + 1.0)) * (ig * xc)
    row = lax.broadcasted_iota(jnp.int32, (BLK, 1), 0)
    bb = jnp.where(row >= PAD_ROWS * is_meta.astype(jnp.int32), bb, 0.0)

    acum, bcum = a, bb
    s = 1
    while s < BLK:
        a_sh = _shift_rows(acum, s, 1.0, row)
        b_sh = _shift_rows(bcum, s, 0.0, row)
        bcum = acum * b_sh + bcum
        acum = acum * a_sh
        s *= 2
    h = acum * hst[0:1, :] + bcum
    y_ref[...] = (h * _gelu_tanh(gate_ref[...])).astype(y_ref.dtype)

    hist = u[BLK - 8:BLK, :]
    hlast = jnp.broadcast_to(h[BLK - 1:BLK, :], (8, WIDTH))
    ubuf[0:8, :] = hist
    hst[...] = hlast

    @pl.when(is_meta)
    def _():
        hist_meta[...] = hist
        h_meta[...] = hlast


POOL_HIST = 16


def _pool_kernel(u_ref, pw_ref, ps_ref, y_ref, ubuf, hist_meta, *, nblk):
    i = pl.program_id(0)
    is_meta = i == 0

    @pl.when(is_meta)
    def _():
        ubuf[0:POOL_HIST, :] = jnp.zeros((POOL_HIST, WIDTH), F32)

    @pl.when(jnp.logical_and(i >= 1, (i - 1) % nblk == 0))
    def _():
        ubuf[0:POOL_HIST, :] = hist_meta[...]

    u = u_ref[...]
    ubuf[POOL_HIST:POOL_HIST + BLK, :] = u
    row = lax.broadcasted_iota(jnp.int32, (BLK, 1), 0)
    meta_i = is_meta.astype(jnp.int32)
    pos1 = row + 1 - PAD_ROWS * meta_i + 2 * POOL_HIST * (1 - meta_i)
    outs = []
    for g, win in enumerate(POOL_WINDOWS):
        sl = slice(g * HEAD_W, (g + 1) * HEAD_W)
        acc = u[:, sl]
        for d in range(1, win):
            acc = acc + ubuf[POOL_HIST - d:POOL_HIST - d + BLK, sl]
        count = jnp.clip(pos1, 1, win).astype(F32)
        pooled = acc / count - u[:, sl]
        outs.append(jnp.dot(pooled.astype(BF16), pw_ref[g].astype(BF16), preferred_element_type=F32))
    y_ref[...] = (jnp.concatenate(outs, axis=1) * ps_ref[...]).astype(y_ref.dtype)

    hist = u[BLK - POOL_HIST:BLK, :]
    ubuf[0:POOL_HIST, :] = hist

    @pl.when(is_meta)
    def _():
        hist_meta[...] = hist


def _lru_pool_kernel(u_ref, gate_ref, cw_ref, cb_ref, wa_ref, ba_ref, wx_ref, bx_ref, lam_ref,
                     pu_ref, pw_ref, ps_ref, ya_ref, yb_ref,
                     ubuf, hst, hist_meta, h_meta, pbuf, phist_meta, *, nblk):
    _lru_kernel(u_ref, gate_ref, cw_ref, cb_ref, wa_ref, ba_ref, wx_ref, bx_ref, lam_ref, ya_ref,
                ubuf, hst, hist_meta, h_meta, nblk=nblk)
    _pool_kernel(pu_ref, pw_ref, ps_ref, yb_ref, pbuf, phist_meta, nblk=nblk)


def _lru_pool(z, cw, cb, wa, ba, wx, bx, lam, pw, ps, nreal, nblk):
    rows = z.shape[0]
    out_spec = pl.BlockSpec((BLK, WIDTH), lambda i: (_row_block(i, nreal), 0))
    return pl.pallas_call(
        functools.partial(_lru_pool_kernel, nblk=nblk),
        out_shape=(jax.ShapeDtypeStruct((rows, WIDTH), BF16), jax.ShapeDtypeStruct((rows, WIDTH), BF16)),
        grid=(nreal + 1,),
        in_specs=[_zspec(0, nreal), _zspec(1, nreal),
                  _pspec((4, WIDTH)), _pspec((1, WIDTH)),
                  _pspec((HEADS, HEAD_W, HEAD_W)), _pspec((1, WIDTH)),
                  _pspec((HEADS, HEAD_W, HEAD_W)), _pspec((1, WIDTH)), _pspec((1, WIDTH)),
                  _zspec(2, nreal), _pspec((4, HEAD_W, HEAD_W)), _pspec((1, WIDTH))],
        out_specs=(out_spec, out_spec),
        scratch_shapes=[pltpu.VMEM((8 + BLK, WIDTH), F32), pltpu.VMEM((8, WIDTH), F32),
                        pltpu.VMEM((8, WIDTH), F32), pltpu.VMEM((8, WIDTH), F32),
                        pltpu.VMEM((POOL_HIST + BLK, WIDTH), F32), pltpu.VMEM((POOL_HIST, WIDTH), F32)],
        compiler_params=_cparams(("arbitrary",)),
        name="lru_pool",
    )(z, z, cw, cb, wa, ba, wx, bx, lam, z, pw, ps)


def _hgrn_chunk(q, z, v, lbh, state_t, valid, ones_bf):
    ls = _log_sigmoid(z)
    x1 = jnp.log(lbh)
    x2 = jnp.log1p(-lbh) + ls
    mx = jnp.maximum(x1, x2)
    g = mx + jnp.log(1.0 + jnp.exp(-jnp.abs(x1 - x2)))
    k = (1.0 - lbh) * _sigmoid(-z)
    if valid is not None:
        g = jnp.where(valid, g, 0.0)
    row = lax.broadcasted_iota(jnp.int32, (CHUNK, 1), 0)
    b = g * LOG2E
    s = 1
    while s < CHUNK:
        b = b + _shift_rows(b, s, 0.0, row)
        s *= 2
    b_last = b[CHUNK - 1:CHUNK, :]

    qe = (q * jnp.exp2(b)).astype(BF16)
    o = lax.dot_general(qe, state_t.astype(BF16), (((1,), (1,)), ((), ())), preferred_element_type=F32)

    col = lax.broadcasted_iota(jnp.int32, (SUB, CHUNK), 1)
    rsub = lax.broadcasted_iota(jnp.int32, (SUB, CHUNK), 0)
    lane = lax.broadcasted_iota(jnp.int32, (SUB, HEAD_W), 1)
    s_rows = []
    for blk in range(CHUNK // SUB):
        lo = blk * SUB
        bi = b[lo:lo + SUB, :]
        qi = q[lo:lo + SUB, :]
        ki = k[lo:lo + SUB, :]
        parts = []
        for sr in range(SUB):
            t0 = 0 if sr < SUB // 2 else SUB // 2
            e = jnp.exp2(jnp.minimum(bi[t0:, :] - bi[sr:sr + 1, :], 0.0))
            parts.append(qi[t0:, :] * e * ki[sr:sr + 1, :])
        m3 = jnp.concatenate(parts, axis=0).astype(BF16)
        red = jnp.dot(m3, ones_bf, preferred_element_type=F32)
        halves = [jnp.zeros((SUB // 2, HEAD_W), F32), jnp.zeros((SUB // 2, HEAD_W), F32)]
        r0 = 0
        for sr in range(SUB):
            for hf in range(0 if sr < SUB // 2 else 1, 2):
                halves[hf] = jnp.where(lane[:SUB // 2] == lo + sr, red[r0:r0 + SUB // 2, :], halves[hf])
                r0 += SUB // 2
        diag = jnp.concatenate(halves, axis=0)[:, :CHUNK]
        s_blk = jnp.where(jnp.logical_and(col >= lo, col - lo <= rsub), diag, 0.0)
        if blk > 0:
            b0 = b[lo - 1:lo, :]
            kt = (k * jnp.exp2(jnp.minimum(b0 - b, 0.0))).astype(BF16)
            qd = (qi * jnp.exp2(bi - b0)).astype(BF16)
            off = lax.dot_general(qd, kt, (((1,), (1,)), ((), ())), preferred_element_type=F32)
            s_blk = jnp.where(col < lo, off, s_blk)
        s_rows.append(s_blk)
    scores = jnp.concatenate(s_rows, axis=0).astype(BF16)
    vb = v.astype(BF16)
    o = o + jnp.dot(scores, vb, preferred_element_type=F32)

    kd = (k * jnp.exp2(b_last - b)).astype(BF16)
    upd = lax.dot_general(vb, kd, (((0,), (0,)), ((), ())), preferred_element_type=F32)
    new_state_t = state_t * jnp.exp2(b_last) + upd
    return o, new_state_t


def _hgrn_kernel(q_ref, f_ref, v_ref, og_ref, lb_ref, nw_ref, y_ref, state, state_meta, *, nblk):
    i = pl.program_id(0)
    is_meta = i == 0

    @pl.when(is_meta)
    def _():
        state[...] = jnp.zeros_like(state)

    @pl.when(jnp.logical_and(i >= 1, (i - 1) % nblk == 0))
    def _():
        state[...] = state_meta[...]

    ones_bf = jnp.ones((HEAD_W, HEAD_W), BF16)
    nw = nw_ref[...]
    for hd in range(HEADS):
        sl = slice(hd * HEAD_W, (hd + 1) * HEAD_W)
        lbh = lb_ref[:, sl]
        st = state[hd]
        for c in range(BLK // CHUNK):
            rs = slice(c * CHUNK, (c + 1) * CHUNK)
            rowg = lax.broadcasted_iota(jnp.int32, (CHUNK, 1), 0) + c * CHUNK
            valid = rowg >= PAD_ROWS * is_meta.astype(jnp.int32)
            o, st = _hgrn_chunk(q_ref[rs, sl], f_ref[rs, sl], v_ref[rs, sl], lbh, st, valid, ones_bf)
            og = og_ref[rs, sl]
            y_ref[rs, sl] = (_rms(o, nw) * (og * _sigmoid(og))).astype(y_ref.dtype)
        state[hd] = st

    @pl.when(is_meta)
    def _():
        state_meta[...] = state[...]


def _hgrn(z_a, z_b, lb, nw, nreal, nblk):
    rows = z_a.shape[0]
    return pl.pallas_call(
        functools.partial(_hgrn_kernel, nblk=nblk),
        out_shape=jax.ShapeDtypeStruct((rows, WIDTH), BF16),
        grid=(nreal + 1,),
        in_specs=[_zspec(3, nreal), _zspec(0, nreal), _zspec(1, nreal), _zspec(2, nreal),
                  _pspec((1, WIDTH)), _pspec((1, HEAD_W))],
        out_specs=pl.BlockSpec((BLK, WIDTH), lambda i: (_row_block(i, nreal), 0)),
        scratch_shapes=[pltpu.VMEM((HEADS, HEAD_W, HEAD_W), F32), pltpu.VMEM((HEADS, HEAD_W, HEAD_W), F32)],
        compiler_params=_cparams(("arbitrary",)),
        name="hgrn2",
    )(z_a, z_b, z_b, z_b, lb, nw)


def _attn_lambda(lamv_ref, lambda_init):
    lv = lamv_ref[...]
    return (jnp.exp(jnp.sum(lv[0:1] * lv[1:2], axis=-1, keepdims=True))
            - jnp.exp(jnp.sum(lv[2:3] * lv[3:4], axis=-1, keepdims=True)) + lambda_init)


def _scaled_q(q_ref):
    qf = (q_ref[...].astype(F32) * (DIFF_HEAD_DIM ** -0.5)).astype(BF16)
    return qf[:, :DIFF_HEAD_DIM], qf[:, DIFF_HEAD_DIM:]


def _fold8(x, op):
    r, c = x.shape
    return op(x.reshape(r // 8, 8, c), axis=0)


def _attn_kernel(relb_ref, q_ref, k_ref, v_ref, km_ref, vm_ref, bias_ref, diag_ref, lamv_ref, sw_ref, y_ref,
                 s_sc, vt_sc, m_sc, l_sc, acc_sc, *, nqb, nblk, lambda_init):
    hd = pl.program_id(0)
    jq = pl.program_id(1) % nqb
    far = relb_ref[REL_BUCKETS - 1, hd] * LOG2E
    t1 = bias_ref[1] * LOG2E
    slot_diag, slot_meta = nqb - 1, nqb

    @pl.when(jq == 0)
    def _():
        for t in range(nblk):
            vt_sc[t // NSUB, :, (t % NSUB) * BLK:(t % NSUB + 1) * BLK] = (
                v_ref[t * BLK:(t + 1) * BLK, :].astype(F32).T.astype(BF16))

    qt = (q_ref[...].astype(F32) * (DIFF_HEAD_DIM ** -0.5)).T
    zero = jnp.zeros((DIFF_HEAD_DIM, QB), F32)
    qtp = (jnp.concatenate([qt[:DIFF_HEAD_DIM], zero], axis=0).astype(BF16),
           jnp.concatenate([zero, qt[DIFF_HEAD_DIM:]], axis=0).astype(BF16))

    def key_chunk(c):
        return k_ref[pl.ds(pl.multiple_of(c * QB, QB), QB), :]

    def score_chunk(k_rows, slot, add_bias, r0=0, c0=0):
        r = k_rows.shape[0]
        for mp in range(2):
            s = add_bias(jnp.dot(k_rows, qtp[mp][:, c0:], preferred_element_type=F32) * LOG2E)
            s_sc[mp, slot, r0:r0 + r, c0:] = s
            m_sc[mp, :, c0:] = jnp.maximum(m_sc[mp, :, c0:], _fold8(s, jnp.max))

    def value_chunk(vt_cols, slot, m8, r0=0, c0=0):
        r = vt_cols.shape[1]
        for mp in range(2):
            p = jnp.exp2(s_sc[mp, slot, r0:r0 + r, c0:].reshape(r // 8, 8, QB - c0) - m8[mp][None, :, c0:])
            l_sc[mp, :, c0:] += jnp.sum(p, axis=0)
            acc_sc[mp, :, c0:] += jnp.dot(vt_cols, p.reshape(r, QB - c0).astype(BF16),
                                         preferred_element_type=F32)

    def diag_tiles(fn):
        for kb in range(NSUB):
            fn(kb, kb * BLK)

    def walk(fn_far, fn_meta, fn_diag):
        def body(c2, carry):
            fn_far(2 * c2)
            fn_far(2 * c2 + 1)
            return carry

        lax.fori_loop(0, jq // 2, body, 0)

        @pl.when(jq % 2 == 1)
        def _():
            fn_far(jq - 1)

        fn_meta()
        fn_diag()

    near = t1 - far
    first = jnp.full((1, BLK), jq, jnp.int32) == 0

    def meta_bias(s):
        head = s[:, 0:BLK] + (far + jnp.where(first, near[PAD_ROWS:, :], 0.0))
        return jnp.concatenate([head, s[:, BLK:] + far], axis=1)

    def meta_values(m8):
        v16 = vm_ref[PAD_ROWS:BLK, :]
        for mp in range(2):
            p = jnp.exp2(s_sc[mp, slot_meta, 0:N_META, :].reshape(N_META // 8, 8, QB) - m8[mp][None])
            l_sc[mp] += jnp.sum(p, axis=0)
            acc_sc[mp] += lax.dot_general(v16, p.reshape(N_META, QB).astype(BF16), (((0,), (0,)), ((), ())),
                                          preferred_element_type=F32)

    def far_bias(c):
        last = jnp.full((1, BLK), c, jnp.int32) == jq - 1

        def add(s):
            corner = s[QB - BLK:, 0:BLK] + (far + jnp.where(last, near, 0.0))
            bottom = jnp.concatenate([corner, s[QB - BLK:, BLK:] + far], axis=1)
            return jnp.concatenate([s[:QB - BLK, :] + far, bottom], axis=0)

        return add

    m_sc[...] = jnp.full_like(m_sc, NEG)
    walk(lambda c: score_chunk(key_chunk(c), c, far_bias(c)),
         lambda: score_chunk(km_ref[PAD_ROWS:BLK, :], slot_meta, meta_bias),
         lambda: diag_tiles(lambda kb, c0: score_chunk(
             k_ref[pl.ds(pl.multiple_of(jq * QB + c0, BLK), BLK), :], slot_diag,
             lambda s: s + diag_ref[c0:c0 + BLK, c0:], r0=c0, c0=c0)))

    m8 = [jnp.broadcast_to(jnp.max(m_sc[mp], axis=0, keepdims=True), (8, QB)) for mp in range(2)]
    l_sc[...] = jnp.zeros_like(l_sc)
    acc_sc[...] = jnp.zeros_like(acc_sc)

    walk(lambda c: value_chunk(vt_sc[c], c, m8),
         lambda: meta_values(m8),
         lambda: diag_tiles(lambda kb, c0: value_chunk(
             vt_sc[jq, :, c0:c0 + BLK], slot_diag, m8, r0=c0, c0=c0)))

    lam = _attn_lambda(lamv_ref, lambda_init)
    l0 = jnp.sum(l_sc[0], axis=0, keepdims=True)
    l1 = jnp.sum(l_sc[1], axis=0, keepdims=True)
    ot = acc_sc[0] / l0 - lam * (acc_sc[1] / l1)
    ot = ot * lax.rsqrt(jnp.mean(ot * ot, axis=0, keepdims=True) + EPS)
    y_ref[...] = (ot.T * sw_ref[...] * (1.0 - lambda_init)).astype(y_ref.dtype)


def _attn_meta_kernel(q_ref, k_ref, v_ref, bias_ref, lamv_ref, sw_ref, y_ref, *, lambda_init):
    qs = _scaled_q(q_ref)
    kb = k_ref[...]
    colmask = jnp.where(lax.broadcasted_iota(jnp.int32, (1, BLK), 1) >= PAD_ROWS, 0.0, NEG)
    outs = []
    bias = bias_ref[0].T + colmask
    for mp in range(2):
        s = lax.dot_general(qs[mp], kb[:, mp * DIFF_HEAD_DIM:(mp + 1) * DIFF_HEAD_DIM],
                            (((1,), (1,)), ((), ())), preferred_element_type=F32) + bias
        p = jnp.exp(s - jnp.max(s, axis=-1, keepdims=True))
        acc = jnp.dot(p.astype(BF16), v_ref[...], preferred_element_type=F32)
        outs.append(acc / jnp.sum(p, axis=-1, keepdims=True))
    o = outs[0] - _attn_lambda(lamv_ref, lambda_init) * outs[1]
    o = _rms(o, sw_ref[...]) * (1.0 - lambda_init)
    row = lax.broadcasted_iota(jnp.int32, (BLK, 1), 0)
    y_ref[...] = jnp.where(row >= PAD_ROWS, o, 0.0).astype(y_ref.dtype)


def _attn(zd, rel_bias, bias_tiles, bias_diag, lam_vec, subln_w, nreal, nblk, lambda_init):
    nqb = nblk * BLK // QB
    nq_total = nreal * BLK // QB
    seq = nblk * BLK
    kcol, vcol = HEADS, 2 * HEADS
    y_main = pl.pallas_call(
        functools.partial(_attn_kernel, nqb=nqb, nblk=nblk, lambda_init=lambda_init),
        out_shape=jax.ShapeDtypeStruct((nreal * BLK, WIDTH), BF16),
        grid=(HEADS, nq_total),
        in_specs=[pl.BlockSpec(memory_space=pltpu.SMEM),
                  pl.BlockSpec((QB, HEAD_W), lambda h, i: (i, h)),
                  pl.BlockSpec((seq, HEAD_W), lambda h, i: (i // nqb, kcol + h)),
                  pl.BlockSpec((seq, HEAD_W), lambda h, i: (i // nqb, vcol + h)),
                  pl.BlockSpec((BLK, HEAD_W), lambda h, i: (nreal, kcol + h)),
                  pl.BlockSpec((BLK, HEAD_W), lambda h, i: (nreal, vcol + h)),
                  pl.BlockSpec((None, 2, BLK, BLK), lambda h, i: (h, 0, 0, 0)),
                  pl.BlockSpec((None, QB, QB), lambda h, i: (h, 0, 0)),
                  pl.BlockSpec((4, DIFF_HEAD_DIM), lambda h, i: (0, 0)),
                  pl.BlockSpec((1, HEAD_W), lambda h, i: (0, 0))],
        out_specs=pl.BlockSpec((QB, HEAD_W), lambda h, i: (i, h)),
        scratch_shapes=[pltpu.VMEM((2, nqb + 1, QB, QB), F32),
                        pltpu.VMEM((nqb, HEAD_W, QB), BF16),
                        pltpu.VMEM((2, 8, QB), F32),
                        pltpu.VMEM((2, 8, QB), F32),
                        pltpu.VMEM((2, HEAD_W, QB), F32)],
        compiler_params=_cparams(("arbitrary", "arbitrary")),
        name="diff_attn",
    )(rel_bias, zd, zd, zd, zd, zd, bias_tiles, bias_diag, lam_vec, subln_w)
    y_meta = pl.pallas_call(
        functools.partial(_attn_meta_kernel, lambda_init=lambda_init),
        out_shape=jax.ShapeDtypeStruct((BLK, WIDTH), BF16),
        grid=(HEADS,),
        in_specs=[pl.BlockSpec((BLK, HEAD_W), lambda h: (nreal, h)),
                  pl.BlockSpec((BLK, HEAD_W), lambda h: (nreal, kcol + h)),
                  pl.BlockSpec((BLK, HEAD_W), lambda h: (nreal, vcol + h)),
                  pl.BlockSpec((None, 2, BLK, BLK), lambda h: (h, 0, 0, 0)),
                  pl.BlockSpec((4, DIFF_HEAD_DIM), lambda h: (0, 0)),
                  pl.BlockSpec((1, HEAD_W), lambda h: (0, 0))],
        out_specs=pl.BlockSpec((BLK, HEAD_W), lambda h: (0, h)),
        compiler_params=_cparams(("arbitrary",)),
        name="diff_attn_meta",
    )(zd, zd, zd, bias_tiles, lam_vec, subln_w)
    return jnp.concatenate([y_main, y_meta], axis=0)


def _largest_tile(rows, cap, align=16):
    best = align
    for t in range(align, cap + 1, align):
        if rows % t == 0:
            best = t
    return best


def _forward(x, meta_tokens, rel_bias, hgrn_lower_bounds, norm_mix_pre, norm_mix_post, norm_ffn_pre,
             norm_ffn_post, w_in, lru_conv_w, lru_conv_b, lru_w_a, lru_b_a, lru_w_x, lru_b_x, lru_lambda,
             pool_w, pool_scale, hgrn_norm, diff_lambda, diff_subln, w_branch, w_out, ffn_w_gu, ffn_w_down):
    bsz, seq, _ = x.shape
    nblk = seq // BLK
    nreal = bsz * nblk
    rows = (nreal + 1) * BLK
    rows_real = nreal * BLK
    tm_big = _largest_tile(rows, 1040)
    tm_epi = _largest_tile(rows, 640)
    tm_down = _largest_tile(rows, 320)
    tm_last = _largest_tile(rows_real, 256)

    def vec(a):
        return a.reshape(1, -1)

    lbs, bias_tiles, bias_diag = _prologue(hgrn_lower_bounds, rel_bias)
    h, hn = _embed(x.reshape(rows_real, D_MODEL), meta_tokens, vec(norm_mix_pre[0]), nreal)

    for layer in range(DEPTH):
        lambda_init = 0.8 - 0.6 * math.exp(-0.3 * layer)
        z = _mixer_in_proj(hn, w_in, layer, 0, 4 * WIDTH, F32, tm_big, 1024, 1024)
        z_b = _mixer_in_proj(hn, w_in, layer, 4 * WIDTH, 3 * WIDTH, F32, tm_big, 3 * WIDTH, WIDTH)
        zd = _mixer_in_proj(hn, w_in, layer, 7 * WIDTH, 3 * WIDTH, BF16, tm_big, 3 * WIDTH, WIDTH)
        y_a, y_b = _lru_pool(z, lru_conv_w[layer], vec(lru_conv_b[layer]), lru_w_a[layer], vec(lru_b_a[layer]),
                             lru_w_x[layer], vec(lru_b_x[layer]), vec(lru_lambda[layer]),
                             pool_w[layer], vec(pool_scale[layer]), nreal, nblk)
        y_c = _hgrn(z, z_b, lbs[layer:layer + 1], vec(hgrn_norm[layer]), nreal, nblk)
        y_d = _attn(zd, rel_bias, bias_tiles, bias_diag, diff_lambda[layer], vec(diff_subln[layer]),
                    nreal, nblk, lambda_init)
        merged = _gate_merge(hn, w_in, (y_a, y_b, y_c, y_d), w_branch, layer, tm_big, 256)
        h, hn = _out_proj(merged, _cast_layer_bf16(w_out, layer, 512), h, vec(norm_mix_post[layer]),
                          vec(norm_ffn_pre[layer]), tm_epi)
        a = _swiglu_up(hn, ffn_w_gu, layer, tm_big, 512)
        last = layer == DEPTH - 1
        w_next = vec(norm_mix_pre[layer + 1]) if not last else vec(norm_mix_pre[layer])
        h, hn = _down_proj(a, _cast_layer_bf16(ffn_w_down, layer, 512), h, vec(norm_ffn_post[layer]), w_next,
                           rows_real if last else rows, tm_last if last else tm_down, not last)
    return h.reshape(bsz, seq, D_MODEL)


def kernel(x, meta_tokens, rel_bias, hgrn_lower_bounds, norm_mix_pre, norm_mix_post, norm_ffn_pre, norm_ffn_post, w_in, lru_conv_w, lru_conv_b, lru_w_a, lru_b_a, lru_w_x, lru_b_x, lru_lambda, pool_w, pool_scale, hgrn_norm, diff_lambda, diff_subln, w_branch, w_out, ffn_w_gu, ffn_w_down):
    return _forward(x, meta_tokens, rel_bias, hgrn_lower_bounds, norm_mix_pre, norm_mix_post, norm_ffn_pre,
                    norm_ffn_post, w_in, lru_conv_w, lru_conv_b, lru_w_a, lru_b_a, lru_w_x, lru_b_x, lru_lambda,
                    pool_w, pool_scale, hgrn_norm, diff_lambda, diff_subln, w_branch, w_out, ffn_w_gu, ffn_w_down)
```

```python
import functools
import math

import numpy as np
import jax
import jax.numpy as jnp
from jax import lax
from jax.experimental import pallas as pl
from jax.experimental.pallas import tpu as pltpu

F32 = jnp.float32
BF16 = jnp.bfloat16

D_MODEL = 2048
SEQ = 2048
DEPTH = 2
N_META = 16
BLK = 128
PAD_ROWS = BLK - N_META
QB = 512
NSUB = QB // BLK
WIDTH = 512
HEADS = 4
HEAD_W = 128
CHUNK = 64
SUB = 16
LRU_C = 8.0
POOL_WINDOWS = (2, 4, 8, 16)
DIFF_HEAD_DIM = 64
REL_BUCKETS = 32
REL_MAX_DIST = 128
FFN_HIDDEN = 5632
MIX_COLS = 10 * WIDTH
NEG = -1e30
EPS = 1e-6
LOG2E = math.log2(math.e)
VMEM_LIMIT = 56 * 1024 * 1024


def _cparams(sem):
    return pltpu.CompilerParams(dimension_semantics=sem, vmem_limit_bytes=VMEM_LIMIT)


def _rms(x, w):
    return x * lax.rsqrt(jnp.mean(x * x, axis=-1, keepdims=True) + EPS) * w


def _log_sigmoid(z):
    return -(jnp.maximum(-z, 0.0) + jnp.log(1.0 + jnp.exp(-jnp.abs(z))))


def _sigmoid(z):
    return 1.0 / (1.0 + jnp.exp(-z))


def _gelu_tanh(x):
    c = math.sqrt(2.0 / math.pi)
    return 0.5 * x * (1.0 + jnp.tanh(c * (x + 0.044715 * (x * x * x))))


def _shift_rows(x, s, fill, row):
    return jnp.where(row >= s, pltpu.roll(x, s, axis=0), fill)


def _bucket_tiles():
    r = np.arange(BLK)[None, :]
    c = np.arange(BLK)[:, None]
    max_exact = REL_BUCKETS // 2

    def bucket(n):
        nf = np.maximum(n, 1).astype(np.float32)
        large = max_exact + (np.log(nf / np.float32(max_exact)) / np.float32(math.log(REL_MAX_DIST / max_exact))
                             * np.float32(REL_BUCKETS - max_exact)).astype(np.int32)
        large = np.minimum(large, REL_BUCKETS - 1)
        return np.where(n < max_exact, n, large).astype(np.int32)

    d0 = r - c
    t0 = np.where(d0 >= 0, bucket(np.maximum(d0, 0)), -1)
    t1 = bucket(BLK + r - c)
    return np.stack([t0, t1]).astype(np.int32)


def _prologue_kernel(lbraw_ref, relb_ref, idx_ref, lb_ref, bias_ref, diag_ref):
    raw = lbraw_ref[...]
    mx = jnp.max(raw, axis=0, keepdims=True)
    e = jnp.exp(raw - mx)
    sm = e / jnp.sum(e, axis=0, keepdims=True)
    cum = sm[0:1]
    lb_ref[0:1, :] = cum - sm[0:1]
    for l in range(1, DEPTH):
        cum = cum + sm[l:l + 1]
        lb_ref[l:l + 1, :] = cum - sm[0:1]
    for t in range(2):
        idx = idx_ref[t]
        for hd in range(HEADS):
            acc = jnp.zeros((BLK, BLK), F32)
            for bk in range(REL_BUCKETS):
                acc = jnp.where(idx == bk, relb_ref[bk, hd], acc)
            bias_ref[hd, t] = jnp.where(idx < 0, NEG, acc)
    for hd in range(HEADS):
        far = relb_ref[REL_BUCKETS - 1, hd]
        for kb in range(NSUB):
            for qb in range(NSUB):
                delta = qb - kb
                if delta == 0:
                    blk = bias_ref[hd, 0]
                elif delta == 1:
                    blk = bias_ref[hd, 1]
                else:
                    blk = jnp.full((BLK, BLK), far if delta > 1 else NEG, F32)
                diag_ref[hd, kb * BLK:(kb + 1) * BLK, qb * BLK:(qb + 1) * BLK] = blk * LOG2E


def _prologue(hgrn_lower_bounds, rel_bias):
    idx = jnp.asarray(_bucket_tiles())
    vmem = pl.BlockSpec(memory_space=pltpu.VMEM)
    return pl.pallas_call(
        _prologue_kernel,
        out_shape=(jax.ShapeDtypeStruct((DEPTH, WIDTH), F32),
                   jax.ShapeDtypeStruct((HEADS, 2, BLK, BLK), F32),
                   jax.ShapeDtypeStruct((HEADS, QB, QB), F32)),
        in_specs=[vmem, pl.BlockSpec(memory_space=pltpu.SMEM), vmem],
        out_specs=(vmem, vmem, vmem),
        name="prologue",
    )(hgrn_lower_bounds, rel_bias, idx)


EMBED_ROWS = 512


def _embed_kernel(x_ref, meta_ref, w_ref, h_ref, hn_ref, *, nfull):
    i = pl.program_id(0)

    @pl.when(i < nfull)
    def _():
        h_ref[...] = x_ref[...]

    @pl.when(i == nfull)
    def _():
        h_ref[...] = jnp.zeros_like(h_ref)
        h_ref[PAD_ROWS:BLK, :] = meta_ref[...]

    hn_ref[...] = _rms(h_ref[...], w_ref[...]).astype(BF16)


def _embed(x2d, meta, w_pre, nreal):
    rows = (nreal + 1) * BLK
    nfull = nreal * BLK // EMBED_ROWS
    return pl.pallas_call(
        functools.partial(_embed_kernel, nfull=nfull),
        out_shape=(jax.ShapeDtypeStruct((rows, D_MODEL), F32),
                   jax.ShapeDtypeStruct((rows, D_MODEL), BF16)),
        grid=(nfull + 1,),
        in_specs=[pl.BlockSpec((EMBED_ROWS, D_MODEL), lambda i: (jnp.minimum(i, nfull - 1), 0)),
                  pl.BlockSpec((N_META, D_MODEL), lambda i: (0, 0)),
                  pl.BlockSpec((1, D_MODEL), lambda i: (0, 0))],
        out_specs=(pl.BlockSpec((EMBED_ROWS, D_MODEL), lambda i: (i, 0)),
                   pl.BlockSpec((EMBED_ROWS, D_MODEL), lambda i: (i, 0))),
        compiler_params=_cparams(("arbitrary",)),
        name="embed",
    )(x2d, meta, w_pre)


def _cast_tiles_once(pairs):
    @pl.when(pl.program_id(1) == 0)
    def _():
        for src, dst in pairs:
            dst[...] = src[...].astype(BF16)


def _matmul_kernel(x_ref, *rest):
    *w_refs, o_ref, wbf = rest
    wblk = w_refs[0].shape[-1]
    _cast_tiles_once([(w, wbf.at[:, j * wblk:(j + 1) * wblk]) for j, w in enumerate(w_refs)])
    o_ref[...] = jnp.dot(x_ref[...], wbf[...], preferred_element_type=F32).astype(o_ref.dtype)


def _mixer_in_proj(hn, w_in, layer, col0, ncols, out_dtype, tm, tn, wblk):
    rows, k = hn.shape
    nw = tn // wblk
    w_specs = [pl.BlockSpec((None, k, wblk), functools.partial(
        lambda n, m, j: (layer, 0, col0 // wblk + n * nw + j), j=j)) for j in range(nw)]
    return pl.pallas_call(
        _matmul_kernel,
        out_shape=jax.ShapeDtypeStruct((rows, ncols), out_dtype),
        grid=(ncols // tn, rows // tm),
        in_specs=[pl.BlockSpec((tm, k), lambda n, m: (m, 0))] + w_specs,
        out_specs=pl.BlockSpec((tm, tn), lambda n, m: (m, n)),
        scratch_shapes=[pltpu.VMEM((k, tn), BF16)],
        compiler_params=_cparams(("arbitrary", "arbitrary")),
        name="mixer_in_proj",
    )(hn, *([w_in] * nw))


def _cast_kernel(w_ref, o_ref):
    o_ref[...] = w_ref[...].astype(BF16)


def _cast_layer_bf16(w, layer, tr):
    _, r, c = w.shape
    return pl.pallas_call(
        _cast_kernel,
        out_shape=jax.ShapeDtypeStruct((r, c), BF16),
        grid=(r // tr,),
        in_specs=[pl.BlockSpec((None, tr, c), lambda i: (layer, i, 0))],
        out_specs=pl.BlockSpec((tr, c), lambda i: (i, 0)),
        compiler_params=_cparams(("arbitrary",)),
        name="cast_bf16",
    )(w)


def _residual_epilogue(acc, h_ref, wpost_ref, wnext_ref, hnew_ref, hn_ref):
    h_new = h_ref[...] + _rms(acc, wpost_ref[...])
    hnew_ref[...] = h_new
    if hn_ref is not None:
        hn_ref[...] = _rms(h_new, wnext_ref[...]).astype(BF16)


EPI_ROWS = 160


def _row_subtiles(tm):
    sub = next(s for s in (EPI_ROWS, 128, 64, 32, 16) if tm % s == 0)
    return [slice(r, r + sub) for r in range(0, tm, sub)]


def _gate_merge_kernel(hn_ref, g0, g1, g2, g3, y0, y1, y2, y3, wb_ref, o_ref, gbf, wbbf):
    _cast_tiles_once([(g, gbf.at[k]) for k, g in enumerate((g0, g1, g2, g3))] + [(wb_ref, wbbf)])
    hn = hn_ref[...]
    acc = None
    for k, y_ref in enumerate((y0, y1, y2, y3)):
        gate = _sigmoid(jnp.dot(hn, gbf[k], preferred_element_type=F32))
        proj = jnp.dot(y_ref[...], wbbf[k], preferred_element_type=F32)
        acc = gate * proj if acc is None else acc + gate * proj
    o_ref[...] = acc.astype(o_ref.dtype)


def _gate_merge(hn, w_in, ys, w_branch, layer, tm, tn):
    rows, k = hn.shape
    gate_specs = [
        pl.BlockSpec((None, k, tn), functools.partial(
            lambda n, m, base: (layer, 0, base + n), base=(MIX_COLS + br * D_MODEL) // tn))
        for br in range(4)]
    y_specs = [pl.BlockSpec((tm, WIDTH), lambda n, m: (m, 0)) for _ in range(4)]
    return pl.pallas_call(
        _gate_merge_kernel,
        out_shape=jax.ShapeDtypeStruct((rows, D_MODEL), BF16),
        grid=(D_MODEL // tn, rows // tm),
        in_specs=[pl.BlockSpec((tm, k), lambda n, m: (m, 0))] + gate_specs + y_specs
                 + [pl.BlockSpec((None, 4, WIDTH, tn), lambda n, m: (layer, 0, 0, n))],
        out_specs=pl.BlockSpec((tm, tn), lambda n, m: (m, n)),
        scratch_shapes=[pltpu.VMEM((4, k, tn), BF16), pltpu.VMEM((4, WIDTH, tn), BF16)],
        compiler_params=_cparams(("arbitrary", "arbitrary")),
        name="gate_merge",
    )(hn, w_in, w_in, w_in, w_in, *ys, w_branch)


def _out_proj_kernel(x_ref, w_ref, h_ref, wpost_ref, wnext_ref, hnew_ref, hn_ref):
    for rs in _row_subtiles(x_ref.shape[0]):
        acc = jnp.dot(x_ref[rs, :], w_ref[...], preferred_element_type=F32)
        _residual_epilogue(acc, h_ref.at[rs, :], wpost_ref, wnext_ref, hnew_ref.at[rs, :], hn_ref.at[rs, :])


def _out_proj(merged, w_out, h, w_post, w_next, tm):
    rows = h.shape[0]
    row_spec = pl.BlockSpec((tm, D_MODEL), lambda m: (m, 0))
    return pl.pallas_call(
        _out_proj_kernel,
        out_shape=(jax.ShapeDtypeStruct((rows, D_MODEL), F32),
                   jax.ShapeDtypeStruct((rows, D_MODEL), BF16)),
        grid=(rows // tm,),
        in_specs=[row_spec, _resident((D_MODEL, D_MODEL), (0, 0)), row_spec,
                  _pspec((1, D_MODEL)), _pspec((1, D_MODEL))],
        out_specs=(row_spec, row_spec),
        compiler_params=_cparams(("arbitrary",)),
        name="out_proj",
    )(merged, w_out, h, w_post, w_next)


def _swiglu_up_kernel(x_ref, wg_ref, wu_ref, o_ref, wgbf, wubf):
    _cast_tiles_once([(wg_ref, wgbf), (wu_ref, wubf)])
    x = x_ref[...]
    g = jnp.dot(x, wgbf[...], preferred_element_type=F32)
    u = jnp.dot(x, wubf[...], preferred_element_type=F32)
    o_ref[...] = (g * _sigmoid(g) * u).astype(o_ref.dtype)


def _swiglu_up(hn, w_gu, layer, tm, tn):
    rows, k = hn.shape
    nt = FFN_HIDDEN // tn
    return pl.pallas_call(
        _swiglu_up_kernel,
        out_shape=jax.ShapeDtypeStruct((rows, FFN_HIDDEN), BF16),
        grid=(nt, rows // tm),
        in_specs=[pl.BlockSpec((tm, k), lambda n, m: (m, 0)),
                  pl.BlockSpec((None, k, tn), lambda n, m: (layer, 0, n)),
                  pl.BlockSpec((None, k, tn), lambda n, m: (layer, 0, nt + n))],
        out_specs=pl.BlockSpec((tm, tn), lambda n, m: (m, n)),
        scratch_shapes=[pltpu.VMEM((k, tn), BF16), pltpu.VMEM((k, tn), BF16)],
        compiler_params=_cparams(("arbitrary", "arbitrary")),
        name="swiglu_up",
    )(hn, w_gu, w_gu)


def _down_proj_kernel(a_ref, w_ref, h_ref, wpost_ref, wnext_ref, hnew_ref, hn_ref=None):
    for rs in _row_subtiles(a_ref.shape[0]):
        acc = jnp.dot(a_ref[rs, :], w_ref[...], preferred_element_type=F32)
        _residual_epilogue(acc, h_ref.at[rs, :], wpost_ref, wnext_ref, hnew_ref.at[rs, :],
                           None if hn_ref is None else hn_ref.at[rs, :])


def _down_proj(a, w_down, h, w_post, w_next, rows_out, tm, emit_hn):
    row_spec = lambda w: pl.BlockSpec((tm, w), lambda m: (m, 0))
    out_shape = [jax.ShapeDtypeStruct((rows_out, D_MODEL), F32)]
    out_specs = [row_spec(D_MODEL)]
    if emit_hn:
        out_shape.append(jax.ShapeDtypeStruct((rows_out, D_MODEL), BF16))
        out_specs.append(row_spec(D_MODEL))
    res = pl.pallas_call(
        _down_proj_kernel,
        out_shape=tuple(out_shape),
        grid=(rows_out // tm,),
        in_specs=[row_spec(FFN_HIDDEN), _resident((FFN_HIDDEN, D_MODEL), (0, 0)), row_spec(D_MODEL),
                  _pspec((1, D_MODEL)), _pspec((1, D_MODEL))],
        out_specs=tuple(out_specs),
        compiler_params=_cparams(("arbitrary",)),
        name="down_proj",
    )(a, w_down, h, w_post, w_next)
    return res if emit_hn else (res[0], None)


def _row_block(i, nreal):
    return (i + nreal) % (nreal + 1)


def _zspec(col_block, nreal):
    return pl.BlockSpec((BLK, WIDTH), lambda i: (_row_block(i, nreal), col_block))


def _pspec(shape):
    nd = len(shape)
    return pl.BlockSpec(shape, lambda *_: (0,) * nd)


def _resident(shape, index):
    return pl.BlockSpec(shape, lambda *_: index, pipeline_mode=pl.Buffered(1))


def _lru_kernel(u_ref, gate_ref, cw_ref, cb_ref, wa_ref, ba_ref, wx_ref, bx_ref, lam_ref, y_ref,
                ubuf, hst, hist_meta, h_meta, *, nblk):
    i = pl.program_id(0)
    is_meta = i == 0

    @pl.when(is_meta)
    def _():
        ubuf[0:8, :] = jnp.zeros((8, WIDTH), F32)
        hst[...] = jnp.zeros_like(hst)

    @pl.when(jnp.logical_and(i >= 1, (i - 1) % nblk == 0))
    def _():
        ubuf[0:8, :] = hist_meta[...]
        hst[...] = h_meta[...]

    u = u_ref[...]
    ubuf[8:8 + BLK, :] = u
    cw = cw_ref[...]
    xc = (cb_ref[...] + cw[3:4] * u + cw[2:3] * ubuf[7:7 + BLK, :]
          + cw[1:2] * ubuf[6:6 + BLK, :] + cw[0:1] * ubuf[5:5 + BLK, :])
    xb = xc.astype(BF16)
    ra, ia = [], []
    for hd in range(HEADS):
        sl = slice(hd * HEAD_W, (hd + 1) * HEAD_W)
        ra.append(jnp.dot(xb[:, sl], wa_ref[hd].astype(BF16), preferred_element_type=F32))
        ia.append(jnp.dot(xb[:, sl], wx_ref[hd].astype(BF16), preferred_element_type=F32))
    r = _sigmoid(jnp.concatenate(ra, axis=1) + ba_ref[...])
    ig = _sigmoid(jnp.concatenate(ia, axis=1) + bx_ref[...])
    lam = lam_ref[...]
    softplus_neg_lam = jnp.maximum(-lam, 0.0) + jnp.log1p(jnp.exp(-jnp.abs(lam)))
    log_a = -LRU_C * r * softplus_neg_lam
    a = jnp.exp(log_a)
    bb = jnp.sqrt(-jnp.tanh(log_a) * (a * a + 1.0)) * (ig * xc)
    row = lax.broadcasted_iota(jnp.int32, (BLK, 1), 0)
    bb = jnp.where(row >= PAD_ROWS * is_meta.astype(jnp.int32), bb, 0.0)

    acum, bcum = a, bb
    s = 1
    while s < BLK:
        a_sh = _shift_rows(acum, s, 1.0, row)
        b_sh = _shift_rows(bcum, s, 0.0, row)
        bcum = acum * b_sh + bcum
        acum = acum * a_sh
        s *= 2
    h = acum * hst[0:1, :] + bcum
    y_ref[...] = (h * _gelu_tanh(gate_ref[...])).astype(y_ref.dtype)

    hist = u[BLK - 8:BLK, :]
    hlast = jnp.broadcast_to(h[BLK - 1:BLK, :], (8, WIDTH))
    ubuf[0:8, :] = hist
    hst[...] = hlast

    @pl.when(is_meta)
    def _():
        hist_meta[...] = hist
        h_meta[...] = hlast


POOL_HIST = 16


def _pool_kernel(u_ref, pw_ref, ps_ref, y_ref, ubuf, hist_meta, *, nblk):
    i = pl.program_id(0)
    is_meta = i == 0

    @pl.when(is_meta)
    def _():
        ubuf[0:POOL_HIST, :] = jnp.zeros((POOL_HIST, WIDTH), F32)

    @pl.when(jnp.logical_and(i >= 1, (i - 1) % nblk == 0))
    def _():
        ubuf[0:POOL_HIST, :] = hist_meta[...]

    u = u_ref[...]
    ubuf[POOL_HIST:POOL_HIST + BLK, :] = u
    row = lax.broadcasted_iota(jnp.int32, (BLK, 1), 0)
    meta_i = is_meta.astype(jnp.int32)
    pos1 = row + 1 - PAD_ROWS * meta_i + 2 * POOL_HIST * (1 - meta_i)
    outs = []
    for g, win in enumerate(POOL_WINDOWS):
        sl = slice(g * HEAD_W, (g + 1) * HEAD_W)
        acc = u[:, sl]
        for d in range(1, win):
            acc = acc + ubuf[POOL_HIST - d:POOL_HIST - d + BLK, sl]
        count = jnp.clip(pos1, 1, win).astype(F32)
        pooled = acc / count - u[:, sl]
        outs.append(jnp.dot(pooled.astype(BF16), pw_ref[g].astype(BF16), preferred_element_type=F32))
    y_ref[...] = (jnp.concatenate(outs, axis=1) * ps_ref[...]).astype(y_ref.dtype)

    hist = u[BLK - POOL_HIST:BLK, :]
    ubuf[0:POOL_HIST, :] = hist

    @pl.when(is_meta)
    def _():
        hist_meta[...] = hist


def _lru_pool_kernel(u_ref, gate_ref, cw_ref, cb_ref, wa_ref, ba_ref, wx_ref, bx_ref, lam_ref,
                     pu_ref, pw_ref, ps_ref, ya_ref, yb_ref,
                     ubuf, hst, hist_meta, h_meta, pbuf, phist_meta, *, nblk):
    _lru_kernel(u_ref, gate_ref, cw_ref, cb_ref, wa_ref, ba_ref, wx_ref, bx_ref, lam_ref, ya_ref,
                ubuf, hst, hist_meta, h_meta, nblk=nblk)
    _pool_kernel(pu_ref, pw_ref, ps_ref, yb_ref, pbuf, phist_meta, nblk=nblk)


def _lru_pool(z, cw, cb, wa, ba, wx, bx, lam, pw, ps, nreal, nblk):
    rows = z.shape[0]
    out_spec = pl.BlockSpec((BLK, WIDTH), lambda i: (_row_block(i, nreal), 0))
    return pl.pallas_call(
        functools.partial(_lru_pool_kernel, nblk=nblk),
        out_shape=(jax.ShapeDtypeStruct((rows, WIDTH), BF16), jax.ShapeDtypeStruct((rows, WIDTH), BF16)),
        grid=(nreal + 1,),
        in_specs=[_zspec(0, nreal), _zspec(1, nreal),
                  _pspec((4, WIDTH)), _pspec((1, WIDTH)),
                  _pspec((HEADS, HEAD_W, HEAD_W)), _pspec((1, WIDTH)),
                  _pspec((HEADS, HEAD_W, HEAD_W)), _pspec((1, WIDTH)), _pspec((1, WIDTH)),
                  _zspec(2, nreal), _pspec((4, HEAD_W, HEAD_W)), _pspec((1, WIDTH))],
        out_specs=(out_spec, out_spec),
        scratch_shapes=[pltpu.VMEM((8 + BLK, WIDTH), F32), pltpu.VMEM((8, WIDTH), F32),
                        pltpu.VMEM((8, WIDTH), F32), pltpu.VMEM((8, WIDTH), F32),
                        pltpu.VMEM((POOL_HIST + BLK, WIDTH), F32), pltpu.VMEM((POOL_HIST, WIDTH), F32)],
        compiler_params=_cparams(("arbitrary",)),
        name="lru_pool",
    )(z, z, cw, cb, wa, ba, wx, bx, lam, z, pw, ps)


def _hgrn_chunk(q, z, v, lbh, state_t, valid, ones_bf):
    ls = _log_sigmoid(z)
    x1 = jnp.log(lbh)
    x2 = jnp.log1p(-lbh) + ls
    mx = jnp.maximum(x1, x2)
    g = mx + jnp.log(1.0 + jnp.exp(-jnp.abs(x1 - x2)))
    k = (1.0 - lbh) * _sigmoid(-z)
    if valid is not None:
        g = jnp.where(valid, g, 0.0)
    row = lax.broadcasted_iota(jnp.int32, (CHUNK, 1), 0)
    b = g * LOG2E
    s = 1
    while s < CHUNK:
        b = b + _shift_rows(b, s, 0.0, row)
        s *= 2
    b_last = b[CHUNK - 1:CHUNK, :]

    qe = (q * jnp.exp2(b)).astype(BF16)
    o = lax.dot_general(qe, state_t.astype(BF16), (((1,), (1,)), ((), ())), preferred_element_type=F32)

    col = lax.broadcasted_iota(jnp.int32, (SUB, CHUNK), 1)
    rsub = lax.broadcasted_iota(jnp.int32, (SUB, CHUNK), 0)
    lane = lax.broadcasted_iota(jnp.int32, (SUB, HEAD_W), 1)
    s_rows = []
    for blk in range(CHUNK // SUB):
        lo = blk * SUB
        bi = b[lo:lo + SUB, :]
        qi = q[lo:lo + SUB, :]
        ki = k[lo:lo + SUB, :]
        parts = []
        for sr in range(SUB):
            t0 = 0 if sr < SUB // 2 else SUB // 2
            e = jnp.exp2(jnp.minimum(bi[t0:, :] - bi[sr:sr + 1, :], 0.0))
            parts.append(qi[t0:, :] * e * ki[sr:sr + 1, :])
        m3 = jnp.concatenate(parts, axis=0).astype(BF16)
        red = jnp.dot(m3, ones_bf, preferred_element_type=F32)
        halves = [jnp.zeros((SUB // 2, HEAD_W), F32), jnp.zeros((SUB // 2, HEAD_W), F32)]
        r0 = 0
        for sr in range(SUB):
            for hf in range(0 if sr < SUB // 2 else 1, 2):
                halves[hf] = jnp.where(lane[:SUB // 2] == lo + sr, red[r0:r0 + SUB // 2, :], halves[hf])
                r0 += SUB // 2
        diag = jnp.concatenate(halves, axis=0)[:, :CHUNK]
        s_blk = jnp.where(jnp.logical_and(col >= lo, col - lo <= rsub), diag, 0.0)
        if blk > 0:
            b0 = b[lo - 1:lo, :]
            kt = (k * jnp.exp2(jnp.minimum(b0 - b, 0.0))).astype(BF16)
            qd = (qi * jnp.exp2(bi - b0)).astype(BF16)
            off = lax.dot_general(qd, kt, (((1,), (1,)), ((), ())), preferred_element_type=F32)
            s_blk = jnp.where(col < lo, off, s_blk)
        s_rows.append(s_blk)
    scores = jnp.concatenate(s_rows, axis=0).astype(BF16)
    vb = v.astype(BF16)
    o = o + jnp.dot(scores, vb, preferred_element_type=F32)

    kd = (k * jnp.exp2(b_last - b)).astype(BF16)
    upd = lax.dot_general(vb, kd, (((0,), (0,)), ((), ())), preferred_element_type=F32)
    new_state_t = state_t * jnp.exp2(b_last) + upd
    return o, new_state_t


def _hgrn_kernel(q_ref, f_ref, v_ref, og_ref, lb_ref, nw_ref, y_ref, state, state_meta, *, nblk):
    i = pl.program_id(0)
    is_meta = i == 0

    @pl.when(is_meta)
    def _():
        state[...] = jnp.zeros_like(state)

    @pl.when(jnp.logical_and(i >= 1, (i - 1) % nblk == 0))
    def _():
        state[...] = state_meta[...]

    ones_bf = jnp.ones((HEAD_W, HEAD_W), BF16)
    nw = nw_ref[...]
    for hd in range(HEADS):
        sl = slice(hd * HEAD_W, (hd + 1) * HEAD_W)
        lbh = lb_ref[:, sl]
        st = state[hd]
        for c in range(BLK // CHUNK):
            rs = slice(c * CHUNK, (c + 1) * CHUNK)
            rowg = lax.broadcasted_iota(jnp.int32, (CHUNK, 1), 0) + c * CHUNK
            valid = rowg >= PAD_ROWS * is_meta.astype(jnp.int32)
            o, st = _hgrn_chunk(q_ref[rs, sl], f_ref[rs, sl], v_ref[rs, sl], lbh, st, valid, ones_bf)
            og = og_ref[rs, sl]
            y_ref[rs, sl] = (_rms(o, nw) * (og * _sigmoid(og))).astype(y_ref.dtype)
        state[hd] = st

    @pl.when(is_meta)
    def _():
        state_meta[...] = state[...]


def _hgrn(z_a, z_b, lb, nw, nreal, nblk):
    rows = z_a.shape[0]
    return pl.pallas_call(
        functools.partial(_hgrn_kernel, nblk=nblk),
        out_shape=jax.ShapeDtypeStruct((rows, WIDTH), BF16),
        grid=(nreal + 1,),
        in_specs=[_zspec(3, nreal), _zspec(0, nreal), _zspec(1, nreal), _zspec(2, nreal),
                  _pspec((1, WIDTH)), _pspec((1, HEAD_W))],
        out_specs=pl.BlockSpec((BLK, WIDTH), lambda i: (_row_block(i, nreal), 0)),
        scratch_shapes=[pltpu.VMEM((HEADS, HEAD_W, HEAD_W), F32), pltpu.VMEM((HEADS, HEAD_W, HEAD_W), F32)],
        compiler_params=_cparams(("arbitrary",)),
        name="hgrn2",
    )(z_a, z_b, z_b, z_b, lb, nw)


def _attn_lambda(lamv_ref, lambda_init):
    lv = lamv_ref[...]
    return (jnp.exp(jnp.sum(lv[0:1] * lv[1:2], axis=-1, keepdims=True))
            - jnp.exp(jnp.sum(lv[2:3] * lv[3:4], axis=-1, keepdims=True)) + lambda_init)


def _scaled_q(q_ref):
    qf = (q_ref[...].astype(F32) * (DIFF_HEAD_DIM ** -0.5)).astype(BF16)
    return qf[:, :DIFF_HEAD_DIM], qf[:, DIFF_HEAD_DIM:]


def _fold8(x, op):
    r, c = x.shape
    return op(x.reshape(r // 8, 8, c), axis=0)


def _attn_kernel(relb_ref, q_ref, k_ref, v_ref, km_ref, vm_ref, bias_ref, diag_ref, lamv_ref, sw_ref, y_ref,
                 s_sc, vt_sc, m_sc, l_sc, acc_sc, *, nqb, nblk, lambda_init):
    hd = pl.program_id(0)
    jq = pl.program_id(1) % nqb
    far = relb_ref[REL_BUCKETS - 1, hd] * LOG2E
    t1 = bias_ref[1] * LOG2E
    slot_diag, slot_meta = nqb - 1, nqb

    @pl.when(jq == 0)
    def _():
        for t in range(nblk):
            vt_sc[t // NSUB, :, (t % NSUB) * BLK:(t % NSUB + 1) * BLK] = (
                v_ref[t * BLK:(t + 1) * BLK, :].astype(F32).T.astype(BF16))

    qt = (q_ref[...].astype(F32) * (DIFF_HEAD_DIM ** -0.5)).T
    zero = jnp.zeros((DIFF_HEAD_DIM, QB), F32)
    qtp = (jnp.concatenate([qt[:DIFF_HEAD_DIM], zero], axis=0).astype(BF16),
           jnp.concatenate([zero, qt[DIFF_HEAD_DIM:]], axis=0).astype(BF16))

    def key_chunk(c):
        return k_ref[c * QB:(c + 1) * QB, :]

    def score_chunk(k_rows, slot, add_bias, r0=0, c0=0):
        r = k_rows.shape[0]
        for mp in range(2):
            s = add_bias(jnp.dot(k_rows, qtp[mp][:, c0:], preferred_element_type=F32) * LOG2E)
            s_sc[mp, slot, r0:r0 + r, c0:] = s
            m_sc[mp, :, c0:] = jnp.maximum(m_sc[mp, :, c0:], _fold8(s, jnp.max))

    def value_chunk(vt_cols, slot, m8, r0=0, c0=0):
        r = vt_cols.shape[1]
        for mp in range(2):
            p = jnp.exp2(s_sc[mp, slot, r0:r0 + r, c0:].reshape(r // 8, 8, QB - c0) - m8[mp][None, :, c0:])
            l_sc[mp, :, c0:] += jnp.sum(p, axis=0)
            acc_sc[mp, :, c0:] += jnp.dot(vt_cols, p.reshape(r, QB - c0).astype(BF16),
                                         preferred_element_type=F32)

    def diag_tiles(fn):
        for kb in range(NSUB):
            fn(kb, kb * BLK)

    def walk(fn_far, fn_meta, fn_diag):
        for n in range(1, nqb):
            @pl.when(jq == n)
            def _(n=n):
                for c in range(n):
                    fn_far(c)

        fn_meta()
        fn_diag()

    near = t1 - far
    first = jnp.full((1, BLK), jq, jnp.int32) == 0

    def meta_bias(s):
        head = s[:, 0:BLK] + (far + jnp.where(first, near[PAD_ROWS:, :], 0.0))
        return jnp.concatenate([head, s[:, BLK:] + far], axis=1)

    def meta_values(m8):
        v16 = vm_ref[PAD_ROWS:BLK, :]
        for mp in range(2):
            p = jnp.exp2(s_sc[mp, slot_meta, 0:N_META, :].reshape(N_META // 8, 8, QB) - m8[mp][None])
            l_sc[mp] += jnp.sum(p, axis=0)
            acc_sc[mp] += lax.dot_general(v16, p.reshape(N_META, QB).astype(BF16), (((0,), (0,)), ((), ())),
                                          preferred_element_type=F32)

    def far_bias(c):
        last = jnp.full((1, BLK), c, jnp.int32) == jq - 1

        def add(s):
            corner = s[QB - BLK:, 0:BLK] + (far + jnp.where(last, near, 0.0))
            bottom = jnp.concatenate([corner, s[QB - BLK:, BLK:] + far], axis=1)
            return jnp.concatenate([s[:QB - BLK, :] + far, bottom], axis=0)

        return add

    m_sc[...] = jnp.full_like(m_sc, NEG)
    walk(lambda c: score_chunk(key_chunk(c), c, far_bias(c)),
         lambda: score_chunk(km_ref[PAD_ROWS:BLK, :], slot_meta, meta_bias),
         lambda: diag_tiles(lambda kb, c0: score_chunk(
             k_ref[pl.ds(pl.multiple_of(jq * QB + c0, BLK), BLK), :], slot_diag,
             lambda s: s + diag_ref[c0:c0 + BLK, c0:], r0=c0, c0=c0)))

    m8 = [jnp.broadcast_to(jnp.max(m_sc[mp], axis=0, keepdims=True), (8, QB)) for mp in range(2)]
    l_sc[...] = jnp.zeros_like(l_sc)
    acc_sc[...] = jnp.zeros_like(acc_sc)

    walk(lambda c: value_chunk(vt_sc[c], c, m8),
         lambda: meta_values(m8),
         lambda: diag_tiles(lambda kb, c0: value_chunk(
             vt_sc[jq, :, c0:c0 + BLK], slot_diag, m8, r0=c0, c0=c0)))

    lam = _attn_lambda(lamv_ref, lambda_init)
    l0 = jnp.sum(l_sc[0], axis=0, keepdims=True)
    l1 = jnp.sum(l_sc[1], axis=0, keepdims=True)
    ot = acc_sc[0] / l0 - lam * (acc_sc[1] / l1)
    ot = ot * lax.rsqrt(jnp.mean(ot * ot, axis=0, keepdims=True) + EPS)
    y_ref[...] = (ot.T * sw_ref[...] * (1.0 - lambda_init)).astype(y_ref.dtype)


def _attn_meta_kernel(q_ref, k_ref, v_ref, bias_ref, lamv_ref, sw_ref, y_ref, *, lambda_init):
    qs = _scaled_q(q_ref)
    kb = k_ref[...]
    colmask = jnp.where(lax.broadcasted_iota(jnp.int32, (1, BLK), 1) >= PAD_ROWS, 0.0, NEG)
    outs = []
    bias = bias_ref[0].T + colmask
    for mp in range(2):
        s = lax.dot_general(qs[mp], kb[:, mp * DIFF_HEAD_DIM:(mp + 1) * DIFF_HEAD_DIM],
                            (((1,), (1,)), ((), ())), preferred_element_type=F32) + bias
        p = jnp.exp(s - jnp.max(s, axis=-1, keepdims=True))
        acc = jnp.dot(p.astype(BF16), v_ref[...], preferred_element_type=F32)
        outs.append(acc / jnp.sum(p, axis=-1, keepdims=True))
    o = outs[0] - _attn_lambda(lamv_ref, lambda_init) * outs[1]
    o = _rms(o, sw_ref[...]) * (1.0 - lambda_init)
    row = lax.broadcasted_iota(jnp.int32, (BLK, 1), 0)
    y_ref[...] = jnp.where(row >= PAD_ROWS, o, 0.0).astype(y_ref.dtype)


def _attn(zd, rel_bias, bias_tiles, bias_diag, lam_vec, subln_w, nreal, nblk, lambda_init):
    nqb = nblk * BLK // QB
    nq_total = nreal * BLK // QB
    seq = nblk * BLK
    kcol, vcol = HEADS, 2 * HEADS
    y_main = pl.pallas_call(
        functools.partial(_attn_kernel, nqb=nqb, nblk=nblk, lambda_init=lambda_init),
        out_shape=jax.ShapeDtypeStruct((nreal * BLK, WIDTH), BF16),
        grid=(HEADS, nq_total),
        in_specs=[pl.BlockSpec(memory_space=pltpu.SMEM),
                  pl.BlockSpec((QB, HEAD_W), lambda h, i: (i, h)),
                  pl.BlockSpec((seq, HEAD_W), lambda h, i: (i // nqb, kcol + h)),
                  pl.BlockSpec((seq, HEAD_W), lambda h, i: (i // nqb, vcol + h)),
                  pl.BlockSpec((BLK, HEAD_W), lambda h, i: (nreal, kcol + h)),
                  pl.BlockSpec((BLK, HEAD_W), lambda h, i: (nreal, vcol + h)),
                  pl.BlockSpec((None, 2, BLK, BLK), lambda h, i: (h, 0, 0, 0)),
                  pl.BlockSpec((None, QB, QB), lambda h, i: (h, 0, 0)),
                  pl.BlockSpec((4, DIFF_HEAD_DIM), lambda h, i: (0, 0)),
                  pl.BlockSpec((1, HEAD_W), lambda h, i: (0, 0))],
        out_specs=pl.BlockSpec((QB, HEAD_W), lambda h, i: (i, h)),
        scratch_shapes=[pltpu.VMEM((2, nqb + 1, QB, QB), F32),
                        pltpu.VMEM((nqb, HEAD_W, QB), BF16),
                        pltpu.VMEM((2, 8, QB), F32),
                        pltpu.VMEM((2, 8, QB), F32),
                        pltpu.VMEM((2, HEAD_W, QB), F32)],
        compiler_params=_cparams(("arbitrary", "arbitrary")),
        name="diff_attn",
    )(rel_bias, zd, zd, zd, zd, zd, bias_tiles, bias_diag, lam_vec, subln_w)
    y_meta = pl.pallas_call(
        functools.partial(_attn_meta_kernel, lambda_init=lambda_init),
        out_shape=jax.ShapeDtypeStruct((BLK, WIDTH), BF16),
        grid=(HEADS,),
        in_specs=[pl.BlockSpec((BLK, HEAD_W), lambda h: (nreal, h)),
                  pl.BlockSpec((BLK, HEAD_W), lambda h: (nreal, kcol + h)),
                  pl.BlockSpec((BLK, HEAD_W), lambda h: (nreal, vcol + h)),
                  pl.BlockSpec((None, 2, BLK, BLK), lambda h: (h, 0, 0, 0)),
                  pl.BlockSpec((4, DIFF_HEAD_DIM), lambda h: (0, 0)),
                  pl.BlockSpec((1, HEAD_W), lambda h: (0, 0))],
        out_specs=pl.BlockSpec((BLK, HEAD_W), lambda h: (0, h)),
        compiler_params=_cparams(("arbitrary",)),
        name="diff_attn_meta",
    )(zd, zd, zd, bias_tiles, lam_vec, subln_w)
    return jnp.concatenate([y_main, y_meta], axis=0)


def _largest_tile(rows, cap, align=16):
    best = align
    for t in range(align, cap + 1, align):
        if rows % t == 0:
            best = t
    return best


def _forward(x, meta_tokens, rel_bias, hgrn_lower_bounds, norm_mix_pre, norm_mix_post, norm_ffn_pre,
             norm_ffn_post, w_in, lru_conv_w, lru_conv_b, lru_w_a, lru_b_a, lru_w_x, lru_b_x, lru_lambda,
             pool_w, pool_scale, hgrn_norm, diff_lambda, diff_subln, w_branch, w_out, ffn_w_gu, ffn_w_down):
    bsz, seq, _ = x.shape
    nblk = seq // BLK
    nreal = bsz * nblk
    rows = (nreal + 1) * BLK
    rows_real = nreal * BLK
    tm_big = _largest_tile(rows, 1040)
    tm_epi = _largest_tile(rows, 640)
    tm_down = _largest_tile(rows, 320)
    tm_last = _largest_tile(rows_real, 256)

    def vec(a):
        return a.reshape(1, -1)

    lbs, bias_tiles, bias_diag = _prologue(hgrn_lower_bounds, rel_bias)
    h, hn = _embed(x.reshape(rows_real, D_MODEL), meta_tokens, vec(norm_mix_pre[0]), nreal)

    for layer in range(DEPTH):
        lambda_init = 0.8 - 0.6 * math.exp(-0.3 * layer)
        z = _mixer_in_proj(hn, w_in, layer, 0, 4 * WIDTH, F32, tm_big, 1024, 1024)
        z_b = _mixer_in_proj(hn, w_in, layer, 4 * WIDTH, 3 * WIDTH, F32, tm_big, 3 * WIDTH, WIDTH)
        zd = _mixer_in_proj(hn, w_in, layer, 7 * WIDTH, 3 * WIDTH, BF16, tm_big, 3 * WIDTH, WIDTH)
        y_a, y_b = _lru_pool(z, lru_conv_w[layer], vec(lru_conv_b[layer]), lru_w_a[layer], vec(lru_b_a[layer]),
                             lru_w_x[layer], vec(lru_b_x[layer]), vec(lru_lambda[layer]),
                             pool_w[layer], vec(pool_scale[layer]), nreal, nblk)
        y_c = _hgrn(z, z_b, lbs[layer:layer + 1], vec(hgrn_norm[layer]), nreal, nblk)
        y_d = _attn(zd, rel_bias, bias_tiles, bias_diag, diff_lambda[layer], vec(diff_subln[layer]),
                    nreal, nblk, lambda_init)
        merged = _gate_merge(hn, w_in, (y_a, y_b, y_c, y_d), w_branch, layer, tm_big, 256)
        h, hn = _out_proj(merged, _cast_layer_bf16(w_out, layer, 512), h, vec(norm_mix_post[layer]),
                          vec(norm_ffn_pre[layer]), tm_epi)
        a = _swiglu_up(hn, ffn_w_gu, layer, tm_big, 512)
        last = layer == DEPTH - 1
        w_next = vec(norm_mix_pre[layer + 1]) if not last else vec(norm_mix_pre[layer])
        h, hn = _down_proj(a, _cast_layer_bf16(ffn_w_down, layer, 512), h, vec(norm_ffn_post[layer]), w_next,
                           rows_real if last else rows, tm_last if last else tm_down, not last)
    return h.reshape(bsz, seq, D_MODEL)


def kernel(x, meta_tokens, rel_bias, hgrn_lower_bounds, norm_mix_pre, norm_mix_post, norm_ffn_pre, norm_ffn_post, w_in, lru_conv_w, lru_conv_b, lru_w_a, lru_b_a, lru_w_x, lru_b_x, lru_lambda, pool_w, pool_scale, hgrn_norm, diff_lambda, diff_subln, w_branch, w_out, ffn_w_gu, ffn_w_down):
    return _forward(x, meta_tokens, rel_bias, hgrn_lower_bounds, norm_mix_pre, norm_mix_post, norm_ffn_pre,
                    norm_ffn_post, w_in, lru_conv_w, lru_conv_b, lru_w_a, lru_b_a, lru_w_x, lru_b_x, lru_lambda,
                    pool_w, pool_scale, hgrn_norm, diff_lambda, diff_subln, w_branch, w_out, ffn_w_gu, ffn_w_down)
```

```python
import functools
import math

import numpy as np
import jax
import jax.numpy as jnp
from jax import lax
from jax.experimental import pallas as pl
from jax.experimental.pallas import tpu as pltpu

F32 = jnp.float32
BF16 = jnp.bfloat16

D_MODEL = 2048
SEQ = 2048
DEPTH = 2
N_META = 16
BLK = 128
PAD_ROWS = BLK - N_META
QB = 512
NSUB = QB // BLK
WIDTH = 512
HEADS = 4
HEAD_W = 128
CHUNK = 64
SUB = 16
LRU_C = 8.0
POOL_WINDOWS = (2, 4, 8, 16)
DIFF_HEAD_DIM = 64
REL_BUCKETS = 32
REL_MAX_DIST = 128
FFN_HIDDEN = 5632
MIX_COLS = 10 * WIDTH
NEG = -1e30
EPS = 1e-6
LOG2E = math.log2(math.e)
VMEM_LIMIT = 56 * 1024 * 1024


def _cparams(sem):
    return pltpu.CompilerParams(dimension_semantics=sem, vmem_limit_bytes=VMEM_LIMIT)


def _rms(x, w):
    return x * lax.rsqrt(jnp.mean(x * x, axis=-1, keepdims=True) + EPS) * w


def _log_sigmoid(z):
    return -(jnp.maximum(-z, 0.0) + jnp.log(1.0 + jnp.exp(-jnp.abs(z))))


def _sigmoid(z):
    return 1.0 / (1.0 + jnp.exp(-z))


def _gelu_tanh(x):
    c = math.sqrt(2.0 / math.pi)
    return 0.5 * x * (1.0 + jnp.tanh(c * (x + 0.044715 * (x * x * x))))


def _shift_rows(x, s, fill, row):
    return jnp.where(row >= s, pltpu.roll(x, s, axis=0), fill)


def _bucket_tiles():
    r = np.arange(BLK)[None, :]
    c = np.arange(BLK)[:, None]
    max_exact = REL_BUCKETS // 2

    def bucket(n):
        nf = np.maximum(n, 1).astype(np.float32)
        large = max_exact + (np.log(nf / np.float32(max_exact)) / np.float32(math.log(REL_MAX_DIST / max_exact))
                             * np.float32(REL_BUCKETS - max_exact)).astype(np.int32)
        large = np.minimum(large, REL_BUCKETS - 1)
        return np.where(n < max_exact, n, large).astype(np.int32)

    d0 = r - c
    t0 = np.where(d0 >= 0, bucket(np.maximum(d0, 0)), -1)
    t1 = bucket(BLK + r - c)
    return np.stack([t0, t1]).astype(np.int32)


def _prologue_kernel(lbraw_ref, relb_ref, idx_ref, lb_ref, bias_ref, diag_ref):
    raw = lbraw_ref[...]
    mx = jnp.max(raw, axis=0, keepdims=True)
    e = jnp.exp(raw - mx)
    sm = e / jnp.sum(e, axis=0, keepdims=True)
    cum = sm[0:1]
    lb_ref[0:1, :] = cum - sm[0:1]
    for l in range(1, DEPTH):
        cum = cum + sm[l:l + 1]
        lb_ref[l:l + 1, :] = cum - sm[0:1]
    for t in range(2):
        idx = idx_ref[t]
        for hd in range(HEADS):
            acc = jnp.zeros((BLK, BLK), F32)
            for bk in range(REL_BUCKETS):
                acc = jnp.where(idx == bk, relb_ref[bk, hd], acc)
            bias_ref[hd, t] = jnp.where(idx < 0, NEG, acc)
    for hd in range(HEADS):
        far = relb_ref[REL_BUCKETS - 1, hd]
        for kb in range(NSUB):
            for qb in range(NSUB):
                delta = qb - kb
                if delta == 0:
                    blk = bias_ref[hd, 0]
                elif delta == 1:
                    blk = bias_ref[hd, 1]
                else:
                    blk = jnp.full((BLK, BLK), far if delta > 1 else NEG, F32)
                diag_ref[hd, kb * BLK:(kb + 1) * BLK, qb * BLK:(qb + 1) * BLK] = blk * LOG2E


def _prologue(hgrn_lower_bounds, rel_bias):
    idx = jnp.asarray(_bucket_tiles())
    vmem = pl.BlockSpec(memory_space=pltpu.VMEM)
    return pl.pallas_call(
        _prologue_kernel,
        out_shape=(jax.ShapeDtypeStruct((DEPTH, WIDTH), F32),
                   jax.ShapeDtypeStruct((HEADS, 2, BLK, BLK), F32),
                   jax.ShapeDtypeStruct((HEADS, QB, QB), F32)),
        in_specs=[vmem, pl.BlockSpec(memory_space=pltpu.SMEM), vmem],
        out_specs=(vmem, vmem, vmem),
        name="prologue",
    )(hgrn_lower_bounds, rel_bias, idx)


EMBED_ROWS = 512


def _embed_kernel(x_ref, meta_ref, w_ref, h_ref, hn_ref, *, nfull):
    i = pl.program_id(0)

    @pl.when(i < nfull)
    def _():
        h_ref[...] = x_ref[...]

    @pl.when(i == nfull)
    def _():
        h_ref[...] = jnp.zeros_like(h_ref)
        h_ref[PAD_ROWS:BLK, :] = meta_ref[...]

    hn_ref[...] = _rms(h_ref[...], w_ref[...]).astype(BF16)


def _embed(x2d, meta, w_pre, nreal):
    rows = (nreal + 1) * BLK
    nfull = nreal * BLK // EMBED_ROWS
    return pl.pallas_call(
        functools.partial(_embed_kernel, nfull=nfull),
        out_shape=(jax.ShapeDtypeStruct((rows, D_MODEL), F32),
                   jax.ShapeDtypeStruct((rows, D_MODEL), BF16)),
        grid=(nfull + 1,),
        in_specs=[pl.BlockSpec((EMBED_ROWS, D_MODEL), lambda i: (jnp.minimum(i, nfull - 1), 0)),
                  pl.BlockSpec((N_META, D_MODEL), lambda i: (0, 0)),
                  pl.BlockSpec((1, D_MODEL), lambda i: (0, 0))],
        out_specs=(pl.BlockSpec((EMBED_ROWS, D_MODEL), lambda i: (i, 0)),
                   pl.BlockSpec((EMBED_ROWS, D_MODEL), lambda i: (i, 0))),
        compiler_params=_cparams(("arbitrary",)),
        name="embed",
    )(x2d, meta, w_pre)


def _cast_tiles_once(pairs):
    @pl.when(pl.program_id(1) == 0)
    def _():
        for src, dst in pairs:
            dst[...] = src[...].astype(BF16)


def _matmul_kernel(x_ref, *rest):
    *w_refs, o_ref, wbf = rest
    wblk = w_refs[0].shape[-1]
    _cast_tiles_once([(w, wbf.at[:, j * wblk:(j + 1) * wblk]) for j, w in enumerate(w_refs)])
    o_ref[...] = jnp.dot(x_ref[...], wbf[...], preferred_element_type=F32).astype(o_ref.dtype)


def _mixer_in_proj(hn, w_in, layer, col0, ncols, out_dtype, tm, tn, wblk):
    rows, k = hn.shape
    nw = tn // wblk
    w_specs = [pl.BlockSpec((None, k, wblk), functools.partial(
        lambda n, m, j: (layer, 0, col0 // wblk + n * nw + j), j=j)) for j in range(nw)]
    return pl.pallas_call(
        _matmul_kernel,
        out_shape=jax.ShapeDtypeStruct((rows, ncols), out_dtype),
        grid=(ncols // tn, rows // tm),
        in_specs=[pl.BlockSpec((tm, k), lambda n, m: (m, 0))] + w_specs,
        out_specs=pl.BlockSpec((tm, tn), lambda n, m: (m, n)),
        scratch_shapes=[pltpu.VMEM((k, tn), BF16)],
        compiler_params=_cparams(("arbitrary", "arbitrary")),
        name="mixer_in_proj",
    )(hn, *([w_in] * nw))


def _cast_kernel(w_ref, o_ref):
    o_ref[...] = w_ref[...].astype(BF16)


def _cast_layer_bf16(w, layer, tr):
    _, r, c = w.shape
    return pl.pallas_call(
        _cast_kernel,
        out_shape=jax.ShapeDtypeStruct((r, c), BF16),
        grid=(r // tr,),
        in_specs=[pl.BlockSpec((None, tr, c), lambda i: (layer, i, 0))],
        out_specs=pl.BlockSpec((tr, c), lambda i: (i, 0)),
        compiler_params=_cparams(("arbitrary",)),
        name="cast_bf16",
    )(w)


def _residual_epilogue(acc, h_ref, wpost_ref, wnext_ref, hnew_ref, hn_ref):
    h_new = h_ref[...] + _rms(acc, wpost_ref[...])
    hnew_ref[...] = h_new
    if hn_ref is not None:
        hn_ref[...] = _rms(h_new, wnext_ref[...]).astype(BF16)


EPI_ROWS = 160


def _row_subtiles(tm):
    sub = next(s for s in (EPI_ROWS, 128, 64, 32, 16) if tm % s == 0)
    return [slice(r, r + sub) for r in range(0, tm, sub)]


def _gate_merge_kernel(hn_ref, g0, g1, g2, g3, y0, y1, y2, y3, wb_ref, o_ref, gbf, wbbf):
    _cast_tiles_once([(g, gbf.at[k]) for k, g in enumerate((g0, g1, g2, g3))] + [(wb_ref, wbbf)])
    hn = hn_ref[...]
    acc = None
    for k, y_ref in enumerate((y0, y1, y2, y3)):
        gate = _sigmoid(jnp.dot(hn, gbf[k], preferred_element_type=F32))
        proj = jnp.dot(y_ref[...], wbbf[k], preferred_element_type=F32)
        acc = gate * proj if acc is None else acc + gate * proj
    o_ref[...] = acc.astype(o_ref.dtype)


def _gate_merge(hn, w_in, ys, w_branch, layer, tm, tn):
    rows, k = hn.shape
    gate_specs = [
        pl.BlockSpec((None, k, tn), functools.partial(
            lambda n, m, base: (layer, 0, base + n), base=(MIX_COLS + br * D_MODEL) // tn))
        for br in range(4)]
    y_specs = [pl.BlockSpec((tm, WIDTH), lambda n, m: (m, 0)) for _ in range(4)]
    return pl.pallas_call(
        _gate_merge_kernel,
        out_shape=jax.ShapeDtypeStruct((rows, D_MODEL), BF16),
        grid=(D_MODEL // tn, rows // tm),
        in_specs=[pl.BlockSpec((tm, k), lambda n, m: (m, 0))] + gate_specs + y_specs
                 + [pl.BlockSpec((None, 4, WIDTH, tn), lambda n, m: (layer, 0, 0, n))],
        out_specs=pl.BlockSpec((tm, tn), lambda n, m: (m, n)),
        scratch_shapes=[pltpu.VMEM((4, k, tn), BF16), pltpu.VMEM((4, WIDTH, tn), BF16)],
        compiler_params=_cparams(("arbitrary", "arbitrary")),
        name="gate_merge",
    )(hn, w_in, w_in, w_in, w_in, *ys, w_branch)


def _out_proj_kernel(x_ref, w_ref, h_ref, wpost_ref, wnext_ref, hnew_ref, hn_ref):
    for rs in _row_subtiles(x_ref.shape[0]):
        acc = jnp.dot(x_ref[rs, :], w_ref[...], preferred_element_type=F32)
        _residual_epilogue(acc, h_ref.at[rs, :], wpost_ref, wnext_ref, hnew_ref.at[rs, :], hn_ref.at[rs, :])


def _out_proj(merged, w_out, h, w_post, w_next, tm):
    rows = h.shape[0]
    row_spec = pl.BlockSpec((tm, D_MODEL), lambda m: (m, 0))
    return pl.pallas_call(
        _out_proj_kernel,
        out_shape=(jax.ShapeDtypeStruct((rows, D_MODEL), F32),
                   jax.ShapeDtypeStruct((rows, D_MODEL), BF16)),
        grid=(rows // tm,),
        in_specs=[row_spec, _resident((D_MODEL, D_MODEL), (0, 0)), row_spec,
                  _pspec((1, D_MODEL)), _pspec((1, D_MODEL))],
        out_specs=(row_spec, row_spec),
        compiler_params=_cparams(("arbitrary",)),
        name="out_proj",
    )(merged, w_out, h, w_post, w_next)


def _swiglu_up_kernel(x_ref, wg_ref, wu_ref, o_ref, wgbf, wubf):
    _cast_tiles_once([(wg_ref, wgbf), (wu_ref, wubf)])
    x = x_ref[...]
    g = jnp.dot(x, wgbf[...], preferred_element_type=F32)
    u = jnp.dot(x, wubf[...], preferred_element_type=F32)
    o_ref[...] = (g * _sigmoid(g) * u).astype(o_ref.dtype)


def _swiglu_up(hn, w_gu, layer, tm, tn):
    rows, k = hn.shape
    nt = FFN_HIDDEN // tn
    return pl.pallas_call(
        _swiglu_up_kernel,
        out_shape=jax.ShapeDtypeStruct((rows, FFN_HIDDEN), BF16),
        grid=(nt, rows // tm),
        in_specs=[pl.BlockSpec((tm, k), lambda n, m: (m, 0)),
                  pl.BlockSpec((None, k, tn), lambda n, m: (layer, 0, n)),
                  pl.BlockSpec((None, k, tn), lambda n, m: (layer, 0, nt + n))],
        out_specs=pl.BlockSpec((tm, tn), lambda n, m: (m, n)),
        scratch_shapes=[pltpu.VMEM((k, tn), BF16), pltpu.VMEM((k, tn), BF16)],
        compiler_params=_cparams(("arbitrary", "arbitrary")),
        name="swiglu_up",
    )(hn, w_gu, w_gu)


def _down_proj_kernel(a_ref, w_ref, h_ref, wpost_ref, wnext_ref, hnew_ref, hn_ref=None):
    for rs in _row_subtiles(a_ref.shape[0]):
        acc = jnp.dot(a_ref[rs, :], w_ref[...], preferred_element_type=F32)
        _residual_epilogue(acc, h_ref.at[rs, :], wpost_ref, wnext_ref, hnew_ref.at[rs, :],
                           None if hn_ref is None else hn_ref.at[rs, :])


def _down_proj(a, w_down, h, w_post, w_next, rows_out, tm, emit_hn):
    row_spec = lambda w: pl.BlockSpec((tm, w), lambda m: (m, 0))
    out_shape = [jax.ShapeDtypeStruct((rows_out, D_MODEL), F32)]
    out_specs = [row_spec(D_MODEL)]
    if emit_hn:
        out_shape.append(jax.ShapeDtypeStruct((rows_out, D_MODEL), BF16))
        out_specs.append(row_spec(D_MODEL))
    res = pl.pallas_call(
        _down_proj_kernel,
        out_shape=tuple(out_shape),
        grid=(rows_out // tm,),
        in_specs=[row_spec(FFN_HIDDEN), _resident((FFN_HIDDEN, D_MODEL), (0, 0)), row_spec(D_MODEL),
                  _pspec((1, D_MODEL)), _pspec((1, D_MODEL))],
        out_specs=tuple(out_specs),
        compiler_params=_cparams(("arbitrary",)),
        name="down_proj",
    )(a, w_down, h, w_post, w_next)
    return res if emit_hn else (res[0], None)


def _row_block(i, nreal):
    return (i + nreal) % (nreal + 1)


def _zspec(col_block, nreal):
    return pl.BlockSpec((BLK, WIDTH), lambda i: (_row_block(i, nreal), col_block))


def _pspec(shape):
    nd = len(shape)
    return pl.BlockSpec(shape, lambda *_: (0,) * nd)


def _resident(shape, index):
    return pl.BlockSpec(shape, lambda *_: index, pipeline_mode=pl.Buffered(1))


def _lru_kernel(u_ref, gate_ref, cw_ref, cb_ref, wa_ref, ba_ref, wx_ref, bx_ref, lam_ref, y_ref,
                ubuf, hst, hist_meta, h_meta, *, nblk):
    i = pl.program_id(0)
    is_meta = i == 0

    @pl.when(is_meta)
    def _():
        ubuf[0:8, :] = jnp.zeros((8, WIDTH), F32)
        hst[...] = jnp.zeros_like(hst)

    @pl.when(jnp.logical_and(i >= 1, (i - 1) % nblk == 0))
    def _():
        ubuf[0:8, :] = hist_meta[...]
        hst[...] = h_meta[...]

    u = u_ref[...]
    ubuf[8:8 + BLK, :] = u
    cw = cw_ref[...]
    xc = (cb_ref[...] + cw[3:4] * u + cw[2:3] * ubuf[7:7 + BLK, :]
          + cw[1:2] * ubuf[6:6 + BLK, :] + cw[0:1] * ubuf[5:5 + BLK, :])
    xb = xc.astype(BF16)
    ra, ia = [], []
    for hd in range(HEADS):
        sl = slice(hd * HEAD_W, (hd + 1) * HEAD_W)
        ra.append(jnp.dot(xb[:, sl], wa_ref[hd].astype(BF16), preferred_element_type=F32))
        ia.append(jnp.dot(xb[:, sl], wx_ref[hd].astype(BF16), preferred_element_type=F32))
    r = _sigmoid(jnp.concatenate(ra, axis=1) + ba_ref[...])
    ig = _sigmoid(jnp.concatenate(ia, axis=1) + bx_ref[...])
    lam = lam_ref[...]
    softplus_neg_lam = jnp.maximum(-lam, 0.0) + jnp.log1p(jnp.exp(-jnp.abs(lam)))
    log_a = -LRU_C * r * softplus_neg_lam
    a = jnp.exp(log_a)
    bb = jnp.sqrt(-jnp.tanh(log_a) * (a * a + 1.0)) * (ig * xc)
    row = lax.broadcasted_iota(jnp.int32, (BLK, 1), 0)
    bb = jnp.where(row >= PAD_ROWS * is_meta.astype(jnp.int32), bb, 0.0)

    acum, bcum = a, bb
    s = 1
    while s < BLK:
        a_sh = _shift_rows(acum, s, 1.0, row)
        b_sh = _shift_rows(bcum, s, 0.0, row)
        bcum = acum * b_sh + bcum
        acum = acum * a_sh
        s *= 2
    h = acum * hst[0:1, :] + bcum
    y_ref[...] = (h * _gelu_tanh(gate_ref[...])).astype(y_ref.dtype)

    hist = u[BLK - 8:BLK, :]
    hlast = jnp.broadcast_to(h[BLK - 1:BLK, :], (8, WIDTH))
    ubuf[0:8, :] = hist
    hst[...] = hlast

    @pl.when(is_meta)
    def _():
        hist_meta[...] = hist
        h_meta[...] = hlast


POOL_HIST = 16


def _pool_kernel(u_ref, pw_ref, ps_ref, y_ref, ubuf, hist_meta, *, nblk):
    i = pl.program_id(0)
    is_meta = i == 0

    @pl.when(is_meta)
    def _():
        ubuf[0:POOL_HIST, :] = jnp.zeros((POOL_HIST, WIDTH), F32)

    @pl.when(jnp.logical_and(i >= 1, (i - 1) % nblk == 0))
    def _():
        ubuf[0:POOL_HIST, :] = hist_meta[...]

    u = u_ref[...]
    ubuf[POOL_HIST:POOL_HIST + BLK, :] = u
    row = lax.broadcasted_iota(jnp.int32, (BLK, 1), 0)
    meta_i = is_meta.astype(jnp.int32)
    pos1 = row + 1 - PAD_ROWS * meta_i + 2 * POOL_HIST * (1 - meta_i)
    outs = []
    for g, win in enumerate(POOL_WINDOWS):
        sl = slice(g * HEAD_W, (g + 1) * HEAD_W)
        acc = u[:, sl]
        for d in range(1, win):
            acc = acc + ubuf[POOL_HIST - d:POOL_HIST - d + BLK, sl]
        count = jnp.clip(pos1, 1, win).astype(F32)
        pooled = acc / count - u[:, sl]
        outs.append(jnp.dot(pooled.astype(BF16), pw_ref[g].astype(BF16), preferred_element_type=F32))
    y_ref[...] = (jnp.concatenate(outs, axis=1) * ps_ref[...]).astype(y_ref.dtype)

    hist = u[BLK - POOL_HIST:BLK, :]
    ubuf[0:POOL_HIST, :] = hist

    @pl.when(is_meta)
    def _():
        hist_meta[...] = hist


def _lru_pool_kernel(u_ref, gate_ref, cw_ref, cb_ref, wa_ref, ba_ref, wx_ref, bx_ref, lam_ref,
                     pu_ref, pw_ref, ps_ref, ya_ref, yb_ref,
                     ubuf, hst, hist_meta, h_meta, pbuf, phist_meta, *, nblk):
    _lru_kernel(u_ref, gate_ref, cw_ref, cb_ref, wa_ref, ba_ref, wx_ref, bx_ref, lam_ref, ya_ref,
                ubuf, hst, hist_meta, h_meta, nblk=nblk)
    _pool_kernel(pu_ref, pw_ref, ps_ref, yb_ref, pbuf, phist_meta, nblk=nblk)


def _lru_pool(z, cw, cb, wa, ba, wx, bx, lam, pw, ps, nreal, nblk):
    rows = z.shape[0]
    out_spec = pl.BlockSpec((BLK, WIDTH), lambda i: (_row_block(i, nreal), 0))
    return pl.pallas_call(
        functools.partial(_lru_pool_kernel, nblk=nblk),
        out_shape=(jax.ShapeDtypeStruct((rows, WIDTH), BF16), jax.ShapeDtypeStruct((rows, WIDTH), BF16)),
        grid=(nreal + 1,),
        in_specs=[_zspec(0, nreal), _zspec(1, nreal),
                  _pspec((4, WIDTH)), _pspec((1, WIDTH)),
                  _pspec((HEADS, HEAD_W, HEAD_W)), _pspec((1, WIDTH)),
                  _pspec((HEADS, HEAD_W, HEAD_W)), _pspec((1, WIDTH)), _pspec((1, WIDTH)),
                  _zspec(2, nreal), _pspec((4, HEAD_W, HEAD_W)), _pspec((1, WIDTH))],
        out_specs=(out_spec, out_spec),
        scratch_shapes=[pltpu.VMEM((8 + BLK, WIDTH), F32), pltpu.VMEM((8, WIDTH), F32),
                        pltpu.VMEM((8, WIDTH), F32), pltpu.VMEM((8, WIDTH), F32),
                        pltpu.VMEM((POOL_HIST + BLK, WIDTH), F32), pltpu.VMEM((POOL_HIST, WIDTH), F32)],
        compiler_params=_cparams(("arbitrary",)),
        name="lru_pool",
    )(z, z, cw, cb, wa, ba, wx, bx, lam, z, pw, ps)


def _hgrn_chunk(q, z, v, lbh, state_t, valid, ones_bf):
    ls = _log_sigmoid(z)
    x1 = jnp.log(lbh)
    x2 = jnp.log1p(-lbh) + ls
    mx = jnp.maximum(x1, x2)
    g = mx + jnp.log(1.0 + jnp.exp(-jnp.abs(x1 - x2)))
    k = (1.0 - lbh) * _sigmoid(-z)
    if valid is not None:
        g = jnp.where(valid, g, 0.0)
    row = lax.broadcasted_iota(jnp.int32, (CHUNK, 1), 0)
    b = g * LOG2E
    s = 1
    while s < CHUNK:
        b = b + _shift_rows(b, s, 0.0, row)
        s *= 2
    b_last = b[CHUNK - 1:CHUNK, :]

    qe = (q * jnp.exp2(b)).astype(BF16)
    o = lax.dot_general(qe, state_t.astype(BF16), (((1,), (1,)), ((), ())), preferred_element_type=F32)

    col = lax.broadcasted_iota(jnp.int32, (SUB, CHUNK), 1)
    rsub = lax.broadcasted_iota(jnp.int32, (SUB, CHUNK), 0)
    lane = lax.broadcasted_iota(jnp.int32, (SUB, HEAD_W), 1)
    s_rows = []
    for blk in range(CHUNK // SUB):
        lo = blk * SUB
        bi = b[lo:lo + SUB, :]
        qi = q[lo:lo + SUB, :]
        ki = k[lo:lo + SUB, :]
        parts = []
        for sr in range(SUB):
            t0 = 0 if sr < SUB // 2 else SUB // 2
            e = jnp.exp2(jnp.minimum(bi[t0:, :] - bi[sr:sr + 1, :], 0.0))
            parts.append(qi[t0:, :] * e * ki[sr:sr + 1, :])
        m3 = jnp.concatenate(parts, axis=0).astype(BF16)
        red = jnp.dot(m3, ones_bf, preferred_element_type=F32)
        halves = [jnp.zeros((SUB // 2, HEAD_W), F32), jnp.zeros((SUB // 2, HEAD_W), F32)]
        r0 = 0
        for sr in range(SUB):
            for hf in range(0 if sr < SUB // 2 else 1, 2):
                halves[hf] = jnp.where(lane[:SUB // 2] == lo + sr, red[r0:r0 + SUB // 2, :], halves[hf])
                r0 += SUB // 2
        diag = jnp.concatenate(halves, axis=0)[:, :CHUNK]
        s_blk = jnp.where(jnp.logical_and(col >= lo, col - lo <= rsub), diag, 0.0)
        if blk > 0:
            b0 = b[lo - 1:lo, :]
            kt = (k * jnp.exp2(jnp.minimum(b0 - b, 0.0))).astype(BF16)
            qd = (qi * jnp.exp2(bi - b0)).astype(BF16)
            off = lax.dot_general(qd, kt, (((1,), (1,)), ((), ())), preferred_element_type=F32)
            s_blk = jnp.where(col < lo, off, s_blk)
        s_rows.append(s_blk)
    scores = jnp.concatenate(s_rows, axis=0).astype(BF16)
    vb = v.astype(BF16)
    o = o + jnp.dot(scores, vb, preferred_element_type=F32)

    kd = (k * jnp.exp2(b_last - b)).astype(BF16)
    upd = lax.dot_general(vb, kd, (((0,), (0,)), ((), ())), preferred_element_type=F32)
    new_state_t = state_t * jnp.exp2(b_last) + upd
    return o, new_state_t


def _hgrn_kernel(q_ref, f_ref, v_ref, og_ref, lb_ref, nw_ref, y_ref, state, state_meta, *, nblk):
    i = pl.program_id(0)
    is_meta = i == 0

    @pl.when(is_meta)
    def _():
        state[...] = jnp.zeros_like(state)

    @pl.when(jnp.logical_and(i >= 1, (i - 1) % nblk == 0))
    def _():
        state[...] = state_meta[...]

    ones_bf = jnp.ones((HEAD_W, HEAD_W), BF16)
    nw = nw_ref[...]
    for hd in range(HEADS):
        sl = slice(hd * HEAD_W, (hd + 1) * HEAD_W)
        lbh = lb_ref[:, sl]
        st = state[hd]
        for c in range(BLK // CHUNK):
            rs = slice(c * CHUNK, (c + 1) * CHUNK)
            rowg = lax.broadcasted_iota(jnp.int32, (CHUNK, 1), 0) + c * CHUNK
            valid = rowg >= PAD_ROWS * is_meta.astype(jnp.int32)
            o, st = _hgrn_chunk(q_ref[rs, sl], f_ref[rs, sl], v_ref[rs, sl], lbh, st, valid, ones_bf)
            og = og_ref[rs, sl]
            y_ref[rs, sl] = (_rms(o, nw) * (og * _sigmoid(og))).astype(y_ref.dtype)
        state[hd] = st

    @pl.when(is_meta)
    def _():
        state_meta[...] = state[...]


def _hgrn(z_a, z_b, lb, nw, nreal, nblk):
    rows = z_a.shape[0]
    return pl.pallas_call(
        functools.partial(_hgrn_kernel, nblk=nblk),
        out_shape=jax.ShapeDtypeStruct((rows, WIDTH), BF16),
        grid=(nreal + 1,),
        in_specs=[_zspec(3, nreal), _zspec(0, nreal), _zspec(1, nreal), _zspec(2, nreal),
                  _pspec((1, WIDTH)), _pspec((1, HEAD_W))],
        out_specs=pl.BlockSpec((BLK, WIDTH), lambda i: (_row_block(i, nreal), 0)),
        scratch_shapes=[pltpu.VMEM((HEADS, HEAD_W, HEAD_W), F32), pltpu.VMEM((HEADS, HEAD_W, HEAD_W), F32)],
        compiler_params=_cparams(("arbitrary",)),
        name="hgrn2",
    )(z_a, z_b, z_b, z_b, lb, nw)


def _attn_lambda(lamv_ref, lambda_init):
    lv = lamv_ref[...]
    return (jnp.exp(jnp.sum(lv[0:1] * lv[1:2], axis=-1, keepdims=True))
            - jnp.exp(jnp.sum(lv[2:3] * lv[3:4], axis=-1, keepdims=True)) + lambda_init)


def _scaled_q(q_ref):
    qf = (q_ref[...].astype(F32) * (DIFF_HEAD_DIM ** -0.5)).astype(BF16)
    return qf[:, :DIFF_HEAD_DIM], qf[:, DIFF_HEAD_DIM:]


def _fold8(x, op):
    r, c = x.shape
    return op(x.reshape(r // 8, 8, c), axis=0)


def _attn_kernel(relb_ref, q_ref, k_ref, v_ref, km_ref, vm_ref, bias_ref, diag_ref, lamv_ref, sw_ref, y_ref,
                 s_sc, vt_sc, m_sc, l_sc, acc_sc, *, nqb, nblk, lambda_init):
    hd = pl.program_id(0)
    jq = pl.program_id(1) % nqb
    far = relb_ref[REL_BUCKETS - 1, hd] * LOG2E
    t1 = bias_ref[1] * LOG2E
    slot_diag, slot_meta = nqb - 1, nqb

    @pl.when(jq == 0)
    def _():
        for t in range(nblk):
            vt_sc[t // NSUB, :, (t % NSUB) * BLK:(t % NSUB + 1) * BLK] = (
                v_ref[t * BLK:(t + 1) * BLK, :].astype(F32).T.astype(BF16))

    qt = (q_ref[...].astype(F32) * (DIFF_HEAD_DIM ** -0.5)).T
    zero = jnp.zeros((DIFF_HEAD_DIM, QB), F32)
    qtp = (jnp.concatenate([qt[:DIFF_HEAD_DIM], zero], axis=0).astype(BF16),
           jnp.concatenate([zero, qt[DIFF_HEAD_DIM:]], axis=0).astype(BF16))

    def key_chunk(c):
        return k_ref[c * QB:(c + 1) * QB, :]

    def score_chunk(k_rows, slot, add_bias, r0=0, c0=0):
        r = k_rows.shape[0]
        for mp in range(2):
            s = add_bias(jnp.dot(k_rows, qtp[mp][:, c0:], preferred_element_type=F32) * LOG2E)
            s_sc[mp, slot, r0:r0 + r, c0:] = s
            m_sc[mp, :, c0:] = jnp.maximum(m_sc[mp, :, c0:], _fold8(s, jnp.max))

    def value_chunk(vt_cols, slot, m8, r0=0, c0=0):
        r = vt_cols.shape[1]
        for mp in range(2):
            p = jnp.exp2(s_sc[mp, slot, r0:r0 + r, c0:].reshape(r // 8, 8, QB - c0) - m8[mp][None, :, c0:])
            l_sc[mp, :, c0:] += jnp.sum(p, axis=0)
            acc_sc[mp, :, c0:] += jnp.dot(vt_cols, p.reshape(r, QB - c0).astype(BF16),
                                         preferred_element_type=F32)

    def diag_tiles(fn):
        for kb in range(NSUB):
            fn(kb, kb * BLK)

    def walk(fn_far, fn_meta, fn_diag):
        for n in range(nqb):
            @pl.when(jq == n)
            def _(n=n):
                for c in range(n):
                    fn_far(c)
                fn_meta()
                fn_diag()

    near = t1 - far
    first = jnp.full((1, BLK), jq, jnp.int32) == 0

    def meta_bias(s):
        head = s[:, 0:BLK] + (far + jnp.where(first, near[PAD_ROWS:, :], 0.0))
        return jnp.concatenate([head, s[:, BLK:] + far], axis=1)

    def meta_values(m8):
        v16 = vm_ref[PAD_ROWS:BLK, :]
        for mp in range(2):
            p = jnp.exp2(s_sc[mp, slot_meta, 0:N_META, :].reshape(N_META // 8, 8, QB) - m8[mp][None])
            l_sc[mp] += jnp.sum(p, axis=0)
            acc_sc[mp] += lax.dot_general(v16, p.reshape(N_META, QB).astype(BF16), (((0,), (0,)), ((), ())),
                                          preferred_element_type=F32)

    def far_bias(c):
        last = jnp.full((1, BLK), c, jnp.int32) == jq - 1

        def add(s):
            corner = s[QB - BLK:, 0:BLK] + (far + jnp.where(last, near, 0.0))
            bottom = jnp.concatenate([corner, s[QB - BLK:, BLK:] + far], axis=1)
            return jnp.concatenate([s[:QB - BLK, :] + far, bottom], axis=0)

        return add

    m_sc[...] = jnp.full_like(m_sc, NEG)
    walk(lambda c: score_chunk(key_chunk(c), c, far_bias(c)),
         lambda: score_chunk(km_ref[PAD_ROWS:BLK, :], slot_meta, meta_bias),
         lambda: diag_tiles(lambda kb, c0: score_chunk(
             k_ref[pl.ds(pl.multiple_of(jq * QB + c0, BLK), BLK), :], slot_diag,
             lambda s: s + diag_ref[c0:c0 + BLK, c0:], r0=c0, c0=c0)))

    m8 = [jnp.broadcast_to(jnp.max(m_sc[mp], axis=0, keepdims=True), (8, QB)) for mp in range(2)]
    l_sc[...] = jnp.zeros_like(l_sc)
    acc_sc[...] = jnp.zeros_like(acc_sc)

    walk(lambda c: value_chunk(vt_sc[c], c, m8),
         lambda: meta_values(m8),
         lambda: diag_tiles(lambda kb, c0: value_chunk(
             vt_sc[jq, :, c0:c0 + BLK], slot_diag, m8, r0=c0, c0=c0)))

    lam = _attn_lambda(lamv_ref, lambda_init)
    l0 = jnp.sum(l_sc[0], axis=0, keepdims=True)
    l1 = jnp.sum(l_sc[1], axis=0, keepdims=True)
    ot = acc_sc[0] / l0 - lam * (acc_sc[1] / l1)
    ot = ot * lax.rsqrt(jnp.mean(ot * ot, axis=0, keepdims=True) + EPS)
    y_ref[...] = (ot.T * sw_ref[...] * (1.0 - lambda_init)).astype(y_ref.dtype)


def _attn_meta_kernel(q_ref, k_ref, v_ref, bias_ref, lamv_ref, sw_ref, y_ref, *, lambda_init):
    qs = _scaled_q(q_ref)
    kb = k_ref[...]
    colmask = jnp.where(lax.broadcasted_iota(jnp.int32, (1, BLK), 1) >= PAD_ROWS, 0.0, NEG)
    outs = []
    bias = bias_ref[0].T + colmask
    for mp in range(2):
        s = lax.dot_general(qs[mp], kb[:, mp * DIFF_HEAD_DIM:(mp + 1) * DIFF_HEAD_DIM],
                            (((1,), (1,)), ((), ())), preferred_element_type=F32) + bias
        p = jnp.exp(s - jnp.max(s, axis=-1, keepdims=True))
        acc = jnp.dot(p.astype(BF16), v_ref[...], preferred_element_type=F32)
        outs.append(acc / jnp.sum(p, axis=-1, keepdims=True))
    o = outs[0] - _attn_lambda(lamv_ref, lambda_init) * outs[1]
    o = _rms(o, sw_ref[...]) * (1.0 - lambda_init)
    row = lax.broadcasted_iota(jnp.int32, (BLK, 1), 0)
    y_ref[...] = jnp.where(row >= PAD_ROWS, o, 0.0).astype(y_ref.dtype)


def _attn(zd, rel_bias, bias_tiles, bias_diag, lam_vec, subln_w, nreal, nblk, lambda_init):
    nqb = nblk * BLK // QB
    nq_total = nreal * BLK // QB
    seq = nblk * BLK
    kcol, vcol = HEADS, 2 * HEADS
    y_main = pl.pallas_call(
        functools.partial(_attn_kernel, nqb=nqb, nblk=nblk, lambda_init=lambda_init),
        out_shape=jax.ShapeDtypeStruct((nreal * BLK, WIDTH), BF16),
        grid=(HEADS, nq_total),
        in_specs=[pl.BlockSpec(memory_space=pltpu.SMEM),
                  pl.BlockSpec((QB, HEAD_W), lambda h, i: (i, h)),
                  pl.BlockSpec((seq, HEAD_W), lambda h, i: (i // nqb, kcol + h)),
                  pl.BlockSpec((seq, HEAD_W), lambda h, i: (i // nqb, vcol + h)),
                  pl.BlockSpec((BLK, HEAD_W), lambda h, i: (nreal, kcol + h)),
                  pl.BlockSpec((BLK, HEAD_W), lambda h, i: (nreal, vcol + h)),
                  pl.BlockSpec((None, 2, BLK, BLK), lambda h, i: (h, 0, 0, 0)),
                  pl.BlockSpec((None, QB, QB), lambda h, i: (h, 0, 0)),
                  pl.BlockSpec((4, DIFF_HEAD_DIM), lambda h, i: (0, 0)),
                  pl.BlockSpec((1, HEAD_W), lambda h, i: (0, 0))],
        out_specs=pl.BlockSpec((QB, HEAD_W), lambda h, i: (i, h)),
        scratch_shapes=[pltpu.VMEM((2, nqb + 1, QB, QB), F32),
                        pltpu.VMEM((nqb, HEAD_W, QB), BF16),
                        pltpu.VMEM((2, 8, QB), F32),
                        pltpu.VMEM((2, 8, QB), F32),
                        pltpu.VMEM((2, HEAD_W, QB), F32)],
        compiler_params=_cparams(("arbitrary", "arbitrary")),
        name="diff_attn",
    )(rel_bias, zd, zd, zd, zd, zd, bias_tiles, bias_diag, lam_vec, subln_w)
    y_meta = pl.pallas_call(
        functools.partial(_attn_meta_kernel, lambda_init=lambda_init),
        out_shape=jax.ShapeDtypeStruct((BLK, WIDTH), BF16),
        grid=(HEADS,),
        in_specs=[pl.BlockSpec((BLK, HEAD_W), lambda h: (nreal, h)),
                  pl.BlockSpec((BLK, HEAD_W), lambda h: (nreal, kcol + h)),
                  pl.BlockSpec((BLK, HEAD_W), lambda h: (nreal, vcol + h)),
                  pl.BlockSpec((None, 2, BLK, BLK), lambda h: (h, 0, 0, 0)),
                  pl.BlockSpec((4, DIFF_HEAD_DIM), lambda h: (0, 0)),
                  pl.BlockSpec((1, HEAD_W), lambda h: (0, 0))],
        out_specs=pl.BlockSpec((BLK, HEAD_W), lambda h: (0, h)),
        compiler_params=_cparams(("arbitrary",)),
        name="diff_attn_meta",
    )(zd, zd, zd, bias_tiles, lam_vec, subln_w)
    return jnp.concatenate([y_main, y_meta], axis=0)


def _largest_tile(rows, cap, align=16):
    best = align
    for t in range(align, cap + 1, align):
        if rows % t == 0:
            best = t
    return best


def _forward(x, meta_tokens, rel_bias, hgrn_lower_bounds, norm_mix_pre, norm_mix_post, norm_ffn_pre,
             norm_ffn_post, w_in, lru_conv_w, lru_conv_b, lru_w_a, lru_b_a, lru_w_x, lru_b_x, lru_lambda,
             pool_w, pool_scale, hgrn_norm, diff_lambda, diff_subln, w_branch, w_out, ffn_w_gu, ffn_w_down):
    bsz, seq, _ = x.shape
    nblk = seq // BLK
    nreal = bsz * nblk
    rows = (nreal + 1) * BLK
    rows_real = nreal * BLK
    tm_big = _largest_tile(rows, 1040)
    tm_epi = _largest_tile(rows, 640)
    tm_down = _largest_tile(rows, 320)
    tm_last = _largest_tile(rows_real, 256)

    def vec(a):
        return a.reshape(1, -1)

    lbs, bias_tiles, bias_diag = _prologue(hgrn_lower_bounds, rel_bias)
    h, hn = _embed(x.reshape(rows_real, D_MODEL), meta_tokens, vec(norm_mix_pre[0]), nreal)

    for layer in range(DEPTH):
        lambda_init = 0.8 - 0.6 * math.exp(-0.3 * layer)
        z = _mixer_in_proj(hn, w_in, layer, 0, 4 * WIDTH, F32, tm_big, 1024, 1024)
        z_b = _mixer_in_proj(hn, w_in, layer, 4 * WIDTH, 3 * WIDTH, F32, tm_big, 3 * WIDTH, WIDTH)
        zd = _mixer_in_proj(hn, w_in, layer, 7 * WIDTH, 3 * WIDTH, BF16, tm_big, 3 * WIDTH, WIDTH)
        y_a, y_b = _lru_pool(z, lru_conv_w[layer], vec(lru_conv_b[layer]), lru_w_a[layer], vec(lru_b_a[layer]),
                             lru_w_x[layer], vec(lru_b_x[layer]), vec(lru_lambda[layer]),
                             pool_w[layer], vec(pool_scale[layer]), nreal, nblk)
        y_c = _hgrn(z, z_b, lbs[layer:layer + 1], vec(hgrn_norm[layer]), nreal, nblk)
        y_d = _attn(zd, rel_bias, bias_tiles, bias_diag, diff_lambda[layer], vec(diff_subln[layer]),
                    nreal, nblk, lambda_init)
        merged = _gate_merge(hn, w_in, (y_a, y_b, y_c, y_d), w_branch, layer, tm_big, 256)
        h, hn = _out_proj(merged, _cast_layer_bf16(w_out, layer, 512), h, vec(norm_mix_post[layer]),
                          vec(norm_ffn_pre[layer]), tm_epi)
        a = _swiglu_up(hn, ffn_w_gu, layer, tm_big, 512)
        last = layer == DEPTH - 1
        w_next = vec(norm_mix_pre[layer + 1]) if not last else vec(norm_mix_pre[layer])
        h, hn = _down_proj(a, _cast_layer_bf16(ffn_w_down, layer, 512), h, vec(norm_ffn_post[layer]), w_next,
                           rows_real if last else rows, tm_last if last else tm_down, not last)
    return h.reshape(bsz, seq, D_MODEL)


def kernel(x, meta_tokens, rel_bias, hgrn_lower_bounds, norm_mix_pre, norm_mix_post, norm_ffn_pre, norm_ffn_post, w_in, lru_conv_w, lru_conv_b, lru_w_a, lru_b_a, lru_w_x, lru_b_x, lru_lambda, pool_w, pool_scale, hgrn_norm, diff_lambda, diff_subln, w_branch, w_out, ffn_w_gu, ffn_w_down):
    return _forward(x, meta_tokens, rel_bias, hgrn_lower_bounds, norm_mix_pre, norm_mix_post, norm_ffn_pre,
                    norm_ffn_post, w_in, lru_conv_w, lru_conv_b, lru_w_a, lru_b_a, lru_w_x, lru_b_x, lru_lambda,
                    pool_w, pool_scale, hgrn_norm, diff_lambda, diff_subln, w_branch, w_out, ffn_w_gu, ffn_w_down)
```

```python
import functools
import math

import numpy as np
import jax
import jax.numpy as jnp
from jax import lax
from jax.experimental import pallas as pl
from jax.experimental.pallas import tpu as pltpu

F32 = jnp.float32
BF16 = jnp.bfloat16

D_MODEL = 2048
SEQ = 2048
DEPTH = 2
N_META = 16
BLK = 128
PAD_ROWS = BLK - N_META
QB = 512
NSUB = QB // BLK
WIDTH = 512
HEADS = 4
HEAD_W = 128
CHUNK = 64
SUB = 16
LRU_C = 8.0
POOL_WINDOWS = (2, 4, 8, 16)
DIFF_HEAD_DIM = 64
REL_BUCKETS = 32
REL_MAX_DIST = 128
FFN_HIDDEN = 5632
MIX_COLS = 10 * WIDTH
NEG = -1e30
EPS = 1e-6
LOG2E = math.log2(math.e)
VMEM_LIMIT = 56 * 1024 * 1024


def _cparams(sem):
    return pltpu.CompilerParams(dimension_semantics=sem, vmem_limit_bytes=VMEM_LIMIT)


def _rms(x, w):
    return x * lax.rsqrt(jnp.mean(x * x, axis=-1, keepdims=True) + EPS) * w


def _log_sigmoid(z):
    return -(jnp.maximum(-z, 0.0) + jnp.log(1.0 + jnp.exp(-jnp.abs(z))))


def _sigmoid(z):
    return 1.0 / (1.0 + jnp.exp(-z))


def _gelu_tanh(x):
    c = math.sqrt(2.0 / math.pi)
    return 0.5 * x * (1.0 + jnp.tanh(c * (x + 0.044715 * (x * x * x))))


def _shift_rows(x, s, fill, row):
    return jnp.where(row >= s, pltpu.roll(x, s, axis=0), fill)


def _bucket_tiles():
    r = np.arange(BLK)[None, :]
    c = np.arange(BLK)[:, None]
    max_exact = REL_BUCKETS // 2

    def bucket(n):
        nf = np.maximum(n, 1).astype(np.float32)
        large = max_exact + (np.log(nf / np.float32(max_exact)) / np.float32(math.log(REL_MAX_DIST / max_exact))
                             * np.float32(REL_BUCKETS - max_exact)).astype(np.int32)
        large = np.minimum(large, REL_BUCKETS - 1)
        return np.where(n < max_exact, n, large).astype(np.int32)

    d0 = r - c
    t0 = np.where(d0 >= 0, bucket(np.maximum(d0, 0)), -1)
    t1 = bucket(BLK + r - c)
    return np.stack([t0, t1]).astype(np.int32)


def _prologue_kernel(lbraw_ref, relb_ref, idx_ref, lb_ref, bias_ref, diag_ref):
    raw = lbraw_ref[...]
    mx = jnp.max(raw, axis=0, keepdims=True)
    e = jnp.exp(raw - mx)
    sm = e / jnp.sum(e, axis=0, keepdims=True)
    cum = sm[0:1]
    lb_ref[0:1, :] = cum - sm[0:1]
    for l in range(1, DEPTH):
        cum = cum + sm[l:l + 1]
        lb_ref[l:l + 1, :] = cum - sm[0:1]
    for t in range(2):
        idx = idx_ref[t]
        for hd in range(HEADS):
            acc = jnp.zeros((BLK, BLK), F32)
            for bk in range(REL_BUCKETS):
                acc = jnp.where(idx == bk, relb_ref[bk, hd], acc)
            bias_ref[hd, t] = jnp.where(idx < 0, NEG, acc)
    for hd in range(HEADS):
        far = relb_ref[REL_BUCKETS - 1, hd]
        for kb in range(NSUB):
            for qb in range(NSUB):
                delta = qb - kb
                if delta == 0:
                    blk = bias_ref[hd, 0]
                elif delta == 1:
                    blk = bias_ref[hd, 1]
                else:
                    blk = jnp.full((BLK, BLK), far if delta > 1 else NEG, F32)
                diag_ref[hd, kb * BLK:(kb + 1) * BLK, qb * BLK:(qb + 1) * BLK] = blk * LOG2E


def _prologue(hgrn_lower_bounds, rel_bias):
    idx = jnp.asarray(_bucket_tiles())
    vmem = pl.BlockSpec(memory_space=pltpu.VMEM)
    return pl.pallas_call(
        _prologue_kernel,
        out_shape=(jax.ShapeDtypeStruct((DEPTH, WIDTH), F32),
                   jax.ShapeDtypeStruct((HEADS, 2, BLK, BLK), F32),
                   jax.ShapeDtypeStruct((HEADS, QB, QB), F32)),
        in_specs=[vmem, pl.BlockSpec(memory_space=pltpu.SMEM), vmem],
        out_specs=(vmem, vmem, vmem),
        name="prologue",
    )(hgrn_lower_bounds, rel_bias, idx)


EMBED_ROWS = 512


def _embed_kernel(x_ref, meta_ref, w_ref, h_ref, hn_ref, *, nfull):
    i = pl.program_id(0)

    @pl.when(i < nfull)
    def _():
        h_ref[...] = x_ref[...]

    @pl.when(i == nfull)
    def _():
        h_ref[...] = jnp.zeros_like(h_ref)
        h_ref[PAD_ROWS:BLK, :] = meta_ref[...]

    hn_ref[...] = _rms(h_ref[...], w_ref[...]).astype(BF16)


def _embed(x2d, meta, w_pre, nreal):
    rows = (nreal + 1) * BLK
    nfull = nreal * BLK // EMBED_ROWS
    return pl.pallas_call(
        functools.partial(_embed_kernel, nfull=nfull),
        out_shape=(jax.ShapeDtypeStruct((rows, D_MODEL), F32),
                   jax.ShapeDtypeStruct((rows, D_MODEL), BF16)),
        grid=(nfull + 1,),
        in_specs=[pl.BlockSpec((EMBED_ROWS, D_MODEL), lambda i: (jnp.minimum(i, nfull - 1), 0)),
                  pl.BlockSpec((N_META, D_MODEL), lambda i: (0, 0)),
                  pl.BlockSpec((1, D_MODEL), lambda i: (0, 0))],
        out_specs=(pl.BlockSpec((EMBED_ROWS, D_MODEL), lambda i: (i, 0)),
                   pl.BlockSpec((EMBED_ROWS, D_MODEL), lambda i: (i, 0))),
        compiler_params=_cparams(("arbitrary",)),
        name="embed",
    )(x2d, meta, w_pre)


def _cast_tiles_once(pairs):
    @pl.when(pl.program_id(1) == 0)
    def _():
        for src, dst in pairs:
            dst[...] = src[...].astype(BF16)


def _matmul_kernel(x_ref, *rest):
    *w_refs, o_ref, wbf = rest
    wblk = w_refs[0].shape[-1]
    _cast_tiles_once([(w, wbf.at[:, j * wblk:(j + 1) * wblk]) for j, w in enumerate(w_refs)])
    o_ref[...] = jnp.dot(x_ref[...], wbf[...], preferred_element_type=F32).astype(o_ref.dtype)


def _mixer_in_proj(hn, w_in, layer, col0, ncols, out_dtype, tm, tn, wblk):
    rows, k = hn.shape
    nw = tn // wblk
    w_specs = [pl.BlockSpec((None, k, wblk), functools.partial(
        lambda n, m, j: (layer, 0, col0 // wblk + n * nw + j), j=j)) for j in range(nw)]
    return pl.pallas_call(
        _matmul_kernel,
        out_shape=jax.ShapeDtypeStruct((rows, ncols), out_dtype),
        grid=(ncols // tn, rows // tm),
        in_specs=[pl.BlockSpec((tm, k), lambda n, m: (m, 0))] + w_specs,
        out_specs=pl.BlockSpec((tm, tn), lambda n, m: (m, n)),
        scratch_shapes=[pltpu.VMEM((k, tn), BF16)],
        compiler_params=_cparams(("arbitrary", "arbitrary")),
        name="mixer_in_proj",
    )(hn, *([w_in] * nw))


def _cast_kernel(w_ref, o_ref):
    o_ref[...] = w_ref[...].astype(BF16)


def _cast_layer_bf16(w, layer, tr):
    _, r, c = w.shape
    return pl.pallas_call(
        _cast_kernel,
        out_shape=jax.ShapeDtypeStruct((r, c), BF16),
        grid=(r // tr,),
        in_specs=[pl.BlockSpec((None, tr, c), lambda i: (layer, i, 0))],
        out_specs=pl.BlockSpec((tr, c), lambda i: (i, 0)),
        compiler_params=_cparams(("arbitrary",)),
        name="cast_bf16",
    )(w)


def _residual_epilogue(acc, h_ref, wpost_ref, wnext_ref, hnew_ref, hn_ref):
    h_new = h_ref[...] + _rms(acc, wpost_ref[...])
    hnew_ref[...] = h_new
    if hn_ref is not None:
        hn_ref[...] = _rms(h_new, wnext_ref[...]).astype(BF16)


EPI_ROWS = 160


def _row_subtiles(tm):
    sub = next(s for s in (EPI_ROWS, 128, 64, 32, 16) if tm % s == 0)
    return [slice(r, r + sub) for r in range(0, tm, sub)]


def _gate_merge_kernel(hn_ref, g0, g1, g2, g3, y0, y1, y2, y3, wb_ref, o_ref, gbf, wbbf):
    _cast_tiles_once([(g, gbf.at[k]) for k, g in enumerate((g0, g1, g2, g3))] + [(wb_ref, wbbf)])
    hn = hn_ref[...]
    acc = None
    for k, y_ref in enumerate((y0, y1, y2, y3)):
        gate = _sigmoid(jnp.dot(hn, gbf[k], preferred_element_type=F32))
        proj = jnp.dot(y_ref[...], wbbf[k], preferred_element_type=F32)
        acc = gate * proj if acc is None else acc + gate * proj
    o_ref[...] = acc.astype(o_ref.dtype)


def _gate_merge(hn, w_in, ys, w_branch, layer, tm, tn):
    rows, k = hn.shape
    gate_specs = [
        pl.BlockSpec((None, k, tn), functools.partial(
            lambda n, m, base: (layer, 0, base + n), base=(MIX_COLS + br * D_MODEL) // tn))
        for br in range(4)]
    y_specs = [pl.BlockSpec((tm, WIDTH), lambda n, m: (m, 0)) for _ in range(4)]
    return pl.pallas_call(
        _gate_merge_kernel,
        out_shape=jax.ShapeDtypeStruct((rows, D_MODEL), BF16),
        grid=(D_MODEL // tn, rows // tm),
        in_specs=[pl.BlockSpec((tm, k), lambda n, m: (m, 0))] + gate_specs + y_specs
                 + [pl.BlockSpec((None, 4, WIDTH, tn), lambda n, m: (layer, 0, 0, n))],
        out_specs=pl.BlockSpec((tm, tn), lambda n, m: (m, n)),
        scratch_shapes=[pltpu.VMEM((4, k, tn), BF16), pltpu.VMEM((4, WIDTH, tn), BF16)],
        compiler_params=_cparams(("arbitrary", "arbitrary")),
        name="gate_merge",
    )(hn, w_in, w_in, w_in, w_in, *ys, w_branch)


def _out_proj_kernel(x_ref, w_ref, h_ref, wpost_ref, wnext_ref, hnew_ref, hn_ref):
    for rs in _row_subtiles(x_ref.shape[0]):
        acc = jnp.dot(x_ref[rs, :], w_ref[...], preferred_element_type=F32)
        _residual_epilogue(acc, h_ref.at[rs, :], wpost_ref, wnext_ref, hnew_ref.at[rs, :], hn_ref.at[rs, :])


def _out_proj(merged, w_out, h, w_post, w_next, tm):
    rows = h.shape[0]
    row_spec = pl.BlockSpec((tm, D_MODEL), lambda m: (m, 0))
    return pl.pallas_call(
        _out_proj_kernel,
        out_shape=(jax.ShapeDtypeStruct((rows, D_MODEL), F32),
                   jax.ShapeDtypeStruct((rows, D_MODEL), BF16)),
        grid=(rows // tm,),
        in_specs=[row_spec, _resident((D_MODEL, D_MODEL), (0, 0)), row_spec,
                  _pspec((1, D_MODEL)), _pspec((1, D_MODEL))],
        out_specs=(row_spec, row_spec),
        compiler_params=_cparams(("arbitrary",)),
        name="out_proj",
    )(merged, w_out, h, w_post, w_next)


def _swiglu_up_kernel(x_ref, wg_ref, wu_ref, o_ref, wgbf, wubf):
    _cast_tiles_once([(wg_ref, wgbf), (wu_ref, wubf)])
    x = x_ref[...]
    g = jnp.dot(x, wgbf[...], preferred_element_type=F32)
    u = jnp.dot(x, wubf[...], preferred_element_type=F32)
    o_ref[...] = (g * _sigmoid(g) * u).astype(o_ref.dtype)


def _swiglu_up(hn, w_gu, layer, tm, tn):
    rows, k = hn.shape
    nt = FFN_HIDDEN // tn
    return pl.pallas_call(
        _swiglu_up_kernel,
        out_shape=jax.ShapeDtypeStruct((rows, FFN_HIDDEN), BF16),
        grid=(nt, rows // tm),
        in_specs=[pl.BlockSpec((tm, k), lambda n, m: (m, 0)),
                  pl.BlockSpec((None, k, tn), lambda n, m: (layer, 0, n)),
                  pl.BlockSpec((None, k, tn), lambda n, m: (layer, 0, nt + n))],
        out_specs=pl.BlockSpec((tm, tn), lambda n, m: (m, n)),
        scratch_shapes=[pltpu.VMEM((k, tn), BF16), pltpu.VMEM((k, tn), BF16)],
        compiler_params=_cparams(("arbitrary", "arbitrary")),
        name="swiglu_up",
    )(hn, w_gu, w_gu)


def _down_proj_kernel(a_ref, w_ref, h_ref, wpost_ref, wnext_ref, hnew_ref, hn_ref=None):
    for rs in _row_subtiles(a_ref.shape[0]):
        acc = jnp.dot(a_ref[rs, :], w_ref[...], preferred_element_type=F32)
        _residual_epilogue(acc, h_ref.at[rs, :], wpost_ref, wnext_ref, hnew_ref.at[rs, :],
                           None if hn_ref is None else hn_ref.at[rs, :])


def _down_proj(a, w_down, h, w_post, w_next, rows_out, tm, emit_hn):
    row_spec = lambda w: pl.BlockSpec((tm, w), lambda m: (m, 0))
    out_shape = [jax.ShapeDtypeStruct((rows_out, D_MODEL), F32)]
    out_specs = [row_spec(D_MODEL)]
    if emit_hn:
        out_shape.append(jax.ShapeDtypeStruct((rows_out, D_MODEL), BF16))
        out_specs.append(row_spec(D_MODEL))
    res = pl.pallas_call(
        _down_proj_kernel,
        out_shape=tuple(out_shape),
        grid=(rows_out // tm,),
        in_specs=[row_spec(FFN_HIDDEN), _resident((FFN_HIDDEN, D_MODEL), (0, 0)), row_spec(D_MODEL),
                  _pspec((1, D_MODEL)), _pspec((1, D_MODEL))],
        out_specs=tuple(out_specs),
        compiler_params=_cparams(("arbitrary",)),
        name="down_proj",
    )(a, w_down, h, w_post, w_next)
    return res if emit_hn else (res[0], None)


def _row_block(i, nreal):
    return (i + nreal) % (nreal + 1)


def _zspec(col_block, nreal):
    return pl.BlockSpec((BLK, WIDTH), lambda i: (_row_block(i, nreal), col_block))


def _pspec(shape):
    nd = len(shape)
    return pl.BlockSpec(shape, lambda *_: (0,) * nd)


def _resident(shape, index):
    return pl.BlockSpec(shape, lambda *_: index, pipeline_mode=pl.Buffered(1))


def _lru_kernel(u_ref, gate_ref, cw_ref, cb_ref, wa_ref, ba_ref, wx_ref, bx_ref, lam_ref, y_ref,
                ubuf, hst, hist_meta, h_meta, *, nblk):
    i = pl.program_id(0)
    is_meta = i == 0

    @pl.when(is_meta)
    def _():
        ubuf[0:8, :] = jnp.zeros((8, WIDTH), F32)
        hst[...] = jnp.zeros_like(hst)

    @pl.when(jnp.logical_and(i >= 1, (i - 1) % nblk == 0))
    def _():
        ubuf[0:8, :] = hist_meta[...]
        hst[...] = h_meta[...]

    u = u_ref[...]
    ubuf[8:8 + BLK, :] = u
    cw = cw_ref[...]
    xc = (cb_ref[...] + cw[3:4] * u + cw[2:3] * ubuf[7:7 + BLK, :]
          + cw[1:2] * ubuf[6:6 + BLK, :] + cw[0:1] * ubuf[5:5 + BLK, :])
    xb = xc.astype(BF16)
    ra, ia = [], []
    for hd in range(HEADS):
        sl = slice(hd * HEAD_W, (hd + 1) * HEAD_W)
        ra.append(jnp.dot(xb[:, sl], wa_ref[hd].astype(BF16), preferred_element_type=F32))
        ia.append(jnp.dot(xb[:, sl], wx_ref[hd].astype(BF16), preferred_element_type=F32))
    r = _sigmoid(jnp.concatenate(ra, axis=1) + ba_ref[...])
    ig = _sigmoid(jnp.concatenate(ia, axis=1) + bx_ref[...])
    lam = lam_ref[...]
    softplus_neg_lam = jnp.maximum(-lam, 0.0) + jnp.log1p(jnp.exp(-jnp.abs(lam)))
    log_a = -LRU_C * r * softplus_neg_lam
    a = jnp.exp(log_a)
    bb = jnp.sqrt(-jnp.tanh(log_a) * (a * a + 1.0)) * (ig * xc)
    row = lax.broadcasted_iota(jnp.int32, (BLK, 1), 0)
    bb = jnp.where(row >= PAD_ROWS * is_meta.astype(jnp.int32), bb, 0.0)

    acum, bcum = a, bb
    s = 1
    while s < BLK:
        a_sh = _shift_rows(acum, s, 1.0, row)
        b_sh = _shift_rows(bcum, s, 0.0, row)
        bcum = acum * b_sh + bcum
        acum = acum * a_sh
        s *= 2
    h = acum * hst[0:1, :] + bcum
    y_ref[...] = (h * _gelu_tanh(gate_ref[...])).astype(y_ref.dtype)

    hist = u[BLK - 8:BLK, :]
    hlast = jnp.broadcast_to(h[BLK - 1:BLK, :], (8, WIDTH))
    ubuf[0:8, :] = hist
    hst[...] = hlast

    @pl.when(is_meta)
    def _():
        hist_meta[...] = hist
        h_meta[...] = hlast


POOL_HIST = 16


def _pool_kernel(u_ref, pw_ref, ps_ref, y_ref, ubuf, hist_meta, *, nblk):
    i = pl.program_id(0)
    is_meta = i == 0

    @pl.when(is_meta)
    def _():
        ubuf[0:POOL_HIST, :] = jnp.zeros((POOL_HIST, WIDTH), F32)

    @pl.when(jnp.logical_and(i >= 1, (i - 1) % nblk == 0))
    def _():
        ubuf[0:POOL_HIST, :] = hist_meta[...]

    u = u_ref[...]
    ubuf[POOL_HIST:POOL_HIST + BLK, :] = u
    row = lax.broadcasted_iota(jnp.int32, (BLK, 1), 0)
    meta_i = is_meta.astype(jnp.int32)
    pos1 = row + 1 - PAD_ROWS * meta_i + 2 * POOL_HIST * (1 - meta_i)
    outs = []
    for g, win in enumerate(POOL_WINDOWS):
        sl = slice(g * HEAD_W, (g + 1) * HEAD_W)
        acc = u[:, sl]
        for d in range(1, win):
            acc = acc + ubuf[POOL_HIST - d:POOL_HIST - d + BLK, sl]
        count = jnp.clip(pos1, 1, win).astype(F32)
        pooled = acc / count - u[:, sl]
        outs.append(jnp.dot(pooled.astype(BF16), pw_ref[g].astype(BF16), preferred_element_type=F32))
    y_ref[...] = (jnp.concatenate(outs, axis=1) * ps_ref[...]).astype(y_ref.dtype)

    hist = u[BLK - POOL_HIST:BLK, :]
    ubuf[0:POOL_HIST, :] = hist

    @pl.when(is_meta)
    def _():
        hist_meta[...] = hist


def _lru_pool_kernel(u_ref, gate_ref, cw_ref, cb_ref, wa_ref, ba_ref, wx_ref, bx_ref, lam_ref,
                     pu_ref, pw_ref, ps_ref, ya_ref, yb_ref,
                     ubuf, hst, hist_meta, h_meta, pbuf, phist_meta, *, nblk):
    _lru_kernel(u_ref, gate_ref, cw_ref, cb_ref, wa_ref, ba_ref, wx_ref, bx_ref, lam_ref, ya_ref,
                ubuf, hst, hist_meta, h_meta, nblk=nblk)
    _pool_kernel(pu_ref, pw_ref, ps_ref, yb_ref, pbuf, phist_meta, nblk=nblk)


def _lru_pool(z, cw, cb, wa, ba, wx, bx, lam, pw, ps, nreal, nblk):
    rows = z.shape[0]
    out_spec = pl.BlockSpec((BLK, WIDTH), lambda i: (_row_block(i, nreal), 0))
    return pl.pallas_call(
        functools.partial(_lru_pool_kernel, nblk=nblk),
        out_shape=(jax.ShapeDtypeStruct((rows, WIDTH), BF16), jax.ShapeDtypeStruct((rows, WIDTH), BF16)),
        grid=(nreal + 1,),
        in_specs=[_zspec(0, nreal), _zspec(1, nreal),
                  _pspec((4, WIDTH)), _pspec((1, WIDTH)),
                  _pspec((HEADS, HEAD_W, HEAD_W)), _pspec((1, WIDTH)),
                  _pspec((HEADS, HEAD_W, HEAD_W)), _pspec((1, WIDTH)), _pspec((1, WIDTH)),
                  _zspec(2, nreal), _pspec((4, HEAD_W, HEAD_W)), _pspec((1, WIDTH))],
        out_specs=(out_spec, out_spec),
        scratch_shapes=[pltpu.VMEM((8 + BLK, WIDTH), F32), pltpu.VMEM((8, WIDTH), F32),
                        pltpu.VMEM((8, WIDTH), F32), pltpu.VMEM((8, WIDTH), F32),
                        pltpu.VMEM((POOL_HIST + BLK, WIDTH), F32), pltpu.VMEM((POOL_HIST, WIDTH), F32)],
        compiler_params=_cparams(("arbitrary",)),
        name="lru_pool",
    )(z, z, cw, cb, wa, ba, wx, bx, lam, z, pw, ps)


def _hgrn_chunk(q, z, v, lbh, state_t, valid, ones_bf):
    ls = _log_sigmoid(z)
    x1 = jnp.log(lbh)
    x2 = jnp.log1p(-lbh) + ls
    mx = jnp.maximum(x1, x2)
    g = mx + jnp.log(1.0 + jnp.exp(-jnp.abs(x1 - x2)))
    k = (1.0 - lbh) * _sigmoid(-z)
    if valid is not None:
        g = jnp.where(valid, g, 0.0)
    row = lax.broadcasted_iota(jnp.int32, (CHUNK, 1), 0)
    b = g * LOG2E
    s = 1
    while s < CHUNK:
        b = b + _shift_rows(b, s, 0.0, row)
        s *= 2
    b_last = b[CHUNK - 1:CHUNK, :]

    qe = (q * jnp.exp2(b)).astype(BF16)
    o = lax.dot_general(qe, state_t.astype(BF16), (((1,), (1,)), ((), ())), preferred_element_type=F32)

    col = lax.broadcasted_iota(jnp.int32, (SUB, CHUNK), 1)
    rsub = lax.broadcasted_iota(jnp.int32, (SUB, CHUNK), 0)
    lane = lax.broadcasted_iota(jnp.int32, (SUB, HEAD_W), 1)
    s_rows = []
    for blk in range(CHUNK // SUB):
        lo = blk * SUB
        bi = b[lo:lo + SUB, :]
        qi = q[lo:lo + SUB, :]
        ki = k[lo:lo + SUB, :]
        parts = []
        for sr in range(SUB):
            t0 = 0 if sr < SUB // 2 else SUB // 2
            e = jnp.exp2(jnp.minimum(bi[t0:, :] - bi[sr:sr + 1, :], 0.0))
            parts.append(qi[t0:, :] * e * ki[sr:sr + 1, :])
        m3 = jnp.concatenate(parts, axis=0).astype(BF16)
        red = jnp.dot(m3, ones_bf, preferred_element_type=F32)
        halves = [jnp.zeros((SUB // 2, HEAD_W), F32), jnp.zeros((SUB // 2, HEAD_W), F32)]
        r0 = 0
        for sr in range(SUB):
            for hf in range(0 if sr < SUB // 2 else 1, 2):
                halves[hf] = jnp.where(lane[:SUB // 2] == lo + sr, red[r0:r0 + SUB // 2, :], halves[hf])
                r0 += SUB // 2
        diag = jnp.concatenate(halves, axis=0)[:, :CHUNK]
        s_blk = jnp.where(jnp.logical_and(col >= lo, col - lo <= rsub), diag, 0.0)
        if blk > 0:
            b0 = b[lo - 1:lo, :]
            kt = (k * jnp.exp2(jnp.minimum(b0 - b, 0.0))).astype(BF16)
            qd = (qi * jnp.exp2(bi - b0)).astype(BF16)
            off = lax.dot_general(qd, kt, (((1,), (1,)), ((), ())), preferred_element_type=F32)
            s_blk = jnp.where(col < lo, off, s_blk)
        s_rows.append(s_blk)
    scores = jnp.concatenate(s_rows, axis=0).astype(BF16)
    vb = v.astype(BF16)
    o = o + jnp.dot(scores, vb, preferred_element_type=F32)

    kd = (k * jnp.exp2(b_last - b)).astype(BF16)
    upd = lax.dot_general(vb, kd, (((0,), (0,)), ((), ())), preferred_element_type=F32)
    new_state_t = state_t * jnp.exp2(b_last) + upd
    return o, new_state_t


def _hgrn_kernel(q_ref, f_ref, v_ref, og_ref, lb_ref, nw_ref, y_ref, state, state_meta, *, nblk):
    i = pl.program_id(0)
    is_meta = i == 0

    @pl.when(is_meta)
    def _():
        state[...] = jnp.zeros_like(state)

    @pl.when(jnp.logical_and(i >= 1, (i - 1) % nblk == 0))
    def _():
        state[...] = state_meta[...]

    ones_bf = jnp.ones((HEAD_W, HEAD_W), BF16)
    nw = nw_ref[...]
    for hd in range(HEADS):
        sl = slice(hd * HEAD_W, (hd + 1) * HEAD_W)
        lbh = lb_ref[:, sl]
        st = state[hd]
        for c in range(BLK // CHUNK):
            rs = slice(c * CHUNK, (c + 1) * CHUNK)
            rowg = lax.broadcasted_iota(jnp.int32, (CHUNK, 1), 0) + c * CHUNK
            valid = rowg >= PAD_ROWS * is_meta.astype(jnp.int32)
            o, st = _hgrn_chunk(q_ref[rs, sl], f_ref[rs, sl], v_ref[rs, sl], lbh, st, valid, ones_bf)
            og = og_ref[rs, sl]
            y_ref[rs, sl] = (_rms(o, nw) * (og * _sigmoid(og))).astype(y_ref.dtype)
        state[hd] = st

    @pl.when(is_meta)
    def _():
        state_meta[...] = state[...]


def _hgrn(z_a, z_b, lb, nw, nreal, nblk):
    rows = z_a.shape[0]
    return pl.pallas_call(
        functools.partial(_hgrn_kernel, nblk=nblk),
        out_shape=jax.ShapeDtypeStruct((rows, WIDTH), BF16),
        grid=(nreal + 1,),
        in_specs=[_zspec(3, nreal), _zspec(0, nreal), _zspec(1, nreal), _zspec(2, nreal),
                  _pspec((1, WIDTH)), _pspec((1, HEAD_W))],
        out_specs=pl.BlockSpec((BLK, WIDTH), lambda i: (_row_block(i, nreal), 0)),
        scratch_shapes=[pltpu.VMEM((HEADS, HEAD_W, HEAD_W), F32), pltpu.VMEM((HEADS, HEAD_W, HEAD_W), F32)],
        compiler_params=_cparams(("arbitrary",)),
        name="hgrn2",
    )(z_a, z_b, z_b, z_b, lb, nw)


def _attn_lambda(lamv_ref, lambda_init):
    lv = lamv_ref[...]
    return (jnp.exp(jnp.sum(lv[0:1] * lv[1:2], axis=-1, keepdims=True))
            - jnp.exp(jnp.sum(lv[2:3] * lv[3:4], axis=-1, keepdims=True)) + lambda_init)


def _scaled_q(q_ref):
    qf = (q_ref[...].astype(F32) * (DIFF_HEAD_DIM ** -0.5)).astype(BF16)
    return qf[:, :DIFF_HEAD_DIM], qf[:, DIFF_HEAD_DIM:]


def _fold8(x, op):
    r, c = x.shape
    return op(x.reshape(r // 8, 8, c), axis=0)


def _attn_kernel(relb_ref, q_ref, k_ref, v_ref, km_ref, vm_ref, bias_ref, diag_ref, lamv_ref, sw_ref, y_ref,
                 s_sc, vt_sc, m_sc, l_sc, acc_sc, *, nqb, nblk, lambda_init):
    hd = pl.program_id(0)
    jq = pl.program_id(1) % nqb
    far = relb_ref[REL_BUCKETS - 1, hd] * LOG2E
    t1 = bias_ref[1] * LOG2E
    slot_diag, slot_meta = nqb - 1, nqb

    @pl.when(jq == 0)
    def _():
        for t in range(nblk):
            vt_sc[t // NSUB, :, (t % NSUB) * BLK:(t % NSUB + 1) * BLK] = (
                v_ref[t * BLK:(t + 1) * BLK, :].astype(F32).T.astype(BF16))

    qt = (q_ref[...].astype(F32) * (DIFF_HEAD_DIM ** -0.5)).T
    zero = jnp.zeros((DIFF_HEAD_DIM, QB), F32)
    qtp = (jnp.concatenate([qt[:DIFF_HEAD_DIM], zero], axis=0).astype(BF16),
           jnp.concatenate([zero, qt[DIFF_HEAD_DIM:]], axis=0).astype(BF16))

    def key_chunk(c):
        return k_ref[c * QB:(c + 1) * QB, :]

    def score_chunk(k_rows, slot, add_bias, r0=0, c0=0):
        r = k_rows.shape[0]
        for mp in range(2):
            s = add_bias(jnp.dot(k_rows, qtp[mp][:, c0:], preferred_element_type=F32) * LOG2E)
            s_sc[mp, slot, r0:r0 + r, c0:] = s
            m_sc[mp, :, c0:] = jnp.maximum(m_sc[mp, :, c0:], _fold8(s, jnp.max))

    def value_chunk(vt_cols, slot, m8, r0=0, c0=0):
        r = vt_cols.shape[1]
        for mp in range(2):
            p = jnp.exp2(s_sc[mp, slot, r0:r0 + r, c0:].reshape(r // 8, 8, QB - c0) - m8[mp][None, :, c0:])
            l_sc[mp, :, c0:] += jnp.sum(p, axis=0)
            acc_sc[mp, :, c0:] += jnp.dot(vt_cols, p.reshape(r, QB - c0).astype(BF16),
                                         preferred_element_type=F32)

    def diag_tiles(fn):
        for kb in range(NSUB):
            fn(kb, kb * BLK)

    def walk(n, fn_far, fn_meta, fn_diag):
        for c in range(n):
            fn_far(c)
        fn_meta()
        fn_diag()

    near = t1 - far
    first = jnp.full((1, BLK), jq, jnp.int32) == 0

    def meta_bias(s):
        head = s[:, 0:BLK] + (far + jnp.where(first, near[PAD_ROWS:, :], 0.0))
        return jnp.concatenate([head, s[:, BLK:] + far], axis=1)

    def meta_values(m8):
        v16 = vm_ref[PAD_ROWS:BLK, :]
        for mp in range(2):
            p = jnp.exp2(s_sc[mp, slot_meta, 0:N_META, :].reshape(N_META // 8, 8, QB) - m8[mp][None])
            l_sc[mp] += jnp.sum(p, axis=0)
            acc_sc[mp] += lax.dot_general(v16, p.reshape(N_META, QB).astype(BF16), (((0,), (0,)), ((), ())),
                                          preferred_element_type=F32)

    def far_bias(c):
        last = jnp.full((1, BLK), c, jnp.int32) == jq - 1

        def add(s):
            corner = s[QB - BLK:, 0:BLK] + (far + jnp.where(last, near, 0.0))
            bottom = jnp.concatenate([corner, s[QB - BLK:, BLK:] + far], axis=1)
            return jnp.concatenate([s[:QB - BLK, :] + far, bottom], axis=0)

        return add

    def both_passes(n):
        m_sc[...] = jnp.full_like(m_sc, NEG)
        walk(n, lambda c: score_chunk(key_chunk(c), c, far_bias(c)),
             lambda: score_chunk(km_ref[PAD_ROWS:BLK, :], slot_meta, meta_bias),
             lambda: diag_tiles(lambda kb, c0: score_chunk(
                 k_ref[n * QB + c0:n * QB + c0 + BLK, :], slot_diag,
                 lambda s: s + diag_ref[c0:c0 + BLK, c0:], r0=c0, c0=c0)))

        m8 = [jnp.broadcast_to(jnp.max(m_sc[mp], axis=0, keepdims=True), (8, QB)) for mp in range(2)]
        l_sc[...] = jnp.zeros_like(l_sc)
        acc_sc[...] = jnp.zeros_like(acc_sc)

        walk(n, lambda c: value_chunk(vt_sc[c], c, m8),
             lambda: meta_values(m8),
             lambda: diag_tiles(lambda kb, c0: value_chunk(
                 vt_sc[n, :, c0:c0 + BLK], slot_diag, m8, r0=c0, c0=c0)))

        lam = _attn_lambda(lamv_ref, lambda_init)
        l0 = jnp.sum(l_sc[0], axis=0, keepdims=True)
        l1 = jnp.sum(l_sc[1], axis=0, keepdims=True)
        ot = acc_sc[0] / l0 - lam * (acc_sc[1] / l1)
        ot = ot * lax.rsqrt(jnp.mean(ot * ot, axis=0, keepdims=True) + EPS)
        y_ref[...] = (ot.T * sw_ref[...] * (1.0 - lambda_init)).astype(y_ref.dtype)

    for n in range(nqb):
        pl.when(jq == n)(functools.partial(both_passes, n))


def _attn_meta_kernel(q_ref, k_ref, v_ref, bias_ref, lamv_ref, sw_ref, y_ref, *, lambda_init):
    qs = _scaled_q(q_ref)
    kb = k_ref[...]
    colmask = jnp.where(lax.broadcasted_iota(jnp.int32, (1, BLK), 1) >= PAD_ROWS, 0.0, NEG)
    outs = []
    bias = bias_ref[0].T + colmask
    for mp in range(2):
        s = lax.dot_general(qs[mp], kb[:, mp * DIFF_HEAD_DIM:(mp + 1) * DIFF_HEAD_DIM],
                            (((1,), (1,)), ((), ())), preferred_element_type=F32) + bias
        p = jnp.exp(s - jnp.max(s, axis=-1, keepdims=True))
        acc = jnp.dot(p.astype(BF16), v_ref[...], preferred_element_type=F32)
        outs.append(acc / jnp.sum(p, axis=-1, keepdims=True))
    o = outs[0] - _attn_lambda(lamv_ref, lambda_init) * outs[1]
    o = _rms(o, sw_ref[...]) * (1.0 - lambda_init)
    row = lax.broadcasted_iota(jnp.int32, (BLK, 1), 0)
    y_ref[...] = jnp.where(row >= PAD_ROWS, o, 0.0).astype(y_ref.dtype)


def _attn(zd, rel_bias, bias_tiles, bias_diag, lam_vec, subln_w, nreal, nblk, lambda_init):
    nqb = nblk * BLK // QB
    nq_total = nreal * BLK // QB
    seq = nblk * BLK
    kcol, vcol = HEADS, 2 * HEADS
    y_main = pl.pallas_call(
        functools.partial(_attn_kernel, nqb=nqb, nblk=nblk, lambda_init=lambda_init),
        out_shape=jax.ShapeDtypeStruct((nreal * BLK, WIDTH), BF16),
        grid=(HEADS, nq_total),
        in_specs=[pl.BlockSpec(memory_space=pltpu.SMEM),
                  pl.BlockSpec((QB, HEAD_W), lambda h, i: (i, h)),
                  pl.BlockSpec((seq, HEAD_W), lambda h, i: (i // nqb, kcol + h)),
                  pl.BlockSpec((seq, HEAD_W), lambda h, i: (i // nqb, vcol + h)),
                  pl.BlockSpec((BLK, HEAD_W), lambda h, i: (nreal, kcol + h)),
                  pl.BlockSpec((BLK, HEAD_W), lambda h, i: (nreal, vcol + h)),
                  pl.BlockSpec((None, 2, BLK, BLK), lambda h, i: (h, 0, 0, 0)),
                  pl.BlockSpec((None, QB, QB), lambda h, i: (h, 0, 0)),
                  pl.BlockSpec((4, DIFF_HEAD_DIM), lambda h, i: (0, 0)),
                  pl.BlockSpec((1, HEAD_W), lambda h, i: (0, 0))],
        out_specs=pl.BlockSpec((QB, HEAD_W), lambda h, i: (i, h)),
        scratch_shapes=[pltpu.VMEM((2, nqb + 1, QB, QB), F32),
                        pltpu.VMEM((nqb, HEAD_W, QB), BF16),
                        pltpu.VMEM((2, 8, QB), F32),
                        pltpu.VMEM((2, 8, QB), F32),
                        pltpu.VMEM((2, HEAD_W, QB), F32)],
        compiler_params=_cparams(("arbitrary", "arbitrary")),
        name="diff_attn",
    )(rel_bias, zd, zd, zd, zd, zd, bias_tiles, bias_diag, lam_vec, subln_w)
    y_meta = pl.pallas_call(
        functools.partial(_attn_meta_kernel, lambda_init=lambda_init),
        out_shape=jax.ShapeDtypeStruct((BLK, WIDTH), BF16),
        grid=(HEADS,),
        in_specs=[pl.BlockSpec((BLK, HEAD_W), lambda h: (nreal, h)),
                  pl.BlockSpec((BLK, HEAD_W), lambda h: (nreal, kcol + h)),
                  pl.BlockSpec((BLK, HEAD_W), lambda h: (nreal, vcol + h)),
                  pl.BlockSpec((None, 2, BLK, BLK), lambda h: (h, 0, 0, 0)),
                  pl.BlockSpec((4, DIFF_HEAD_DIM), lambda h: (0, 0)),
                  pl.BlockSpec((1, HEAD_W), lambda h: (0, 0))],
        out_specs=pl.BlockSpec((BLK, HEAD_W), lambda h: (0, h)),
        compiler_params=_cparams(("arbitrary",)),
        name="diff_attn_meta",
    )(zd, zd, zd, bias_tiles, lam_vec, subln_w)
    return jnp.concatenate([y_main, y_meta], axis=0)


def _largest_tile(rows, cap, align=16):
    best = align
    for t in range(align, cap + 1, align):
        if rows % t == 0:
            best = t
    return best


def _forward(x, meta_tokens, rel_bias, hgrn_lower_bounds, norm_mix_pre, norm_mix_post, norm_ffn_pre,
             norm_ffn_post, w_in, lru_conv_w, lru_conv_b, lru_w_a, lru_b_a, lru_w_x, lru_b_x, lru_lambda,
             pool_w, pool_scale, hgrn_norm, diff_lambda, diff_subln, w_branch, w_out, ffn_w_gu, ffn_w_down):
    bsz, seq, _ = x.shape
    nblk = seq // BLK
    nreal = bsz * nblk
    rows = (nreal + 1) * BLK
    rows_real = nreal * BLK
    tm_big = _largest_tile(rows, 1040)
    tm_epi = _largest_tile(rows, 640)
    tm_down = _largest_tile(rows, 320)
    tm_last = _largest_tile(rows_real, 256)

    def vec(a):
        return a.reshape(1, -1)

    lbs, bias_tiles, bias_diag = _prologue(hgrn_lower_bounds, rel_bias)
    h, hn = _embed(x.reshape(rows_real, D_MODEL), meta_tokens, vec(norm_mix_pre[0]), nreal)

    for layer in range(DEPTH):
        lambda_init = 0.8 - 0.6 * math.exp(-0.3 * layer)
        z = _mixer_in_proj(hn, w_in, layer, 0, 4 * WIDTH, F32, tm_big, 1024, 1024)
        z_b = _mixer_in_proj(hn, w_in, layer, 4 * WIDTH, 3 * WIDTH, F32, tm_big, 3 * WIDTH, WIDTH)
        zd = _mixer_in_proj(hn, w_in, layer, 7 * WIDTH, 3 * WIDTH, BF16, tm_big, 3 * WIDTH, WIDTH)
        y_a, y_b = _lru_pool(z, lru_conv_w[layer], vec(lru_conv_b[layer]), lru_w_a[layer], vec(lru_b_a[layer]),
                             lru_w_x[layer], vec(lru_b_x[layer]), vec(lru_lambda[layer]),
                             pool_w[layer], vec(pool_scale[layer]), nreal, nblk)
        y_c = _hgrn(z, z_b, lbs[layer:layer + 1], vec(hgrn_norm[layer]), nreal, nblk)
        y_d = _attn(zd, rel_bias, bias_tiles, bias_diag, diff_lambda[layer], vec(diff_subln[layer]),
                    nreal, nblk, lambda_init)
        merged = _gate_merge(hn, w_in, (y_a, y_b, y_c, y_d), w_branch, layer, tm_big, 256)
        h, hn = _out_proj(merged, _cast_layer_bf16(w_out, layer, 512), h, vec(norm_mix_post[layer]),
                          vec(norm_ffn_pre[layer]), tm_epi)
        a = _swiglu_up(hn, ffn_w_gu, layer, tm_big, 512)
        last = layer == DEPTH - 1
        w_next = vec(norm_mix_pre[layer + 1]) if not last else vec(norm_mix_pre[layer])
        h, hn = _down_proj(a, _cast_layer_bf16(ffn_w_down, layer, 512), h, vec(norm_ffn_post[layer]), w_next,
                           rows_real if last else rows, tm_last if last else tm_down, not last)
    return h.reshape(bsz, seq, D_MODEL)


def kernel(x, meta_tokens, rel_bias, hgrn_lower_bounds, norm_mix_pre, norm_mix_post, norm_ffn_pre, norm_ffn_post, w_in, lru_conv_w, lru_conv_b, lru_w_a, lru_b_a, lru_w_x, lru_b_x, lru_lambda, pool_w, pool_scale, hgrn_norm, diff_lambda, diff_subln, w_branch, w_out, ffn_w_gu, ffn_w_down):
    return _forward(x, meta_tokens, rel_bias, hgrn_lower_bounds, norm_mix_pre, norm_mix_post, norm_ffn_pre,
                    norm_ffn_post, w_in, lru_conv_w, lru_conv_b, lru_w_a, lru_b_a, lru_w_x, lru_b_x, lru_lambda,
                    pool_w, pool_scale, hgrn_norm, diff_lambda, diff_subln, w_branch, w_out, ffn_w_gu, ffn_w_down)
```

```python
import functools
import math

import numpy as np
import jax
import jax.numpy as jnp
from jax import lax
from jax.experimental import pallas as pl
from jax.experimental.pallas import tpu as pltpu

F32 = jnp.float32
BF16 = jnp.bfloat16

D_MODEL = 2048
SEQ = 2048
DEPTH = 2
N_META = 16
BLK = 128
PAD_ROWS = BLK - N_META
QB = 512
NSUB = QB // BLK
WIDTH = 512
HEADS = 4
HEAD_W = 128
CHUNK = 64
SUB = 16
LRU_C = 8.0
POOL_WINDOWS = (2, 4, 8, 16)
DIFF_HEAD_DIM = 64
REL_BUCKETS = 32
REL_MAX_DIST = 128
FFN_HIDDEN = 5632
MIX_COLS = 10 * WIDTH
NEG = -1e30
EPS = 1e-6
LOG2E = math.log2(math.e)
VMEM_LIMIT = 56 * 1024 * 1024


def _cparams(sem):
    return pltpu.CompilerParams(dimension_semantics=sem, vmem_limit_bytes=VMEM_LIMIT)


def _rms(x, w):
    return x * lax.rsqrt(jnp.mean(x * x, axis=-1, keepdims=True) + EPS) * w


def _log_sigmoid(z):
    return -(jnp.maximum(-z, 0.0) + jnp.log(1.0 + jnp.exp(-jnp.abs(z))))


def _sigmoid(z):
    return 1.0 / (1.0 + jnp.exp(-z))


def _gelu_tanh(x):
    c = math.sqrt(2.0 / math.pi)
    return 0.5 * x * (1.0 + jnp.tanh(c * (x + 0.044715 * (x * x * x))))


def _shift_rows(x, s, fill, row):
    return jnp.where(row >= s, pltpu.roll(x, s, axis=0), fill)


def _bucket_tiles():
    r = np.arange(BLK)[None, :]
    c = np.arange(BLK)[:, None]
    max_exact = REL_BUCKETS // 2

    def bucket(n):
        nf = np.maximum(n, 1).astype(np.float32)
        large = max_exact + (np.log(nf / np.float32(max_exact)) / np.float32(math.log(REL_MAX_DIST / max_exact))
                             * np.float32(REL_BUCKETS - max_exact)).astype(np.int32)
        large = np.minimum(large, REL_BUCKETS - 1)
        return np.where(n < max_exact, n, large).astype(np.int32)

    d0 = r - c
    t0 = np.where(d0 >= 0, bucket(np.maximum(d0, 0)), -1)
    t1 = bucket(BLK + r - c)
    return np.stack([t0, t1]).astype(np.int32)


def _prologue_kernel(lbraw_ref, relb_ref, idx_ref, lb_ref, bias_ref, diag_ref):
    raw = lbraw_ref[...]
    mx = jnp.max(raw, axis=0, keepdims=True)
    e = jnp.exp(raw - mx)
    sm = e / jnp.sum(e, axis=0, keepdims=True)
    cum = sm[0:1]
    lb_ref[0:1, :] = cum - sm[0:1]
    for l in range(1, DEPTH):
        cum = cum + sm[l:l + 1]
        lb_ref[l:l + 1, :] = cum - sm[0:1]
    for t in range(2):
        idx = idx_ref[t]
        for hd in range(HEADS):
            acc = jnp.zeros((BLK, BLK), F32)
            for bk in range(REL_BUCKETS):
                acc = jnp.where(idx == bk, relb_ref[bk, hd], acc)
            bias_ref[hd, t] = jnp.where(idx < 0, NEG, acc)
    for hd in range(HEADS):
        far = relb_ref[REL_BUCKETS - 1, hd]
        for kb in range(NSUB):
            for qb in range(NSUB):
                delta = qb - kb
                if delta == 0:
                    blk = bias_ref[hd, 0]
                elif delta == 1:
                    blk = bias_ref[hd, 1]
                else:
                    blk = jnp.full((BLK, BLK), far if delta > 1 else NEG, F32)
                diag_ref[hd, kb * BLK:(kb + 1) * BLK, qb * BLK:(qb + 1) * BLK] = blk * LOG2E


def _prologue(hgrn_lower_bounds, rel_bias):
    idx = jnp.asarray(_bucket_tiles())
    vmem = pl.BlockSpec(memory_space=pltpu.VMEM)
    return pl.pallas_call(
        _prologue_kernel,
        out_shape=(jax.ShapeDtypeStruct((DEPTH, WIDTH), F32),
                   jax.ShapeDtypeStruct((HEADS, 2, BLK, BLK), F32),
                   jax.ShapeDtypeStruct((HEADS, QB, QB), F32)),
        in_specs=[vmem, pl.BlockSpec(memory_space=pltpu.SMEM), vmem],
        out_specs=(vmem, vmem, vmem),
        name="prologue",
    )(hgrn_lower_bounds, rel_bias, idx)


EMBED_ROWS = 512


def _embed_kernel(x_ref, meta_ref, w_ref, h_ref, hn_ref, *, nfull):
    i = pl.program_id(0)

    @pl.when(i < nfull)
    def _():
        h_ref[...] = x_ref[...]

    @pl.when(i == nfull)
    def _():
        h_ref[...] = jnp.zeros_like(h_ref)
        h_ref[PAD_ROWS:BLK, :] = meta_ref[...]

    hn_ref[...] = _rms(h_ref[...], w_ref[...]).astype(BF16)


def _embed(x2d, meta, w_pre, nreal):
    rows = (nreal + 1) * BLK
    nfull = nreal * BLK // EMBED_ROWS
    return pl.pallas_call(
        functools.partial(_embed_kernel, nfull=nfull),
        out_shape=(jax.ShapeDtypeStruct((rows, D_MODEL), F32),
                   jax.ShapeDtypeStruct((rows, D_MODEL), BF16)),
        grid=(nfull + 1,),
        in_specs=[pl.BlockSpec((EMBED_ROWS, D_MODEL), lambda i: (jnp.minimum(i, nfull - 1), 0)),
                  pl.BlockSpec((N_META, D_MODEL), lambda i: (0, 0)),
                  pl.BlockSpec((1, D_MODEL), lambda i: (0, 0))],
        out_specs=(pl.BlockSpec((EMBED_ROWS, D_MODEL), lambda i: (i, 0)),
                   pl.BlockSpec((EMBED_ROWS, D_MODEL), lambda i: (i, 0))),
        compiler_params=_cparams(("arbitrary",)),
        name="embed",
    )(x2d, meta, w_pre)


def _cast_tiles_once(pairs):
    @pl.when(pl.program_id(1) == 0)
    def _():
        for src, dst in pairs:
            dst[...] = src[...].astype(BF16)


def _matmul_kernel(x_ref, *rest):
    *w_refs, o_ref, wbf = rest
    wblk = w_refs[0].shape[-1]
    _cast_tiles_once([(w, wbf.at[:, j * wblk:(j + 1) * wblk]) for j, w in enumerate(w_refs)])
    o_ref[...] = jnp.dot(x_ref[...], wbf[...], preferred_element_type=F32).astype(o_ref.dtype)


def _mixer_in_proj(hn, w_in, layer, col0, ncols, out_dtype, tm, tn, wblk):
    rows, k = hn.shape
    nw = tn // wblk
    w_specs = [pl.BlockSpec((None, k, wblk), functools.partial(
        lambda n, m, j: (layer, 0, col0 // wblk + n * nw + j), j=j)) for j in range(nw)]
    return pl.pallas_call(
        _matmul_kernel,
        out_shape=jax.ShapeDtypeStruct((rows, ncols), out_dtype),
        grid=(ncols // tn, rows // tm),
        in_specs=[pl.BlockSpec((tm, k), lambda n, m: (m, 0))] + w_specs,
        out_specs=pl.BlockSpec((tm, tn), lambda n, m: (m, n)),
        scratch_shapes=[pltpu.VMEM((k, tn), BF16)],
        compiler_params=_cparams(("arbitrary", "arbitrary")),
        name="mixer_in_proj",
    )(hn, *([w_in] * nw))


def _cast_kernel(w_ref, o_ref):
    o_ref[...] = w_ref[...].astype(BF16)


def _cast_layer_bf16(w, layer, tr):
    _, r, c = w.shape
    return pl.pallas_call(
        _cast_kernel,
        out_shape=jax.ShapeDtypeStruct((r, c), BF16),
        grid=(r // tr,),
        in_specs=[pl.BlockSpec((None, tr, c), lambda i: (layer, i, 0))],
        out_specs=pl.BlockSpec((tr, c), lambda i: (i, 0)),
        compiler_params=_cparams(("arbitrary",)),
        name="cast_bf16",
    )(w)


def _residual_epilogue(acc, h_ref, wpost_ref, wnext_ref, hnew_ref, hn_ref):
    h_new = h_ref[...] + _rms(acc, wpost_ref[...])
    hnew_ref[...] = h_new
    if hn_ref is not None:
        hn_ref[...] = _rms(h_new, wnext_ref[...]).astype(BF16)


EPI_ROWS = 160


def _row_subtiles(tm):
    sub = next(s for s in (EPI_ROWS, 128, 64, 32, 16) if tm % s == 0)
    return [slice(r, r + sub) for r in range(0, tm, sub)]


def _gate_merge_kernel(hn_ref, g0, g1, g2, g3, y0, y1, y2, y3, wb_ref, o_ref, gbf, wbbf):
    _cast_tiles_once([(g, gbf.at[k]) for k, g in enumerate((g0, g1, g2, g3))] + [(wb_ref, wbbf)])
    hn = hn_ref[...]
    acc = None
    for k, y_ref in enumerate((y0, y1, y2, y3)):
        gate = _sigmoid(jnp.dot(hn, gbf[k], preferred_element_type=F32))
        proj = jnp.dot(y_ref[...], wbbf[k], preferred_element_type=F32)
        acc = gate * proj if acc is None else acc + gate * proj
    o_ref[...] = acc.astype(o_ref.dtype)


def _gate_merge(hn, w_in, ys, w_branch, layer, tm, tn):
    rows, k = hn.shape
    gate_specs = [
        pl.BlockSpec((None, k, tn), functools.partial(
            lambda n, m, base: (layer, 0, base + n), base=(MIX_COLS + br * D_MODEL) // tn))
        for br in range(4)]
    y_specs = [pl.BlockSpec((tm, WIDTH), lambda n, m: (m, 0)) for _ in range(4)]
    return pl.pallas_call(
        _gate_merge_kernel,
        out_shape=jax.ShapeDtypeStruct((rows, D_MODEL), BF16),
        grid=(D_MODEL // tn, rows // tm),
        in_specs=[pl.BlockSpec((tm, k), lambda n, m: (m, 0))] + gate_specs + y_specs
                 + [pl.BlockSpec((None, 4, WIDTH, tn), lambda n, m: (layer, 0, 0, n))],
        out_specs=pl.BlockSpec((tm, tn), lambda n, m: (m, n)),
        scratch_shapes=[pltpu.VMEM((4, k, tn), BF16), pltpu.VMEM((4, WIDTH, tn), BF16)],
        compiler_params=_cparams(("arbitrary", "arbitrary")),
        name="gate_merge",
    )(hn, w_in, w_in, w_in, w_in, *ys, w_branch)


def _out_proj_kernel(x_ref, w_ref, h_ref, wpost_ref, wnext_ref, hnew_ref, hn_ref):
    for rs in _row_subtiles(x_ref.shape[0]):
        acc = jnp.dot(x_ref[rs, :], w_ref[...], preferred_element_type=F32)
        _residual_epilogue(acc, h_ref.at[rs, :], wpost_ref, wnext_ref, hnew_ref.at[rs, :], hn_ref.at[rs, :])


def _out_proj(merged, w_out, h, w_post, w_next, tm):
    rows = h.shape[0]
    row_spec = pl.BlockSpec((tm, D_MODEL), lambda m: (m, 0))
    return pl.pallas_call(
        _out_proj_kernel,
        out_shape=(jax.ShapeDtypeStruct((rows, D_MODEL), F32),
                   jax.ShapeDtypeStruct((rows, D_MODEL), BF16)),
        grid=(rows // tm,),
        in_specs=[row_spec, _resident((D_MODEL, D_MODEL), (0, 0)), row_spec,
                  _pspec((1, D_MODEL)), _pspec((1, D_MODEL))],
        out_specs=(row_spec, row_spec),
        compiler_params=_cparams(("arbitrary",)),
        name="out_proj",
    )(merged, w_out, h, w_post, w_next)


def _swiglu_up_kernel(x_ref, wg_ref, wu_ref, o_ref, wgbf, wubf):
    _cast_tiles_once([(wg_ref, wgbf), (wu_ref, wubf)])
    x = x_ref[...]
    g = jnp.dot(x, wgbf[...], preferred_element_type=F32)
    u = jnp.dot(x, wubf[...], preferred_element_type=F32)
    o_ref[...] = (g * _sigmoid(g) * u).astype(o_ref.dtype)


def _swiglu_up(hn, w_gu, layer, tm, tn):
    rows, k = hn.shape
    nt = FFN_HIDDEN // tn
    return pl.pallas_call(
        _swiglu_up_kernel,
        out_shape=jax.ShapeDtypeStruct((rows, FFN_HIDDEN), BF16),
        grid=(nt, rows // tm),
        in_specs=[pl.BlockSpec((tm, k), lambda n, m: (m, 0)),
                  pl.BlockSpec((None, k, tn), lambda n, m: (layer, 0, n)),
                  pl.BlockSpec((None, k, tn), lambda n, m: (layer, 0, nt + n))],
        out_specs=pl.BlockSpec((tm, tn), lambda n, m: (m, n)),
        scratch_shapes=[pltpu.VMEM((k, tn), BF16), pltpu.VMEM((k, tn), BF16)],
        compiler_params=_cparams(("arbitrary", "arbitrary")),
        name="swiglu_up",
    )(hn, w_gu, w_gu)


def _down_proj_kernel(a_ref, w_ref, h_ref, wpost_ref, wnext_ref, hnew_ref, hn_ref=None):
    for rs in _row_subtiles(a_ref.shape[0]):
        acc = jnp.dot(a_ref[rs, :], w_ref[...], preferred_element_type=F32)
        _residual_epilogue(acc, h_ref.at[rs, :], wpost_ref, wnext_ref, hnew_ref.at[rs, :],
                           None if hn_ref is None else hn_ref.at[rs, :])


def _down_proj(a, w_down, h, w_post, w_next, rows_out, tm, emit_hn):
    row_spec = lambda w: pl.BlockSpec((tm, w), lambda m: (m, 0))
    out_shape = [jax.ShapeDtypeStruct((rows_out, D_MODEL), F32)]
    out_specs = [row_spec(D_MODEL)]
    if emit_hn:
        out_shape.append(jax.ShapeDtypeStruct((rows_out, D_MODEL), BF16))
        out_specs.append(row_spec(D_MODEL))
    res = pl.pallas_call(
        _down_proj_kernel,
        out_shape=tuple(out_shape),
        grid=(rows_out // tm,),
        in_specs=[row_spec(FFN_HIDDEN), _resident((FFN_HIDDEN, D_MODEL), (0, 0)), row_spec(D_MODEL),
                  _pspec((1, D_MODEL)), _pspec((1, D_MODEL))],
        out_specs=tuple(out_specs),
        compiler_params=_cparams(("arbitrary",)),
        name="down_proj",
    )(a, w_down, h, w_post, w_next)
    return res if emit_hn else (res[0], None)


def _row_block(i, nreal):
    return (i + nreal) % (nreal + 1)


def _zspec(col_block, nreal):
    return pl.BlockSpec((BLK, WIDTH), lambda i: (_row_block(i, nreal), col_block))


def _pspec(shape):
    nd = len(shape)
    return pl.BlockSpec(shape, lambda *_: (0,) * nd)


def _resident(shape, index):
    return pl.BlockSpec(shape, lambda *_: index, pipeline_mode=pl.Buffered(1))


def _lru_kernel(u_ref, gate_ref, cw_ref, cb_ref, wa_ref, ba_ref, wx_ref, bx_ref, lam_ref, y_ref,
                ubuf, hst, hist_meta, h_meta, *, nblk):
    i = pl.program_id(0)
    is_meta = i == 0

    @pl.when(is_meta)
    def _():
        ubuf[0:8, :] = jnp.zeros((8, WIDTH), F32)
        hst[...] = jnp.zeros_like(hst)

    @pl.when(jnp.logical_and(i >= 1, (i - 1) % nblk == 0))
    def _():
        ubuf[0:8, :] = hist_meta[...]
        hst[...] = h_meta[...]

    u = u_ref[...]
    ubuf[8:8 + BLK, :] = u
    cw = cw_ref[...]
    xc = (cb_ref[...] + cw[3:4] * u + cw[2:3] * ubuf[7:7 + BLK, :]
          + cw[1:2] * ubuf[6:6 + BLK, :] + cw[0:1] * ubuf[5:5 + BLK, :])
    xb = xc.astype(BF16)
    ra, ia = [], []
    for hd in range(HEADS):
        sl = slice(hd * HEAD_W, (hd + 1) * HEAD_W)
        ra.append(jnp.dot(xb[:, sl], wa_ref[hd].astype(BF16), preferred_element_type=F32))
        ia.append(jnp.dot(xb[:, sl], wx_ref[hd].astype(BF16), preferred_element_type=F32))
    r = _sigmoid(jnp.concatenate(ra, axis=1) + ba_ref[...])
    ig = _sigmoid(jnp.concatenate(ia, axis=1) + bx_ref[...])
    lam = lam_ref[...]
    softplus_neg_lam = jnp.maximum(-lam, 0.0) + jnp.log1p(jnp.exp(-jnp.abs(lam)))
    log_a = -LRU_C * r * softplus_neg_lam
    a = jnp.exp(log_a)
    bb = jnp.sqrt(-jnp.tanh(log_a) * (a * a + 1.0)) * (ig * xc)
    row = lax.broadcasted_iota(jnp.int32, (BLK, 1), 0)
    bb = jnp.where(row >= PAD_ROWS * is_meta.astype(jnp.int32), bb, 0.0)

    acum, bcum = a, bb
    s = 1
    while s < BLK:
        a_sh = _shift_rows(acum, s, 1.0, row)
        b_sh = _shift_rows(bcum, s, 0.0, row)
        bcum = acum * b_sh + bcum
        acum = acum * a_sh
        s *= 2
    h = acum * hst[0:1, :] + bcum
    y_ref[...] = (h * _gelu_tanh(gate_ref[...])).astype(y_ref.dtype)

    hist = u[BLK - 8:BLK, :]
    hlast = jnp.broadcast_to(h[BLK - 1:BLK, :], (8, WIDTH))
    ubuf[0:8, :] = hist
    hst[...] = hlast

    @pl.when(is_meta)
    def _():
        hist_meta[...] = hist
        h_meta[...] = hlast


POOL_HIST = 16


def _pool_kernel(u_ref, pw_ref, ps_ref, y_ref, ubuf, hist_meta, *, nblk):
    i = pl.program_id(0)
    is_meta = i == 0

    @pl.when(is_meta)
    def _():
        ubuf[0:POOL_HIST, :] = jnp.zeros((POOL_HIST, WIDTH), F32)

    @pl.when(jnp.logical_and(i >= 1, (i - 1) % nblk == 0))
    def _():
        ubuf[0:POOL_HIST, :] = hist_meta[...]

    u = u_ref[...]
    ubuf[POOL_HIST:POOL_HIST + BLK, :] = u
    row = lax.broadcasted_iota(jnp.int32, (BLK, 1), 0)
    meta_i = is_meta.astype(jnp.int32)
    pos1 = row + 1 - PAD_ROWS * meta_i + 2 * POOL_HIST * (1 - meta_i)
    outs = []
    for g, win in enumerate(POOL_WINDOWS):
        sl = slice(g * HEAD_W, (g + 1) * HEAD_W)
        acc = u[:, sl]
        for d in range(1, win):
            acc = acc + ubuf[POOL_HIST - d:POOL_HIST - d + BLK, sl]
        count = jnp.clip(pos1, 1, win).astype(F32)
        pooled = acc / count - u[:, sl]
        outs.append(jnp.dot(pooled.astype(BF16), pw_ref[g].astype(BF16), preferred_element_type=F32))
    y_ref[...] = (jnp.concatenate(outs, axis=1) * ps_ref[...]).astype(y_ref.dtype)

    hist = u[BLK - POOL_HIST:BLK, :]
    ubuf[0:POOL_HIST, :] = hist

    @pl.when(is_meta)
    def _():
        hist_meta[...] = hist


def _lru_pool_kernel(u_ref, gate_ref, cw_ref, cb_ref, wa_ref, ba_ref, wx_ref, bx_ref, lam_ref,
                     pu_ref, pw_ref, ps_ref, ya_ref, yb_ref,
                     ubuf, hst, hist_meta, h_meta, pbuf, phist_meta, *, nblk):
    _lru_kernel(u_ref, gate_ref, cw_ref, cb_ref, wa_ref, ba_ref, wx_ref, bx_ref, lam_ref, ya_ref,
                ubuf, hst, hist_meta, h_meta, nblk=nblk)
    _pool_kernel(pu_ref, pw_ref, ps_ref, yb_ref, pbuf, phist_meta, nblk=nblk)


def _lru_pool(z, cw, cb, wa, ba, wx, bx, lam, pw, ps, nreal, nblk):
    rows = z.shape[0]
    out_spec = pl.BlockSpec((BLK, WIDTH), lambda i: (_row_block(i, nreal), 0))
    return pl.pallas_call(
        functools.partial(_lru_pool_kernel, nblk=nblk),
        out_shape=(jax.ShapeDtypeStruct((rows, WIDTH), BF16), jax.ShapeDtypeStruct((rows, WIDTH), BF16)),
        grid=(nreal + 1,),
        in_specs=[_zspec(0, nreal), _zspec(1, nreal),
                  _pspec((4, WIDTH)), _pspec((1, WIDTH)),
                  _pspec((HEADS, HEAD_W, HEAD_W)), _pspec((1, WIDTH)),
                  _pspec((HEADS, HEAD_W, HEAD_W)), _pspec((1, WIDTH)), _pspec((1, WIDTH)),
                  _zspec(2, nreal), _pspec((4, HEAD_W, HEAD_W)), _pspec((1, WIDTH))],
        out_specs=(out_spec, out_spec),
        scratch_shapes=[pltpu.VMEM((8 + BLK, WIDTH), F32), pltpu.VMEM((8, WIDTH), F32),
                        pltpu.VMEM((8, WIDTH), F32), pltpu.VMEM((8, WIDTH), F32),
                        pltpu.VMEM((POOL_HIST + BLK, WIDTH), F32), pltpu.VMEM((POOL_HIST, WIDTH), F32)],
        compiler_params=_cparams(("arbitrary",)),
        name="lru_pool",
    )(z, z, cw, cb, wa, ba, wx, bx, lam, z, pw, ps)


def _hgrn_chunk(q, z, v, lbh, state_t, valid, ones_bf):
    ls = _log_sigmoid(z)
    x1 = jnp.log(lbh)
    x2 = jnp.log1p(-lbh) + ls
    mx = jnp.maximum(x1, x2)
    g = mx + jnp.log(1.0 + jnp.exp(-jnp.abs(x1 - x2)))
    k = (1.0 - lbh) * _sigmoid(-z)
    if valid is not None:
        g = jnp.where(valid, g, 0.0)
    row = lax.broadcasted_iota(jnp.int32, (CHUNK, 1), 0)
    b = g * LOG2E
    s = 1
    while s < CHUNK:
        b = b + _shift_rows(b, s, 0.0, row)
        s *= 2
    b_last = b[CHUNK - 1:CHUNK, :]

    qe = (q * jnp.exp2(b)).astype(BF16)
    o = lax.dot_general(qe, state_t.astype(BF16), (((1,), (1,)), ((), ())), preferred_element_type=F32)

    col = lax.broadcasted_iota(jnp.int32, (SUB, CHUNK), 1)
    rsub = lax.broadcasted_iota(jnp.int32, (SUB, CHUNK), 0)
    lane = lax.broadcasted_iota(jnp.int32, (SUB, HEAD_W), 1)
    s_rows = []
    for blk in range(CHUNK // SUB):
        lo = blk * SUB
        bi = b[lo:lo + SUB, :]
        qi = q[lo:lo + SUB, :]
        ki = k[lo:lo + SUB, :]
        parts = []
        for sr in range(SUB):
            t0 = 0 if sr < SUB // 2 else SUB // 2
            e = jnp.exp2(jnp.minimum(bi[t0:, :] - bi[sr:sr + 1, :], 0.0))
            parts.append(qi[t0:, :] * e * ki[sr:sr + 1, :])
        m3 = jnp.concatenate(parts, axis=0).astype(BF16)
        red = jnp.dot(m3, ones_bf, preferred_element_type=F32)
        halves = [jnp.zeros((SUB // 2, HEAD_W), F32), jnp.zeros((SUB // 2, HEAD_W), F32)]
        r0 = 0
        for sr in range(SUB):
            for hf in range(0 if sr < SUB // 2 else 1, 2):
                halves[hf] = jnp.where(lane[:SUB // 2] == lo + sr, red[r0:r0 + SUB // 2, :], halves[hf])
                r0 += SUB // 2
        diag = jnp.concatenate(halves, axis=0)[:, :CHUNK]
        s_blk = jnp.where(jnp.logical_and(col >= lo, col - lo <= rsub), diag, 0.0)
        if blk > 0:
            b0 = b[lo - 1:lo, :]
            kt = (k * jnp.exp2(jnp.minimum(b0 - b, 0.0))).astype(BF16)
            qd = (qi * jnp.exp2(bi - b0)).astype(BF16)
            off = lax.dot_general(qd, kt, (((1,), (1,)), ((), ())), preferred_element_type=F32)
            s_blk = jnp.where(col < lo, off, s_blk)
        s_rows.append(s_blk)
    scores = jnp.concatenate(s_rows, axis=0).astype(BF16)
    vb = v.astype(BF16)
    o = o + jnp.dot(scores, vb, preferred_element_type=F32)

    kd = (k * jnp.exp2(b_last - b)).astype(BF16)
    upd = lax.dot_general(vb, kd, (((0,), (0,)), ((), ())), preferred_element_type=F32)
    new_state_t = state_t * jnp.exp2(b_last) + upd
    return o, new_state_t


def _hgrn_kernel(q_ref, f_ref, v_ref, og_ref, lb_ref, nw_ref, y_ref, state, state_meta, *, nblk):
    i = pl.program_id(0)
    is_meta = i == 0

    @pl.when(is_meta)
    def _():
        state[...] = jnp.zeros_like(state)

    @pl.when(jnp.logical_and(i >= 1, (i - 1) % nblk == 0))
    def _():
        state[...] = state_meta[...]

    ones_bf = jnp.ones((HEAD_W, HEAD_W), BF16)
    nw = nw_ref[...]
    for hd in range(HEADS):
        sl = slice(hd * HEAD_W, (hd + 1) * HEAD_W)
        lbh = lb_ref[:, sl]
        st = state[hd]
        for c in range(BLK // CHUNK):
            rs = slice(c * CHUNK, (c + 1) * CHUNK)
            rowg = lax.broadcasted_iota(jnp.int32, (CHUNK, 1), 0) + c * CHUNK
            valid = rowg >= PAD_ROWS * is_meta.astype(jnp.int32)
            o, st = _hgrn_chunk(q_ref[rs, sl], f_ref[rs, sl], v_ref[rs, sl], lbh, st, valid, ones_bf)
            og = og_ref[rs, sl]
            y_ref[rs, sl] = (_rms(o, nw) * (og * _sigmoid(og))).astype(y_ref.dtype)
        state[hd] = st

    @pl.when(is_meta)
    def _():
        state_meta[...] = state[...]


def _hgrn(z_a, z_b, lb, nw, nreal, nblk):
    rows = z_a.shape[0]
    return pl.pallas_call(
        functools.partial(_hgrn_kernel, nblk=nblk),
        out_shape=jax.ShapeDtypeStruct((rows, WIDTH), BF16),
        grid=(nreal + 1,),
        in_specs=[_zspec(3, nreal), _zspec(0, nreal), _zspec(1, nreal), _zspec(2, nreal),
                  _pspec((1, WIDTH)), _pspec((1, HEAD_W))],
        out_specs=pl.BlockSpec((BLK, WIDTH), lambda i: (_row_block(i, nreal), 0)),
        scratch_shapes=[pltpu.VMEM((HEADS, HEAD_W, HEAD_W), F32), pltpu.VMEM((HEADS, HEAD_W, HEAD_W), F32)],
        compiler_params=_cparams(("arbitrary",)),
        name="hgrn2",
    )(z_a, z_b, z_b, z_b, lb, nw)


def _attn_lambda(lamv_ref, lambda_init):
    lv = lamv_ref[...]
    return (jnp.exp(jnp.sum(lv[0:1] * lv[1:2], axis=-1, keepdims=True))
            - jnp.exp(jnp.sum(lv[2:3] * lv[3:4], axis=-1, keepdims=True)) + lambda_init)


def _scaled_q(q_ref):
    qf = (q_ref[...].astype(F32) * (DIFF_HEAD_DIM ** -0.5)).astype(BF16)
    return qf[:, :DIFF_HEAD_DIM], qf[:, DIFF_HEAD_DIM:]


def _fold8(x, op):
    r, c = x.shape
    return op(x.reshape(r // 8, 8, c), axis=0)


def _attn_kernel(relb_ref, q_ref, k_ref, v_ref, km_ref, vm_ref, bias_ref, diag_ref, lamv_ref, sw_ref, y_ref,
                 s_sc, vt_sc, m_sc, l_sc, acc_sc, *, nqb, nblk, lambda_init):
    hd = pl.program_id(0)
    jq = pl.program_id(1) % nqb
    far = relb_ref[REL_BUCKETS - 1, hd] * LOG2E
    t1 = bias_ref[1] * LOG2E
    slot_diag, slot_meta = nqb - 1, nqb

    qtp = None

    def prepare(n):
        nonlocal qtp
        if n == 0:
            for t in range(nblk):
                vt_sc[t // NSUB, :, (t % NSUB) * BLK:(t % NSUB + 1) * BLK] = (
                    v_ref[t * BLK:(t + 1) * BLK, :].astype(F32).T.astype(BF16))
        qt = (q_ref[...].astype(F32) * (DIFF_HEAD_DIM ** -0.5)).T
        zero = jnp.zeros((DIFF_HEAD_DIM, QB), F32)
        qtp = (jnp.concatenate([qt[:DIFF_HEAD_DIM], zero], axis=0).astype(BF16),
               jnp.concatenate([zero, qt[DIFF_HEAD_DIM:]], axis=0).astype(BF16))

    def key_chunk(c):
        return k_ref[c * QB:(c + 1) * QB, :]

    def score_chunk(k_rows, slot, add_bias, r0=0, c0=0):
        r = k_rows.shape[0]
        for mp in range(2):
            s = add_bias(jnp.dot(k_rows, qtp[mp][:, c0:], preferred_element_type=F32) * LOG2E)
            s_sc[mp, slot, r0:r0 + r, c0:] = s
            m_sc[mp, :, c0:] = jnp.maximum(m_sc[mp, :, c0:], _fold8(s, jnp.max))

    def value_chunk(vt_cols, slot, m8, r0=0, c0=0):
        r = vt_cols.shape[1]
        for mp in range(2):
            p = jnp.exp2(s_sc[mp, slot, r0:r0 + r, c0:].reshape(r // 8, 8, QB - c0) - m8[mp][None, :, c0:])
            l_sc[mp, :, c0:] += jnp.sum(p, axis=0)
            acc_sc[mp, :, c0:] += jnp.dot(vt_cols, p.reshape(r, QB - c0).astype(BF16),
                                         preferred_element_type=F32)

    def diag_tiles(fn):
        for kb in range(NSUB):
            fn(kb, kb * BLK)

    def walk(n, fn_far, fn_meta, fn_diag):
        for c in range(n):
            fn_far(c)
        fn_meta()
        fn_diag()

    near = t1 - far
    first = jnp.full((1, BLK), jq, jnp.int32) == 0

    def meta_bias(s):
        head = s[:, 0:BLK] + (far + jnp.where(first, near[PAD_ROWS:, :], 0.0))
        return jnp.concatenate([head, s[:, BLK:] + far], axis=1)

    def meta_values(m8):
        v16 = vm_ref[PAD_ROWS:BLK, :]
        for mp in range(2):
            p = jnp.exp2(s_sc[mp, slot_meta, 0:N_META, :].reshape(N_META // 8, 8, QB) - m8[mp][None])
            l_sc[mp] += jnp.sum(p, axis=0)
            acc_sc[mp] += lax.dot_general(v16, p.reshape(N_META, QB).astype(BF16), (((0,), (0,)), ((), ())),
                                          preferred_element_type=F32)

    def far_bias(c):
        last = jnp.full((1, BLK), c, jnp.int32) == jq - 1

        def add(s):
            corner = s[QB - BLK:, 0:BLK] + (far + jnp.where(last, near, 0.0))
            bottom = jnp.concatenate([corner, s[QB - BLK:, BLK:] + far], axis=1)
            return jnp.concatenate([s[:QB - BLK, :] + far, bottom], axis=0)

        return add

    def both_passes(n):
        prepare(n)
        m_sc[...] = jnp.full_like(m_sc, NEG)
        walk(n, lambda c: score_chunk(key_chunk(c), c, far_bias(c)),
             lambda: score_chunk(km_ref[PAD_ROWS:BLK, :], slot_meta, meta_bias),
             lambda: diag_tiles(lambda kb, c0: score_chunk(
                 k_ref[n * QB + c0:n * QB + c0 + BLK, :], slot_diag,
                 lambda s: s + diag_ref[c0:c0 + BLK, c0:], r0=c0, c0=c0)))

        m8 = [jnp.broadcast_to(jnp.max(m_sc[mp], axis=0, keepdims=True), (8, QB)) for mp in range(2)]
        l_sc[...] = jnp.zeros_like(l_sc)
        acc_sc[...] = jnp.zeros_like(acc_sc)

        walk(n, lambda c: value_chunk(vt_sc[c], c, m8),
             lambda: meta_values(m8),
             lambda: diag_tiles(lambda kb, c0: value_chunk(
                 vt_sc[n, :, c0:c0 + BLK], slot_diag, m8, r0=c0, c0=c0)))

        lam = _attn_lambda(lamv_ref, lambda_init)
        l0 = jnp.sum(l_sc[0], axis=0, keepdims=True)
        l1 = jnp.sum(l_sc[1], axis=0, keepdims=True)
        ot = acc_sc[0] / l0 - lam * (acc_sc[1] / l1)
        ot = ot * lax.rsqrt(jnp.mean(ot * ot, axis=0, keepdims=True) + EPS)
        y_ref[...] = (ot.T * sw_ref[...] * (1.0 - lambda_init)).astype(y_ref.dtype)

    for n in range(nqb):
        pl.when(jq == n)(functools.partial(both_passes, n))


def _attn_meta_kernel(q_ref, k_ref, v_ref, bias_ref, lamv_ref, sw_ref, y_ref, *, lambda_init):
    qs = _scaled_q(q_ref)
    kb = k_ref[...]
    colmask = jnp.where(lax.broadcasted_iota(jnp.int32, (1, BLK), 1) >= PAD_ROWS, 0.0, NEG)
    outs = []
    bias = bias_ref[0].T + colmask
    for mp in range(2):
        s = lax.dot_general(qs[mp], kb[:, mp * DIFF_HEAD_DIM:(mp + 1) * DIFF_HEAD_DIM],
                            (((1,), (1,)), ((), ())), preferred_element_type=F32) + bias
        p = jnp.exp(s - jnp.max(s, axis=-1, keepdims=True))
        acc = jnp.dot(p.astype(BF16), v_ref[...], preferred_element_type=F32)
        outs.append(acc / jnp.sum(p, axis=-1, keepdims=True))
    o = outs[0] - _attn_lambda(lamv_ref, lambda_init) * outs[1]
    o = _rms(o, sw_ref[...]) * (1.0 - lambda_init)
    row = lax.broadcasted_iota(jnp.int32, (BLK, 1), 0)
    y_ref[...] = jnp.where(row >= PAD_ROWS, o, 0.0).astype(y_ref.dtype)


def _attn(zd, rel_bias, bias_tiles, bias_diag, lam_vec, subln_w, nreal, nblk, lambda_init):
    nqb = nblk * BLK // QB
    nq_total = nreal * BLK // QB
    seq = nblk * BLK
    kcol, vcol = HEADS, 2 * HEADS
    y_main = pl.pallas_call(
        functools.partial(_attn_kernel, nqb=nqb, nblk=nblk, lambda_init=lambda_init),
        out_shape=jax.ShapeDtypeStruct((nreal * BLK, WIDTH), BF16),
        grid=(HEADS, nq_total),
        in_specs=[pl.BlockSpec(memory_space=pltpu.SMEM),
                  pl.BlockSpec((QB, HEAD_W), lambda h, i: (i, h)),
                  pl.BlockSpec((seq, HEAD_W), lambda h, i: (i // nqb, kcol + h)),
                  pl.BlockSpec((seq, HEAD_W), lambda h, i: (i // nqb, vcol + h)),
                  pl.BlockSpec((BLK, HEAD_W), lambda h, i: (nreal, kcol + h)),
                  pl.BlockSpec((BLK, HEAD_W), lambda h, i: (nreal, vcol + h)),
                  pl.BlockSpec((None, 2, BLK, BLK), lambda h, i: (h, 0, 0, 0)),
                  pl.BlockSpec((None, QB, QB), lambda h, i: (h, 0, 0)),
                  pl.BlockSpec((4, DIFF_HEAD_DIM), lambda h, i: (0, 0)),
                  pl.BlockSpec((1, HEAD_W), lambda h, i: (0, 0))],
        out_specs=pl.BlockSpec((QB, HEAD_W), lambda h, i: (i, h)),
        scratch_shapes=[pltpu.VMEM((2, nqb + 1, QB, QB), F32),
                        pltpu.VMEM((nqb, HEAD_W, QB), BF16),
                        pltpu.VMEM((2, 8, QB), F32),
                        pltpu.VMEM((2, 8, QB), F32),
                        pltpu.VMEM((2, HEAD_W, QB), F32)],
        compiler_params=_cparams(("arbitrary", "arbitrary")),
        name="diff_attn",
    )(rel_bias, zd, zd, zd, zd, zd, bias_tiles, bias_diag, lam_vec, subln_w)
    y_meta = pl.pallas_call(
        functools.partial(_attn_meta_kernel, lambda_init=lambda_init),
        out_shape=jax.ShapeDtypeStruct((BLK, WIDTH), BF16),
        grid=(HEADS,),
        in_specs=[pl.BlockSpec((BLK, HEAD_W), lambda h: (nreal, h)),
                  pl.BlockSpec((BLK, HEAD_W), lambda h: (nreal, kcol + h)),
                  pl.BlockSpec((BLK, HEAD_W), lambda h: (nreal, vcol + h)),
                  pl.BlockSpec((None, 2, BLK, BLK), lambda h: (h, 0, 0, 0)),
                  pl.BlockSpec((4, DIFF_HEAD_DIM), lambda h: (0, 0)),
                  pl.BlockSpec((1, HEAD_W), lambda h: (0, 0))],
        out_specs=pl.BlockSpec((BLK, HEAD_W), lambda h: (0, h)),
        compiler_params=_cparams(("arbitrary",)),
        name="diff_attn_meta",
    )(zd, zd, zd, bias_tiles, lam_vec, subln_w)
    return jnp.concatenate([y_main, y_meta], axis=0)


def _largest_tile(rows, cap, align=16):
    best = align
    for t in range(align, cap + 1, align):
        if rows % t == 0:
            best = t
    return best


def _forward(x, meta_tokens, rel_bias, hgrn_lower_bounds, norm_mix_pre, norm_mix_post, norm_ffn_pre,
             norm_ffn_post, w_in, lru_conv_w, lru_conv_b, lru_w_a, lru_b_a, lru_w_x, lru_b_x, lru_lambda,
             pool_w, pool_scale, hgrn_norm, diff_lambda, diff_subln, w_branch, w_out, ffn_w_gu, ffn_w_down):
    bsz, seq, _ = x.shape
    nblk = seq // BLK
    nreal = bsz * nblk
    rows = (nreal + 1) * BLK
    rows_real = nreal * BLK
    tm_big = _largest_tile(rows, 1040)
    tm_epi = _largest_tile(rows, 640)
    tm_down = _largest_tile(rows, 320)
    tm_last = _largest_tile(rows_real, 256)

    def vec(a):
        return a.reshape(1, -1)

    lbs, bias_tiles, bias_diag = _prologue(hgrn_lower_bounds, rel_bias)
    h, hn = _embed(x.reshape(rows_real, D_MODEL), meta_tokens, vec(norm_mix_pre[0]), nreal)

    for layer in range(DEPTH):
        lambda_init = 0.8 - 0.6 * math.exp(-0.3 * layer)
        z = _mixer_in_proj(hn, w_in, layer, 0, 4 * WIDTH, F32, tm_big, 1024, 1024)
        z_b = _mixer_in_proj(hn, w_in, layer, 4 * WIDTH, 3 * WIDTH, F32, tm_big, 3 * WIDTH, WIDTH)
        zd = _mixer_in_proj(hn, w_in, layer, 7 * WIDTH, 3 * WIDTH, BF16, tm_big, 3 * WIDTH, WIDTH)
        y_a, y_b = _lru_pool(z, lru_conv_w[layer], vec(lru_conv_b[layer]), lru_w_a[layer], vec(lru_b_a[layer]),
                             lru_w_x[layer], vec(lru_b_x[layer]), vec(lru_lambda[layer]),
                             pool_w[layer], vec(pool_scale[layer]), nreal, nblk)
        y_c = _hgrn(z, z_b, lbs[layer:layer + 1], vec(hgrn_norm[layer]), nreal, nblk)
        y_d = _attn(zd, rel_bias, bias_tiles, bias_diag, diff_lambda[layer], vec(diff_subln[layer]),
                    nreal, nblk, lambda_init)
        merged = _gate_merge(hn, w_in, (y_a, y_b, y_c, y_d), w_branch, layer, tm_big, 256)
        h, hn = _out_proj(merged, _cast_layer_bf16(w_out, layer, 512), h, vec(norm_mix_post[layer]),
                          vec(norm_ffn_pre[layer]), tm_epi)
        a = _swiglu_up(hn, ffn_w_gu, layer, tm_big, 512)
        last = layer == DEPTH - 1
        w_next = vec(norm_mix_pre[layer + 1]) if not last else vec(norm_mix_pre[layer])
        h, hn = _down_proj(a, _cast_layer_bf16(ffn_w_down, layer, 512), h, vec(norm_ffn_post[layer]), w_next,
                           rows_real if last else rows, tm_last if last else tm_down, not last)
    return h.reshape(bsz, seq, D_MODEL)


def kernel(x, meta_tokens, rel_bias, hgrn_lower_bounds, norm_mix_pre, norm_mix_post, norm_ffn_pre, norm_ffn_post, w_in, lru_conv_w, lru_conv_b, lru_w_a, lru_b_a, lru_w_x, lru_b_x, lru_lambda, pool_w, pool_scale, hgrn_norm, diff_lambda, diff_subln, w_branch, w_out, ffn_w_gu, ffn_w_down):
    return _forward(x, meta_tokens, rel_bias, hgrn_lower_bounds, norm_mix_pre, norm_mix_post, norm_ffn_pre,
                    norm_ffn_post, w_in, lru_conv_w, lru_conv_b, lru_w_a, lru_b_a, lru_w_x, lru_b_x, lru_lambda,
                    pool_w, pool_scale, hgrn_norm, diff_lambda, diff_subln, w_branch, w_out, ffn_w_gu, ffn_w_down)
```
